```python
import jax, jax.numpy as jnp
from jax import lax
import numpy as np

D_MODEL = 2048
BATCH = 16
SEQ = 256
DEPTH = 1
DEC_BATCH = 4
DEC_SEQ = 2048
PAST_LEN = 256

GRID_W = 64
ATT_HEADS = 8
ATT_KV_HEADS = 2
ATT_HEAD_DIM = 128
ATT_GROUP = ATT_HEADS // ATT_KV_HEADS
ATT_WIDTH = ATT_HEADS * ATT_HEAD_DIM
KV_WIDTH = ATT_KV_HEADS * ATT_HEAD_DIM
Q_BLOCK = 128
ROPE_THETA = 10000.0
MLSTM_HEADS = 4
MLSTM_HEAD_DIM = 256
MLSTM_WIDTH = MLSTM_HEADS * MLSTM_HEAD_DIM
MLSTM_CHUNK = 128
N_GATES = 4 * MLSTM_HEADS
MIX_WIDTH = ATT_WIDTH + MLSTM_WIDTH
IN_WIDTH = ATT_WIDTH + 2 * KV_WIDTH + 4 * MLSTM_WIDTH + N_GATES
N_GROUPS = 4
EXPERTS_PER_GROUP = 8
N_EXPERTS = N_GROUPS * EXPERTS_PER_GROUP
TOP_K_INNER = 2
D_EXPERT = 512
N_MOD = 6
EPS = 1e-6

kernel_name = 'hymba_diffusion_gqa_mlstm_hmoe'


def rmsnorm(x, g):
    xf = x.astype(jnp.float32)
    y = xf * lax.rsqrt(jnp.mean(xf * xf, axis=-1, keepdims=True) + EPS)
    return (y * g.astype(jnp.float32)).astype(x.dtype)


def modulation(cond, mod_w, mod_b):
    m = jax.nn.silu(cond) @ mod_w + mod_b
    return jnp.split(m[..., None, :], N_MOD, axis=-1)


def modulate(h, shift, scale):
    return h * (1 + scale) + shift


def axial_rope_tables(rows):
    row = jnp.repeat(jnp.arange(rows), GRID_W).astype(jnp.float32)
    col = jnp.tile(jnp.arange(GRID_W), rows).astype(jnp.float32)
    n_freq = ATT_HEAD_DIM // 4
    inv = ROPE_THETA ** (-jnp.arange(n_freq, dtype=jnp.float32) / n_freq)
    ang = jnp.concatenate([row[:, None] * inv, col[:, None] * inv], axis=-1)
    return jnp.cos(ang), jnp.sin(ang)


def apply_rope(x, cos, sin):
    xf = x.astype(jnp.float32).reshape(*x.shape[:-1], -1, 2)
    shape = (1, cos.shape[0]) + (1,) * (x.ndim - 3) + (cos.shape[1],)
    c, s = cos.reshape(shape), sin.reshape(shape)
    x0, x1 = xf[..., 0], xf[..., 1]
    return jnp.stack([x0 * c - x1 * s, x0 * s + x1 * c], axis=-1).reshape(x.shape).astype(x.dtype)


def project(h, w_in, b_gates, q_norm_g, k_norm_g):
    b, n, _ = h.shape
    p = h @ w_in
    splits = np.cumsum([ATT_WIDTH, KV_WIDTH, KV_WIDTH, MLSTM_WIDTH, MLSTM_WIDTH, MLSTM_WIDTH, MLSTM_WIDTH])
    qa, ka, va, qm, km, vm, om, gates = jnp.split(p, splits, axis=-1)
    qa = rmsnorm(qa.reshape(b, n, ATT_KV_HEADS, ATT_GROUP, ATT_HEAD_DIM), q_norm_g)
    ka = rmsnorm(ka.reshape(b, n, ATT_KV_HEADS, ATT_HEAD_DIM), k_norm_g)
    va = va.reshape(b, n, ATT_KV_HEADS, ATT_HEAD_DIM)
    mshape = (b, n, MLSTM_HEADS, MLSTM_HEAD_DIM)
    gates = gates.astype(jnp.float32) + b_gates.astype(jnp.float32)
    return qa, ka, va, qm.reshape(mshape), km.reshape(mshape), vm.reshape(mshape), om, gates


def block_attention(q, k, v):
    b, n, kvh, g, hd = q.shape
    qb = jnp.moveaxis(q.reshape(b, n // Q_BLOCK, Q_BLOCK, kvh, g, hd), 1, 0)
    scale = hd ** -0.5
    kf = k.astype(jnp.float32)

    def one_block(qblk):
        s = jnp.einsum('bqkgd,bskd->bkgqs', qblk.astype(jnp.float32), kf) * scale
        p = jax.nn.softmax(s, axis=-1)
        return jnp.einsum('bkgqs,bskd->bqkgd', p.astype(v.dtype), v)

    o = lax.map(one_block, qb)
    return jnp.moveaxis(o, 0, 1).reshape(b, n, kvh * g * hd)


def mlstm_chunkwise(q, k, v, log_i, log_f, state):
    b, n, h, d = q.shape
    L = MLSTM_CHUNK
    nc = n // L

    def to_chunks(a):
        return jnp.moveaxis(a.reshape(b, nc, L, *a.shape[2:]), 1, 0)

    xs = (to_chunks(q), to_chunks(k), to_chunks(v), to_chunks(log_i), to_chunks(log_f))
    lower = jnp.tril(jnp.ones((L, L), dtype=bool))
    state = tuple(s.astype(jnp.float32) for s in state)

    def step(carry, inp):
        C, nv, m = carry
        qc, kc, vc, li, lf = inp
        qc = qc.astype(jnp.float32) * (d ** -0.5)
        kc = kc.astype(jnp.float32)
        vc = vc.astype(jnp.float32)
        bT = jnp.moveaxis(jnp.cumsum(lf, axis=1), 1, 2)
        liT = jnp.moveaxis(li, 1, 2)
        dmat = jnp.where(lower, bT[..., :, None] - bT[..., None, :] + liT[..., None, :], -jnp.inf)
        m_inter = bT + m[..., None]
        m_t = jnp.maximum(m_inter, dmat.max(axis=-1))
        w_intra = jnp.exp(dmat - m_t[..., None])
        w_inter = jnp.exp(m_inter - m_t)
        qk = jnp.einsum('blhd,bshd->bhls', qc, kc) * w_intra
        num = jnp.einsum('bhls,bshd->bhld', qk, vc) + w_inter[..., None] * jnp.einsum('blhd,bhde->bhle', qc, C)
        den = qk.sum(axis=-1) + w_inter * jnp.einsum('blhd,bhd->bhl', qc, nv)
        h_out = num / jnp.maximum(jnp.abs(den), jnp.exp(-m_t))[..., None]
        b_last = bT[..., -1]
        g_s = b_last[..., None] - bT + liT
        m_new = jnp.maximum(b_last + m, g_s.max(axis=-1))
        kw = kc * jnp.moveaxis(jnp.exp(g_s - m_new[..., None]), 1, 2)[..., None]
        wc = jnp.exp(b_last + m - m_new)
        C_new = wc[..., None, None] * C + jnp.einsum('bshd,bshe->bhde', kw, vc)
        n_new = wc[..., None] * nv + kw.sum(axis=1)
        return (C_new, n_new, m_new), jnp.moveaxis(h_out, 1, 2)

    state, hs = lax.scan(step, state, xs)
    return jnp.moveaxis(hs, 0, 1).reshape(b, n, h, d), state


def mlstm_bidir(q, k, v, gates, state_fwd, state_bwd):
    i_f, f_f, i_b, f_b = jnp.split(gates, 4, axis=-1)
    h_f, s_f = mlstm_chunkwise(q, k, v, i_f, jax.nn.log_sigmoid(f_f), state_fwd)
    fl = lambda a: jnp.flip(a, axis=1)
    h_b, s_b = mlstm_chunkwise(fl(q), fl(k), fl(v), fl(i_b), fl(jax.nn.log_sigmoid(f_b)), state_bwd)
    return h_f + fl(h_b), s_f, s_b


def mix_out(attn, hm, om, mlstm_norm_g, w_out):
    b, n, _ = attn.shape
    hm = rmsnorm(hm, mlstm_norm_g.reshape(MLSTM_HEADS, MLSTM_HEAD_DIM)).reshape(b, n, MLSTM_WIDTH)
    hm = hm.astype(om.dtype) * jax.nn.sigmoid(om)
    return jnp.concatenate([attn, hm], axis=-1) @ w_out


def hier_moe(h, rgw, rgb, rew, reb, wg, wu, wd):
    shp = h.shape
    hf = h.reshape(-1, shp[-1])
    t = hf.shape[0]
    g_logit = (hf @ rgw).astype(jnp.float32) + rgb.astype(jnp.float32)
    grp = jnp.argmax(g_logit, axis=-1)
    p_grp = jnp.take_along_axis(jax.nn.softmax(g_logit, axis=-1), grp[:, None], axis=-1)
    e_logit = jnp.einsum('td,gde->tge', hf, rew).astype(jnp.float32) + reb.astype(jnp.float32)
    e_sel = jnp.take_along_axis(e_logit, grp[:, None, None], axis=1)[:, 0]
    top_p, top_i = lax.top_k(jax.nn.softmax(e_sel, axis=-1), TOP_K_INNER)
    weights = p_grp * top_p / top_p.sum(axis=-1, keepdims=True)
    expert_id = grp[:, None] * EXPERTS_PER_GROUP + top_i
    combine = jnp.einsum('tk,tke->te', weights, jax.nn.one_hot(expert_id, N_EXPERTS, dtype=jnp.float32))
    out = jnp.zeros((t, shp[-1]), jnp.float32)
    for e in range(N_EXPERTS):
        a = jax.nn.silu(hf @ wg[e]) * (hf @ wu[e])
        out = out + combine[:, e:e + 1] * (a @ wd[e]).astype(jnp.float32)
    return out.astype(h.dtype).reshape(shp)


def context_layer(x, c_ctx, lw):
    mw, mb, n1, n2, w_in, bg, qg, kg, mg, w_out, rgw, rgb, rew, reb, wg, wu, wd = lw
    sh1, sc1, g1, sh2, sc2, g2 = modulation(c_ctx, mw, mb)
    b = x.shape[0]
    h = modulate(rmsnorm(x, n1), sh1, sc1)
    qa, ka, va, qm, km, vm, om, gates = project(h, w_in, bg, qg, kg)
    attn = block_attention(qa, ka, va)
    zero = (jnp.zeros((b, MLSTM_HEADS, MLSTM_HEAD_DIM, MLSTM_HEAD_DIM), jnp.float32),
            jnp.zeros((b, MLSTM_HEADS, MLSTM_HEAD_DIM), jnp.float32),
            jnp.zeros((b, MLSTM_HEADS), jnp.float32))
    hm, s_f, s_b = mlstm_bidir(qm, km, vm, gates, zero, zero)
    x = x + g1 * mix_out(attn, hm, om, mg, w_out)
    h = modulate(rmsnorm(x, n2), sh2, sc2)
    x = x + g2 * hier_moe(h, rgw, rgb, rew, reb, wg, wu, wd)
    sC = jnp.stack([s_f[0], s_b[0]], axis=1)
    sn = jnp.stack([s_f[1], s_b[1]], axis=1)
    sm = jnp.stack([s_f[2], s_b[2]], axis=1)
    return x, ka, va, sC, sn, sm


def latent_layer(x, c, ck, cv, sC, sn, sm, cos, sin, lw):
    mw, mb, n1, n2, w_in, bg, qg, kg, mg, w_out, rgw, rgb, rew, reb, wg, wu, wd = lw
    sh1, sc1, g1, sh2, sc2, g2 = modulation(c, mw, mb)
    h = modulate(rmsnorm(x, n1), sh1, sc1)
    qa, ka, va, qm, km, vm, om, gates = project(h, w_in, bg, qg, kg)
    qa = apply_rope(qa, cos, sin)
    ka = apply_rope(ka, cos, sin)
    k_all = jnp.concatenate([ck.astype(ka.dtype), ka], axis=1)
    v_all = jnp.concatenate([cv.astype(va.dtype), va], axis=1)
    attn = block_attention(qa, k_all, v_all)
    hm, _, _ = mlstm_bidir(qm, km, vm, gates, (sC[:, 0], sn[:, 0], sm[:, 0]), (sC[:, 1], sn[:, 1], sm[:, 1]))
    x = x + g1 * mix_out(attn, hm, om, mg, w_out)
    h = modulate(rmsnorm(x, n2), sh2, sc2)
    return x + g2 * hier_moe(h, rgw, rgb, rew, reb, wg, wu, wd)


def setup_inputs(seed: int = 0) -> dict:
    key = jax.random.key(seed)
    ks = jax.random.split(key, 26)
    nrm = jax.random.normal
    f32 = jnp.float32
    d = D_MODEL
    b_gates = jnp.concatenate([
        0.1 * nrm(ks[20], (DEPTH, MLSTM_HEADS), f32),
        3.0 + 0.5 * nrm(ks[21], (DEPTH, MLSTM_HEADS), f32),
        0.1 * nrm(ks[22], (DEPTH, MLSTM_HEADS), f32),
        3.0 + 0.5 * nrm(ks[23], (DEPTH, MLSTM_HEADS), f32)], axis=-1)
    return {
        'x_prompt': nrm(ks[0], (BATCH, SEQ, d), f32),
        'x_sample': nrm(ks[1], (DEC_BATCH, DEC_SEQ, d), f32),
        'cache_k': nrm(ks[2], (DEC_BATCH, DEPTH, PAST_LEN, ATT_KV_HEADS, ATT_HEAD_DIM), f32),
        'cache_v': nrm(ks[3], (DEC_BATCH, DEPTH, PAST_LEN, ATT_KV_HEADS, ATT_HEAD_DIM), f32),
        'state_C': 0.1 * nrm(ks[4], (DEC_BATCH, DEPTH, 2, MLSTM_HEADS, MLSTM_HEAD_DIM, MLSTM_HEAD_DIM), f32),
        'state_n': 0.1 * nrm(ks[5], (DEC_BATCH, DEPTH, 2, MLSTM_HEADS, MLSTM_HEAD_DIM), f32),
        'state_m': 0.5 * nrm(ks[6], (DEC_BATCH, DEPTH, 2, MLSTM_HEADS), f32),
        'c': nrm(ks[7], (DEC_BATCH, d), f32),
        'c_ctx': nrm(ks[8], (d,), f32),
        'mod_w': 0.5 * d ** -0.5 * nrm(ks[9], (DEPTH, d, N_MOD * d), f32),
        'mod_b': 0.02 * nrm(ks[10], (DEPTH, N_MOD * d), f32),
        'norm1_g': 1.0 + 0.05 * nrm(ks[11], (DEPTH, d), f32),
        'norm2_g': 1.0 + 0.05 * nrm(ks[12], (DEPTH, d), f32),
        'w_in': d ** -0.5 * nrm(ks[13], (DEPTH, d, IN_WIDTH), f32),
        'b_gates': b_gates,
        'q_norm_g': 1.0 + 0.05 * nrm(ks[14], (DEPTH, ATT_HEAD_DIM), f32),
        'k_norm_g': 1.0 + 0.05 * nrm(ks[15], (DEPTH, ATT_HEAD_DIM), f32),
        'mlstm_norm_g': 1.0 + 0.05 * nrm(ks[16], (DEPTH, MLSTM_WIDTH), f32),
        'w_out': MIX_WIDTH ** -0.5 * nrm(ks[17], (DEPTH, MIX_WIDTH, d), f32),
        'router_group_w': d ** -0.5 * nrm(ks[18], (DEPTH, d, N_GROUPS), f32),
        'router_group_b': 0.01 * nrm(ks[19], (DEPTH, N_GROUPS), f32),
        'router_expert_w': d ** -0.5 * nrm(ks[24], (DEPTH, N_GROUPS, d, EXPERTS_PER_GROUP), f32),
        'router_expert_b': 0.01 * nrm(ks[25], (DEPTH, N_GROUPS, EXPERTS_PER_GROUP), f32),
        'expert_w_gate': d ** -0.5 * nrm(jax.random.fold_in(key, 101), (DEPTH, N_EXPERTS, d, D_EXPERT), f32),
        'expert_w_up': d ** -0.5 * nrm(jax.random.fold_in(key, 102), (DEPTH, N_EXPERTS, d, D_EXPERT), f32),
        'expert_w_down': D_EXPERT ** -0.5 * nrm(jax.random.fold_in(key, 103), (DEPTH, N_EXPERTS, D_EXPERT, d), f32),
    }


def reference(x_prompt, x_sample, cache_k, cache_v, state_C, state_n, state_m, c, c_ctx,
              mod_w, mod_b, norm1_g, norm2_g, w_in, b_gates, q_norm_g, k_norm_g, mlstm_norm_g, w_out,
              router_group_w, router_group_b, router_expert_w, router_expert_b,
              expert_w_gate, expert_w_up, expert_w_down):
    rows = x_sample.shape[1] // GRID_W
    cos, sin = axial_rope_tables(rows)
    xp, xs = x_prompt, x_sample
    ks_out, vs_out, cs_out, ns_out, ms_out = [], [], [], [], []
    for l in range(DEPTH):
        lw = (mod_w[l], mod_b[l], norm1_g[l], norm2_g[l], w_in[l], b_gates[l], q_norm_g[l], k_norm_g[l],
              mlstm_norm_g[l], w_out[l], router_group_w[l], router_group_b[l], router_expert_w[l],
              router_expert_b[l], expert_w_gate[l], expert_w_up[l], expert_w_down[l])
        xp, ka, va, sC, sn, sm = context_layer(xp, c_ctx, lw)
        ks_out.append(ka)
        vs_out.append(va)
        cs_out.append(sC)
        ns_out.append(sn)
        ms_out.append(sm)
        xs = latent_layer(xs, c, cache_k[:, l], cache_v[:, l], state_C[:, l], state_n[:, l], state_m[:, l],
                          cos, sin, lw)
    new_cache_k = jnp.stack(ks_out, axis=1)
    new_cache_v = jnp.stack(vs_out, axis=1)
    new_state_C = jnp.stack(cs_out, axis=1)
    new_state_n = jnp.stack(ns_out, axis=1)
    new_state_m = jnp.stack(ms_out, axis=1)
    return (xp, xs, new_cache_k, new_cache_v, new_state_C, new_state_n, new_state_m)
```

```python
import functools

import numpy as np
import jax
import jax.numpy as jnp
from jax import lax
from jax.experimental import pallas as pl
from jax.experimental.pallas import tpu as pltpu

F32 = jnp.float32
BF16 = jnp.bfloat16

D_MODEL = 2048
GRID_W = 64
ATT_HEADS = 8
ATT_KV_HEADS = 2
ATT_HEAD_DIM = 128
ATT_GROUP = ATT_HEADS // ATT_KV_HEADS
ATT_WIDTH = ATT_HEADS * ATT_HEAD_DIM
KV_WIDTH = ATT_KV_HEADS * ATT_HEAD_DIM
ROPE_THETA = 10000.0
MLSTM_HEADS = 4
MLSTM_HEAD_DIM = 256
MLSTM_WIDTH = MLSTM_HEADS * MLSTM_HEAD_DIM
CHUNK = 128
N_GATES = 4 * MLSTM_HEADS
N_CHAINS = 2 * MLSTM_HEADS
MAIN_WIDTH = ATT_WIDTH + 2 * KV_WIDTH + 4 * MLSTM_WIDTH
N_GROUPS = 4
EXPERTS_PER_GROUP = 8
N_EXPERTS = N_GROUPS * EXPERTS_PER_GROUP
D_EXPERT = 512
N_MOD = 6
EPS = 1e-6

LANES = 128
AUG = MLSTM_HEAD_DIM + LANES
N_SCAN_ROWS = 5 * N_CHAINS
ROUTER_LANE0 = N_GROUPS
VMEM_LIMIT = 56 * 1024 * 1024


def _cparams(*sem):
    return pltpu.CompilerParams(dimension_semantics=sem, vmem_limit_bytes=VMEM_LIMIT)


def _const_spec(shape):
    nd = len(shape)
    return pl.BlockSpec(shape, lambda *_: (0,) * nd, pipeline_mode=pl.Buffered(1))


def _split_hi_lo(x):
    hi = x.astype(BF16)
    lo = (x - hi.astype(F32)).astype(BF16)
    return hi, lo


def _dot(a, b):
    return jnp.dot(a, b, preferred_element_type=F32)


def _dot3(a_hi, a_lo, b_hi, b_lo):
    return _dot(a_hi, b_hi) + _dot(a_lo, b_hi) + _dot(a_hi, b_lo)


def _rms(x, g):
    return x * lax.rsqrt(jnp.mean(x * x, axis=-1, keepdims=True) + EPS) * g


def _mod_kernel(c_ref, w_ref, b_ref, o_ref):
    c = c_ref[...]
    s = c / (1.0 + jnp.exp(-c))
    s_hi = s.astype(BF16).astype(F32)
    lhs = jnp.concatenate([s_hi, s - s_hi], axis=0).astype(BF16)
    w_hi, w_lo = _split_hi_lo(w_ref[...])
    r = _dot(lhs, w_hi)
    r2 = _dot(lhs, w_lo)
    o_ref[...] = r[:8] + r[8:] + r2[:8] + b_ref[...]


def _modulation(cond, mod_w, mod_b):
    n = mod_w.shape[1]
    tn = 512
    return pl.pallas_call(
        _mod_kernel,
        grid=(n // tn,),
        in_specs=[pl.BlockSpec((8, D_MODEL), lambda j: (0, 0)),
                  pl.BlockSpec((D_MODEL, tn), lambda j: (0, j)),
                  pl.BlockSpec((1, tn), lambda j: (0, j))],
        out_specs=pl.BlockSpec((8, tn), lambda j: (0, j)),
        out_shape=jax.ShapeDtypeStruct((8, n), F32),
        compiler_params=_cparams("parallel"),
        name="modulation",
    )(cond, mod_w, mod_b)


def _pair_swap(x):
    lane = lax.broadcasted_iota(jnp.int32, x.shape, 1)
    return jnp.where((lane & 1) == 0, pltpu.roll(x, LANES - 1, 1), pltpu.roll(x, 1, 1))


def _proj_kernel(x_ref, mod_ref, n1_ref, w_ref, wgh_ref, wgl_ref, bg_ref, qg_ref, kg_ref, cos_ref, sin_ref,
                 q_ref, k_ref, v_ref, qm_ref, kmt_ref, vm_ref, om_ref, g_ref, *kv_refs, rope, tm):
    mod = mod_ref[0]
    shift, scale = mod[:, :D_MODEL], mod[:, D_MODEL:2 * D_MODEL]
    h = _rms(x_ref[...], n1_ref[...]) * (1.0 + scale) + shift
    h_hi, h_lo = _split_hi_lo(h)

    def rot(seg):
        return seg * cos_ref[...] + _pair_swap(seg) * sin_ref[...] if rope else seg

    qa = _dot(h_hi, w_ref[:, :ATT_WIDTH])
    for hh in range(ATT_HEADS):
        sl = slice(hh * ATT_HEAD_DIM, (hh + 1) * ATT_HEAD_DIM)
        seg = rot(_rms(qa[:, sl], qg_ref[...]))
        q_ref[:, sl] = (seg * ATT_HEAD_DIM ** -0.5).astype(BF16)

    kv = _dot(h_hi, w_ref[:, ATT_WIDTH:ATT_WIDTH + 2 * KV_WIDTH])
    for hh in range(ATT_KV_HEADS):
        sl = slice(hh * ATT_HEAD_DIM, (hh + 1) * ATT_HEAD_DIM)
        seg = _rms(kv[:, sl], kg_ref[...])
        if kv_refs:
            kv_refs[0][:, sl] = seg
        k_ref[:, sl] = rot(seg).astype(BF16)
    va = kv[:, KV_WIDTH:]
    if kv_refs:
        kv_refs[1][...] = va
    v_ref[...] = va.astype(BF16)

    c0 = ATT_WIDTH + 2 * KV_WIDTH
    qm_ref[...] = (_dot(h_hi, w_ref[:, c0:c0 + MLSTM_WIDTH]) * MLSTM_HEAD_DIM ** -0.5).astype(BF16)
    km = _dot(h_hi, w_ref[:, c0 + MLSTM_WIDTH:c0 + 2 * MLSTM_WIDTH])
    for cc in range(tm // CHUNK):
        kmt_ref[cc] = km[cc * CHUNK:(cc + 1) * CHUNK, :].T.astype(BF16)
    vm_ref[...] = _dot(h_hi, w_ref[:, c0 + 2 * MLSTM_WIDTH:c0 + 3 * MLSTM_WIDTH]).astype(BF16)
    om_ref[...] = _dot(h_hi, w_ref[:, c0 + 3 * MLSTM_WIDTH:c0 + 4 * MLSTM_WIDTH])
    g_ref[...] = _dot3(h_hi, h_lo, wgh_ref[...], wgl_ref[...]) + bg_ref[...]


def _project(x2d, mod3, row_of_tile, n1, w_main, wg_hi, wg_lo, bg, qg, kg, cos_t, sin_t, *, rope, emit_kv, tm=256):
    t = x2d.shape[0]
    n_pos = cos_t.shape[0] // tm
    row = lambda i: (i, 0)
    in_specs = [pl.BlockSpec((tm, D_MODEL), row),
                pl.BlockSpec((1, 1, N_MOD * D_MODEL), lambda i: (row_of_tile(i), 0, 0)),
                _const_spec((1, D_MODEL)),
                _const_spec((D_MODEL, MAIN_WIDTH)),
                _const_spec((D_MODEL, LANES)), _const_spec((D_MODEL, LANES)), _const_spec((1, LANES)),
                _const_spec((1, ATT_HEAD_DIM)), _const_spec((1, ATT_HEAD_DIM)),
                pl.BlockSpec((tm, ATT_HEAD_DIM), lambda i: (i % n_pos, 0)),
                pl.BlockSpec((tm, ATT_HEAD_DIM), lambda i: (i % n_pos, 0))]
    out_shape = [jax.ShapeDtypeStruct((t, ATT_WIDTH), BF16), jax.ShapeDtypeStruct((t, KV_WIDTH), BF16),
                 jax.ShapeDtypeStruct((t, KV_WIDTH), BF16), jax.ShapeDtypeStruct((t, MLSTM_WIDTH), BF16),
                 jax.ShapeDtypeStruct((t // CHUNK, MLSTM_WIDTH, CHUNK), BF16),
                 jax.ShapeDtypeStruct((t, MLSTM_WIDTH), BF16), jax.ShapeDtypeStruct((t, MLSTM_WIDTH), F32),
                 jax.ShapeDtypeStruct((t, LANES), F32)]
    out_specs = [pl.BlockSpec((tm, ATT_WIDTH), row), pl.BlockSpec((tm, KV_WIDTH), row),
                 pl.BlockSpec((tm, KV_WIDTH), row), pl.BlockSpec((tm, MLSTM_WIDTH), row),
                 pl.BlockSpec((tm // CHUNK, MLSTM_WIDTH, CHUNK), lambda i: (i, 0, 0)),
                 pl.BlockSpec((tm, MLSTM_WIDTH), row), pl.BlockSpec((tm, MLSTM_WIDTH), row),
                 pl.BlockSpec((tm, LANES), row)]
    if emit_kv:
        out_shape += [jax.ShapeDtypeStruct((t, KV_WIDTH), F32)] * 2
        out_specs += [pl.BlockSpec((tm, KV_WIDTH), row)] * 2
    return pl.pallas_call(
        functools.partial(_proj_kernel, rope=rope, tm=tm),
        grid=(t // tm,), in_specs=in_specs, out_specs=out_specs, out_shape=out_shape,
        compiler_params=_cparams("parallel"), name="in_proj",
    )(x2d, mod3, n1, w_main, wg_hi, wg_lo, bg, qg, kg, cos_t, sin_t)


def _lane_scan(x, op, fill, is_fwd, lane):
    s = 1
    while s < CHUNK:
        from_left = jnp.where(lane >= s, pltpu.roll(x, s, 1), fill)
        from_right = jnp.where(lane < CHUNK - s, pltpu.roll(x, CHUNK - s, 1), fill)
        x = op(x, jnp.where(is_fwd, from_left, from_right))
        s *= 2
    return x


def _scan_kernel(g_ref, row_ref, col_ref, *, n_chunks):
    lane = lax.broadcasted_iota(jnp.int32, (N_CHAINS, CHUNK), 1)
    is_fwd = lax.broadcasted_iota(jnp.int32, (N_CHAINS, CHUNK), 0) < MLSTM_HEADS
    for cc in range(n_chunks):
        gt = g_ref[cc * CHUNK:(cc + 1) * CHUNK, :].T
        li, f = gt[:N_CHAINS], gt[N_CHAINS:2 * N_CHAINS]
        lf = jnp.minimum(f, 0.0) - jnp.log(1.0 + jnp.exp(-jnp.abs(f)))
        b = _lane_scan(lf, jnp.add, 0.0, is_fwd, lane)
        a = li - b
        run_max = _lane_scan(a, jnp.maximum, -jnp.inf, is_fwd, lane)
        all_max = jnp.broadcast_to(jnp.max(a, axis=1, keepdims=True), a.shape)
        last = jnp.where(is_fwd, CHUNK - 1, 0)
        total = jnp.broadcast_to(jnp.sum(jnp.where(lane == last, b, 0.0), axis=1, keepdims=True), a.shape)
        rows = jnp.concatenate([a, b, run_max, all_max, total], axis=0)
        row_ref[cc] = rows
        padded = jnp.concatenate([rows, jnp.zeros((LANES - N_SCAN_ROWS, CHUNK), F32)], axis=0)
        col_ref[cc * CHUNK:(cc + 1) * CHUNK, :] = padded.T


def _gate_scans(g, tb=1024):
    t = g.shape[0]
    tb = min(tb, t)
    return pl.pallas_call(
        functools.partial(_scan_kernel, n_chunks=tb // CHUNK),
        grid=(t // tb,),
        in_specs=[pl.BlockSpec((tb, LANES), lambda i: (i, 0))],
        out_specs=[pl.BlockSpec((tb // CHUNK, N_SCAN_ROWS, CHUNK), lambda i: (i, 0, 0)),
                   pl.BlockSpec((tb, LANES), lambda i: (i, 0))],
        out_shape=[jax.ShapeDtypeStruct((t // CHUNK, N_SCAN_ROWS, CHUNK), F32),
                   jax.ShapeDtypeStruct((t, LANES), F32)],
        compiler_params=_cparams("parallel"), name="gate_scans",
    )(g)


def _mlstm_kernel(*refs, n_chunks, has_state, emit_state):
    it = iter(refs)
    q_ref, kt_ref, v_ref, row_ref, col_ref = [next(it) for _ in range(5)]
    c0_ref, m0_ref = (next(it), next(it)) if has_state else (None, None)
    o_ref = next(it)
    c_out, n_out, m_out = (next(it), next(it), next(it)) if emit_state else (None, None, None)
    caug, hb = next(it), next(it)

    b = pl.program_id(0)
    if has_state:
        caug[...] = c0_ref[0]
        m_init = tuple(jnp.full((1, 1), m0_ref[b * N_CHAINS + r], F32) for r in range(N_CHAINS))
    else:
        caug[...] = jnp.zeros(caug.shape, F32)
        m_init = tuple(jnp.zeros((1, 1), F32) for _ in range(N_CHAINS))

    sub = lax.broadcasted_iota(jnp.int32, (CHUNK, CHUNK), 0)
    lane = lax.broadcasted_iota(jnp.int32, (CHUNK, CHUNK), 1)
    ones_col = (lax.broadcasted_iota(jnp.int32, (CHUNK, LANES), 1) == 0).astype(BF16)

    def chain(r, cc, m_prev):
        d, hd = divmod(r, MLSTM_HEADS)
        t0 = pl.multiple_of(cc * CHUNK, CHUNK)
        hs = slice(hd * MLSTM_HEAD_DIM, (hd + 1) * MLSTM_HEAD_DIM)
        rows = row_ref[cc]
        cols = col_ref[pl.ds(t0, CHUNK), :]
        row = lambda k: rows[k * N_CHAINS + r:k * N_CHAINS + r + 1, :]
        col = lambda k: cols[:, k * N_CHAINS + r:k * N_CHAINS + r + 1]
        q = q_ref[pl.ds(t0, CHUNK), hs]
        kt = kt_ref[cc, hs, :]
        vaug = jnp.concatenate([v_ref[pl.ds(t0, CHUNK), hs], ones_col], axis=1)

        m_col = jnp.maximum(m_prev, col(2))
        keep = (lane <= sub) if d == 0 else (lane >= sub)
        w = jnp.where(keep, jnp.exp(row(0) - m_col), 0.0)
        w_inter = jnp.exp(m_prev - m_col)
        p = (_dot(q, kt) * w).astype(BF16)
        intra = _dot(p, vaug)
        inter = _dot(q, caug[r].astype(BF16))
        num = intra[:, :MLSTM_HEAD_DIM] + w_inter * inter[:, :MLSTM_HEAD_DIM]
        den = intra[:, MLSTM_HEAD_DIM:MLSTM_HEAD_DIM + 1] + w_inter * inter[:, MLSTM_HEAD_DIM:MLSTM_HEAD_DIM + 1]
        h_out = num / jnp.maximum(jnp.abs(den), jnp.exp(-(col(1) + m_col)))

        m_last = jnp.maximum(m_prev, row(3))
        kw = (kt.astype(F32) * jnp.exp(row(0) - m_last)).astype(BF16)
        caug[r] = jnp.exp(m_prev - m_last[:, :1]) * caug[r] + _dot(kw, vaug)
        return h_out, (row(4) + m_last)[:, :1]

    def body(c, ms):
        new = []
        for r in range(N_CHAINS):
            cc = c if r < MLSTM_HEADS else n_chunks - 1 - c
            h_out, m_new = chain(r, cc, ms[r])
            hd = r % MLSTM_HEADS
            dst = o_ref if r < MLSTM_HEADS else hb
            dst[pl.ds(pl.multiple_of(cc * CHUNK, CHUNK), CHUNK), hd * MLSTM_HEAD_DIM:(hd + 1) * MLSTM_HEAD_DIM] = h_out
            new.append(m_new)
        return tuple(new)

    ms = lax.fori_loop(0, n_chunks, body, m_init)
    o_ref[...] += hb[...]
    if emit_state:
        for r in range(N_CHAINS):
            c_out[0, r] = caug[r, :, :MLSTM_HEAD_DIM]
            n_out[0, r] = caug[r, :, MLSTM_HEAD_DIM:].T[:1, :]
            m_out[0, r] = jnp.broadcast_to(ms[r], (1, LANES))


def _mlstm(qm, kmt, vm, rowq, colq, n_batch, seq, state=None, emit_state=False):
    nc = seq // CHUNK
    mode = dict(pipeline_mode=pl.Buffered(1)) if seq * MLSTM_WIDTH * 4 > (2 << 20) else {}
    in_specs = [pl.BlockSpec((seq, MLSTM_WIDTH), lambda b: (b, 0), **mode),
                pl.BlockSpec((nc, MLSTM_WIDTH, CHUNK), lambda b: (b, 0, 0), **mode),
                pl.BlockSpec((seq, MLSTM_WIDTH), lambda b: (b, 0), **mode),
                pl.BlockSpec((nc, N_SCAN_ROWS, CHUNK), lambda b: (b, 0, 0), **mode),
                pl.BlockSpec((seq, LANES), lambda b: (b, 0), **mode)]
    args = [qm, kmt, vm, rowq, colq]
    if state is not None:
        in_specs += [pl.BlockSpec((1, N_CHAINS, MLSTM_HEAD_DIM, AUG), lambda b: (b, 0, 0, 0), **mode),
                     pl.BlockSpec(memory_space=pltpu.SMEM)]
        args += list(state)
    out_shape = [jax.ShapeDtypeStruct((n_batch * seq, MLSTM_WIDTH), F32)]
    out_specs = [pl.BlockSpec((seq, MLSTM_WIDTH), lambda b: (b, 0))]
    if emit_state:
        out_shape += [jax.ShapeDtypeStruct((n_batch, N_CHAINS, MLSTM_HEAD_DIM, MLSTM_HEAD_DIM), F32),
                      jax.ShapeDtypeStruct((n_batch, N_CHAINS, 1, MLSTM_HEAD_DIM), F32),
                      jax.ShapeDtypeStruct((n_batch, N_CHAINS, 1, LANES), F32)]
        out_specs += [pl.BlockSpec((1, N_CHAINS, MLSTM_HEAD_DIM, MLSTM_HEAD_DIM), lambda b: (b, 0, 0, 0)),
                      pl.BlockSpec((1, N_CHAINS, 1, MLSTM_HEAD_DIM), lambda b: (b, 0, 0, 0)),
                      pl.BlockSpec((1, N_CHAINS, 1, LANES), lambda b: (b, 0, 0, 0))]
    return pl.pallas_call(
        functools.partial(_mlstm_kernel, n_chunks=nc, has_state=state is not None, emit_state=emit_state),
        grid=(n_batch,), in_specs=in_specs, out_specs=out_specs, out_shape=out_shape,
        scratch_shapes=[pltpu.VMEM((N_CHAINS, MLSTM_HEAD_DIM, AUG), F32), pltpu.VMEM((seq, MLSTM_WIDTH), F32)],
        compiler_params=_cparams("parallel"), name="mlstm",
    )(*args)


def _attn_kernel(*refs, tq, has_cache):
    if has_cache:
        q_ref, k_ref, v_ref, ck_ref, cv_ref, o_ref = refs
    else:
        q_ref, k_ref, v_ref, o_ref = refs
    q = q_ref[...]
    qs = jnp.concatenate([q[:, g * ATT_HEAD_DIM:(g + 1) * ATT_HEAD_DIM] for g in range(ATT_GROUP)], axis=0)
    nt = (((1,), (1,)), ((), ()))
    s = lax.dot_general(qs, k_ref[...], nt, preferred_element_type=F32)
    m = jnp.max(s, axis=-1, keepdims=True)
    if has_cache:
        sc = lax.dot_general(qs, ck_ref[...].astype(BF16), nt, preferred_element_type=F32)
        m = jnp.maximum(m, jnp.max(sc, axis=-1, keepdims=True))
    p = jnp.exp(s - m)
    l = jnp.sum(p, axis=-1, keepdims=True)
    o = _dot(p.astype(BF16), v_ref[...])
    if has_cache:
        pc = jnp.exp(sc - m)
        l = l + jnp.sum(pc, axis=-1, keepdims=True)
        o = o + _dot(pc.astype(BF16), cv_ref[...].astype(BF16))
    o = o / l
    for g in range(ATT_GROUP):
        o_ref[:, g * ATT_HEAD_DIM:(g + 1) * ATT_HEAD_DIM] = o[g * tq:(g + 1) * tq].astype(BF16)


def _attention(q, k, v, n_batch, seq, cache=None, tq=128):
    nqb = seq // tq
    gw = ATT_GROUP * ATT_HEAD_DIM
    in_specs = [pl.BlockSpec((tq, gw), lambda b, h, i: (b * nqb + i, h)),
                pl.BlockSpec((seq, ATT_HEAD_DIM), lambda b, h, i: (b, h)),
                pl.BlockSpec((seq, ATT_HEAD_DIM), lambda b, h, i: (b, h))]
    args = [q, k, v]
    if cache is not None:
        past = cache[0].shape[0] // n_batch
        in_specs += [pl.BlockSpec((past, ATT_HEAD_DIM), lambda b, h, i: (b, h))] * 2
        args += list(cache)
    return pl.pallas_call(
        functools.partial(_attn_kernel, tq=tq, has_cache=cache is not None),
        grid=(n_batch, ATT_KV_HEADS, nqb), in_specs=in_specs,
        out_specs=pl.BlockSpec((tq, gw), lambda b, h, i: (b * nqb + i, h)),
        out_shape=jax.ShapeDtypeStruct((n_batch * seq, ATT_WIDTH), BF16),
        compiler_params=_cparams("parallel", "parallel", "parallel"), name="attention",
    )(*args)


def _mix_kernel(attn_ref, hm_ref, om_ref, x_ref, mod_ref, mg_ref, wo_ref, n2_ref, rwh_ref, rwl_ref, rb_ref,
                x1_ref, h2_ref, comb_ref):
    mod = mod_ref[0]
    gate1 = mod[:, 2 * D_MODEL:3 * D_MODEL]
    shift2, scale2 = mod[:, 3 * D_MODEL:4 * D_MODEL], mod[:, 4 * D_MODEL:5 * D_MODEL]
    hm = hm_ref[...]
    mg = mg_ref[...]
    parts = []
    for hd in range(MLSTM_HEADS):
        sl = slice(hd * MLSTM_HEAD_DIM, (hd + 1) * MLSTM_HEAD_DIM)
        parts.append(_rms(hm[:, sl], mg[:, sl]))
    om = om_ref[...]
    hmg = jnp.concatenate(parts, axis=1) * (1.0 / (1.0 + jnp.exp(-om)))
    y = _dot(attn_ref[...], wo_ref[:ATT_WIDTH, :]) + _dot(hmg.astype(BF16), wo_ref[ATT_WIDTH:, :])
    x1 = x_ref[...] + gate1 * y
    x1_ref[...] = x1
    h2 = _rms(x1, n2_ref[...]) * (1.0 + scale2) + shift2
    h2_hi, h2_lo = _split_hi_lo(h2)
    h2_ref[...] = h2_hi

    lg = _dot3(h2_hi, h2_lo, rwh_ref[...], rwl_ref[...]) + rb_ref[...]
    lane = lax.broadcasted_iota(jnp.int32, lg.shape, 1).astype(F32)
    neg = -jnp.inf
    first = lambda hit: jnp.min(jnp.where(hit, lane, float(LANES)), axis=-1, keepdims=True)
    gl = jnp.where(lane < N_GROUPS, lg, neg)
    gmax = jnp.max(gl, axis=-1, keepdims=True)
    grp = first(gl == gmax)
    p_grp = 1.0 / jnp.sum(jnp.exp(gl - gmax), axis=-1, keepdims=True)
    lo = ROUTER_LANE0 + grp * EXPERTS_PER_GROUP
    el = jnp.where((lane >= lo) & (lane < lo + EXPERTS_PER_GROUP), lg, neg)
    m1 = jnp.max(el, axis=-1, keepdims=True)
    i1 = first(el == m1)
    el2 = jnp.where(lane == i1, neg, el)
    m2 = jnp.max(el2, axis=-1, keepdims=True)
    i2 = first(el2 == m2)
    r = jnp.exp(m2 - m1)
    w1 = p_grp / (1.0 + r)
    comb_ref[...] = jnp.where(lane == i1, w1, 0.0) + jnp.where(lane == i2, w1 * r, 0.0)


def _mix_out(attn, hm, om, x2d, mod3, row_of_tile, mg, w_out, n2, rw_hi, rw_lo, rb, tm=256):
    t = x2d.shape[0]
    row = lambda i: (i, 0)
    return pl.pallas_call(
        _mix_kernel, grid=(t // tm,),
        in_specs=[pl.BlockSpec((tm, ATT_WIDTH), row), pl.BlockSpec((tm, MLSTM_WIDTH), row),
                  pl.BlockSpec((tm, MLSTM_WIDTH), row), pl.BlockSpec((tm, D_MODEL), row),
                  pl.BlockSpec((1, 1, N_MOD * D_MODEL), lambda i: (row_of_tile(i), 0, 0)),
                  _const_spec((1, MLSTM_WIDTH)), _const_spec((D_MODEL, D_MODEL)), _const_spec((1, D_MODEL)),
                  _const_spec((D_MODEL, LANES)), _const_spec((D_MODEL, LANES)), _const_spec((1, LANES))],
        out_specs=[pl.BlockSpec((tm, D_MODEL), row), pl.BlockSpec((tm, D_MODEL), row), pl.BlockSpec((tm, LANES), row)],
        out_shape=[jax.ShapeDtypeStruct((t, D_MODEL), F32), jax.ShapeDtypeStruct((t, D_MODEL), BF16),
                   jax.ShapeDtypeStruct((t, LANES), F32)],
        compiler_params=_cparams("parallel"), name="mix_out",
    )(attn, hm, om, x2d, mod3, mg, w_out, n2, rw_hi, rw_lo, rb)


def _moe_kernel(h_ref, comb_ref, x1_ref, mod_ref, wg_ref, wu_ref, wd_ref, o_ref, acc_ref):
    e = pl.program_id(1)

    @pl.when(e == 0)
    def _():
        acc_ref[...] = jnp.zeros(acc_ref.shape, F32)

    h = h_ref[...]
    g = _dot(h, wg_ref[0].astype(BF16))
    u = _dot(h, wu_ref[0].astype(BF16))
    a = (g / (1.0 + jnp.exp(-g))) * u
    comb = comb_ref[...]
    lane = lax.broadcasted_iota(jnp.int32, comb.shape, 1)
    w = jnp.sum(jnp.where(lane == e + ROUTER_LANE0, comb, 0.0), axis=-1, keepdims=True)
    acc_ref[...] += w * _dot(a.astype(BF16), wd_ref[0].astype(BF16))

    @pl.when(e == N_EXPERTS - 1)
    def _():
        gate2 = mod_ref[0][:, 5 * D_MODEL:]
        o_ref[...] = x1_ref[...] + gate2 * acc_ref[...]


def _moe(h2, comb, x1, mod3, row_of_tile, wg, wu, wd, tm=512):
    t = h2.shape[0]
    row = lambda i, e: (i, 0)
    return pl.pallas_call(
        _moe_kernel, grid=(t // tm, N_EXPERTS),
        in_specs=[pl.BlockSpec((tm, D_MODEL), row), pl.BlockSpec((tm, LANES), row), pl.BlockSpec((tm, D_MODEL), row),
                  pl.BlockSpec((1, 1, N_MOD * D_MODEL), lambda i, e: (row_of_tile(i), 0, 0)),
                  pl.BlockSpec((1, D_MODEL, D_EXPERT), lambda i, e: (e, 0, 0)),
                  pl.BlockSpec((1, D_MODEL, D_EXPERT), lambda i, e: (e, 0, 0)),
                  pl.BlockSpec((1, D_EXPERT, D_MODEL), lambda i, e: (e, 0, 0))],
        out_specs=pl.BlockSpec((tm, D_MODEL), row),
        out_shape=jax.ShapeDtypeStruct((t, D_MODEL), F32),
        scratch_shapes=[pltpu.VMEM((tm, D_MODEL), F32)],
        compiler_params=_cparams("parallel", "arbitrary"), name="experts",
    )(h2, comb, x1, mod3, wg, wu, wd)


def _rope_tables(seq):
    pos = np.arange(seq)
    n_freq = ATT_HEAD_DIM // 4
    inv = ROPE_THETA ** (-np.arange(n_freq, dtype=np.float32) / n_freq)
    ang = np.concatenate([(pos // GRID_W).astype(np.float32)[:, None] * inv,
                          (pos % GRID_W).astype(np.float32)[:, None] * inv], axis=-1).astype(np.float32)
    ang = jnp.asarray(ang)
    cos, sin = jnp.cos(ang), jnp.sin(ang)
    cos_t = jnp.repeat(cos, 2, axis=1)
    sin_t = jnp.stack([-sin, sin], axis=-1).reshape(seq, ATT_HEAD_DIM)
    return cos_t, sin_t


def _pad_lanes(a):
    return jnp.pad(a, ((0, 0), (0, LANES - a.shape[1])))


def _layer(x2d, n_batch, seq, mod3, row_of_tile, lw, *, rope_tabs, cache, state, emit):
    (n1, n2, w_main, wg_hi, wg_lo, bg, qg, kg, mg, w_out, rw_hi, rw_lo, rb, ewg, ewu, ewd) = lw
    cos_t, sin_t = rope_tabs
    outs = _project(x2d, mod3, row_of_tile(256), n1, w_main, wg_hi, wg_lo, bg, qg, kg, cos_t, sin_t,
                    rope=cache is not None, emit_kv=emit)
    q, k, v, qm, kmt, vm, om, g = outs[:8]
    rowq, colq = _gate_scans(g)
    attn = _attention(q, k, v, n_batch, seq, cache=cache, tq=128 if seq > 256 else 256)
    ml = _mlstm(qm, kmt, vm, rowq, colq, n_batch, seq, state=state, emit_state=emit)
    x1, h2, comb = _mix_out(attn, ml[0], om, x2d, mod3, row_of_tile(256), mg, w_out, n2, rw_hi, rw_lo, rb)
    y = _moe(h2, comb, x1, mod3, row_of_tile(512), ewg, ewu, ewd)
    return y, outs[8:], ml[1:]


def kernel(x_prompt, x_sample, cache_k, cache_v, state_C, state_n, state_m, c, c_ctx, mod_w, mod_b, norm1_g, norm2_g,
           w_in, b_gates, q_norm_g, k_norm_g, mlstm_norm_g, w_out, router_group_w, router_group_b, router_expert_w,
           router_expert_b, expert_w_gate, expert_w_up, expert_w_down):
    assert mod_w.shape[0] == 1, "single-layer stack"
    n_ctx, s_ctx, _ = x_prompt.shape
    n_lat, s_lat, _ = x_sample.shape
    ctx_row = n_lat

    cond = jnp.concatenate([c, c_ctx[None], jnp.zeros((8 - n_lat - 1, D_MODEL), F32)], axis=0)
    mod3 = _modulation(cond, mod_w[0], mod_b[0][None]).reshape(8, 1, N_MOD * D_MODEL)

    perm = np.concatenate([np.arange(0, 4), np.arange(8, 12), np.arange(4, 8), np.arange(12, 16)])
    wg_hi, wg_lo = _split_hi_lo(_pad_lanes(w_in[0][:, MAIN_WIDTH:][:, perm]))
    rw = jnp.concatenate([router_group_w[0], jnp.moveaxis(router_expert_w[0], 0, 1).reshape(D_MODEL, N_EXPERTS)], axis=1)
    rw_hi, rw_lo = _split_hi_lo(_pad_lanes(rw))
    rb = _pad_lanes(jnp.concatenate([router_group_b[0], router_expert_b[0].reshape(-1)])[None])
    lw = (norm1_g, norm2_g, w_in[0][:, :MAIN_WIDTH].astype(BF16), wg_hi, wg_lo, _pad_lanes(b_gates[0][perm][None]),
          q_norm_g, k_norm_g, mlstm_norm_g, w_out[0].astype(BF16), rw_hi, rw_lo, rb,
          expert_w_gate[0], expert_w_up[0], expert_w_down[0])
    rope_tabs = _rope_tables(s_lat)

    yp, (ka, va), (s_c, s_n, s_m) = _layer(
        x_prompt.reshape(n_ctx * s_ctx, D_MODEL), n_ctx, s_ctx, mod3, lambda tm: (lambda i: ctx_row), lw,
        rope_tabs=rope_tabs, cache=None, state=None, emit=True)

    caug0 = jnp.concatenate([state_C[:, 0], state_n[:, 0][..., None],
                             jnp.zeros(state_n[:, 0].shape + (LANES - 1,), F32)], axis=-1)
    caug0 = caug0.reshape(n_lat, N_CHAINS, MLSTM_HEAD_DIM, AUG)
    past = cache_k.shape[2]
    cache = (cache_k[:, 0].reshape(n_lat * past, KV_WIDTH), cache_v[:, 0].reshape(n_lat * past, KV_WIDTH))
    ys, _, _ = _layer(
        x_sample.reshape(n_lat * s_lat, D_MODEL), n_lat, s_lat, mod3, lambda tm: (lambda i: i // (s_lat // tm)), lw,
        rope_tabs=rope_tabs, cache=cache, state=(caug0, state_m[:, 0].reshape(-1)), emit=False)

    kv_shape = (n_ctx, 1, s_ctx, ATT_KV_HEADS, ATT_HEAD_DIM)
    return (yp.reshape(x_prompt.shape), ys.reshape(x_sample.shape), ka.reshape(kv_shape), va.reshape(kv_shape),
            s_c.reshape(n_ctx, 1, 2, MLSTM_HEADS, MLSTM_HEAD_DIM, MLSTM_HEAD_DIM),
            s_n.reshape(n_ctx, 1, 2, MLSTM_HEADS, MLSTM_HEAD_DIM), s_m[..., 0, 0].reshape(n_ctx, 1, 2, MLSTM_HEADS))
```

```python
import functools

import numpy as np
import jax
import jax.numpy as jnp
from jax import lax
from jax.experimental import pallas as pl
from jax.experimental.pallas import tpu as pltpu

F32 = jnp.float32
BF16 = jnp.bfloat16

D_MODEL = 2048
GRID_W = 64
ATT_HEADS = 8
ATT_KV_HEADS = 2
ATT_HEAD_DIM = 128
ATT_GROUP = ATT_HEADS // ATT_KV_HEADS
ATT_WIDTH = ATT_HEADS * ATT_HEAD_DIM
KV_WIDTH = ATT_KV_HEADS * ATT_HEAD_DIM
ROPE_THETA = 10000.0
MLSTM_HEADS = 4
MLSTM_HEAD_DIM = 256
MLSTM_WIDTH = MLSTM_HEADS * MLSTM_HEAD_DIM
CHUNK = 128
N_GATES = 4 * MLSTM_HEADS
N_CHAINS = 2 * MLSTM_HEADS
MAIN_WIDTH = ATT_WIDTH + 2 * KV_WIDTH + 4 * MLSTM_WIDTH
N_GROUPS = 4
EXPERTS_PER_GROUP = 8
N_EXPERTS = N_GROUPS * EXPERTS_PER_GROUP
D_EXPERT = 512
N_MOD = 6
TM_MOE = 256
EPS = 1e-6

LANES = 128
AUG = MLSTM_HEAD_DIM + LANES
N_SCAN_ROWS = 5 * N_CHAINS
ROUTER_LANE0 = N_GROUPS
VMEM_LIMIT = 56 * 1024 * 1024


def _cparams(*sem):
    return pltpu.CompilerParams(dimension_semantics=sem, vmem_limit_bytes=VMEM_LIMIT)


def _const_spec(shape):
    nd = len(shape)
    return pl.BlockSpec(shape, lambda *_: (0,) * nd, pipeline_mode=pl.Buffered(1))


def _split_hi_lo(x):
    hi = x.astype(BF16)
    lo = (x - hi.astype(F32)).astype(BF16)
    return hi, lo


def _dot(a, b):
    return jnp.dot(a, b, preferred_element_type=F32)


def _dot3(a_hi, a_lo, b_hi, b_lo):
    return _dot(a_hi, b_hi) + _dot(a_lo, b_hi) + _dot(a_hi, b_lo)


def _rms(x, g):
    return x * lax.rsqrt(jnp.mean(x * x, axis=-1, keepdims=True) + EPS) * g


def _mod_kernel(c_ref, w_ref, b_ref, o_ref):
    c = c_ref[...]
    s = c / (1.0 + jnp.exp(-c))
    s_hi = s.astype(BF16).astype(F32)
    lhs = jnp.concatenate([s_hi, s - s_hi], axis=0).astype(BF16)
    w_hi, w_lo = _split_hi_lo(w_ref[...])
    r = _dot(lhs, w_hi)
    r2 = _dot(lhs, w_lo)
    o_ref[...] = r[:8] + r[8:] + r2[:8] + b_ref[...]


def _modulation(cond, mod_w, mod_b):
    n = mod_w.shape[1]
    tn = 512
    return pl.pallas_call(
        _mod_kernel,
        grid=(n // tn,),
        in_specs=[pl.BlockSpec((8, D_MODEL), lambda j: (0, 0)),
                  pl.BlockSpec((D_MODEL, tn), lambda j: (0, j)),
                  pl.BlockSpec((1, tn), lambda j: (0, j))],
        out_specs=pl.BlockSpec((8, tn), lambda j: (0, j)),
        out_shape=jax.ShapeDtypeStruct((8, n), F32),
        compiler_params=_cparams("parallel"),
        name="modulation",
    )(cond, mod_w, mod_b)


def _pair_swap(x):
    lane = lax.broadcasted_iota(jnp.int32, x.shape, 1)
    return jnp.where((lane & 1) == 0, pltpu.roll(x, LANES - 1, 1), pltpu.roll(x, 1, 1))


def _proj_kernel(x_ref, mod_ref, n1_ref, w_ref, wgh_ref, wgl_ref, bg_ref, qg_ref, kg_ref, cos_ref, sin_ref,
                 q_ref, k_ref, v_ref, qm_ref, kmt_ref, vm_ref, om_ref, g_ref, *kv_refs, rope, tm):
    mod = mod_ref[0]
    shift, scale = mod[:, :D_MODEL], mod[:, D_MODEL:2 * D_MODEL]
    h = _rms(x_ref[...], n1_ref[...]) * (1.0 + scale) + shift
    h_hi, h_lo = _split_hi_lo(h)

    def rot(seg):
        return seg * cos_ref[...] + _pair_swap(seg) * sin_ref[...] if rope else seg

    qa = _dot(h_hi, w_ref[:, :ATT_WIDTH])
    for hh in range(ATT_HEADS):
        sl = slice(hh * ATT_HEAD_DIM, (hh + 1) * ATT_HEAD_DIM)
        seg = rot(_rms(qa[:, sl], qg_ref[...]))
        q_ref[:, sl] = (seg * ATT_HEAD_DIM ** -0.5).astype(BF16)

    kv = _dot(h_hi, w_ref[:, ATT_WIDTH:ATT_WIDTH + 2 * KV_WIDTH])
    for hh in range(ATT_KV_HEADS):
        sl = slice(hh * ATT_HEAD_DIM, (hh + 1) * ATT_HEAD_DIM)
        seg = _rms(kv[:, sl], kg_ref[...])
        if kv_refs:
            kv_refs[0][:, sl] = seg
        k_ref[:, sl] = rot(seg).astype(BF16)
    va = kv[:, KV_WIDTH:]
    if kv_refs:
        kv_refs[1][...] = va
    v_ref[...] = va.astype(BF16)

    c0 = ATT_WIDTH + 2 * KV_WIDTH
    qm_ref[...] = (_dot(h_hi, w_ref[:, c0:c0 + MLSTM_WIDTH]) * MLSTM_HEAD_DIM ** -0.5).astype(BF16)
    km = _dot(h_hi, w_ref[:, c0 + MLSTM_WIDTH:c0 + 2 * MLSTM_WIDTH])
    for cc in range(tm // CHUNK):
        kmt_ref[cc] = km[cc * CHUNK:(cc + 1) * CHUNK, :].T.astype(BF16)
    vm_ref[...] = _dot(h_hi, w_ref[:, c0 + 2 * MLSTM_WIDTH:c0 + 3 * MLSTM_WIDTH]).astype(BF16)
    om_ref[...] = _dot(h_hi, w_ref[:, c0 + 3 * MLSTM_WIDTH:c0 + 4 * MLSTM_WIDTH])
    g_ref[...] = _dot3(h_hi, h_lo, wgh_ref[...], wgl_ref[...]) + bg_ref[...]


def _project(x2d, mod3, row_of_tile, n1, w_main, wg_hi, wg_lo, bg, qg, kg, cos_t, sin_t, *, rope, emit_kv, tm=256):
    t = x2d.shape[0]
    n_pos = cos_t.shape[0] // tm
    row = lambda i: (i, 0)
    in_specs = [pl.BlockSpec((tm, D_MODEL), row),
                pl.BlockSpec((1, 1, N_MOD * D_MODEL), lambda i: (row_of_tile(i), 0, 0)),
                _const_spec((1, D_MODEL)),
                _const_spec((D_MODEL, MAIN_WIDTH)),
                _const_spec((D_MODEL, LANES)), _const_spec((D_MODEL, LANES)), _const_spec((1, LANES)),
                _const_spec((1, ATT_HEAD_DIM)), _const_spec((1, ATT_HEAD_DIM)),
                pl.BlockSpec((tm, ATT_HEAD_DIM), lambda i: (i % n_pos, 0)),
                pl.BlockSpec((tm, ATT_HEAD_DIM), lambda i: (i % n_pos, 0))]
    out_shape = [jax.ShapeDtypeStruct((t, ATT_WIDTH), BF16), jax.ShapeDtypeStruct((t, KV_WIDTH), BF16),
                 jax.ShapeDtypeStruct((t, KV_WIDTH), BF16), jax.ShapeDtypeStruct((t, MLSTM_WIDTH), BF16),
                 jax.ShapeDtypeStruct((t // CHUNK, MLSTM_WIDTH, CHUNK), BF16),
                 jax.ShapeDtypeStruct((t, MLSTM_WIDTH), BF16), jax.ShapeDtypeStruct((t, MLSTM_WIDTH), F32),
                 jax.ShapeDtypeStruct((t, LANES), F32)]
    out_specs = [pl.BlockSpec((tm, ATT_WIDTH), row), pl.BlockSpec((tm, KV_WIDTH), row),
                 pl.BlockSpec((tm, KV_WIDTH), row), pl.BlockSpec((tm, MLSTM_WIDTH), row),
                 pl.BlockSpec((tm // CHUNK, MLSTM_WIDTH, CHUNK), lambda i: (i, 0, 0)),
                 pl.BlockSpec((tm, MLSTM_WIDTH), row), pl.BlockSpec((tm, MLSTM_WIDTH), row),
                 pl.BlockSpec((tm, LANES), row)]
    if emit_kv:
        out_shape += [jax.ShapeDtypeStruct((t, KV_WIDTH), F32)] * 2
        out_specs += [pl.BlockSpec((tm, KV_WIDTH), row)] * 2
    return pl.pallas_call(
        functools.partial(_proj_kernel, rope=rope, tm=tm),
        grid=(t // tm,), in_specs=in_specs, out_specs=out_specs, out_shape=out_shape,
        compiler_params=_cparams("parallel"), name="in_proj",
    )(x2d, mod3, n1, w_main, wg_hi, wg_lo, bg, qg, kg, cos_t, sin_t)


def _lane_scan(x, op, fill, is_fwd, lane):
    s = 1
    while s < CHUNK:
        from_left = jnp.where(lane >= s, pltpu.roll(x, s, 1), fill)
        from_right = jnp.where(lane < CHUNK - s, pltpu.roll(x, CHUNK - s, 1), fill)
        x = op(x, jnp.where(is_fwd, from_left, from_right))
        s *= 2
    return x


def _scan_kernel(g_ref, row_ref, col_ref, *, n_chunks):
    lane = lax.broadcasted_iota(jnp.int32, (N_CHAINS, CHUNK), 1)
    is_fwd = lax.broadcasted_iota(jnp.int32, (N_CHAINS, CHUNK), 0) < MLSTM_HEADS
    for cc in range(n_chunks):
        gt = g_ref[cc * CHUNK:(cc + 1) * CHUNK, :].T
        li, f = gt[:N_CHAINS], gt[N_CHAINS:2 * N_CHAINS]
        lf = jnp.minimum(f, 0.0) - jnp.log(1.0 + jnp.exp(-jnp.abs(f)))
        b = _lane_scan(lf, jnp.add, 0.0, is_fwd, lane)
        a = li - b
        run_max = _lane_scan(a, jnp.maximum, -jnp.inf, is_fwd, lane)
        all_max = jnp.broadcast_to(jnp.max(a, axis=1, keepdims=True), a.shape)
        last = jnp.where(is_fwd, CHUNK - 1, 0)
        total = jnp.broadcast_to(jnp.sum(jnp.where(lane == last, b, 0.0), axis=1, keepdims=True), a.shape)
        rows = jnp.concatenate([a, b, run_max, all_max, total], axis=0)
        row_ref[cc] = rows
        padded = jnp.concatenate([rows, jnp.zeros((LANES - N_SCAN_ROWS, CHUNK), F32)], axis=0)
        col_ref[cc * CHUNK:(cc + 1) * CHUNK, :] = padded.T


def _gate_scans(g, tb=1024):
    t = g.shape[0]
    tb = min(tb, t)
    return pl.pallas_call(
        functools.partial(_scan_kernel, n_chunks=tb // CHUNK),
        grid=(t // tb,),
        in_specs=[pl.BlockSpec((tb, LANES), lambda i: (i, 0))],
        out_specs=[pl.BlockSpec((tb // CHUNK, N_SCAN_ROWS, CHUNK), lambda i: (i, 0, 0)),
                   pl.BlockSpec((tb, LANES), lambda i: (i, 0))],
        out_shape=[jax.ShapeDtypeStruct((t // CHUNK, N_SCAN_ROWS, CHUNK), F32),
                   jax.ShapeDtypeStruct((t, LANES), F32)],
        compiler_params=_cparams("parallel"), name="gate_scans",
    )(g)


def _mlstm_kernel(*refs, n_chunks, has_state, emit_state):
    it = iter(refs)
    q_ref, kt_ref, v_ref, row_ref, col_ref = [next(it) for _ in range(5)]
    c0_ref, m0_ref = (next(it), next(it)) if has_state else (None, None)
    o_ref = next(it)
    c_out, n_out, m_out = (next(it), next(it), next(it)) if emit_state else (None, None, None)
    caug, hb = next(it), next(it)

    b = pl.program_id(0)
    if has_state:
        caug[...] = c0_ref[0]
        m_init = tuple(jnp.full((1, 1), m0_ref[b * N_CHAINS + r], F32) for r in range(N_CHAINS))
    else:
        caug[...] = jnp.zeros(caug.shape, F32)
        m_init = tuple(jnp.zeros((1, 1), F32) for _ in range(N_CHAINS))

    sub = lax.broadcasted_iota(jnp.int32, (CHUNK, CHUNK), 0)
    lane = lax.broadcasted_iota(jnp.int32, (CHUNK, CHUNK), 1)
    ones_col = (lax.broadcasted_iota(jnp.int32, (CHUNK, LANES), 1) == 0).astype(BF16)

    def chain(r, cc, m_prev):
        d, hd = divmod(r, MLSTM_HEADS)
        t0 = pl.multiple_of(cc * CHUNK, CHUNK)
        hs = slice(hd * MLSTM_HEAD_DIM, (hd + 1) * MLSTM_HEAD_DIM)
        rows = row_ref[cc]
        cols = col_ref[pl.ds(t0, CHUNK), :]
        row = lambda k: rows[k * N_CHAINS + r:k * N_CHAINS + r + 1, :]
        col = lambda k: cols[:, k * N_CHAINS + r:k * N_CHAINS + r + 1]
        q = q_ref[pl.ds(t0, CHUNK), hs]
        kt = kt_ref[cc, hs, :]
        vaug = jnp.concatenate([v_ref[pl.ds(t0, CHUNK), hs], ones_col], axis=1)

        m_col = jnp.maximum(m_prev, col(2))
        keep = (lane <= sub) if d == 0 else (lane >= sub)
        w = jnp.where(keep, jnp.exp(row(0) - m_col), 0.0)
        w_inter = jnp.exp(m_prev - m_col)
        p = (_dot(q, kt) * w).astype(BF16)
        intra = _dot(p, vaug)
        inter = _dot(q, caug[r].astype(BF16))
        num = intra[:, :MLSTM_HEAD_DIM] + w_inter * inter[:, :MLSTM_HEAD_DIM]
        den = intra[:, MLSTM_HEAD_DIM:MLSTM_HEAD_DIM + 1] + w_inter * inter[:, MLSTM_HEAD_DIM:MLSTM_HEAD_DIM + 1]
        h_out = num / jnp.maximum(jnp.abs(den), jnp.exp(-(col(1) + m_col)))

        m_last = jnp.maximum(m_prev, row(3))
        kw = (kt.astype(F32) * jnp.exp(row(0) - m_last)).astype(BF16)
        caug[r] = jnp.exp(m_prev - m_last[:, :1]) * caug[r] + _dot(kw, vaug)
        return h_out, (row(4) + m_last)[:, :1]

    def body(c, ms):
        new = []
        for r in range(N_CHAINS):
            cc = c if r < MLSTM_HEADS else n_chunks - 1 - c
            h_out, m_new = chain(r, cc, ms[r])
            hd = r % MLSTM_HEADS
            dst = o_ref if r < MLSTM_HEADS else hb
            dst[pl.ds(pl.multiple_of(cc * CHUNK, CHUNK), CHUNK), hd * MLSTM_HEAD_DIM:(hd + 1) * MLSTM_HEAD_DIM] = h_out
            new.append(m_new)
        return tuple(new)

    ms = lax.fori_loop(0, n_chunks, body, m_init)
    o_ref[...] += hb[...]
    if emit_state:
        for r in range(N_CHAINS):
            c_out[0, r] = caug[r, :, :MLSTM_HEAD_DIM]
            n_out[0, r] = caug[r, :, MLSTM_HEAD_DIM:].T[:1, :]
            m_out[0, r] = jnp.broadcast_to(ms[r], (1, LANES))


def _mlstm(qm, kmt, vm, rowq, colq, n_batch, seq, state=None, emit_state=False):
    nc = seq // CHUNK
    mode = dict(pipeline_mode=pl.Buffered(1)) if seq * MLSTM_WIDTH * 4 > (2 << 20) else {}
    in_specs = [pl.BlockSpec((seq, MLSTM_WIDTH), lambda b: (b, 0), **mode),
                pl.BlockSpec((nc, MLSTM_WIDTH, CHUNK), lambda b: (b, 0, 0), **mode),
                pl.BlockSpec((seq, MLSTM_WIDTH), lambda b: (b, 0), **mode),
                pl.BlockSpec((nc, N_SCAN_ROWS, CHUNK), lambda b: (b, 0, 0), **mode),
                pl.BlockSpec((seq, LANES), lambda b: (b, 0), **mode)]
    args = [qm, kmt, vm, rowq, colq]
    if state is not None:
        in_specs += [pl.BlockSpec((1, N_CHAINS, MLSTM_HEAD_DIM, AUG), lambda b: (b, 0, 0, 0), **mode),
                     pl.BlockSpec(memory_space=pltpu.SMEM)]
        args += list(state)
    out_shape = [jax.ShapeDtypeStruct((n_batch * seq, MLSTM_WIDTH), F32)]
    out_specs = [pl.BlockSpec((seq, MLSTM_WIDTH), lambda b: (b, 0))]
    if emit_state:
        out_shape += [jax.ShapeDtypeStruct((n_batch, N_CHAINS, MLSTM_HEAD_DIM, MLSTM_HEAD_DIM), F32),
                      jax.ShapeDtypeStruct((n_batch, N_CHAINS, 1, MLSTM_HEAD_DIM), F32),
                      jax.ShapeDtypeStruct((n_batch, N_CHAINS, 1, LANES), F32)]
        out_specs += [pl.BlockSpec((1, N_CHAINS, MLSTM_HEAD_DIM, MLSTM_HEAD_DIM), lambda b: (b, 0, 0, 0)),
                      pl.BlockSpec((1, N_CHAINS, 1, MLSTM_HEAD_DIM), lambda b: (b, 0, 0, 0)),
                      pl.BlockSpec((1, N_CHAINS, 1, LANES), lambda b: (b, 0, 0, 0))]
    return pl.pallas_call(
        functools.partial(_mlstm_kernel, n_chunks=nc, has_state=state is not None, emit_state=emit_state),
        grid=(n_batch,), in_specs=in_specs, out_specs=out_specs, out_shape=out_shape,
        scratch_shapes=[pltpu.VMEM((N_CHAINS, MLSTM_HEAD_DIM, AUG), F32), pltpu.VMEM((seq, MLSTM_WIDTH), F32)],
        compiler_params=_cparams("parallel"), name="mlstm",
    )(*args)


def _attn_kernel(*refs, tq, has_cache):
    if has_cache:
        q_ref, k_ref, v_ref, ck_ref, cv_ref, o_ref = refs
    else:
        q_ref, k_ref, v_ref, o_ref = refs
    q = q_ref[...]
    qs = jnp.concatenate([q[:, g * ATT_HEAD_DIM:(g + 1) * ATT_HEAD_DIM] for g in range(ATT_GROUP)], axis=0)
    nt = (((1,), (1,)), ((), ()))
    s = lax.dot_general(qs, k_ref[...], nt, preferred_element_type=F32)
    m = jnp.max(s, axis=-1, keepdims=True)
    if has_cache:
        sc = lax.dot_general(qs, ck_ref[...].astype(BF16), nt, preferred_element_type=F32)
        m = jnp.maximum(m, jnp.max(sc, axis=-1, keepdims=True))
    p = jnp.exp(s - m)
    l = jnp.sum(p, axis=-1, keepdims=True)
    o = _dot(p.astype(BF16), v_ref[...])
    if has_cache:
        pc = jnp.exp(sc - m)
        l = l + jnp.sum(pc, axis=-1, keepdims=True)
        o = o + _dot(pc.astype(BF16), cv_ref[...].astype(BF16))
    o = o / l
    for g in range(ATT_GROUP):
        o_ref[:, g * ATT_HEAD_DIM:(g + 1) * ATT_HEAD_DIM] = o[g * tq:(g + 1) * tq].astype(BF16)


def _attention(q, k, v, n_batch, seq, cache=None, tq=128):
    nqb = seq // tq
    gw = ATT_GROUP * ATT_HEAD_DIM
    in_specs = [pl.BlockSpec((tq, gw), lambda b, h, i: (b * nqb + i, h)),
                pl.BlockSpec((seq, ATT_HEAD_DIM), lambda b, h, i: (b, h)),
                pl.BlockSpec((seq, ATT_HEAD_DIM), lambda b, h, i: (b, h))]
    args = [q, k, v]
    if cache is not None:
        past = cache[0].shape[0] // n_batch
        in_specs += [pl.BlockSpec((past, ATT_HEAD_DIM), lambda b, h, i: (b, h))] * 2
        args += list(cache)
    return pl.pallas_call(
        functools.partial(_attn_kernel, tq=tq, has_cache=cache is not None),
        grid=(n_batch, ATT_KV_HEADS, nqb), in_specs=in_specs,
        out_specs=pl.BlockSpec((tq, gw), lambda b, h, i: (b * nqb + i, h)),
        out_shape=jax.ShapeDtypeStruct((n_batch * seq, ATT_WIDTH), BF16),
        compiler_params=_cparams("parallel", "parallel", "parallel"), name="attention",
    )(*args)


def _mix_kernel(attn_ref, hm_ref, om_ref, x_ref, mod_ref, mg_ref, wo_ref, n2_ref, rwh_ref, rwl_ref, rb_ref,
                x1_ref, h2_ref, comb_ref):
    mod = mod_ref[0]
    gate1 = mod[:, 2 * D_MODEL:3 * D_MODEL]
    shift2, scale2 = mod[:, 3 * D_MODEL:4 * D_MODEL], mod[:, 4 * D_MODEL:5 * D_MODEL]
    hm = hm_ref[...]
    mg = mg_ref[...]
    parts = []
    for hd in range(MLSTM_HEADS):
        sl = slice(hd * MLSTM_HEAD_DIM, (hd + 1) * MLSTM_HEAD_DIM)
        parts.append(_rms(hm[:, sl], mg[:, sl]))
    om = om_ref[...]
    hmg = jnp.concatenate(parts, axis=1) * (1.0 / (1.0 + jnp.exp(-om)))
    y = _dot(attn_ref[...], wo_ref[:ATT_WIDTH, :]) + _dot(hmg.astype(BF16), wo_ref[ATT_WIDTH:, :])
    x1 = x_ref[...] + gate1 * y
    x1_ref[...] = x1
    h2 = _rms(x1, n2_ref[...]) * (1.0 + scale2) + shift2
    h2_hi, h2_lo = _split_hi_lo(h2)
    h2_ref[...] = h2

    lg = _dot3(h2_hi, h2_lo, rwh_ref[...], rwl_ref[...]) + rb_ref[...]
    lane = lax.broadcasted_iota(jnp.int32, lg.shape, 1).astype(F32)
    neg = -jnp.inf
    first = lambda hit: jnp.min(jnp.where(hit, lane, float(LANES)), axis=-1, keepdims=True)
    gl = jnp.where(lane < N_GROUPS, lg, neg)
    gmax = jnp.max(gl, axis=-1, keepdims=True)
    grp = first(gl == gmax)
    p_grp = 1.0 / jnp.sum(jnp.exp(gl - gmax), axis=-1, keepdims=True)
    lo = ROUTER_LANE0 + grp * EXPERTS_PER_GROUP
    el = jnp.where((lane >= lo) & (lane < lo + EXPERTS_PER_GROUP), lg, neg)
    m1 = jnp.max(el, axis=-1, keepdims=True)
    i1 = first(el == m1)
    el2 = jnp.where(lane == i1, neg, el)
    m2 = jnp.max(el2, axis=-1, keepdims=True)
    i2 = first(el2 == m2)
    r = jnp.exp(m2 - m1)
    w1 = p_grp / (1.0 + r)
    w2 = w1 * r
    comb_ref[...] = jnp.where(lane == 0.0, i1 - ROUTER_LANE0, jnp.where(lane == 1.0, i2 - ROUTER_LANE0,
                              jnp.where(lane == 2.0, w1, jnp.where(lane == 3.0, w2, 0.0))))


def _mix_out(attn, hm, om, x2d, mod3, row_of_tile, mg, w_out, n2, rw_hi, rw_lo, rb, tm=256):
    t = x2d.shape[0]
    row = lambda i: (i, 0)
    return pl.pallas_call(
        _mix_kernel, grid=(t // tm,),
        in_specs=[pl.BlockSpec((tm, ATT_WIDTH), row), pl.BlockSpec((tm, MLSTM_WIDTH), row),
                  pl.BlockSpec((tm, MLSTM_WIDTH), row), pl.BlockSpec((tm, D_MODEL), row),
                  pl.BlockSpec((1, 1, N_MOD * D_MODEL), lambda i: (row_of_tile(i), 0, 0)),
                  _const_spec((1, MLSTM_WIDTH)), _const_spec((D_MODEL, D_MODEL)), _const_spec((1, D_MODEL)),
                  _const_spec((D_MODEL, LANES)), _const_spec((D_MODEL, LANES)), _const_spec((1, LANES))],
        out_specs=[pl.BlockSpec((tm, D_MODEL), row), pl.BlockSpec((tm, D_MODEL), row), pl.BlockSpec((tm, LANES), row)],
        out_shape=[jax.ShapeDtypeStruct((t, D_MODEL), F32), jax.ShapeDtypeStruct((t, D_MODEL), F32),
                   jax.ShapeDtypeStruct((t, LANES), F32)],
        compiler_params=_cparams("parallel"), name="mix_out",
    )(attn, hm, om, x2d, mod3, mg, w_out, n2, rw_hi, rw_lo, rb)


def _rank_kernel(route_ref, rank_ref, cnt_ref, run_ref, tri_ref):
    tr = route_ref.shape[0]

    @pl.when(pl.program_id(0) == 0)
    def _():
        run_ref[...] = jnp.zeros(run_ref.shape, F32)
        tri_ref[...] = (lax.broadcasted_iota(jnp.int32, (tr, tr), 1)
                        < lax.broadcasted_iota(jnp.int32, (tr, tr), 0)).astype(BF16)

    route = route_ref[...]
    lane = lax.broadcasted_iota(jnp.int32, route.shape, 1).astype(F32)
    hit1, hit2 = lane == route[:, 0:1], lane == route[:, 1:2]
    onehot = jnp.where(hit1, 1.0, jnp.where(hit2, 1.0, 0.0))
    before = _dot(tri_ref[...], onehot.astype(BF16)) + run_ref[0:1, :]
    r1 = jnp.sum(jnp.where(hit1, before, 0.0), axis=-1, keepdims=True)
    r2 = jnp.sum(jnp.where(hit2, before, 0.0), axis=-1, keepdims=True)
    rank_ref[...] = jnp.where(lane == 0.0, r1, jnp.where(lane == 1.0, r2, 0.0))
    run_ref[...] = run_ref[...] + jnp.sum(onehot, axis=0, keepdims=True)
    cnt_ref[...] = run_ref[...]


def _ranks(route, tr=512):
    t = route.shape[0]
    return pl.pallas_call(
        _rank_kernel, grid=(t // tr,),
        in_specs=[pl.BlockSpec((tr, LANES), lambda i: (i, 0))],
        out_specs=[pl.BlockSpec((tr, LANES), lambda i: (i, 0)), pl.BlockSpec((8, LANES), lambda i: (0, 0))],
        out_shape=[jax.ShapeDtypeStruct((t, LANES), F32), jax.ShapeDtypeStruct((8, LANES), F32)],
        scratch_shapes=[pltpu.VMEM((8, LANES), F32), pltpu.VMEM((tr, tr), BF16)],
        compiler_params=_cparams("arbitrary"), name="expert_ranks",
    )(route)


def _row_copies(pos_ref, base, n_rows, make_copy):
    def body(r, carry):
        for k in range(2):
            make_copy(k, r, pos_ref[(base + r) * 2 + k]).start()
        return carry
    lax.fori_loop(0, n_rows, body, 0, unroll=8)


def _dispatch_kernel(*refs, td, first):
    if first:
        pos_ref, tail_ref, h_ref, xs_ref, zero_ref, sem = refs
    else:
        pos_ref, tail_ref, h_ref, _, xs_ref, zero_ref, sem = refs
    i = pl.program_id(0)

    if first:
        @pl.when(i == 0)
        def _():
            zero_ref[...] = jnp.zeros(zero_ref.shape, F32)
            tail_copy = lambda e: pltpu.make_async_copy(
                zero_ref, xs_ref.at[pl.ds(pl.multiple_of(tail_ref[e], TM_MOE), TM_MOE)], sem)
            for e in range(N_EXPERTS):
                pl.when(tail_ref[e] >= 0)(lambda e=e: tail_copy(e).start())
            for e in range(N_EXPERTS):
                pl.when(tail_ref[e] >= 0)(lambda e=e: tail_copy(e).wait())

    _row_copies(pos_ref, i * td, td,
                lambda k, r, p: pltpu.make_async_copy(h_ref.at[pl.ds(r, 1)], xs_ref.at[pl.ds(p, 1)], sem))
    for _ in range(2):
        pltpu.make_async_copy(h_ref, xs_ref.at[pl.ds(0, td)], sem).wait()


def _dispatch(pos, tails, h2, xs, n_rows, td=256):
    t = h2.shape[0]
    first = xs is None
    in_specs = [pl.BlockSpec((td, D_MODEL), lambda i, *_: (i, 0))]
    args = [pos, tails, h2]
    if not first:
        in_specs.append(pl.BlockSpec(memory_space=pl.ANY))
        args.append(xs)
    return pl.pallas_call(
        functools.partial(_dispatch_kernel, td=td, first=first),
        grid_spec=pltpu.PrefetchScalarGridSpec(
            num_scalar_prefetch=2, grid=(t // td,), in_specs=in_specs,
            out_specs=pl.BlockSpec(memory_space=pl.ANY),
            scratch_shapes=[pltpu.VMEM((TM_MOE, D_MODEL), F32), pltpu.SemaphoreType.DMA]),
        out_shape=jax.ShapeDtypeStruct((n_rows, D_MODEL), F32),
        input_output_aliases={} if first else {3: 0},
        compiler_params=_cparams("arbitrary"), name="dispatch",
    )(*args)


def _expert_kernel(te_ref, nu_ref, xs_ref, wg_ref, wu_ref, wd_ref, ys_ref, wgb, wub, wdb):
    j = pl.program_id(0)

    @pl.when(j < nu_ref[0])
    def _():
        @pl.when((j == 0) | (te_ref[j] != te_ref[jnp.maximum(j - 1, 0)]))
        def _():
            wgb[...] = wg_ref[0].astype(BF16)
            wub[...] = wu_ref[0].astype(BF16)
            wdb[...] = wd_ref[0].astype(BF16)

        x = xs_ref[...].astype(BF16)
        g = _dot(x, wgb[...])
        u = _dot(x, wub[...])
        a = (g / (1.0 + jnp.exp(-g))) * u
        ys_ref[...] = _dot(a.astype(BF16), wdb[...])


def _experts(tile_expert, n_used, xs, wg, wu, wd):
    n_tiles = xs.shape[0] // TM_MOE
    tile = lambda j, te, nu: (jnp.minimum(j, nu[0] - 1), 0)
    wspec = lambda shape: pl.BlockSpec((1,) + shape, lambda j, te, nu: (te[j], 0, 0))
    return pl.pallas_call(
        _expert_kernel,
        grid_spec=pltpu.PrefetchScalarGridSpec(
            num_scalar_prefetch=2, grid=(n_tiles,),
            in_specs=[pl.BlockSpec((TM_MOE, D_MODEL), tile), wspec((D_MODEL, D_EXPERT)), wspec((D_MODEL, D_EXPERT)),
                      wspec((D_EXPERT, D_MODEL))],
            out_specs=pl.BlockSpec((TM_MOE, D_MODEL), tile),
            scratch_shapes=[pltpu.VMEM((D_MODEL, D_EXPERT), BF16), pltpu.VMEM((D_MODEL, D_EXPERT), BF16),
                            pltpu.VMEM((D_EXPERT, D_MODEL), BF16)]),
        out_shape=jax.ShapeDtypeStruct(xs.shape, F32),
        compiler_params=_cparams("arbitrary"), name="experts",
    )(tile_expert, n_used, xs, wg, wu, wd)


def _combine_kernel(pos_ref, x1_ref, route_ref, mod_ref, ys_ref, o_ref, ybuf, sem, *, tc):
    _row_copies(pos_ref, pl.program_id(0) * tc, tc,
                lambda k, r, p: pltpu.make_async_copy(ys_ref.at[pl.ds(p, 1)], ybuf.at[k, pl.ds(r, 1)], sem))
    for k in range(2):
        pltpu.make_async_copy(ys_ref.at[pl.ds(0, tc)], ybuf.at[k], sem).wait()
    route = route_ref[...]
    gate2 = mod_ref[0][:, 5 * D_MODEL:]
    o_ref[...] = x1_ref[...] + gate2 * (route[:, 2:3] * ybuf[0] + route[:, 3:4] * ybuf[1])


def _combine(pos, x1, route, mod3, row_of_tile, ys, tc=256):
    t = x1.shape[0]
    row = lambda i, *_: (i, 0)
    return pl.pallas_call(
        functools.partial(_combine_kernel, tc=tc),
        grid_spec=pltpu.PrefetchScalarGridSpec(
            num_scalar_prefetch=1, grid=(t // tc,),
            in_specs=[pl.BlockSpec((tc, D_MODEL), row), pl.BlockSpec((tc, LANES), row),
                      pl.BlockSpec((1, 1, N_MOD * D_MODEL), lambda i, *_: (row_of_tile(i), 0, 0)),
                      pl.BlockSpec(memory_space=pl.ANY)],
            out_specs=pl.BlockSpec((tc, D_MODEL), row),
            scratch_shapes=[pltpu.VMEM((2, tc, D_MODEL), F32), pltpu.SemaphoreType.DMA]),
        out_shape=jax.ShapeDtypeStruct((t, D_MODEL), F32),
        compiler_params=_cparams("arbitrary"), name="combine",
    )(pos, x1, route, mod3, ys)


def _routing_plan(route, ranks, counts):
    n_tiles = route.shape[0] * 2 // TM_MOE + N_EXPERTS
    e = route[:, :2].astype(jnp.int32)
    cnt = counts[0, :N_EXPERTS].astype(jnp.int32)
    padded = (cnt + TM_MOE - 1) // TM_MOE * TM_MOE
    ends = jnp.cumsum(padded)
    pos = (ends - padded)[e] + ranks[:, :2].astype(jnp.int32)
    n_used = ends[-1:] // TM_MOE
    tile_start = jnp.arange(n_tiles, dtype=jnp.int32) * TM_MOE
    tile_expert = jnp.sum((ends[None, :] <= tile_start[:, None]).astype(jnp.int32), axis=1)
    tile_expert = jnp.minimum(tile_expert, tile_expert[n_used[0] - 1]).astype(jnp.int32)
    tails = jnp.where(padded > 0, ends - TM_MOE, -1).astype(jnp.int32)
    return pos.reshape(-1), tile_expert, n_used.astype(jnp.int32), tails, n_tiles * TM_MOE


def _rope_tables(seq):
    pos = np.arange(seq)
    n_freq = ATT_HEAD_DIM // 4
    inv = ROPE_THETA ** (-np.arange(n_freq, dtype=np.float32) / n_freq)
    ang = np.concatenate([(pos // GRID_W).astype(np.float32)[:, None] * inv,
                          (pos % GRID_W).astype(np.float32)[:, None] * inv], axis=-1).astype(np.float32)
    ang = jnp.asarray(ang)
    cos, sin = jnp.cos(ang), jnp.sin(ang)
    cos_t = jnp.repeat(cos, 2, axis=1)
    sin_t = jnp.stack([-sin, sin], axis=-1).reshape(seq, ATT_HEAD_DIM)
    return cos_t, sin_t


def _pad_lanes(a):
    return jnp.pad(a, ((0, 0), (0, LANES - a.shape[1])))


def _layer(x2d, n_batch, seq, mod3, row_of_tile, lw, *, rope_tabs, cache, state, emit):
    (n1, n2, w_main, wg_hi, wg_lo, bg, qg, kg, mg, w_out, rw_hi, rw_lo, rb, ewg, ewu, ewd) = lw
    cos_t, sin_t = rope_tabs
    outs = _project(x2d, mod3, row_of_tile(256), n1, w_main, wg_hi, wg_lo, bg, qg, kg, cos_t, sin_t,
                    rope=cache is not None, emit_kv=emit)
    q, k, v, qm, kmt, vm, om, g = outs[:8]
    rowq, colq = _gate_scans(g)
    attn = _attention(q, k, v, n_batch, seq, cache=cache, tq=128 if seq > 256 else 256)
    ml = _mlstm(qm, kmt, vm, rowq, colq, n_batch, seq, state=state, emit_state=emit)
    x1, h2, route = _mix_out(attn, ml[0], om, x2d, mod3, row_of_tile(256), mg, w_out, n2, rw_hi, rw_lo, rb)
    return (x1, h2, route), outs[8:], ml[1:]


def kernel(x_prompt, x_sample, cache_k, cache_v, state_C, state_n, state_m, c, c_ctx, mod_w, mod_b, norm1_g, norm2_g,
           w_in, b_gates, q_norm_g, k_norm_g, mlstm_norm_g, w_out, router_group_w, router_group_b, router_expert_w,
           router_expert_b, expert_w_gate, expert_w_up, expert_w_down):
    assert mod_w.shape[0] == 1, "single-layer stack"
    n_ctx, s_ctx, _ = x_prompt.shape
    n_lat, s_lat, _ = x_sample.shape
    ctx_row = n_lat

    cond = jnp.concatenate([c, c_ctx[None], jnp.zeros((8 - n_lat - 1, D_MODEL), F32)], axis=0)
    mod3 = _modulation(cond, mod_w[0], mod_b[0][None]).reshape(8, 1, N_MOD * D_MODEL)

    perm = np.concatenate([np.arange(0, 4), np.arange(8, 12), np.arange(4, 8), np.arange(12, 16)])
    wg_hi, wg_lo = _split_hi_lo(_pad_lanes(w_in[0][:, MAIN_WIDTH:][:, perm]))
    rw = jnp.concatenate([router_group_w[0], jnp.moveaxis(router_expert_w[0], 0, 1).reshape(D_MODEL, N_EXPERTS)], axis=1)
    rw_hi, rw_lo = _split_hi_lo(_pad_lanes(rw))
    rb = _pad_lanes(jnp.concatenate([router_group_b[0], router_expert_b[0].reshape(-1)])[None])
    lw = (norm1_g, norm2_g, w_in[0][:, :MAIN_WIDTH].astype(BF16), wg_hi, wg_lo, _pad_lanes(b_gates[0][perm][None]),
          q_norm_g, k_norm_g, mlstm_norm_g, w_out[0].astype(BF16), rw_hi, rw_lo, rb,
          expert_w_gate[0], expert_w_up[0], expert_w_down[0])
    rope_tabs = _rope_tables(s_lat)

    ctx_rows = lambda tm: (lambda i: ctx_row)
    lat_rows = lambda tm: (lambda i: i // (s_lat // tm))
    (x1p, h2p, routep), (ka, va), (s_c, s_n, s_m) = _layer(
        x_prompt.reshape(n_ctx * s_ctx, D_MODEL), n_ctx, s_ctx, mod3, ctx_rows, lw,
        rope_tabs=rope_tabs, cache=None, state=None, emit=True)

    caug0 = jnp.concatenate([state_C[:, 0], state_n[:, 0][..., None],
                             jnp.zeros(state_n[:, 0].shape + (LANES - 1,), F32)], axis=-1)
    caug0 = caug0.reshape(n_lat, N_CHAINS, MLSTM_HEAD_DIM, AUG)
    past = cache_k.shape[2]
    cache = (cache_k[:, 0].reshape(n_lat * past, KV_WIDTH), cache_v[:, 0].reshape(n_lat * past, KV_WIDTH))
    (x1s, h2s, routes), _, _ = _layer(
        x_sample.reshape(n_lat * s_lat, D_MODEL), n_lat, s_lat, mod3, lat_rows, lw,
        rope_tabs=rope_tabs, cache=cache, state=(caug0, state_m[:, 0].reshape(-1)), emit=False)

    route = jnp.concatenate([routep, routes], axis=0)
    ranks, counts = _ranks(route)
    pos, tile_expert, n_used, tails, n_rows = _routing_plan(route, ranks, counts)
    n_pairs_ctx = 2 * n_ctx * s_ctx
    xs = _dispatch(pos[:n_pairs_ctx], tails, h2p, None, n_rows)
    xs = _dispatch(pos[n_pairs_ctx:], tails, h2s, xs, n_rows)
    y_sorted = _experts(tile_expert, n_used, xs, lw[13], lw[14], lw[15])
    yp = _combine(pos[:n_pairs_ctx], x1p, routep, mod3, ctx_rows(256), y_sorted)
    ys = _combine(pos[n_pairs_ctx:], x1s, routes, mod3, lat_rows(256), y_sorted)

    kv_shape = (n_ctx, 1, s_ctx, ATT_KV_HEADS, ATT_HEAD_DIM)
    return (yp.reshape(x_prompt.shape), ys.reshape(x_sample.shape), ka.reshape(kv_shape), va.reshape(kv_shape),
            s_c.reshape(n_ctx, 1, 2, MLSTM_HEADS, MLSTM_HEAD_DIM, MLSTM_HEAD_DIM),
            s_n.reshape(n_ctx, 1, 2, MLSTM_HEADS, MLSTM_HEAD_DIM), s_m[..., 0, 0].reshape(n_ctx, 1, 2, MLSTM_HEADS))
```

```python
import functools

import numpy as np
import jax
import jax.numpy as jnp
from jax import lax
from jax.experimental import pallas as pl
from jax.experimental.pallas import tpu as pltpu

F32 = jnp.float32
BF16 = jnp.bfloat16

D_MODEL = 2048
GRID_W = 64
ATT_HEADS = 8
ATT_KV_HEADS = 2
ATT_HEAD_DIM = 128
ATT_GROUP = ATT_HEADS // ATT_KV_HEADS
ATT_WIDTH = ATT_HEADS * ATT_HEAD_DIM
KV_WIDTH = ATT_KV_HEADS * ATT_HEAD_DIM
ROPE_THETA = 10000.0
MLSTM_HEADS = 4
MLSTM_HEAD_DIM = 256
MLSTM_WIDTH = MLSTM_HEADS * MLSTM_HEAD_DIM
CHUNK = 128
N_GATES = 4 * MLSTM_HEADS
N_CHAINS = 2 * MLSTM_HEADS
MAIN_WIDTH = ATT_WIDTH + 2 * KV_WIDTH + 4 * MLSTM_WIDTH
N_GROUPS = 4
EXPERTS_PER_GROUP = 8
N_EXPERTS = N_GROUPS * EXPERTS_PER_GROUP
D_EXPERT = 512
N_MOD = 6
ROW_TILES = D_MODEL // 128
TM_MOE = 256
ATT_KEY_CHUNK = 512
Q_SCALE = ATT_HEAD_DIM ** -0.5 * float(np.log2(np.e))
EPS = 1e-6

LANES = 128
AUG = MLSTM_HEAD_DIM + LANES
N_SCAN_ROWS = 5 * N_CHAINS
ROUTER_LANE0 = N_GROUPS
VMEM_LIMIT = 56 * 1024 * 1024


def _cparams(*sem):
    return pltpu.CompilerParams(dimension_semantics=sem, vmem_limit_bytes=VMEM_LIMIT)


def _const_spec(shape):
    nd = len(shape)
    return pl.BlockSpec(shape, lambda *_: (0,) * nd, pipeline_mode=pl.Buffered(1))


def _split_hi_lo(x):
    hi = x.astype(BF16)
    lo = (x - hi.astype(F32)).astype(BF16)
    return hi, lo


def _dot(a, b):
    return jnp.dot(a, b, preferred_element_type=F32)


def _dot3(a_hi, a_lo, b_hi, b_lo):
    return _dot(a_hi, b_hi) + _dot(a_lo, b_hi) + _dot(a_hi, b_lo)


def _rms(x, g):
    return x * lax.rsqrt(jnp.mean(x * x, axis=-1, keepdims=True) + EPS) * g


def _mod_kernel(c_ref, w_ref, b_ref, o_ref):
    c = c_ref[...]
    s = c / (1.0 + jnp.exp(-c))
    s_hi = s.astype(BF16).astype(F32)
    lhs = jnp.concatenate([s_hi, s - s_hi], axis=0).astype(BF16)
    w_hi, w_lo = _split_hi_lo(w_ref[...])
    r = _dot(lhs, w_hi)
    r2 = _dot(lhs, w_lo)
    o_ref[...] = r[:8] + r[8:] + r2[:8] + b_ref[...]


def _modulation(cond, mod_w, mod_b):
    n = mod_w.shape[1]
    tn = 512
    return pl.pallas_call(
        _mod_kernel,
        grid=(n // tn,),
        in_specs=[pl.BlockSpec((8, D_MODEL), lambda j: (0, 0)),
                  pl.BlockSpec((D_MODEL, tn), lambda j: (0, j)),
                  pl.BlockSpec((1, tn), lambda j: (0, j))],
        out_specs=pl.BlockSpec((8, tn), lambda j: (0, j)),
        out_shape=jax.ShapeDtypeStruct((8, n), F32),
        compiler_params=_cparams("parallel"),
        name="modulation",
    )(cond, mod_w, mod_b)


def _pair_swap(x):
    lane = lax.broadcasted_iota(jnp.int32, x.shape, 1)
    return jnp.where((lane & 1) == 0, pltpu.roll(x, LANES - 1, 1), pltpu.roll(x, 1, 1))


def _proj_kernel(x_ref, mod_ref, n1_ref, w_ref, wgh_ref, wgl_ref, bg_ref, qg_ref, kg_ref, cos_ref, sin_ref,
                 q_ref, k_ref, v_ref, qm_ref, kmt_ref, vm_ref, om_ref, g_ref, *kv_refs, rope, tm):
    mod = mod_ref[0]
    shift, scale = mod[:, :D_MODEL], mod[:, D_MODEL:2 * D_MODEL]
    h = _rms(x_ref[...], n1_ref[...]) * (1.0 + scale) + shift
    h_hi, h_lo = _split_hi_lo(h)

    def rot(seg):
        return seg * cos_ref[...] + _pair_swap(seg) * sin_ref[...] if rope else seg

    qa = _dot(h_hi, w_ref[:, :ATT_WIDTH])
    for hh in range(ATT_HEADS):
        sl = slice(hh * ATT_HEAD_DIM, (hh + 1) * ATT_HEAD_DIM)
        seg = rot(_rms(qa[:, sl], qg_ref[...]))
        q_ref[:, sl] = (seg * Q_SCALE).astype(BF16)

    kv = _dot(h_hi, w_ref[:, ATT_WIDTH:ATT_WIDTH + 2 * KV_WIDTH])
    for hh in range(ATT_KV_HEADS):
        sl = slice(hh * ATT_HEAD_DIM, (hh + 1) * ATT_HEAD_DIM)
        seg = _rms(kv[:, sl], kg_ref[...])
        if kv_refs:
            kv_refs[0][:, sl] = seg
        k_ref[:, sl] = rot(seg).astype(BF16)
    va = kv[:, KV_WIDTH:]
    if kv_refs:
        kv_refs[1][...] = va
    v_ref[...] = va.astype(BF16)

    c0 = ATT_WIDTH + 2 * KV_WIDTH
    qm_ref[...] = (_dot(h_hi, w_ref[:, c0:c0 + MLSTM_WIDTH]) * MLSTM_HEAD_DIM ** -0.5).astype(BF16)
    km = _dot(h_hi, w_ref[:, c0 + MLSTM_WIDTH:c0 + 2 * MLSTM_WIDTH])
    for cc in range(tm // CHUNK):
        kmt_ref[cc] = km[cc * CHUNK:(cc + 1) * CHUNK, :].T.astype(BF16)
    vm_ref[...] = _dot(h_hi, w_ref[:, c0 + 2 * MLSTM_WIDTH:c0 + 3 * MLSTM_WIDTH]).astype(BF16)
    om_ref[...] = _dot(h_hi, w_ref[:, c0 + 3 * MLSTM_WIDTH:c0 + 4 * MLSTM_WIDTH])
    g_ref[...] = _dot3(h_hi, h_lo, wgh_ref[...], wgl_ref[...]) + bg_ref[...]


def _project(x2d, mod3, row_of_tile, n1, w_main, wg_hi, wg_lo, bg, qg, kg, cos_t, sin_t, *, rope, emit_kv, tm=256):
    t = x2d.shape[0]
    n_pos = cos_t.shape[0] // tm
    row = lambda i: (i, 0)
    in_specs = [pl.BlockSpec((tm, D_MODEL), row),
                pl.BlockSpec((1, 1, N_MOD * D_MODEL), lambda i: (row_of_tile(i), 0, 0)),
                _const_spec((1, D_MODEL)),
                _const_spec((D_MODEL, MAIN_WIDTH)),
                _const_spec((D_MODEL, LANES)), _const_spec((D_MODEL, LANES)), _const_spec((1, LANES)),
                _const_spec((1, ATT_HEAD_DIM)), _const_spec((1, ATT_HEAD_DIM)),
                pl.BlockSpec((tm, ATT_HEAD_DIM), lambda i: (i % n_pos, 0)),
                pl.BlockSpec((tm, ATT_HEAD_DIM), lambda i: (i % n_pos, 0))]
    out_shape = [jax.ShapeDtypeStruct((t, ATT_WIDTH), BF16), jax.ShapeDtypeStruct((t, KV_WIDTH), BF16),
                 jax.ShapeDtypeStruct((t, KV_WIDTH), BF16), jax.ShapeDtypeStruct((t, MLSTM_WIDTH), BF16),
                 jax.ShapeDtypeStruct((t // CHUNK, MLSTM_WIDTH, CHUNK), BF16),
                 jax.ShapeDtypeStruct((t, MLSTM_WIDTH), BF16), jax.ShapeDtypeStruct((t, MLSTM_WIDTH), F32),
                 jax.ShapeDtypeStruct((t, LANES), F32)]
    out_specs = [pl.BlockSpec((tm, ATT_WIDTH), row), pl.BlockSpec((tm, KV_WIDTH), row),
                 pl.BlockSpec((tm, KV_WIDTH), row), pl.BlockSpec((tm, MLSTM_WIDTH), row),
                 pl.BlockSpec((tm // CHUNK, MLSTM_WIDTH, CHUNK), lambda i: (i, 0, 0)),
                 pl.BlockSpec((tm, MLSTM_WIDTH), row), pl.BlockSpec((tm, MLSTM_WIDTH), row),
                 pl.BlockSpec((tm, LANES), row)]
    if emit_kv:
        out_shape += [jax.ShapeDtypeStruct((t, KV_WIDTH), F32)] * 2
        out_specs += [pl.BlockSpec((tm, KV_WIDTH), row)] * 2
    return pl.pallas_call(
        functools.partial(_proj_kernel, rope=rope, tm=tm),
        grid=(t // tm,), in_specs=in_specs, out_specs=out_specs, out_shape=out_shape,
        compiler_params=_cparams("parallel"), name="in_proj",
    )(x2d, mod3, n1, w_main, wg_hi, wg_lo, bg, qg, kg, cos_t, sin_t)


def _lane_scan(x, op, fill, is_fwd, lane):
    s = 1
    while s < CHUNK:
        from_left = jnp.where(lane >= s, pltpu.roll(x, s, 1), fill)
        from_right = jnp.where(lane < CHUNK - s, pltpu.roll(x, CHUNK - s, 1), fill)
        x = op(x, jnp.where(is_fwd, from_left, from_right))
        s *= 2
    return x


def _scan_kernel(g_ref, row_ref, col_ref, *, n_chunks):
    lane = lax.broadcasted_iota(jnp.int32, (N_CHAINS, CHUNK), 1)
    is_fwd = lax.broadcasted_iota(jnp.int32, (N_CHAINS, CHUNK), 0) < MLSTM_HEADS
    for cc in range(n_chunks):
        gt = g_ref[cc * CHUNK:(cc + 1) * CHUNK, :].T
        li, f = gt[:N_CHAINS], gt[N_CHAINS:2 * N_CHAINS]
        lf = jnp.minimum(f, 0.0) - jnp.log(1.0 + jnp.exp(-jnp.abs(f)))
        b = _lane_scan(lf, jnp.add, 0.0, is_fwd, lane)
        a = li - b
        run_max = _lane_scan(a, jnp.maximum, -jnp.inf, is_fwd, lane)
        all_max = jnp.broadcast_to(jnp.max(a, axis=1, keepdims=True), a.shape)
        last = jnp.where(is_fwd, CHUNK - 1, 0)
        total = jnp.broadcast_to(jnp.sum(jnp.where(lane == last, b, 0.0), axis=1, keepdims=True), a.shape)
        rows = jnp.concatenate([a, b, run_max, all_max, total], axis=0)
        row_ref[cc] = rows
        padded = jnp.concatenate([rows, jnp.zeros((LANES - N_SCAN_ROWS, CHUNK), F32)], axis=0)
        col_ref[cc * CHUNK:(cc + 1) * CHUNK, :] = padded.T


def _gate_scans(g, tb=1024):
    t = g.shape[0]
    tb = min(tb, t)
    return pl.pallas_call(
        functools.partial(_scan_kernel, n_chunks=tb // CHUNK),
        grid=(t // tb,),
        in_specs=[pl.BlockSpec((tb, LANES), lambda i: (i, 0))],
        out_specs=[pl.BlockSpec((tb // CHUNK, N_SCAN_ROWS, CHUNK), lambda i: (i, 0, 0)),
                   pl.BlockSpec((tb, LANES), lambda i: (i, 0))],
        out_shape=[jax.ShapeDtypeStruct((t // CHUNK, N_SCAN_ROWS, CHUNK), F32),
                   jax.ShapeDtypeStruct((t, LANES), F32)],
        compiler_params=_cparams("parallel"), name="gate_scans",
    )(g)


def _mlstm_kernel(*refs, n_chunks, has_state, emit_state):
    it = iter(refs)
    q_ref, kt_ref, v_ref, row_ref, col_ref = [next(it) for _ in range(5)]
    c0_ref, m0_ref = (next(it), next(it)) if has_state else (None, None)
    o_ref = next(it)
    c_out, n_out, m_out = (next(it), next(it), next(it)) if emit_state else (None, None, None)
    caug, hb = next(it), next(it)

    b = pl.program_id(0)
    if has_state:
        caug[...] = c0_ref[0]
        m_init = tuple(jnp.full((1, 1), m0_ref[b * N_CHAINS + r], F32) for r in range(N_CHAINS))
    else:
        caug[...] = jnp.zeros(caug.shape, F32)
        m_init = tuple(jnp.zeros((1, 1), F32) for _ in range(N_CHAINS))

    sub = lax.broadcasted_iota(jnp.int32, (CHUNK, CHUNK), 0)
    lane = lax.broadcasted_iota(jnp.int32, (CHUNK, CHUNK), 1)
    ones_col = (lax.broadcasted_iota(jnp.int32, (CHUNK, LANES), 1) == 0).astype(BF16)

    def chain(r, cc, m_prev):
        d, hd = divmod(r, MLSTM_HEADS)
        t0 = pl.multiple_of(cc * CHUNK, CHUNK)
        hs = slice(hd * MLSTM_HEAD_DIM, (hd + 1) * MLSTM_HEAD_DIM)
        rows = row_ref[cc]
        cols = col_ref[pl.ds(t0, CHUNK), :]
        row = lambda k: rows[k * N_CHAINS + r:k * N_CHAINS + r + 1, :]
        col = lambda k: cols[:, k * N_CHAINS + r:k * N_CHAINS + r + 1]
        q = q_ref[pl.ds(t0, CHUNK), hs]
        kt = kt_ref[cc, hs, :]
        vaug = jnp.concatenate([v_ref[pl.ds(t0, CHUNK), hs], ones_col], axis=1)

        m_col = jnp.maximum(m_prev, col(2))
        keep = (lane <= sub) if d == 0 else (lane >= sub)
        w = jnp.where(keep, jnp.exp(row(0) - m_col), 0.0)
        w_inter = jnp.exp(m_prev - m_col)
        p = (_dot(q, kt) * w).astype(BF16)
        intra = _dot(p, vaug)
        inter = _dot(q, caug[r].astype(BF16))
        num = intra[:, :MLSTM_HEAD_DIM] + w_inter * inter[:, :MLSTM_HEAD_DIM]
        den = intra[:, MLSTM_HEAD_DIM:MLSTM_HEAD_DIM + 1] + w_inter * inter[:, MLSTM_HEAD_DIM:MLSTM_HEAD_DIM + 1]
        h_out = num / jnp.maximum(jnp.abs(den), jnp.exp(-(col(1) + m_col)))

        m_last = jnp.maximum(m_prev, row(3))
        kw = (kt.astype(F32) * jnp.exp(row(0) - m_last)).astype(BF16)
        caug[r] = jnp.exp(m_prev - m_last[:, :1]) * caug[r] + _dot(kw, vaug)
        return h_out, (row(4) + m_last)[:, :1]

    def body(c, ms):
        new = []
        for r in range(N_CHAINS):
            cc = c if r < MLSTM_HEADS else n_chunks - 1 - c
            h_out, m_new = chain(r, cc, ms[r])
            hd = r % MLSTM_HEADS
            dst = o_ref if r < MLSTM_HEADS else hb
            dst[pl.ds(pl.multiple_of(cc * CHUNK, CHUNK), CHUNK), hd * MLSTM_HEAD_DIM:(hd + 1) * MLSTM_HEAD_DIM] = h_out
            new.append(m_new)
        return tuple(new)

    ms = lax.fori_loop(0, n_chunks, body, m_init)
    o_ref[...] += hb[...]
    if emit_state:
        for r in range(N_CHAINS):
            c_out[0, r] = caug[r, :, :MLSTM_HEAD_DIM]
            n_out[0, r] = caug[r, :, MLSTM_HEAD_DIM:].T[:1, :]
            m_out[0, r] = jnp.broadcast_to(ms[r], (1, LANES))


def _mlstm(qm, kmt, vm, rowq, colq, n_batch, seq, state=None, emit_state=False):
    nc = seq // CHUNK
    mode = dict(pipeline_mode=pl.Buffered(1)) if seq * MLSTM_WIDTH * 4 > (2 << 20) else {}
    in_specs = [pl.BlockSpec((seq, MLSTM_WIDTH), lambda b: (b, 0), **mode),
                pl.BlockSpec((nc, MLSTM_WIDTH, CHUNK), lambda b: (b, 0, 0), **mode),
                pl.BlockSpec((seq, MLSTM_WIDTH), lambda b: (b, 0), **mode),
                pl.BlockSpec((nc, N_SCAN_ROWS, CHUNK), lambda b: (b, 0, 0), **mode),
                pl.BlockSpec((seq, LANES), lambda b: (b, 0), **mode)]
    args = [qm, kmt, vm, rowq, colq]
    if state is not None:
        in_specs += [pl.BlockSpec((1, N_CHAINS, MLSTM_HEAD_DIM, AUG), lambda b: (b, 0, 0, 0), **mode),
                     pl.BlockSpec(memory_space=pltpu.SMEM)]
        args += list(state)
    out_shape = [jax.ShapeDtypeStruct((n_batch * seq, MLSTM_WIDTH), F32)]
    out_specs = [pl.BlockSpec((seq, MLSTM_WIDTH), lambda b: (b, 0))]
    if emit_state:
        out_shape += [jax.ShapeDtypeStruct((n_batch, N_CHAINS, MLSTM_HEAD_DIM, MLSTM_HEAD_DIM), F32),
                      jax.ShapeDtypeStruct((n_batch, N_CHAINS, 1, MLSTM_HEAD_DIM), F32),
                      jax.ShapeDtypeStruct((n_batch, N_CHAINS, 1, LANES), F32)]
        out_specs += [pl.BlockSpec((1, N_CHAINS, MLSTM_HEAD_DIM, MLSTM_HEAD_DIM), lambda b: (b, 0, 0, 0)),
                      pl.BlockSpec((1, N_CHAINS, 1, MLSTM_HEAD_DIM), lambda b: (b, 0, 0, 0)),
                      pl.BlockSpec((1, N_CHAINS, 1, LANES), lambda b: (b, 0, 0, 0))]
    return pl.pallas_call(
        functools.partial(_mlstm_kernel, n_chunks=nc, has_state=state is not None, emit_state=emit_state),
        grid=(n_batch,), in_specs=in_specs, out_specs=out_specs, out_shape=out_shape,
        scratch_shapes=[pltpu.VMEM((N_CHAINS, MLSTM_HEAD_DIM, AUG), F32), pltpu.VMEM((seq, MLSTM_WIDTH), F32)],
        compiler_params=_cparams("parallel"), name="mlstm",
    )(*args)


def _attn_kernel(*refs, tq, has_cache):
    if has_cache:
        q_ref, k_ref, v_ref, ck_ref, cv_ref, o_ref = refs
    else:
        q_ref, k_ref, v_ref, o_ref = refs
    q = q_ref[...]
    qs = jnp.concatenate([q[:, g * ATT_HEAD_DIM:(g + 1) * ATT_HEAD_DIM] for g in range(ATT_GROUP)], axis=0)
    nt = (((1,), (1,)), ((), ()))
    seq = k_ref.shape[0]
    kc = min(seq, ATT_KEY_CHUNK)
    chunks = [(k_ref, v_ref, c * kc) for c in range(seq // kc)]
    if has_cache:
        chunks.insert(0, (ck_ref, cv_ref, None))
    m = l = o = None
    for kr, vr, start in chunks:
        kk, vv = (kr[...], vr[...]) if start is None else (kr[start:start + kc, :], vr[start:start + kc, :])
        s = lax.dot_general(qs, kk.astype(BF16), nt, preferred_element_type=F32)
        mc = jnp.max(s, axis=-1, keepdims=True)
        if m is None:
            m = mc
            p = jnp.exp2(s - m)
            l = jnp.sum(p, axis=-1, keepdims=True)
            o = _dot(p.astype(BF16), vv.astype(BF16))
        else:
            m_new = jnp.maximum(m, mc)
            alpha = jnp.exp2(m - m_new)
            p = jnp.exp2(s - m_new)
            l = alpha * l + jnp.sum(p, axis=-1, keepdims=True)
            o = alpha * o + _dot(p.astype(BF16), vv.astype(BF16))
            m = m_new
    o = o / l
    for g in range(ATT_GROUP):
        o_ref[:, g * ATT_HEAD_DIM:(g + 1) * ATT_HEAD_DIM] = o[g * tq:(g + 1) * tq].astype(BF16)


def _attention(q, k, v, n_batch, seq, cache=None, tq=128):
    nqb = seq // tq
    gw = ATT_GROUP * ATT_HEAD_DIM
    in_specs = [pl.BlockSpec((tq, gw), lambda b, h, i: (b * nqb + i, h)),
                pl.BlockSpec((seq, ATT_HEAD_DIM), lambda b, h, i: (b, h)),
                pl.BlockSpec((seq, ATT_HEAD_DIM), lambda b, h, i: (b, h))]
    args = [q, k, v]
    if cache is not None:
        past = cache[0].shape[0] // n_batch
        in_specs += [pl.BlockSpec((past, ATT_HEAD_DIM), lambda b, h, i: (b, h))] * 2
        args += list(cache)
    return pl.pallas_call(
        functools.partial(_attn_kernel, tq=tq, has_cache=cache is not None),
        grid=(n_batch, ATT_KV_HEADS, nqb), in_specs=in_specs,
        out_specs=pl.BlockSpec((tq, gw), lambda b, h, i: (b * nqb + i, h)),
        out_shape=jax.ShapeDtypeStruct((n_batch * seq, ATT_WIDTH), BF16),
        compiler_params=_cparams("parallel", "parallel", "parallel"), name="attention",
    )(*args)


def _mix_kernel(attn_ref, hm_ref, om_ref, x_ref, mod_ref, mg_ref, wo_ref, n2_ref, rwh_ref, rwl_ref, rb_ref,
                x1_ref, h2_ref, comb_ref):
    mod = mod_ref[0]
    gate1 = mod[:, 2 * D_MODEL:3 * D_MODEL]
    shift2, scale2 = mod[:, 3 * D_MODEL:4 * D_MODEL], mod[:, 4 * D_MODEL:5 * D_MODEL]
    hm = hm_ref[...]
    mg = mg_ref[...]
    parts = []
    for hd in range(MLSTM_HEADS):
        sl = slice(hd * MLSTM_HEAD_DIM, (hd + 1) * MLSTM_HEAD_DIM)
        parts.append(_rms(hm[:, sl], mg[:, sl]))
    om = om_ref[...]
    hmg = jnp.concatenate(parts, axis=1) * (1.0 / (1.0 + jnp.exp(-om)))
    y = _dot(attn_ref[...], wo_ref[:ATT_WIDTH, :]) + _dot(hmg.astype(BF16), wo_ref[ATT_WIDTH:, :])
    x1 = x_ref[...] + gate1 * y
    x1_ref[...] = x1
    h2 = _rms(x1, n2_ref[...]) * (1.0 + scale2) + shift2
    h2_hi, h2_lo = _split_hi_lo(h2)
    _store_row_major(h2_ref, h2)

    lg = _dot3(h2_hi, h2_lo, rwh_ref[...], rwl_ref[...]) + rb_ref[...]
    lane = lax.broadcasted_iota(jnp.int32, lg.shape, 1).astype(F32)
    neg = -jnp.inf
    first = lambda hit: jnp.min(jnp.where(hit, lane, float(LANES)), axis=-1, keepdims=True)
    gl = jnp.where(lane < N_GROUPS, lg, neg)
    gmax = jnp.max(gl, axis=-1, keepdims=True)
    grp = first(gl == gmax)
    p_grp = 1.0 / jnp.sum(jnp.exp(gl - gmax), axis=-1, keepdims=True)
    lo = ROUTER_LANE0 + grp * EXPERTS_PER_GROUP
    el = jnp.where((lane >= lo) & (lane < lo + EXPERTS_PER_GROUP), lg, neg)
    m1 = jnp.max(el, axis=-1, keepdims=True)
    i1 = first(el == m1)
    el2 = jnp.where(lane == i1, neg, el)
    m2 = jnp.max(el2, axis=-1, keepdims=True)
    i2 = first(el2 == m2)
    r = jnp.exp(m2 - m1)
    w1 = p_grp / (1.0 + r)
    w2 = w1 * r
    comb_ref[...] = jnp.where(lane == 0.0, i1 - ROUTER_LANE0, jnp.where(lane == 1.0, i2 - ROUTER_LANE0,
                              jnp.where(lane == 2.0, w1, jnp.where(lane == 3.0, w2, 0.0))))


def _mix_out(attn, hm, om, x2d, mod3, row_of_tile, mg, w_out, n2, rw_hi, rw_lo, rb, tm=256):
    t = x2d.shape[0]
    row = lambda i: (i, 0)
    return pl.pallas_call(
        _mix_kernel, grid=(t // tm,),
        in_specs=[pl.BlockSpec((tm, ATT_WIDTH), row), pl.BlockSpec((tm, MLSTM_WIDTH), row),
                  pl.BlockSpec((tm, MLSTM_WIDTH), row), pl.BlockSpec((tm, D_MODEL), row),
                  pl.BlockSpec((1, 1, N_MOD * D_MODEL), lambda i: (row_of_tile(i), 0, 0)),
                  _const_spec((1, MLSTM_WIDTH)), _const_spec((D_MODEL, D_MODEL)), _const_spec((1, D_MODEL)),
                  _const_spec((D_MODEL, LANES)), _const_spec((D_MODEL, LANES)), _const_spec((1, LANES))],
        out_specs=[pl.BlockSpec((tm, D_MODEL), row), pl.BlockSpec((tm * ROW_TILES, LANES), row),
                   pl.BlockSpec((tm, LANES), row)],
        out_shape=[jax.ShapeDtypeStruct((t, D_MODEL), F32), jax.ShapeDtypeStruct((t * ROW_TILES, LANES), F32),
                   jax.ShapeDtypeStruct((t, LANES), F32)],
        compiler_params=_cparams("parallel"), name="mix_out",
    )(attn, hm, om, x2d, mod3, mg, w_out, n2, rw_hi, rw_lo, rb)


def _rank_kernel(route_ref, rank_ref, cnt_ref, run_ref, tri_ref):
    tr = route_ref.shape[0]

    @pl.when(pl.program_id(0) == 0)
    def _():
        run_ref[...] = jnp.zeros(run_ref.shape, F32)
        tri_ref[...] = (lax.broadcasted_iota(jnp.int32, (tr, tr), 1)
                        < lax.broadcasted_iota(jnp.int32, (tr, tr), 0)).astype(BF16)

    route = route_ref[...]
    lane = lax.broadcasted_iota(jnp.int32, route.shape, 1).astype(F32)
    hit1, hit2 = lane == route[:, 0:1], lane == route[:, 1:2]
    onehot = jnp.where(hit1, 1.0, jnp.where(hit2, 1.0, 0.0))
    before = _dot(tri_ref[...], onehot.astype(BF16)) + run_ref[0:1, :]
    r1 = jnp.sum(jnp.where(hit1, before, 0.0), axis=-1, keepdims=True)
    r2 = jnp.sum(jnp.where(hit2, before, 0.0), axis=-1, keepdims=True)
    rank_ref[...] = jnp.where(lane == 0.0, r1, jnp.where(lane == 1.0, r2, 0.0))
    run_ref[...] = run_ref[...] + jnp.sum(onehot, axis=0, keepdims=True)
    cnt_ref[...] = run_ref[...]


def _ranks(route, tr=512):
    t = route.shape[0]
    return pl.pallas_call(
        _rank_kernel, grid=(t // tr,),
        in_specs=[pl.BlockSpec((tr, LANES), lambda i: (i, 0))],
        out_specs=[pl.BlockSpec((tr, LANES), lambda i: (i, 0)), pl.BlockSpec((8, LANES), lambda i: (0, 0))],
        out_shape=[jax.ShapeDtypeStruct((t, LANES), F32), jax.ShapeDtypeStruct((8, LANES), F32)],
        scratch_shapes=[pltpu.VMEM((8, LANES), F32), pltpu.VMEM((tr, tr), BF16)],
        compiler_params=_cparams("arbitrary"), name="expert_ranks",
    )(route)


def _row_slot(ref, r, n=1):
    return ref.at[pl.ds(pl.multiple_of(r * ROW_TILES, ROW_TILES), n * ROW_TILES)]


def _load_row_major(ref, n):
    return jnp.concatenate([ref[pl.ds(t, n, stride=ROW_TILES), :] for t in range(ROW_TILES)], axis=1)


def _store_row_major(ref, x):
    for t in range(ROW_TILES):
        ref[pl.ds(t, x.shape[0], stride=ROW_TILES), :] = x[:, t * LANES:(t + 1) * LANES]


def _row_copies(pos_ref, base, n_rows, make_copy):
    def body(r, carry):
        for k in range(2):
            make_copy(k, r, pos_ref[(base + r) * 2 + k]).start()
        return carry
    lax.fori_loop(0, n_rows, body, 0, unroll=8)


def _dispatch_kernel(*refs, td, first):
    if first:
        pos_ref, tail_ref, h_ref, xs_ref, zero_ref, sem = refs
    else:
        pos_ref, tail_ref, h_ref, _, xs_ref, zero_ref, sem = refs
    i = pl.program_id(0)

    if first:
        @pl.when(i == 0)
        def _():
            zero_ref[...] = jnp.zeros(zero_ref.shape, F32)
            tail_copy = lambda e: pltpu.make_async_copy(zero_ref, _row_slot(xs_ref, tail_ref[e], TM_MOE), sem)
            for e in range(N_EXPERTS):
                pl.when(tail_ref[e] >= 0)(lambda e=e: tail_copy(e).start())
            for e in range(N_EXPERTS):
                pl.when(tail_ref[e] >= 0)(lambda e=e: tail_copy(e).wait())

    _row_copies(pos_ref, i * td, td,
                lambda k, r, p: pltpu.make_async_copy(_row_slot(h_ref, r), _row_slot(xs_ref, p), sem))
    for _ in range(2):
        pltpu.make_async_copy(h_ref, _row_slot(xs_ref, 0, td), sem).wait()


def _dispatch(pos, tails, h2, xs, n_rows, td=256):
    t = h2.shape[0] // ROW_TILES
    first = xs is None
    in_specs = [pl.BlockSpec((td * ROW_TILES, LANES), lambda i, *_: (i, 0))]
    args = [pos, tails, h2]
    if not first:
        in_specs.append(pl.BlockSpec(memory_space=pl.ANY))
        args.append(xs)
    return pl.pallas_call(
        functools.partial(_dispatch_kernel, td=td, first=first),
        grid_spec=pltpu.PrefetchScalarGridSpec(
            num_scalar_prefetch=2, grid=(t // td,), in_specs=in_specs,
            out_specs=pl.BlockSpec(memory_space=pl.ANY),
            scratch_shapes=[pltpu.VMEM((TM_MOE * ROW_TILES, LANES), F32), pltpu.SemaphoreType.DMA]),
        out_shape=jax.ShapeDtypeStruct((n_rows * ROW_TILES, LANES), F32),
        input_output_aliases={} if first else {3: 0},
        compiler_params=_cparams("arbitrary"), name="dispatch",
    )(*args)


def _expert_kernel(te_ref, nu_ref, xs_ref, wg_ref, wu_ref, wd_ref, ys_ref, wgb, wub, wdb):
    j = pl.program_id(0)

    @pl.when(j < nu_ref[0])
    def _():
        @pl.when((j == 0) | (te_ref[j] != te_ref[jnp.maximum(j - 1, 0)]))
        def _():
            wgb[...] = wg_ref[0].astype(BF16)
            wub[...] = wu_ref[0].astype(BF16)
            wdb[...] = wd_ref[0].astype(BF16)

        x = _load_row_major(xs_ref, TM_MOE).astype(BF16)
        g = _dot(x, wgb[...])
        u = _dot(x, wub[...])
        a = (g / (1.0 + jnp.exp(-g))) * u
        _store_row_major(ys_ref, _dot(a.astype(BF16), wdb[...]))


def _experts(tile_expert, n_used, xs, wg, wu, wd):
    n_tiles = xs.shape[0] // (TM_MOE * ROW_TILES)
    tile = lambda j, te, nu: (jnp.minimum(j, nu[0] - 1), 0)
    wspec = lambda shape: pl.BlockSpec((1,) + shape, lambda j, te, nu: (te[j], 0, 0))
    return pl.pallas_call(
        _expert_kernel,
        grid_spec=pltpu.PrefetchScalarGridSpec(
            num_scalar_prefetch=2, grid=(n_tiles,),
            in_specs=[pl.BlockSpec((TM_MOE * ROW_TILES, LANES), tile), wspec((D_MODEL, D_EXPERT)),
                      wspec((D_MODEL, D_EXPERT)), wspec((D_EXPERT, D_MODEL))],
            out_specs=pl.BlockSpec((TM_MOE * ROW_TILES, LANES), tile),
            scratch_shapes=[pltpu.VMEM((D_MODEL, D_EXPERT), BF16), pltpu.VMEM((D_MODEL, D_EXPERT), BF16),
                            pltpu.VMEM((D_EXPERT, D_MODEL), BF16)]),
        out_shape=jax.ShapeDtypeStruct(xs.shape, F32),
        compiler_params=_cparams("arbitrary"), name="experts",
    )(tile_expert, n_used, xs, wg, wu, wd)


def _combine_kernel(pos_ref, x1_ref, route_ref, mod_ref, ys_ref, o_ref, ybuf, sem, *, tc):
    _row_copies(pos_ref, pl.program_id(0) * tc, tc,
                lambda k, r, p: pltpu.make_async_copy(_row_slot(ys_ref, p), _row_slot(ybuf.at[k], r), sem))
    for k in range(2):
        pltpu.make_async_copy(_row_slot(ys_ref, 0, tc), ybuf.at[k], sem).wait()
    route = route_ref[...]
    w1 = jnp.broadcast_to(route[:, 2:3], (tc, LANES))
    w2 = jnp.broadcast_to(route[:, 3:4], (tc, LANES))
    for t in range(ROW_TILES):
        sl = slice(t * LANES, (t + 1) * LANES)
        rows = pl.ds(t, tc, stride=ROW_TILES)
        y = w1 * ybuf[0, rows, :] + w2 * ybuf[1, rows, :]
        o_ref[:, sl] = x1_ref[:, sl] + mod_ref[0, :, 5 * D_MODEL + t * LANES:5 * D_MODEL + (t + 1) * LANES] * y


def _combine(pos, x1, route, mod3, row_of_tile, ys, tc=256):
    t = x1.shape[0]
    row = lambda i, *_: (i, 0)
    return pl.pallas_call(
        functools.partial(_combine_kernel, tc=tc),
        grid_spec=pltpu.PrefetchScalarGridSpec(
            num_scalar_prefetch=1, grid=(t // tc,),
            in_specs=[pl.BlockSpec((tc, D_MODEL), row), pl.BlockSpec((tc, LANES), row),
                      pl.BlockSpec((1, 1, N_MOD * D_MODEL), lambda i, *_: (row_of_tile(i), 0, 0)),
                      pl.BlockSpec(memory_space=pl.ANY)],
            out_specs=pl.BlockSpec((tc, D_MODEL), row),
            scratch_shapes=[pltpu.VMEM((2, tc * ROW_TILES, LANES), F32), pltpu.SemaphoreType.DMA]),
        out_shape=jax.ShapeDtypeStruct((t, D_MODEL), F32),
        compiler_params=_cparams("arbitrary"), name="combine",
    )(pos, x1, route, mod3, ys)


def _routing_plan(route, ranks, counts):
    n_tiles = route.shape[0] * 2 // TM_MOE + N_EXPERTS
    e = route[:, :2].astype(jnp.int32)
    cnt = counts[0, :N_EXPERTS].astype(jnp.int32)
    padded = (cnt + TM_MOE - 1) // TM_MOE * TM_MOE
    ends = jnp.cumsum(padded)
    pos = (ends - padded)[e] + ranks[:, :2].astype(jnp.int32)
    n_used = ends[-1:] // TM_MOE
    tile_start = jnp.arange(n_tiles, dtype=jnp.int32) * TM_MOE
    tile_expert = jnp.sum((ends[None, :] <= tile_start[:, None]).astype(jnp.int32), axis=1)
    tile_expert = jnp.minimum(tile_expert, tile_expert[n_used[0] - 1]).astype(jnp.int32)
    tails = jnp.where(padded > 0, ends - TM_MOE, -1).astype(jnp.int32)
    return pos.reshape(-1), tile_expert, n_used.astype(jnp.int32), tails, n_tiles * TM_MOE


def _rope_tables(seq):
    pos = np.arange(seq)
    n_freq = ATT_HEAD_DIM // 4
    inv = ROPE_THETA ** (-np.arange(n_freq, dtype=np.float32) / n_freq)
    ang = np.concatenate([(pos // GRID_W).astype(np.float32)[:, None] * inv,
                          (pos % GRID_W).astype(np.float32)[:, None] * inv], axis=-1).astype(np.float32)
    ang = jnp.asarray(ang)
    cos, sin = jnp.cos(ang), jnp.sin(ang)
    cos_t = jnp.repeat(cos, 2, axis=1)
    sin_t = jnp.stack([-sin, sin], axis=-1).reshape(seq, ATT_HEAD_DIM)
    return cos_t, sin_t


def _pad_lanes(a):
    return jnp.pad(a, ((0, 0), (0, LANES - a.shape[1])))


def _layer(x2d, n_batch, seq, mod3, row_of_tile, lw, *, rope_tabs, cache, state, emit):
    (n1, n2, w_main, wg_hi, wg_lo, bg, qg, kg, mg, w_out, rw_hi, rw_lo, rb, ewg, ewu, ewd) = lw
    cos_t, sin_t = rope_tabs
    outs = _project(x2d, mod3, row_of_tile(256), n1, w_main, wg_hi, wg_lo, bg, qg, kg, cos_t, sin_t,
                    rope=cache is not None, emit_kv=emit)
    q, k, v, qm, kmt, vm, om, g = outs[:8]
    rowq, colq = _gate_scans(g)
    attn = _attention(q, k, v, n_batch, seq, cache=cache, tq=256)
    ml = _mlstm(qm, kmt, vm, rowq, colq, n_batch, seq, state=state, emit_state=emit)
    x1, h2, route = _mix_out(attn, ml[0], om, x2d, mod3, row_of_tile(256), mg, w_out, n2, rw_hi, rw_lo, rb)
    return (x1, h2, route), outs[8:], ml[1:]


def kernel(x_prompt, x_sample, cache_k, cache_v, state_C, state_n, state_m, c, c_ctx, mod_w, mod_b, norm1_g, norm2_g,
           w_in, b_gates, q_norm_g, k_norm_g, mlstm_norm_g, w_out, router_group_w, router_group_b, router_expert_w,
           router_expert_b, expert_w_gate, expert_w_up, expert_w_down):
    assert mod_w.shape[0] == 1, "single-layer stack"
    n_ctx, s_ctx, _ = x_prompt.shape
    n_lat, s_lat, _ = x_sample.shape
    ctx_row = n_lat

    cond = jnp.concatenate([c, c_ctx[None], jnp.zeros((8 - n_lat - 1, D_MODEL), F32)], axis=0)
    mod3 = _modulation(cond, mod_w[0], mod_b[0][None]).reshape(8, 1, N_MOD * D_MODEL)

    perm = np.concatenate([np.arange(0, 4), np.arange(8, 12), np.arange(4, 8), np.arange(12, 16)])
    wg_hi, wg_lo = _split_hi_lo(_pad_lanes(w_in[0][:, MAIN_WIDTH:][:, perm]))
    rw = jnp.concatenate([router_group_w[0], jnp.moveaxis(router_expert_w[0], 0, 1).reshape(D_MODEL, N_EXPERTS)], axis=1)
    rw_hi, rw_lo = _split_hi_lo(_pad_lanes(rw))
    rb = _pad_lanes(jnp.concatenate([router_group_b[0], router_expert_b[0].reshape(-1)])[None])
    lw = (norm1_g, norm2_g, w_in[0][:, :MAIN_WIDTH].astype(BF16), wg_hi, wg_lo, _pad_lanes(b_gates[0][perm][None]),
          q_norm_g, k_norm_g, mlstm_norm_g, w_out[0].astype(BF16), rw_hi, rw_lo, rb,
          expert_w_gate[0], expert_w_up[0], expert_w_down[0])
    rope_tabs = _rope_tables(s_lat)

    ctx_rows = lambda tm: (lambda i: ctx_row)
    lat_rows = lambda tm: (lambda i: i // (s_lat // tm))
    (x1p, h2p, routep), (ka, va), (s_c, s_n, s_m) = _layer(
        x_prompt.reshape(n_ctx * s_ctx, D_MODEL), n_ctx, s_ctx, mod3, ctx_rows, lw,
        rope_tabs=rope_tabs, cache=None, state=None, emit=True)

    caug0 = jnp.concatenate([state_C[:, 0], state_n[:, 0][..., None],
                             jnp.zeros(state_n[:, 0].shape + (LANES - 1,), F32)], axis=-1)
    caug0 = caug0.reshape(n_lat, N_CHAINS, MLSTM_HEAD_DIM, AUG)
    past = cache_k.shape[2]
    cache = (cache_k[:, 0].reshape(n_lat * past, KV_WIDTH), cache_v[:, 0].reshape(n_lat * past, KV_WIDTH))
    (x1s, h2s, routes), _, _ = _layer(
        x_sample.reshape(n_lat * s_lat, D_MODEL), n_lat, s_lat, mod3, lat_rows, lw,
        rope_tabs=rope_tabs, cache=cache, state=(caug0, state_m[:, 0].reshape(-1)), emit=False)

    route = jnp.concatenate([routep, routes], axis=0)
    ranks, counts = _ranks(route)
    pos, tile_expert, n_used, tails, n_rows = _routing_plan(route, ranks, counts)
    n_pairs_ctx = 2 * n_ctx * s_ctx
    xs = _dispatch(pos[:n_pairs_ctx], tails, h2p, None, n_rows)
    xs = _dispatch(pos[n_pairs_ctx:], tails, h2s, xs, n_rows)
    y_sorted = _experts(tile_expert, n_used, xs, lw[13], lw[14], lw[15])
    yp = _combine(pos[:n_pairs_ctx], x1p, routep, mod3, ctx_rows(256), y_sorted)
    ys = _combine(pos[n_pairs_ctx:], x1s, routes, mod3, lat_rows(256), y_sorted)

    kv_shape = (n_ctx, 1, s_ctx, ATT_KV_HEADS, ATT_HEAD_DIM)
    return (yp.reshape(x_prompt.shape), ys.reshape(x_sample.shape), ka.reshape(kv_shape), va.reshape(kv_shape),
            s_c.reshape(n_ctx, 1, 2, MLSTM_HEADS, MLSTM_HEAD_DIM, MLSTM_HEAD_DIM),
            s_n.reshape(n_ctx, 1, 2, MLSTM_HEADS, MLSTM_HEAD_DIM), s_m[..., 0, 0].reshape(n_ctx, 1, 2, MLSTM_HEADS))
```

```python
import functools

import numpy as np
import jax
import jax.numpy as jnp
from jax import lax
from jax.experimental import pallas as pl
from jax.experimental.pallas import tpu as pltpu

F32 = jnp.float32
BF16 = jnp.bfloat16

D_MODEL = 2048
GRID_W = 64
ATT_HEADS = 8
ATT_KV_HEADS = 2
ATT_HEAD_DIM = 128
ATT_GROUP = ATT_HEADS // ATT_KV_HEADS
ATT_WIDTH = ATT_HEADS * ATT_HEAD_DIM
KV_WIDTH = ATT_KV_HEADS * ATT_HEAD_DIM
ROPE_THETA = 10000.0
MLSTM_HEADS = 4
MLSTM_HEAD_DIM = 256
MLSTM_WIDTH = MLSTM_HEADS * MLSTM_HEAD_DIM
CHUNK = 128
N_GATES = 4 * MLSTM_HEADS
N_CHAINS = 2 * MLSTM_HEADS
MAIN_WIDTH = ATT_WIDTH + 2 * KV_WIDTH + 4 * MLSTM_WIDTH
N_GROUPS = 4
EXPERTS_PER_GROUP = 8
N_EXPERTS = N_GROUPS * EXPERTS_PER_GROUP
D_EXPERT = 512
N_MOD = 6
TM_MOE = 256
ATT_KEY_CHUNK = 512
Q_SCALE = ATT_HEAD_DIM ** -0.5 * float(np.log2(np.e))
EPS = 1e-6

LANES = 128
AUG = MLSTM_HEAD_DIM + LANES
N_SCAN_ROWS = 5 * N_CHAINS
ROUTER_LANE0 = N_GROUPS
VMEM_LIMIT = 56 * 1024 * 1024


def _cparams(*sem):
    return pltpu.CompilerParams(dimension_semantics=sem, vmem_limit_bytes=VMEM_LIMIT)


def _const_spec(shape):
    nd = len(shape)
    return pl.BlockSpec(shape, lambda *_: (0,) * nd, pipeline_mode=pl.Buffered(1))


def _split_hi_lo(x):
    hi = x.astype(BF16)
    lo = (x - hi.astype(F32)).astype(BF16)
    return hi, lo


def _dot(a, b):
    return jnp.dot(a, b, preferred_element_type=F32)


def _dot3(a_hi, a_lo, b_hi, b_lo):
    return _dot(a_hi, b_hi) + _dot(a_lo, b_hi) + _dot(a_hi, b_lo)


def _rms(x, g):
    return x * lax.rsqrt(jnp.mean(x * x, axis=-1, keepdims=True) + EPS) * g


def _mod_kernel(c_ref, w_ref, b_ref, o_ref):
    c = c_ref[...]
    s = c / (1.0 + jnp.exp(-c))
    s_hi = s.astype(BF16).astype(F32)
    lhs = jnp.concatenate([s_hi, s - s_hi], axis=0).astype(BF16)
    w_hi, w_lo = _split_hi_lo(w_ref[...])
    r = _dot(lhs, w_hi)
    r2 = _dot(lhs, w_lo)
    o_ref[...] = r[:8] + r[8:] + r2[:8] + b_ref[...]


def _modulation(cond, mod_w, mod_b):
    n = mod_w.shape[1]
    tn = 512
    return pl.pallas_call(
        _mod_kernel,
        grid=(n // tn,),
        in_specs=[pl.BlockSpec((8, D_MODEL), lambda j: (0, 0)),
                  pl.BlockSpec((D_MODEL, tn), lambda j: (0, j)),
                  pl.BlockSpec((1, tn), lambda j: (0, j))],
        out_specs=pl.BlockSpec((8, tn), lambda j: (0, j)),
        out_shape=jax.ShapeDtypeStruct((8, n), F32),
        compiler_params=_cparams("parallel"),
        name="modulation",
    )(cond, mod_w, mod_b)


def _cast_kernel(w_ref, o_ref):
    o_ref[...] = w_ref[0].astype(BF16)


def _cast_bf16(w, n_cols, tn=512):
    rows = w.shape[1]
    return pl.pallas_call(
        _cast_kernel, grid=(n_cols // tn,),
        in_specs=[pl.BlockSpec((1, rows, tn), lambda j: (0, 0, j))],
        out_specs=pl.BlockSpec((rows, tn), lambda j: (0, j)),
        out_shape=jax.ShapeDtypeStruct((rows, n_cols), BF16),
        compiler_params=_cparams("parallel"), name="cast_bf16",
    )(w)


def _pair_swap(x):
    lane = lax.broadcasted_iota(jnp.int32, x.shape, 1)
    return jnp.where((lane & 1) == 0, pltpu.roll(x, LANES - 1, 1), pltpu.roll(x, 1, 1))


def _proj_kernel(x_ref, mod_ref, n1_ref, w_ref, wgh_ref, wgl_ref, bg_ref, qg_ref, kg_ref, cos_ref, sin_ref,
                 q_ref, k_ref, v_ref, qm_ref, kmt_ref, vm_ref, om_ref, g_ref, *kv_refs, rope, tm):
    mod = mod_ref[0]
    shift, scale = mod[:, :D_MODEL], mod[:, D_MODEL:2 * D_MODEL]
    h = _rms(x_ref[...], n1_ref[...]) * (1.0 + scale) + shift
    h_hi, h_lo = _split_hi_lo(h)

    def rot(seg):
        return seg * cos_ref[...] + _pair_swap(seg) * sin_ref[...] if rope else seg

    qa = _dot(h_hi, w_ref[:, :ATT_WIDTH])
    for hh in range(ATT_HEADS):
        sl = slice(hh * ATT_HEAD_DIM, (hh + 1) * ATT_HEAD_DIM)
        seg = rot(_rms(qa[:, sl], qg_ref[...]))
        q_ref[:, sl] = (seg * Q_SCALE).astype(BF16)

    kv = _dot(h_hi, w_ref[:, ATT_WIDTH:ATT_WIDTH + 2 * KV_WIDTH])
    for hh in range(ATT_KV_HEADS):
        sl = slice(hh * ATT_HEAD_DIM, (hh + 1) * ATT_HEAD_DIM)
        seg = _rms(kv[:, sl], kg_ref[...])
        if kv_refs:
            kv_refs[0][:, sl] = seg
        k_ref[:, sl] = rot(seg).astype(BF16)
    va = kv[:, KV_WIDTH:]
    if kv_refs:
        kv_refs[1][...] = va
    v_ref[...] = va.astype(BF16)

    c0 = ATT_WIDTH + 2 * KV_WIDTH
    qm_ref[...] = (_dot(h_hi, w_ref[:, c0:c0 + MLSTM_WIDTH]) * MLSTM_HEAD_DIM ** -0.5).astype(BF16)
    km = _dot(h_hi, w_ref[:, c0 + MLSTM_WIDTH:c0 + 2 * MLSTM_WIDTH])
    for cc in range(tm // CHUNK):
        kmt_ref[cc] = km[cc * CHUNK:(cc + 1) * CHUNK, :].T.astype(BF16)
    vm_ref[...] = _dot(h_hi, w_ref[:, c0 + 2 * MLSTM_WIDTH:c0 + 3 * MLSTM_WIDTH]).astype(BF16)
    om_ref[...] = _dot(h_hi, w_ref[:, c0 + 3 * MLSTM_WIDTH:c0 + 4 * MLSTM_WIDTH])
    g_ref[...] = _dot3(h_hi, h_lo, wgh_ref[...], wgl_ref[...]) + bg_ref[...]


def _project(x2d, mod3, row_of_tile, n1, w_main, wg_hi, wg_lo, bg, qg, kg, cos_t, sin_t, *, rope, emit_kv, tm=256):
    t = x2d.shape[0]
    n_pos = cos_t.shape[0] // tm
    row = lambda i: (i, 0)
    in_specs = [pl.BlockSpec((tm, D_MODEL), row),
                pl.BlockSpec((1, 1, N_MOD * D_MODEL), lambda i: (row_of_tile(i), 0, 0)),
                _const_spec((1, D_MODEL)),
                _const_spec((D_MODEL, MAIN_WIDTH)),
                _const_spec((D_MODEL, LANES)), _const_spec((D_MODEL, LANES)), _const_spec((1, LANES)),
                _const_spec((1, ATT_HEAD_DIM)), _const_spec((1, ATT_HEAD_DIM)),
                pl.BlockSpec((tm, ATT_HEAD_DIM), lambda i: (i % n_pos, 0)),
                pl.BlockSpec((tm, ATT_HEAD_DIM), lambda i: (i % n_pos, 0))]
    out_shape = [jax.ShapeDtypeStruct((t, ATT_WIDTH), BF16), jax.ShapeDtypeStruct((t, KV_WIDTH), BF16),
                 jax.ShapeDtypeStruct((t, KV_WIDTH), BF16), jax.ShapeDtypeStruct((t, MLSTM_WIDTH), BF16),
                 jax.ShapeDtypeStruct((t // CHUNK, MLSTM_WIDTH, CHUNK), BF16),
                 jax.ShapeDtypeStruct((t, MLSTM_WIDTH), BF16), jax.ShapeDtypeStruct((t, MLSTM_WIDTH), F32),
                 jax.ShapeDtypeStruct((t, LANES), F32)]
    out_specs = [pl.BlockSpec((tm, ATT_WIDTH), row), pl.BlockSpec((tm, KV_WIDTH), row),
                 pl.BlockSpec((tm, KV_WIDTH), row), pl.BlockSpec((tm, MLSTM_WIDTH), row),
                 pl.BlockSpec((tm // CHUNK, MLSTM_WIDTH, CHUNK), lambda i: (i, 0, 0)),
                 pl.BlockSpec((tm, MLSTM_WIDTH), row), pl.BlockSpec((tm, MLSTM_WIDTH), row),
                 pl.BlockSpec((tm, LANES), row)]
    if emit_kv:
        out_shape += [jax.ShapeDtypeStruct((t, KV_WIDTH), F32)] * 2
        out_specs += [pl.BlockSpec((tm, KV_WIDTH), row)] * 2
    return pl.pallas_call(
        functools.partial(_proj_kernel, rope=rope, tm=tm),
        grid=(t // tm,), in_specs=in_specs, out_specs=out_specs, out_shape=out_shape,
        compiler_params=_cparams("parallel"), name="in_proj",
    )(x2d, mod3, n1, w_main, wg_hi, wg_lo, bg, qg, kg, cos_t, sin_t)


def _lane_scan(x, op, fill, is_fwd, lane):
    s = 1
    while s < CHUNK:
        from_left = jnp.where(lane >= s, pltpu.roll(x, s, 1), fill)
        from_right = jnp.where(lane < CHUNK - s, pltpu.roll(x, CHUNK - s, 1), fill)
        x = op(x, jnp.where(is_fwd, from_left, from_right))
        s *= 2
    return x


def _scan_kernel(g_ref, row_ref, col_ref, *, n_chunks):
    lane = lax.broadcasted_iota(jnp.int32, (N_CHAINS, CHUNK), 1)
    is_fwd = lax.broadcasted_iota(jnp.int32, (N_CHAINS, CHUNK), 0) < MLSTM_HEADS
    for cc in range(n_chunks):
        gt = g_ref[cc * CHUNK:(cc + 1) * CHUNK, :].T
        li, f = gt[:N_CHAINS], gt[N_CHAINS:2 * N_CHAINS]
        lf = jnp.minimum(f, 0.0) - jnp.log(1.0 + jnp.exp(-jnp.abs(f)))
        b = _lane_scan(lf, jnp.add, 0.0, is_fwd, lane)
        a = li - b
        run_max = _lane_scan(a, jnp.maximum, -jnp.inf, is_fwd, lane)
        all_max = jnp.broadcast_to(jnp.max(a, axis=1, keepdims=True), a.shape)
        last = jnp.where(is_fwd, CHUNK - 1, 0)
        total = jnp.broadcast_to(jnp.sum(jnp.where(lane == last, b, 0.0), axis=1, keepdims=True), a.shape)
        rows = jnp.concatenate([a, b, run_max, all_max, total], axis=0)
        row_ref[cc] = rows
        padded = jnp.concatenate([rows, jnp.zeros((LANES - N_SCAN_ROWS, CHUNK), F32)], axis=0)
        col_ref[cc * CHUNK:(cc + 1) * CHUNK, :] = padded.T


def _gate_scans(g, tb=1024):
    t = g.shape[0]
    tb = min(tb, t)
    return pl.pallas_call(
        functools.partial(_scan_kernel, n_chunks=tb // CHUNK),
        grid=(t // tb,),
        in_specs=[pl.BlockSpec((tb, LANES), lambda i: (i, 0))],
        out_specs=[pl.BlockSpec((tb // CHUNK, N_SCAN_ROWS, CHUNK), lambda i: (i, 0, 0)),
                   pl.BlockSpec((tb, LANES), lambda i: (i, 0))],
        out_shape=[jax.ShapeDtypeStruct((t // CHUNK, N_SCAN_ROWS, CHUNK), F32),
                   jax.ShapeDtypeStruct((t, LANES), F32)],
        compiler_params=_cparams("parallel"), name="gate_scans",
    )(g)


def _mlstm_kernel(*refs, n_chunks, has_state, emit_state):
    it = iter(refs)
    q_ref, kt_ref, v_ref, row_ref, col_ref = [next(it) for _ in range(5)]
    c0_ref, m0_ref = (next(it), next(it)) if has_state else (None, None)
    o_ref = next(it)
    c_out, n_out, m_out = (next(it), next(it), next(it)) if emit_state else (None, None, None)
    caug, hb = next(it), next(it)

    b = pl.program_id(0)
    if has_state:
        caug[...] = c0_ref[0]
        m_init = tuple(jnp.full((1, 1), m0_ref[b * N_CHAINS + r], F32) for r in range(N_CHAINS))
    else:
        caug[...] = jnp.zeros(caug.shape, F32)
        m_init = tuple(jnp.zeros((1, 1), F32) for _ in range(N_CHAINS))

    sub = lax.broadcasted_iota(jnp.int32, (CHUNK, CHUNK), 0)
    lane = lax.broadcasted_iota(jnp.int32, (CHUNK, CHUNK), 1)
    ones_col = (lax.broadcasted_iota(jnp.int32, (CHUNK, LANES), 1) == 0).astype(BF16)

    def chain(r, cc, m_prev):
        d, hd = divmod(r, MLSTM_HEADS)
        t0 = pl.multiple_of(cc * CHUNK, CHUNK)
        hs = slice(hd * MLSTM_HEAD_DIM, (hd + 1) * MLSTM_HEAD_DIM)
        rows = row_ref[cc]
        cols = col_ref[pl.ds(t0, CHUNK), :]
        row = lambda k: rows[k * N_CHAINS + r:k * N_CHAINS + r + 1, :]
        col = lambda k: cols[:, k * N_CHAINS + r:k * N_CHAINS + r + 1]
        q = q_ref[pl.ds(t0, CHUNK), hs]
        kt = kt_ref[cc, hs, :]
        vaug = jnp.concatenate([v_ref[pl.ds(t0, CHUNK), hs], ones_col], axis=1)

        m_col = jnp.maximum(m_prev, col(2))
        keep = (lane <= sub) if d == 0 else (lane >= sub)
        w = jnp.where(keep, jnp.exp(row(0) - m_col), 0.0)
        w_inter = jnp.exp(m_prev - m_col)
        p = (_dot(q, kt) * w).astype(BF16)
        intra = _dot(p, vaug)
        inter = _dot(q, caug[r].astype(BF16))
        num = intra[:, :MLSTM_HEAD_DIM] + w_inter * inter[:, :MLSTM_HEAD_DIM]
        den = intra[:, MLSTM_HEAD_DIM:MLSTM_HEAD_DIM + 1] + w_inter * inter[:, MLSTM_HEAD_DIM:MLSTM_HEAD_DIM + 1]
        h_out = num / jnp.maximum(jnp.abs(den), jnp.exp(-(col(1) + m_col)))

        m_last = jnp.maximum(m_prev, row(3))
        kw = (kt.astype(F32) * jnp.exp(row(0) - m_last)).astype(BF16)
        caug[r] = jnp.exp(m_prev - m_last[:, :1]) * caug[r] + _dot(kw, vaug)
        return h_out, (row(4) + m_last)[:, :1]

    def body(c, ms):
        new = []
        for r in range(N_CHAINS):
            cc = c if r < MLSTM_HEADS else n_chunks - 1 - c
            h_out, m_new = chain(r, cc, ms[r])
            hd = r % MLSTM_HEADS
            dst = o_ref if r < MLSTM_HEADS else hb
            dst[pl.ds(pl.multiple_of(cc * CHUNK, CHUNK), CHUNK), hd * MLSTM_HEAD_DIM:(hd + 1) * MLSTM_HEAD_DIM] = h_out
            new.append(m_new)
        return tuple(new)

    ms = lax.fori_loop(0, n_chunks, body, m_init)
    o_ref[...] += hb[...]
    if emit_state:
        for r in range(N_CHAINS):
            c_out[0, r] = caug[r, :, :MLSTM_HEAD_DIM]
            n_out[0, r] = caug[r, :, MLSTM_HEAD_DIM:].T[:1, :]
            m_out[0, r] = jnp.broadcast_to(ms[r], (1, LANES))


def _mlstm(qm, kmt, vm, rowq, colq, n_batch, seq, state=None, emit_state=False):
    nc = seq // CHUNK
    mode = dict(pipeline_mode=pl.Buffered(1)) if seq * MLSTM_WIDTH * 4 > (2 << 20) else {}
    in_specs = [pl.BlockSpec((seq, MLSTM_WIDTH), lambda b: (b, 0), **mode),
                pl.BlockSpec((nc, MLSTM_WIDTH, CHUNK), lambda b: (b, 0, 0), **mode),
                pl.BlockSpec((seq, MLSTM_WIDTH), lambda b: (b, 0), **mode),
                pl.BlockSpec((nc, N_SCAN_ROWS, CHUNK), lambda b: (b, 0, 0), **mode),
                pl.BlockSpec((seq, LANES), lambda b: (b, 0), **mode)]
    args = [qm, kmt, vm, rowq, colq]
    if state is not None:
        in_specs += [pl.BlockSpec((1, N_CHAINS, MLSTM_HEAD_DIM, AUG), lambda b: (b, 0, 0, 0), **mode),
                     pl.BlockSpec(memory_space=pltpu.SMEM)]
        args += list(state)
    out_shape = [jax.ShapeDtypeStruct((n_batch * seq, MLSTM_WIDTH), F32)]
    out_specs = [pl.BlockSpec((seq, MLSTM_WIDTH), lambda b: (b, 0))]
    if emit_state:
        out_shape += [jax.ShapeDtypeStruct((n_batch, N_CHAINS, MLSTM_HEAD_DIM, MLSTM_HEAD_DIM), F32),
                      jax.ShapeDtypeStruct((n_batch, N_CHAINS, 1, MLSTM_HEAD_DIM), F32),
                      jax.ShapeDtypeStruct((n_batch, N_CHAINS, 1, LANES), F32)]
        out_specs += [pl.BlockSpec((1, N_CHAINS, MLSTM_HEAD_DIM, MLSTM_HEAD_DIM), lambda b: (b, 0, 0, 0)),
                      pl.BlockSpec((1, N_CHAINS, 1, MLSTM_HEAD_DIM), lambda b: (b, 0, 0, 0)),
                      pl.BlockSpec((1, N_CHAINS, 1, LANES), lambda b: (b, 0, 0, 0))]
    return pl.pallas_call(
        functools.partial(_mlstm_kernel, n_chunks=nc, has_state=state is not None, emit_state=emit_state),
        grid=(n_batch,), in_specs=in_specs, out_specs=out_specs, out_shape=out_shape,
        scratch_shapes=[pltpu.VMEM((N_CHAINS, MLSTM_HEAD_DIM, AUG), F32), pltpu.VMEM((seq, MLSTM_WIDTH), F32)],
        compiler_params=_cparams("parallel"), name="mlstm",
    )(*args)


def _attn_kernel(*refs, tq, has_cache):
    if has_cache:
        q_ref, k_ref, v_ref, ck_ref, cv_ref, o_ref = refs
    else:
        q_ref, k_ref, v_ref, o_ref = refs
    q = q_ref[...]
    qs = jnp.concatenate([q[:, g * ATT_HEAD_DIM:(g + 1) * ATT_HEAD_DIM] for g in range(ATT_GROUP)], axis=0)
    nt = (((1,), (1,)), ((), ()))
    seq = k_ref.shape[0]
    kc = min(seq, ATT_KEY_CHUNK)
    chunks = [(k_ref, v_ref, c * kc) for c in range(seq // kc)]
    if has_cache:
        chunks.insert(0, (ck_ref, cv_ref, None))
    m = l = o = None
    for kr, vr, start in chunks:
        kk, vv = (kr[...], vr[...]) if start is None else (kr[start:start + kc, :], vr[start:start + kc, :])
        s = lax.dot_general(qs, kk.astype(BF16), nt, preferred_element_type=F32)
        mc = jnp.max(s, axis=-1, keepdims=True)
        if m is None:
            m = mc
            p = jnp.exp2(s - m)
            l = jnp.sum(p, axis=-1, keepdims=True)
            o = _dot(p.astype(BF16), vv.astype(BF16))
        else:
            m_new = jnp.maximum(m, mc)
            alpha = jnp.exp2(m - m_new)
            p = jnp.exp2(s - m_new)
            l = alpha * l + jnp.sum(p, axis=-1, keepdims=True)
            o = alpha * o + _dot(p.astype(BF16), vv.astype(BF16))
            m = m_new
    o = o / l
    for g in range(ATT_GROUP):
        o_ref[:, g * ATT_HEAD_DIM:(g + 1) * ATT_HEAD_DIM] = o[g * tq:(g + 1) * tq].astype(BF16)


def _attention(q, k, v, n_batch, seq, cache=None, tq=256):
    nqb = seq // tq
    gw = ATT_GROUP * ATT_HEAD_DIM
    in_specs = [pl.BlockSpec((tq, gw), lambda b, h, i: (b * nqb + i, h)),
                pl.BlockSpec((seq, ATT_HEAD_DIM), lambda b, h, i: (b, h)),
                pl.BlockSpec((seq, ATT_HEAD_DIM), lambda b, h, i: (b, h))]
    args = [q, k, v]
    if cache is not None:
        past = cache[0].shape[0] // n_batch
        in_specs += [pl.BlockSpec((past, ATT_HEAD_DIM), lambda b, h, i: (b, h))] * 2
        args += list(cache)
    return pl.pallas_call(
        functools.partial(_attn_kernel, tq=tq, has_cache=cache is not None),
        grid=(n_batch, ATT_KV_HEADS, nqb), in_specs=in_specs,
        out_specs=pl.BlockSpec((tq, gw), lambda b, h, i: (b * nqb + i, h)),
        out_shape=jax.ShapeDtypeStruct((n_batch * seq, ATT_WIDTH), BF16),
        compiler_params=_cparams("parallel", "parallel", "parallel"), name="attention",
    )(*args)


def _mix_kernel(attn_ref, hm_ref, om_ref, x_ref, mod_ref, mg_ref, wo_ref, n2_ref, rwh_ref, rwl_ref, rb_ref, *rest):
    x1_ref, h2_ref, route_ref = rest[-3:]
    mod = mod_ref[0]
    gate1 = mod[:, 2 * D_MODEL:3 * D_MODEL]
    shift2, scale2 = mod[:, 3 * D_MODEL:4 * D_MODEL], mod[:, 4 * D_MODEL:5 * D_MODEL]
    hm = hm_ref[...]
    mg = mg_ref[...]
    parts = []
    for hd in range(MLSTM_HEADS):
        sl = slice(hd * MLSTM_HEAD_DIM, (hd + 1) * MLSTM_HEAD_DIM)
        parts.append(_rms(hm[:, sl], mg[:, sl]))
    om = om_ref[...]
    hmg = jnp.concatenate(parts, axis=1) * (1.0 / (1.0 + jnp.exp(-om)))
    y = _dot(attn_ref[...], wo_ref[:ATT_WIDTH, :]) + _dot(hmg.astype(BF16), wo_ref[ATT_WIDTH:, :])
    x1 = x_ref[...] + gate1 * y
    x1_ref[...] = x1
    h2 = _rms(x1, n2_ref[...]) * (1.0 + scale2) + shift2
    h2_hi, h2_lo = _split_hi_lo(h2)
    h2_ref[...] = h2

    lg = _dot3(h2_hi, h2_lo, rwh_ref[...], rwl_ref[...]) + rb_ref[...]
    lane = lax.broadcasted_iota(jnp.int32, lg.shape, 1).astype(F32)
    neg = -jnp.inf
    first = lambda hit: jnp.min(jnp.where(hit, lane, float(LANES)), axis=-1, keepdims=True)
    gl = jnp.where(lane < N_GROUPS, lg, neg)
    gmax = jnp.max(gl, axis=-1, keepdims=True)
    grp = first(gl == gmax)
    p_grp = 1.0 / jnp.sum(jnp.exp(gl - gmax), axis=-1, keepdims=True)
    lo = ROUTER_LANE0 + grp * EXPERTS_PER_GROUP
    el = jnp.where((lane >= lo) & (lane < lo + EXPERTS_PER_GROUP), lg, neg)
    m1 = jnp.max(el, axis=-1, keepdims=True)
    i1 = first(el == m1)
    el2 = jnp.where(lane == i1, neg, el)
    m2 = jnp.max(el2, axis=-1, keepdims=True)
    i2 = first(el2 == m2)
    r = jnp.exp(m2 - m1)
    w1 = p_grp / (1.0 + r)
    w2 = w1 * r
    route_ref[...] = jnp.where(lane == 0.0, i1 - ROUTER_LANE0, jnp.where(lane == 1.0, i2 - ROUTER_LANE0,
                               jnp.where(lane == 2.0, w1, jnp.where(lane == 3.0, w2, 0.0))))


def _mix_out(attn, hm, om, x2d, mod3, row_of_tile, mg, w_out, n2, rw_hi, rw_lo, rb, h2_all, tile0, t_total, tm=256):
    t = x2d.shape[0]
    row = lambda i: (i, 0)
    in_specs = [pl.BlockSpec((tm, ATT_WIDTH), row), pl.BlockSpec((tm, MLSTM_WIDTH), row),
                pl.BlockSpec((tm, MLSTM_WIDTH), row), pl.BlockSpec((tm, D_MODEL), row),
                pl.BlockSpec((1, 1, N_MOD * D_MODEL), lambda i: (row_of_tile(i), 0, 0)),
                _const_spec((1, MLSTM_WIDTH)), _const_spec((D_MODEL, D_MODEL)), _const_spec((1, D_MODEL)),
                _const_spec((D_MODEL, LANES)), _const_spec((D_MODEL, LANES)), _const_spec((1, LANES))]
    args = [attn, hm, om, x2d, mod3, mg, w_out, n2, rw_hi, rw_lo, rb]
    aliases = {}
    if h2_all is not None:
        aliases = {len(args): 1}
        in_specs.append(pl.BlockSpec(memory_space=pl.ANY))
        args.append(h2_all)
    return pl.pallas_call(
        _mix_kernel, grid=(t // tm,), in_specs=in_specs,
        out_specs=[pl.BlockSpec((tm, D_MODEL), row), pl.BlockSpec((tm, D_MODEL), lambda i: (tile0 + i, 0)),
                   pl.BlockSpec((tm, LANES), row)],
        out_shape=[jax.ShapeDtypeStruct((t, D_MODEL), F32), jax.ShapeDtypeStruct((t_total, D_MODEL), F32),
                   jax.ShapeDtypeStruct((t, LANES), F32)],
        input_output_aliases=aliases,
        compiler_params=_cparams("parallel"), name="mix_out",
    )(*args)


def _rank_kernel(route_ref, rank_ref, cnt_ref, run_ref, tri_ref):
    tr = route_ref.shape[0]

    @pl.when(pl.program_id(0) == 0)
    def _():
        run_ref[...] = jnp.zeros(run_ref.shape, F32)
        tri_ref[...] = (lax.broadcasted_iota(jnp.int32, (tr, tr), 1)
                        < lax.broadcasted_iota(jnp.int32, (tr, tr), 0)).astype(BF16)

    route = route_ref[...]
    lane = lax.broadcasted_iota(jnp.int32, route.shape, 1).astype(F32)
    hit1, hit2 = lane == route[:, 0:1], lane == route[:, 1:2]
    onehot = jnp.where(hit1, 1.0, jnp.where(hit2, 1.0, 0.0))
    before = _dot(tri_ref[...], onehot.astype(BF16)) + run_ref[0:1, :]
    r1 = jnp.sum(jnp.where(hit1, before, 0.0), axis=-1, keepdims=True)
    r2 = jnp.sum(jnp.where(hit2, before, 0.0), axis=-1, keepdims=True)
    rank_ref[...] = jnp.where(lane == 0.0, r1, jnp.where(lane == 1.0, r2, 0.0))
    run_ref[...] = run_ref[...] + jnp.sum(onehot, axis=0, keepdims=True)
    cnt_ref[...] = run_ref[...]


def _ranks(route, tr=512):
    t = route.shape[0]
    return pl.pallas_call(
        _rank_kernel, grid=(t // tr,),
        in_specs=[pl.BlockSpec((tr, LANES), lambda i: (i, 0))],
        out_specs=[pl.BlockSpec((tr, LANES), lambda i: (i, 0)), pl.BlockSpec((8, LANES), lambda i: (0, 0))],
        out_shape=[jax.ShapeDtypeStruct((t, LANES), F32), jax.ShapeDtypeStruct((8, LANES), F32)],
        scratch_shapes=[pltpu.VMEM((8, LANES), F32), pltpu.VMEM((tr, tr), BF16)],
        compiler_params=_cparams("arbitrary"), name="expert_ranks",
    )(route)


def _routing_plan(route, ranks, counts):
    t = route.shape[0]
    n_tiles = 2 * t // TM_MOE + N_EXPERTS
    e = route[:, :2].T.astype(jnp.int32)
    rank = ranks[:, :2].T.astype(jnp.int32)
    cnt = counts[0, :N_EXPERTS].astype(jnp.int32)
    padded = (cnt + TM_MOE - 1) // TM_MOE * TM_MOE
    ends = jnp.cumsum(padded)
    pos = ((ends - padded)[e] + rank).reshape(-1)
    n_used = (ends[-1:] // TM_MOE).astype(jnp.int32)
    tile_start = jnp.arange(n_tiles, dtype=jnp.int32) * TM_MOE
    tile_expert = jnp.sum((ends[None, :] <= tile_start[:, None]).astype(jnp.int32), axis=1)
    tile_expert = jnp.minimum(tile_expert, tile_expert[n_used[0] - 1]).astype(jnp.int32)
    rows = jnp.arange(n_tiles * TM_MOE, dtype=jnp.int32)
    spare = 2 * t + tile_expert[rows // TM_MOE] * TM_MOE + rows % TM_MOE
    pair_of_row = spare.at[pos].set(jnp.arange(2 * t, dtype=jnp.int32), unique_indices=True)
    return tile_expert, n_used, pair_of_row


def _moe_kernel(te_ref, nu_ref, pair_ref, h_ref, wg_ref, wu_ref, wd_ref, y_ref,
                xbuf, ybuf, wgb, wub, wdb, gsem, ssem, *, n_tokens):
    j = pl.program_id(0)
    n_used = nu_ref[0]
    slot = j % 2

    def gather(tile, s):
        def body(r, carry):
            p = pair_ref[tile * TM_MOE + r]
            src = jnp.where(p >= 2 * n_tokens, 0, jnp.where(p >= n_tokens, p - n_tokens, p))
            pltpu.make_async_copy(h_ref.at[pl.ds(src, 1)], xbuf.at[s, pl.ds(r, 1)], gsem.at[s]).start()
            return carry
        lax.fori_loop(0, TM_MOE, body, 0, unroll=8)

    def scatter(tile, s):
        def body(r, carry):
            p = pair_ref[tile * TM_MOE + r]
            pltpu.make_async_copy(ybuf.at[s, pl.ds(r, 1)], y_ref.at[pl.ds(p, 1)], ssem.at[s]).start()
            return carry
        lax.fori_loop(0, TM_MOE, body, 0, unroll=8)

    wait_gather = lambda s: pltpu.make_async_copy(h_ref.at[pl.ds(0, TM_MOE)], xbuf.at[s], gsem.at[s]).wait()
    wait_scatter = lambda s: pltpu.make_async_copy(ybuf.at[s], y_ref.at[pl.ds(0, TM_MOE)], ssem.at[s]).wait()

    @pl.when(j == 0)
    def _():
        gather(0, 0)

    @pl.when(j < n_used)
    def _():
        @pl.when(j + 1 < n_used)
        def _():
            gather(j + 1, 1 - slot)

        wait_gather(slot)

        @pl.when((j == 0) | (te_ref[j] != te_ref[jnp.maximum(j - 1, 0)]))
        def _():
            wgb[...] = wg_ref[0].astype(BF16)
            wub[...] = wu_ref[0].astype(BF16)
            wdb[...] = wd_ref[0].astype(BF16)

        @pl.when(j >= 2)
        def _():
            wait_scatter(slot)

        x = xbuf[slot].astype(BF16)
        g = _dot(x, wgb[...])
        u = _dot(x, wub[...])
        a = (g / (1.0 + jnp.exp(-g))) * u
        ybuf[slot] = _dot(a.astype(BF16), wdb[...])
        scatter(j, slot)

        @pl.when(j == n_used - 1)
        def _():
            wait_scatter(slot)

            @pl.when(j >= 1)
            def _():
                wait_scatter(1 - slot)


def _experts(tile_expert, n_used, pair_of_row, h2_all, wg, wu, wd):
    t = h2_all.shape[0]
    n_tiles = pair_of_row.shape[0] // TM_MOE
    wspec = lambda shape: pl.BlockSpec((1,) + shape, lambda j, te, nu, pr: (te[j], 0, 0))
    return pl.pallas_call(
        functools.partial(_moe_kernel, n_tokens=t),
        grid_spec=pltpu.PrefetchScalarGridSpec(
            num_scalar_prefetch=3, grid=(n_tiles,),
            in_specs=[pl.BlockSpec(memory_space=pl.ANY), wspec((D_MODEL, D_EXPERT)), wspec((D_MODEL, D_EXPERT)),
                      wspec((D_EXPERT, D_MODEL))],
            out_specs=pl.BlockSpec(memory_space=pl.ANY),
            scratch_shapes=[pltpu.VMEM((2, TM_MOE, D_MODEL), F32), pltpu.VMEM((2, TM_MOE, D_MODEL), F32),
                            pltpu.VMEM((D_MODEL, D_EXPERT), BF16), pltpu.VMEM((D_MODEL, D_EXPERT), BF16),
                            pltpu.VMEM((D_EXPERT, D_MODEL), BF16),
                            pltpu.SemaphoreType.DMA((2,)), pltpu.SemaphoreType.DMA((2,))]),
        out_shape=jax.ShapeDtypeStruct((2 * t + N_EXPERTS * TM_MOE, D_MODEL), F32),
        compiler_params=_cparams("arbitrary"), name="experts",
    )(tile_expert, n_used, pair_of_row, h2_all, wg, wu, wd)


def _combine_kernel(x1_ref, route_ref, mod_ref, y0_ref, y1_ref, o_ref):
    route = route_ref[...]
    gate2 = mod_ref[0][:, 5 * D_MODEL:]
    o_ref[...] = x1_ref[...] + gate2 * (route[:, 2:3] * y0_ref[...] + route[:, 3:4] * y1_ref[...])


def _combine(x1, route, mod3, row_of_tile, y, tile0, slot_tiles, tc=256):
    t = x1.shape[0]
    row = lambda i: (i, 0)
    return pl.pallas_call(
        _combine_kernel, grid=(t // tc,),
        in_specs=[pl.BlockSpec((tc, D_MODEL), row), pl.BlockSpec((tc, LANES), row),
                  pl.BlockSpec((1, 1, N_MOD * D_MODEL), lambda i: (row_of_tile(i), 0, 0)),
                  pl.BlockSpec((tc, D_MODEL), lambda i: (tile0 + i, 0)),
                  pl.BlockSpec((tc, D_MODEL), lambda i: (slot_tiles + tile0 + i, 0))],
        out_specs=pl.BlockSpec((tc, D_MODEL), row),
        out_shape=jax.ShapeDtypeStruct((t, D_MODEL), F32),
        compiler_params=_cparams("parallel"), name="combine",
    )(x1, route, mod3, y, y)


def _rope_tables(seq):
    pos = np.arange(seq)
    n_freq = ATT_HEAD_DIM // 4
    inv = ROPE_THETA ** (-np.arange(n_freq, dtype=np.float32) / n_freq)
    ang = np.concatenate([(pos // GRID_W).astype(np.float32)[:, None] * inv,
                          (pos % GRID_W).astype(np.float32)[:, None] * inv], axis=-1).astype(np.float32)
    ang = jnp.asarray(ang)
    cos, sin = jnp.cos(ang), jnp.sin(ang)
    cos_t = jnp.repeat(cos, 2, axis=1)
    sin_t = jnp.stack([-sin, sin], axis=-1).reshape(seq, ATT_HEAD_DIM)
    return cos_t, sin_t


def _pad_lanes(a):
    return jnp.pad(a, ((0, 0), (0, LANES - a.shape[1])))


def _layer(x2d, n_batch, seq, mod3, row_of_tile, lw, *, rope_tabs, cache, state, emit, h2_all, tile0, t_total):
    (n1, n2, w_main, wg_hi, wg_lo, bg, qg, kg, mg, w_out, rw_hi, rw_lo, rb) = lw
    cos_t, sin_t = rope_tabs
    outs = _project(x2d, mod3, row_of_tile(256), n1, w_main, wg_hi, wg_lo, bg, qg, kg, cos_t, sin_t,
                    rope=cache is not None, emit_kv=emit)
    q, k, v, qm, kmt, vm, om, g = outs[:8]
    rowq, colq = _gate_scans(g)
    attn = _attention(q, k, v, n_batch, seq, cache=cache)
    ml = _mlstm(qm, kmt, vm, rowq, colq, n_batch, seq, state=state, emit_state=emit)
    mixed = _mix_out(attn, ml[0], om, x2d, mod3, row_of_tile(256), mg, w_out, n2, rw_hi, rw_lo, rb,
                     h2_all, tile0, t_total)
    return mixed, outs[8:], ml[1:]


def kernel(x_prompt, x_sample, cache_k, cache_v, state_C, state_n, state_m, c, c_ctx, mod_w, mod_b, norm1_g, norm2_g,
           w_in, b_gates, q_norm_g, k_norm_g, mlstm_norm_g, w_out, router_group_w, router_group_b, router_expert_w,
           router_expert_b, expert_w_gate, expert_w_up, expert_w_down):
    assert mod_w.shape[0] == 1, "single-layer stack"
    n_ctx, s_ctx, _ = x_prompt.shape
    n_lat, s_lat, _ = x_sample.shape
    t_ctx, t_lat = n_ctx * s_ctx, n_lat * s_lat
    t_all = t_ctx + t_lat
    ctx_row = n_lat

    cond = jnp.concatenate([c, c_ctx[None], jnp.zeros((8 - n_lat - 1, D_MODEL), F32)], axis=0)
    mod3 = _modulation(cond, mod_w[0], mod_b[0][None]).reshape(8, 1, N_MOD * D_MODEL)

    perm = np.concatenate([np.arange(0, 4), np.arange(8, 12), np.arange(4, 8), np.arange(12, 16)])
    wg_hi, wg_lo = _split_hi_lo(_pad_lanes(w_in[0][:, MAIN_WIDTH:][:, perm]))
    rw = jnp.concatenate([router_group_w[0], jnp.moveaxis(router_expert_w[0], 0, 1).reshape(D_MODEL, N_EXPERTS)], axis=1)
    rw_hi, rw_lo = _split_hi_lo(_pad_lanes(rw))
    rb = _pad_lanes(jnp.concatenate([router_group_b[0], router_expert_b[0].reshape(-1)])[None])
    lw = (norm1_g, norm2_g, _cast_bf16(w_in, MAIN_WIDTH), wg_hi, wg_lo, _pad_lanes(b_gates[0][perm][None]),
          q_norm_g, k_norm_g, mlstm_norm_g, _cast_bf16(w_out, D_MODEL), rw_hi, rw_lo, rb)
    rope_tabs = _rope_tables(s_lat)

    ctx_rows = lambda tm: (lambda i: ctx_row)
    lat_rows = lambda tm: (lambda i: i // (s_lat // tm))
    (x1p, h2_all, routep), (ka, va), (s_c, s_n, s_m) = _layer(
        x_prompt.reshape(t_ctx, D_MODEL), n_ctx, s_ctx, mod3, ctx_rows, lw,
        rope_tabs=rope_tabs, cache=None, state=None, emit=True, h2_all=None, tile0=0, t_total=t_all)

    caug0 = jnp.concatenate([state_C[:, 0], state_n[:, 0][..., None],
                             jnp.zeros(state_n[:, 0].shape + (LANES - 1,), F32)], axis=-1)
    caug0 = caug0.reshape(n_lat, N_CHAINS, MLSTM_HEAD_DIM, AUG)
    past = cache_k.shape[2]
    cache = (cache_k[:, 0].reshape(n_lat * past, KV_WIDTH), cache_v[:, 0].reshape(n_lat * past, KV_WIDTH))
    (x1s, h2_all, routes), _, _ = _layer(
        x_sample.reshape(t_lat, D_MODEL), n_lat, s_lat, mod3, lat_rows, lw,
        rope_tabs=rope_tabs, cache=cache, state=(caug0, state_m[:, 0].reshape(-1)), emit=False,
        h2_all=h2_all, tile0=t_ctx // 256, t_total=t_all)

    route = jnp.concatenate([routep, routes], axis=0)
    ranks, counts = _ranks(route)
    tile_expert, n_used, pair_of_row = _routing_plan(route, ranks, counts)
    y = _experts(tile_expert, n_used, pair_of_row, h2_all, expert_w_gate[0], expert_w_up[0], expert_w_down[0])
    yp = _combine(x1p, routep, mod3, ctx_rows(256), y, 0, t_all // 256)
    ys = _combine(x1s, routes, mod3, lat_rows(256), y, t_ctx // 256, t_all // 256)

    kv_shape = (n_ctx, 1, s_ctx, ATT_KV_HEADS, ATT_HEAD_DIM)
    return (yp.reshape(x_prompt.shape), ys.reshape(x_sample.shape), ka.reshape(kv_shape), va.reshape(kv_shape),
            s_c.reshape(n_ctx, 1, 2, MLSTM_HEADS, MLSTM_HEAD_DIM, MLSTM_HEAD_DIM),
            s_n.reshape(n_ctx, 1, 2, MLSTM_HEADS, MLSTM_HEAD_DIM), s_m[..., 0, 0].reshape(n_ctx, 1, 2, MLSTM_HEADS))
```

```python
import functools

import numpy as np
import jax
import jax.numpy as jnp
from jax import lax
from jax.experimental import pallas as pl
from jax.experimental.pallas import tpu as pltpu

F32 = jnp.float32
BF16 = jnp.bfloat16

D_MODEL = 2048
GRID_W = 64
ATT_HEADS = 8
ATT_KV_HEADS = 2
ATT_HEAD_DIM = 128
ATT_GROUP = ATT_HEADS // ATT_KV_HEADS
ATT_WIDTH = ATT_HEADS * ATT_HEAD_DIM
KV_WIDTH = ATT_KV_HEADS * ATT_HEAD_DIM
ROPE_THETA = 10000.0
MLSTM_HEADS = 4
MLSTM_HEAD_DIM = 256
MLSTM_WIDTH = MLSTM_HEADS * MLSTM_HEAD_DIM
CHUNK = 128
N_GATES = 4 * MLSTM_HEADS
N_CHAINS = 2 * MLSTM_HEADS
MAIN_WIDTH = ATT_WIDTH + 2 * KV_WIDTH + 4 * MLSTM_WIDTH
N_GROUPS = 4
EXPERTS_PER_GROUP = 8
N_EXPERTS = N_GROUPS * EXPERTS_PER_GROUP
D_EXPERT = 512
N_MOD = 6
TM_MOE = 256
MIX_TILE = 512
ATT_KEY_CHUNK = 512
Q_SCALE = ATT_HEAD_DIM ** -0.5 * float(np.log2(np.e))
EPS = 1e-6

LANES = 128
AUG = MLSTM_HEAD_DIM + LANES
N_SCAN_ROWS = 5 * N_CHAINS
ROUTER_LANE0 = N_GROUPS
VMEM_LIMIT = 56 * 1024 * 1024


def _cparams(*sem):
    return pltpu.CompilerParams(dimension_semantics=sem, vmem_limit_bytes=VMEM_LIMIT)


def _const_spec(shape):
    nd = len(shape)
    return pl.BlockSpec(shape, lambda *_: (0,) * nd, pipeline_mode=pl.Buffered(1))


def _split_hi_lo(x):
    hi = x.astype(BF16)
    lo = (x - hi.astype(F32)).astype(BF16)
    return hi, lo


def _dot(a, b):
    return jnp.dot(a, b, preferred_element_type=F32)


def _hi_lo_cat(w):
    return jnp.concatenate(_split_hi_lo(w), axis=1)


def _dot_hi_lo(a_hi, a_lo, w_ref):
    r = _dot(a_hi, w_ref[...])
    return r[:, :LANES] + r[:, LANES:] + _dot(a_lo, w_ref[:, :LANES])


def _rms(x, g):
    return x * lax.rsqrt(jnp.mean(x * x, axis=-1, keepdims=True) + EPS) * g


def _mod_kernel(c_ref, w_ref, b_ref, o_ref):
    c = c_ref[...]
    s = c / (1.0 + jnp.exp(-c))
    s_hi = s.astype(BF16).astype(F32)
    lhs = jnp.concatenate([s_hi, s - s_hi], axis=0).astype(BF16)
    w_hi, w_lo = _split_hi_lo(w_ref[...])
    r = _dot(lhs, w_hi)
    r2 = _dot(lhs, w_lo)
    o_ref[...] = r[:8] + r[8:] + r2[:8] + b_ref[...]


def _modulation(cond, mod_w, mod_b):
    n = mod_w.shape[1]
    tn = 512
    return pl.pallas_call(
        _mod_kernel,
        grid=(n // tn,),
        in_specs=[pl.BlockSpec((8, D_MODEL), lambda j: (0, 0)),
                  pl.BlockSpec((D_MODEL, tn), lambda j: (0, j)),
                  pl.BlockSpec((1, tn), lambda j: (0, j))],
        out_specs=pl.BlockSpec((8, tn), lambda j: (0, j)),
        out_shape=jax.ShapeDtypeStruct((8, n), F32),
        compiler_params=_cparams("parallel"),
        name="modulation",
    )(cond, mod_w, mod_b)


def _cast_kernel(w_ref, o_ref):
    o_ref[...] = w_ref[0].astype(BF16)


def _cast_bf16(w, n_cols, tn=512):
    rows = w.shape[1]
    return pl.pallas_call(
        _cast_kernel, grid=(n_cols // tn,),
        in_specs=[pl.BlockSpec((1, rows, tn), lambda j: (0, 0, j))],
        out_specs=pl.BlockSpec((rows, tn), lambda j: (0, j)),
        out_shape=jax.ShapeDtypeStruct((rows, n_cols), BF16),
        compiler_params=_cparams("parallel"), name="cast_bf16",
    )(w)


def _pair_swap(x):
    lane = lax.broadcasted_iota(jnp.int32, x.shape, 1)
    return jnp.where((lane & 1) == 0, pltpu.roll(x, LANES - 1, 1), pltpu.roll(x, 1, 1))


def _proj_kernel(x_ref, mod_ref, n1_ref, w_ref, wg_ref, bg_ref, qg_ref, kg_ref, cos_ref, sin_ref,
                 q_ref, k_ref, v_ref, qm_ref, kmt_ref, vm_ref, om_ref, g_ref, *kv_refs, rope, tm):
    mod = mod_ref[0]
    shift, scale = mod[:, :D_MODEL], mod[:, D_MODEL:2 * D_MODEL]
    h = _rms(x_ref[...], n1_ref[...]) * (1.0 + scale) + shift
    h_hi, h_lo = _split_hi_lo(h)

    def rot(seg):
        return seg * cos_ref[...] + _pair_swap(seg) * sin_ref[...] if rope else seg

    qa = _dot(h_hi, w_ref[:, :ATT_WIDTH])
    for hh in range(ATT_HEADS):
        sl = slice(hh * ATT_HEAD_DIM, (hh + 1) * ATT_HEAD_DIM)
        seg = rot(_rms(qa[:, sl], qg_ref[...]))
        q_ref[:, sl] = (seg * Q_SCALE).astype(BF16)

    kv = _dot(h_hi, w_ref[:, ATT_WIDTH:ATT_WIDTH + 2 * KV_WIDTH])
    for hh in range(ATT_KV_HEADS):
        sl = slice(hh * ATT_HEAD_DIM, (hh + 1) * ATT_HEAD_DIM)
        seg = _rms(kv[:, sl], kg_ref[...])
        if kv_refs:
            kv_refs[0][:, sl] = seg
        k_ref[:, sl] = rot(seg).astype(BF16)
    va = kv[:, KV_WIDTH:]
    if kv_refs:
        kv_refs[1][...] = va
    v_ref[...] = va.astype(BF16)

    c0 = ATT_WIDTH + 2 * KV_WIDTH
    qm_ref[...] = (_dot(h_hi, w_ref[:, c0:c0 + MLSTM_WIDTH]) * MLSTM_HEAD_DIM ** -0.5).astype(BF16)
    km = _dot(h_hi, w_ref[:, c0 + MLSTM_WIDTH:c0 + 2 * MLSTM_WIDTH])
    for cc in range(tm // CHUNK):
        kmt_ref[cc] = km[cc * CHUNK:(cc + 1) * CHUNK, :].T.astype(BF16)
    vm_ref[...] = _dot(h_hi, w_ref[:, c0 + 2 * MLSTM_WIDTH:c0 + 3 * MLSTM_WIDTH]).astype(BF16)
    om_ref[...] = _dot(h_hi, w_ref[:, c0 + 3 * MLSTM_WIDTH:c0 + 4 * MLSTM_WIDTH])
    g_ref[...] = _dot_hi_lo(h_hi, h_lo, wg_ref) + bg_ref[...]


def _project(x2d, mod3, row_of_tile, n1, w_main, wg_cat, bg, qg, kg, cos_t, sin_t, *, rope, emit_kv, tm=256):
    t = x2d.shape[0]
    n_pos = cos_t.shape[0] // tm
    row = lambda i: (i, 0)
    in_specs = [pl.BlockSpec((tm, D_MODEL), row),
                pl.BlockSpec((1, 1, N_MOD * D_MODEL), lambda i: (row_of_tile(i), 0, 0)),
                _const_spec((1, D_MODEL)),
                _const_spec((D_MODEL, MAIN_WIDTH)),
                _const_spec((D_MODEL, 2 * LANES)), _const_spec((1, LANES)),
                _const_spec((1, ATT_HEAD_DIM)), _const_spec((1, ATT_HEAD_DIM)),
                pl.BlockSpec((tm, ATT_HEAD_DIM), lambda i: (i % n_pos, 0)),
                pl.BlockSpec((tm, ATT_HEAD_DIM), lambda i: (i % n_pos, 0))]
    out_shape = [jax.ShapeDtypeStruct((t, ATT_WIDTH), BF16), jax.ShapeDtypeStruct((t, KV_WIDTH), BF16),
                 jax.ShapeDtypeStruct((t, KV_WIDTH), BF16), jax.ShapeDtypeStruct((t, MLSTM_WIDTH), BF16),
                 jax.ShapeDtypeStruct((t // CHUNK, MLSTM_WIDTH, CHUNK), BF16),
                 jax.ShapeDtypeStruct((t, MLSTM_WIDTH), BF16), jax.ShapeDtypeStruct((t, MLSTM_WIDTH), F32),
                 jax.ShapeDtypeStruct((t, LANES), F32)]
    out_specs = [pl.BlockSpec((tm, ATT_WIDTH), row), pl.BlockSpec((tm, KV_WIDTH), row),
                 pl.BlockSpec((tm, KV_WIDTH), row), pl.BlockSpec((tm, MLSTM_WIDTH), row),
                 pl.BlockSpec((tm // CHUNK, MLSTM_WIDTH, CHUNK), lambda i: (i, 0, 0)),
                 pl.BlockSpec((tm, MLSTM_WIDTH), row), pl.BlockSpec((tm, MLSTM_WIDTH), row),
                 pl.BlockSpec((tm, LANES), row)]
    if emit_kv:
        out_shape += [jax.ShapeDtypeStruct((t, KV_WIDTH), F32)] * 2
        out_specs += [pl.BlockSpec((tm, KV_WIDTH), row)] * 2
    return pl.pallas_call(
        functools.partial(_proj_kernel, rope=rope, tm=tm),
        grid=(t // tm,), in_specs=in_specs, out_specs=out_specs, out_shape=out_shape,
        compiler_params=_cparams("parallel"), name="in_proj",
    )(x2d, mod3, n1, w_main, wg_cat, bg, qg, kg, cos_t, sin_t)


def _lane_scan(x, op, fill, is_fwd, lane):
    s = 1
    while s < CHUNK:
        from_left = jnp.where(lane >= s, pltpu.roll(x, s, 1), fill)
        from_right = jnp.where(lane < CHUNK - s, pltpu.roll(x, CHUNK - s, 1), fill)
        x = op(x, jnp.where(is_fwd, from_left, from_right))
        s *= 2
    return x


def _scan_kernel(g_ref, row_ref, col_ref, *, n_chunks):
    lane = lax.broadcasted_iota(jnp.int32, (N_CHAINS, CHUNK), 1)
    is_fwd = lax.broadcasted_iota(jnp.int32, (N_CHAINS, CHUNK), 0) < MLSTM_HEADS
    for cc in range(n_chunks):
        gt = g_ref[cc * CHUNK:(cc + 1) * CHUNK, :].T
        li, f = gt[:N_CHAINS], gt[N_CHAINS:2 * N_CHAINS]
        lf = jnp.minimum(f, 0.0) - jnp.log(1.0 + jnp.exp(-jnp.abs(f)))
        b = _lane_scan(lf, jnp.add, 0.0, is_fwd, lane)
        a = li - b
        run_max = _lane_scan(a, jnp.maximum, -jnp.inf, is_fwd, lane)
        all_max = jnp.broadcast_to(jnp.max(a, axis=1, keepdims=True), a.shape)
        last = jnp.where(is_fwd, CHUNK - 1, 0)
        total = jnp.broadcast_to(jnp.sum(jnp.where(lane == last, b, 0.0), axis=1, keepdims=True), a.shape)
        rows = jnp.concatenate([a, b, run_max, all_max, total], axis=0)
        row_ref[cc] = rows
        padded = jnp.concatenate([rows, jnp.zeros((LANES - N_SCAN_ROWS, CHUNK), F32)], axis=0)
        col_ref[cc * CHUNK:(cc + 1) * CHUNK, :] = padded.T


def _gate_scans(g, tb=1024):
    t = g.shape[0]
    tb = min(tb, t)
    return pl.pallas_call(
        functools.partial(_scan_kernel, n_chunks=tb // CHUNK),
        grid=(t // tb,),
        in_specs=[pl.BlockSpec((tb, LANES), lambda i: (i, 0))],
        out_specs=[pl.BlockSpec((tb // CHUNK, N_SCAN_ROWS, CHUNK), lambda i: (i, 0, 0)),
                   pl.BlockSpec((tb, LANES), lambda i: (i, 0))],
        out_shape=[jax.ShapeDtypeStruct((t // CHUNK, N_SCAN_ROWS, CHUNK), F32),
                   jax.ShapeDtypeStruct((t, LANES), F32)],
        compiler_params=_cparams("parallel"), name="gate_scans",
    )(g)


def _mlstm_kernel(*refs, n_chunks, has_state, emit_state):
    it = iter(refs)
    q_ref, kt_ref, v_ref, row_ref, col_ref = [next(it) for _ in range(5)]
    c0_ref, m0_ref = (next(it), next(it)) if has_state else (None, None)
    o_ref = next(it)
    c_out, n_out, m_out = (next(it), next(it), next(it)) if emit_state else (None, None, None)
    caug, hb = next(it), next(it)

    b = pl.program_id(0)
    if has_state:
        caug[...] = c0_ref[0]
        m_init = tuple(jnp.full((1, 1), m0_ref[b * N_CHAINS + r], F32) for r in range(N_CHAINS))
    else:
        caug[...] = jnp.zeros(caug.shape, F32)
        m_init = tuple(jnp.zeros((1, 1), F32) for _ in range(N_CHAINS))

    sub = lax.broadcasted_iota(jnp.int32, (CHUNK, CHUNK), 0)
    lane = lax.broadcasted_iota(jnp.int32, (CHUNK, CHUNK), 1)
    ones_col = (lax.broadcasted_iota(jnp.int32, (CHUNK, LANES), 1) == 0).astype(BF16)

    def chain(r, cc, m_prev):
        d, hd = divmod(r, MLSTM_HEADS)
        t0 = pl.multiple_of(cc * CHUNK, CHUNK)
        hs = slice(hd * MLSTM_HEAD_DIM, (hd + 1) * MLSTM_HEAD_DIM)
        rows = row_ref[cc]
        cols = col_ref[pl.ds(t0, CHUNK), :]
        row = lambda k: rows[k * N_CHAINS + r:k * N_CHAINS + r + 1, :]
        col = lambda k: cols[:, k * N_CHAINS + r:k * N_CHAINS + r + 1]
        q = q_ref[pl.ds(t0, CHUNK), hs]
        kt = kt_ref[cc, hs, :]
        vaug = jnp.concatenate([v_ref[pl.ds(t0, CHUNK), hs], ones_col], axis=1)

        m_col = jnp.maximum(m_prev, col(2))
        keep = (lane <= sub) if d == 0 else (lane >= sub)
        w = jnp.where(keep, jnp.exp(row(0) - m_col), 0.0)
        w_inter = jnp.exp(m_prev - m_col)
        p = (_dot(q, kt) * w).astype(BF16)
        intra = _dot(p, vaug)
        inter = _dot(q, caug[r].astype(BF16))
        num = intra[:, :MLSTM_HEAD_DIM] + w_inter * inter[:, :MLSTM_HEAD_DIM]
        den = intra[:, MLSTM_HEAD_DIM:MLSTM_HEAD_DIM + 1] + w_inter * inter[:, MLSTM_HEAD_DIM:MLSTM_HEAD_DIM + 1]
        h_out = num / jnp.maximum(jnp.abs(den), jnp.exp(-(col(1) + m_col)))

        m_last = jnp.maximum(m_prev, row(3))
        kw = (kt.astype(F32) * jnp.exp(row(0) - m_last)).astype(BF16)
        caug[r] = jnp.exp(m_prev - m_last[:, :1]) * caug[r] + _dot(kw, vaug)
        return h_out, (row(4) + m_last)[:, :1]

    def body(c, ms):
        new = []
        for r in range(N_CHAINS):
            cc = c if r < MLSTM_HEADS else n_chunks - 1 - c
            h_out, m_new = chain(r, cc, ms[r])
            hd = r % MLSTM_HEADS
            dst = o_ref if r < MLSTM_HEADS else hb
            dst[pl.ds(pl.multiple_of(cc * CHUNK, CHUNK), CHUNK), hd * MLSTM_HEAD_DIM:(hd + 1) * MLSTM_HEAD_DIM] = h_out
            new.append(m_new)
        return tuple(new)

    ms = lax.fori_loop(0, n_chunks, body, m_init)
    o_ref[...] += hb[...]
    if emit_state:
        for r in range(N_CHAINS):
            c_out[0, r] = caug[r, :, :MLSTM_HEAD_DIM]
            n_out[0, r] = caug[r, :, MLSTM_HEAD_DIM:].T[:1, :]
            m_out[0, r] = jnp.broadcast_to(ms[r], (1, LANES))


def _mlstm(qm, kmt, vm, rowq, colq, n_batch, seq, state=None, emit_state=False):
    nc = seq // CHUNK
    mode = dict(pipeline_mode=pl.Buffered(1)) if seq * MLSTM_WIDTH * 4 > (2 << 20) else {}
    in_specs = [pl.BlockSpec((seq, MLSTM_WIDTH), lambda b: (b, 0), **mode),
                pl.BlockSpec((nc, MLSTM_WIDTH, CHUNK), lambda b: (b, 0, 0), **mode),
                pl.BlockSpec((seq, MLSTM_WIDTH), lambda b: (b, 0), **mode),
                pl.BlockSpec((nc, N_SCAN_ROWS, CHUNK), lambda b: (b, 0, 0), **mode),
                pl.BlockSpec((seq, LANES), lambda b: (b, 0), **mode)]
    args = [qm, kmt, vm, rowq, colq]
    if state is not None:
        in_specs += [pl.BlockSpec((1, N_CHAINS, MLSTM_HEAD_DIM, AUG), lambda b: (b, 0, 0, 0), **mode),
                     pl.BlockSpec(memory_space=pltpu.SMEM)]
        args += list(state)
    out_shape = [jax.ShapeDtypeStruct((n_batch * seq, MLSTM_WIDTH), F32)]
    out_specs = [pl.BlockSpec((seq, MLSTM_WIDTH), lambda b: (b, 0))]
    if emit_state:
        out_shape += [jax.ShapeDtypeStruct((n_batch, N_CHAINS, MLSTM_HEAD_DIM, MLSTM_HEAD_DIM), F32),
                      jax.ShapeDtypeStruct((n_batch, N_CHAINS, 1, MLSTM_HEAD_DIM), F32),
                      jax.ShapeDtypeStruct((n_batch, N_CHAINS, 1, LANES), F32)]
        out_specs += [pl.BlockSpec((1, N_CHAINS, MLSTM_HEAD_DIM, MLSTM_HEAD_DIM), lambda b: (b, 0, 0, 0)),
                      pl.BlockSpec((1, N_CHAINS, 1, MLSTM_HEAD_DIM), lambda b: (b, 0, 0, 0)),
                      pl.BlockSpec((1, N_CHAINS, 1, LANES), lambda b: (b, 0, 0, 0))]
    return pl.pallas_call(
        functools.partial(_mlstm_kernel, n_chunks=nc, has_state=state is not None, emit_state=emit_state),
        grid=(n_batch,), in_specs=in_specs, out_specs=out_specs, out_shape=out_shape,
        scratch_shapes=[pltpu.VMEM((N_CHAINS, MLSTM_HEAD_DIM, AUG), F32), pltpu.VMEM((seq, MLSTM_WIDTH), F32)],
        compiler_params=_cparams("parallel"), name="mlstm",
    )(*args)


def _attn_kernel(*refs, tq, has_cache):
    if has_cache:
        q_ref, k_ref, v_ref, ck_ref, cv_ref, o_ref = refs
    else:
        q_ref, k_ref, v_ref, o_ref = refs
    q = q_ref[...]
    qs = jnp.concatenate([q[:, g * ATT_HEAD_DIM:(g + 1) * ATT_HEAD_DIM] for g in range(ATT_GROUP)], axis=0)
    nt = (((1,), (1,)), ((), ()))
    seq = k_ref.shape[0]
    kc = min(seq, ATT_KEY_CHUNK)
    chunks = [(k_ref, v_ref, c * kc) for c in range(seq // kc)]
    if has_cache:
        chunks.insert(0, (ck_ref, cv_ref, None))
    m = l = o = None
    for kr, vr, start in chunks:
        kk, vv = (kr[...], vr[...]) if start is None else (kr[start:start + kc, :], vr[start:start + kc, :])
        s = lax.dot_general(qs, kk.astype(BF16), nt, preferred_element_type=F32)
        mc = jnp.max(s, axis=-1, keepdims=True)
        if m is None:
            m = mc
            p = jnp.exp2(s - m)
            l = jnp.sum(p, axis=-1, keepdims=True)
            o = _dot(p.astype(BF16), vv.astype(BF16))
        else:
            m_new = jnp.maximum(m, mc)
            alpha = jnp.exp2(m - m_new)
            p = jnp.exp2(s - m_new)
            l = alpha * l + jnp.sum(p, axis=-1, keepdims=True)
            o = alpha * o + _dot(p.astype(BF16), vv.astype(BF16))
            m = m_new
    o = o / l
    for g in range(ATT_GROUP):
        o_ref[:, g * ATT_HEAD_DIM:(g + 1) * ATT_HEAD_DIM] = o[g * tq:(g + 1) * tq].astype(BF16)


def _attention(q, k, v, n_batch, seq, cache=None, tq=256):
    nqb = seq // tq
    gw = ATT_GROUP * ATT_HEAD_DIM
    in_specs = [pl.BlockSpec((tq, gw), lambda b, h, i: (b * nqb + i, h)),
                pl.BlockSpec((seq, ATT_HEAD_DIM), lambda b, h, i: (b, h)),
                pl.BlockSpec((seq, ATT_HEAD_DIM), lambda b, h, i: (b, h))]
    args = [q, k, v]
    if cache is not None:
        past = cache[0].shape[0] // n_batch
        in_specs += [pl.BlockSpec((past, ATT_HEAD_DIM), lambda b, h, i: (b, h))] * 2
        args += list(cache)
    return pl.pallas_call(
        functools.partial(_attn_kernel, tq=tq, has_cache=cache is not None),
        grid=(n_batch, ATT_KV_HEADS, nqb), in_specs=in_specs,
        out_specs=pl.BlockSpec((tq, gw), lambda b, h, i: (b * nqb + i, h)),
        out_shape=jax.ShapeDtypeStruct((n_batch * seq, ATT_WIDTH), BF16),
        compiler_params=_cparams("parallel", "parallel", "parallel"), name="attention",
    )(*args)


def _mix_kernel(attn_ref, hm_ref, om_ref, x_ref, mod_ref, mg_ref, wo_ref, n2_ref, rw_ref, rb_ref, *rest, sub):
    for r0 in range(0, x_ref.shape[0], sub):
        _mix_rows(slice(r0, r0 + sub), attn_ref, hm_ref, om_ref, x_ref, mod_ref, mg_ref, wo_ref, n2_ref, rw_ref,
                  rb_ref, *rest[-3:])


def _mix_rows(rows, attn_ref, hm_ref, om_ref, x_ref, mod_ref, mg_ref, wo_ref, n2_ref, rw_ref, rb_ref,
              x1_ref, h2_ref, route_ref):
    mod = mod_ref[0]
    gate1 = mod[:, 2 * D_MODEL:3 * D_MODEL]
    shift2, scale2 = mod[:, 3 * D_MODEL:4 * D_MODEL], mod[:, 4 * D_MODEL:5 * D_MODEL]
    hm = hm_ref[rows, :]
    mg = mg_ref[...]
    parts = []
    for hd in range(MLSTM_HEADS):
        sl = slice(hd * MLSTM_HEAD_DIM, (hd + 1) * MLSTM_HEAD_DIM)
        parts.append(_rms(hm[:, sl], mg[:, sl]))
    om = om_ref[rows, :]
    hmg = jnp.concatenate(parts, axis=1) * (1.0 / (1.0 + jnp.exp(-om)))
    y = _dot(attn_ref[rows, :], wo_ref[:ATT_WIDTH, :]) + _dot(hmg.astype(BF16), wo_ref[ATT_WIDTH:, :])
    x1 = x_ref[rows, :] + gate1 * y
    x1_ref[rows, :] = x1
    h2 = _rms(x1, n2_ref[...]) * (1.0 + scale2) + shift2
    h2_hi, h2_lo = _split_hi_lo(h2)
    h2_ref[rows, :] = h2

    lg = _dot_hi_lo(h2_hi, h2_lo, rw_ref) + rb_ref[...]
    lane = lax.broadcasted_iota(jnp.int32, lg.shape, 1).astype(F32)
    neg = -jnp.inf
    first = lambda hit: jnp.min(jnp.where(hit, lane, float(LANES)), axis=-1, keepdims=True)
    gl = jnp.where(lane < N_GROUPS, lg, neg)
    gmax = jnp.max(gl, axis=-1, keepdims=True)
    grp = first(gl == gmax)
    p_grp = 1.0 / jnp.sum(jnp.exp(gl - gmax), axis=-1, keepdims=True)
    lo = ROUTER_LANE0 + grp * EXPERTS_PER_GROUP
    el = jnp.where((lane >= lo) & (lane < lo + EXPERTS_PER_GROUP), lg, neg)
    m1 = jnp.max(el, axis=-1, keepdims=True)
    i1 = first(el == m1)
    el2 = jnp.where(lane == i1, neg, el)
    m2 = jnp.max(el2, axis=-1, keepdims=True)
    i2 = first(el2 == m2)
    r = jnp.exp(m2 - m1)
    w1 = p_grp / (1.0 + r)
    w2 = w1 * r
    route_ref[rows, :] = jnp.where(lane == 0.0, i1 - ROUTER_LANE0, jnp.where(lane == 1.0, i2 - ROUTER_LANE0,
                                   jnp.where(lane == 2.0, w1, jnp.where(lane == 3.0, w2, 0.0))))


def _mix_out(attn, hm, om, x2d, mod3, row_of_tile, mg, w_out, n2, rw_cat, rb, h2_all, tile0, t_total, tm=512):
    t = x2d.shape[0]
    row = lambda i: (i, 0)
    in_specs = [pl.BlockSpec((tm, ATT_WIDTH), row), pl.BlockSpec((tm, MLSTM_WIDTH), row),
                pl.BlockSpec((tm, MLSTM_WIDTH), row), pl.BlockSpec((tm, D_MODEL), row),
                pl.BlockSpec((1, 1, N_MOD * D_MODEL), lambda i: (row_of_tile(i), 0, 0)),
                _const_spec((1, MLSTM_WIDTH)), _const_spec((D_MODEL, D_MODEL)), _const_spec((1, D_MODEL)),
                _const_spec((D_MODEL, 2 * LANES)), _const_spec((1, LANES))]
    args = [attn, hm, om, x2d, mod3, mg, w_out, n2, rw_cat, rb]
    aliases = {}
    if h2_all is not None:
        aliases = {len(args): 1}
        in_specs.append(pl.BlockSpec(memory_space=pl.ANY))
        args.append(h2_all)
    return pl.pallas_call(
        functools.partial(_mix_kernel, sub=256), grid=(t // tm,), in_specs=in_specs,
        out_specs=[pl.BlockSpec((tm, D_MODEL), row), pl.BlockSpec((tm, D_MODEL), lambda i: (tile0 + i, 0)),
                   pl.BlockSpec((tm, LANES), row)],
        out_shape=[jax.ShapeDtypeStruct((t, D_MODEL), F32), jax.ShapeDtypeStruct((t_total, D_MODEL), F32),
                   jax.ShapeDtypeStruct((t, LANES), F32)],
        input_output_aliases=aliases,
        compiler_params=_cparams("parallel"), name="mix_out",
    )(*args)


def _rank_kernel(route_ref, rank_ref, cnt_ref, run_ref, tri_ref):
    tr = route_ref.shape[0]

    @pl.when(pl.program_id(0) == 0)
    def _():
        run_ref[...] = jnp.zeros(run_ref.shape, F32)
        tri_ref[...] = (lax.broadcasted_iota(jnp.int32, (tr, tr), 1)
                        < lax.broadcasted_iota(jnp.int32, (tr, tr), 0)).astype(BF16)

    route = route_ref[...]
    lane = lax.broadcasted_iota(jnp.int32, route.shape, 1).astype(F32)
    hit1, hit2 = lane == route[:, 0:1], lane == route[:, 1:2]
    onehot = jnp.where(hit1, 1.0, jnp.where(hit2, 1.0, 0.0))
    before = _dot(tri_ref[...], onehot.astype(BF16)) + run_ref[0:1, :]
    r1 = jnp.sum(jnp.where(hit1, before, 0.0), axis=-1, keepdims=True)
    r2 = jnp.sum(jnp.where(hit2, before, 0.0), axis=-1, keepdims=True)
    rank_ref[...] = jnp.where(lane == 0.0, r1, jnp.where(lane == 1.0, r2, 0.0))
    run_ref[...] = run_ref[...] + jnp.sum(onehot, axis=0, keepdims=True)
    cnt_ref[...] = run_ref[...]


def _ranks(route, tr=512):
    t = route.shape[0]
    return pl.pallas_call(
        _rank_kernel, grid=(t // tr,),
        in_specs=[pl.BlockSpec((tr, LANES), lambda i: (i, 0))],
        out_specs=[pl.BlockSpec((tr, LANES), lambda i: (i, 0)), pl.BlockSpec((8, LANES), lambda i: (0, 0))],
        out_shape=[jax.ShapeDtypeStruct((t, LANES), F32), jax.ShapeDtypeStruct((8, LANES), F32)],
        scratch_shapes=[pltpu.VMEM((8, LANES), F32), pltpu.VMEM((tr, tr), BF16)],
        compiler_params=_cparams("arbitrary"), name="expert_ranks",
    )(route)


def _routing_plan(route, ranks, counts):
    t = route.shape[0]
    n_tiles = 2 * t // TM_MOE + N_EXPERTS
    e = route[:, :2].T.astype(jnp.int32)
    rank = ranks[:, :2].T.astype(jnp.int32)
    cnt = counts[0, :N_EXPERTS].astype(jnp.int32)
    padded = (cnt + TM_MOE - 1) // TM_MOE * TM_MOE
    ends = jnp.cumsum(padded)
    starts = ends - padded
    first_row = sum(jnp.where(e == k, starts[k], 0) for k in range(N_EXPERTS))
    pos = (first_row + rank).reshape(-1)
    n_used = (ends[-1:] // TM_MOE).astype(jnp.int32)
    tile_start = jnp.arange(n_tiles, dtype=jnp.int32) * TM_MOE
    tile_expert = jnp.sum((ends[None, :] <= tile_start[:, None]).astype(jnp.int32), axis=1)
    tile_expert = jnp.minimum(tile_expert, tile_expert[n_used[0] - 1]).astype(jnp.int32)
    tails = jnp.where(padded > 0, ends - TM_MOE, -1).astype(jnp.int32)
    return pos, tile_expert, n_used, tails, n_tiles * TM_MOE


def _row_copies(pos_ref, n_pairs, tok0, n_rows, make_copy):
    def body(r, carry):
        for k in range(2):
            make_copy(k, r, pos_ref[k * n_pairs + tok0 + r]).start()
        return carry
    lax.fori_loop(0, n_rows, body, 0, unroll=8)


def _dispatch_kernel(pos_ref, tail_ref, h_ref, xs_ref, zero_ref, sem, *, td, n_tokens):
    i = pl.program_id(0)

    @pl.when(i == 0)
    def _():
        zero_ref[...] = jnp.zeros(zero_ref.shape, F32)
        tail_copy = lambda e: pltpu.make_async_copy(
            zero_ref, xs_ref.at[pl.ds(pl.multiple_of(tail_ref[e], TM_MOE), TM_MOE)], sem)
        for e in range(N_EXPERTS):
            pl.when(tail_ref[e] >= 0)(lambda e=e: tail_copy(e).start())
        for e in range(N_EXPERTS):
            pl.when(tail_ref[e] >= 0)(lambda e=e: tail_copy(e).wait())

    _row_copies(pos_ref, n_tokens, i * td, td,
                lambda k, r, p: pltpu.make_async_copy(h_ref.at[pl.ds(r, 1)], xs_ref.at[pl.ds(p, 1)], sem))
    for _ in range(2):
        pltpu.make_async_copy(h_ref, xs_ref.at[pl.ds(0, td)], sem).wait()


def _dispatch(pos, tails, h2, n_rows, td=256):
    t = h2.shape[0]
    return pl.pallas_call(
        functools.partial(_dispatch_kernel, td=td, n_tokens=t),
        grid_spec=pltpu.PrefetchScalarGridSpec(
            num_scalar_prefetch=2, grid=(t // td,),
            in_specs=[pl.BlockSpec((td, D_MODEL), lambda i, *_: (i, 0))],
            out_specs=pl.BlockSpec(memory_space=pl.ANY),
            scratch_shapes=[pltpu.VMEM((TM_MOE, D_MODEL), F32), pltpu.SemaphoreType.DMA]),
        out_shape=jax.ShapeDtypeStruct((n_rows, D_MODEL), F32),
        compiler_params=_cparams("arbitrary"), name="dispatch",
    )(pos, tails, h2)


def _expert_kernel(te_ref, nu_ref, xs_ref, wg_ref, wu_ref, wd_ref, ys_ref, wgb, wub, wdb):
    j = pl.program_id(0)

    @pl.when(j < nu_ref[0])
    def _():
        @pl.when((j == 0) | (te_ref[j] != te_ref[jnp.maximum(j - 1, 0)]))
        def _():
            wgb[...] = wg_ref[0].astype(BF16)
            wub[...] = wu_ref[0].astype(BF16)
            wdb[...] = wd_ref[0].astype(BF16)

        x = xs_ref[...].astype(BF16)
        g = _dot(x, wgb[...])
        u = _dot(x, wub[...])
        a = (g / (1.0 + jnp.exp(-g))) * u
        ys_ref[...] = _dot(a.astype(BF16), wdb[...])


def _experts(tile_expert, n_used, xs, wg, wu, wd):
    n_tiles = xs.shape[0] // TM_MOE
    tile = lambda j, te, nu: (jnp.minimum(j, nu[0] - 1), 0)
    wspec = lambda shape: pl.BlockSpec((1,) + shape, lambda j, te, nu: (te[j], 0, 0))
    return pl.pallas_call(
        _expert_kernel,
        grid_spec=pltpu.PrefetchScalarGridSpec(
            num_scalar_prefetch=2, grid=(n_tiles,),
            in_specs=[pl.BlockSpec((TM_MOE, D_MODEL), tile), wspec((D_MODEL, D_EXPERT)), wspec((D_MODEL, D_EXPERT)),
                      wspec((D_EXPERT, D_MODEL))],
            out_specs=pl.BlockSpec((TM_MOE, D_MODEL), tile),
            scratch_shapes=[pltpu.VMEM((D_MODEL, D_EXPERT), BF16), pltpu.VMEM((D_MODEL, D_EXPERT), BF16),
                            pltpu.VMEM((D_EXPERT, D_MODEL), BF16)]),
        out_shape=jax.ShapeDtypeStruct(xs.shape, F32),
        compiler_params=_cparams("arbitrary"), name="experts",
    )(tile_expert, n_used, xs, wg, wu, wd)


def _combine_kernel(pos_ref, x1_ref, route_ref, mod_ref, ys_ref, o_ref, ybuf, sem, *, tc, n_tokens, tok0):
    _row_copies(pos_ref, n_tokens, tok0 + pl.program_id(0) * tc, tc,
                lambda k, r, p: pltpu.make_async_copy(ys_ref.at[pl.ds(p, 1)], ybuf.at[k, pl.ds(r, 1)], sem))
    for k in range(2):
        pltpu.make_async_copy(ys_ref.at[pl.ds(0, tc)], ybuf.at[k], sem).wait()
    route = route_ref[...]
    gate2 = mod_ref[0][:, 5 * D_MODEL:]
    o_ref[...] = x1_ref[...] + gate2 * (route[:, 2:3] * ybuf[0] + route[:, 3:4] * ybuf[1])


def _combine(pos, x1, route, mod3, row_of_tile, ys, tok0, n_tokens, tc=256):
    t = x1.shape[0]
    row = lambda i, *_: (i, 0)
    return pl.pallas_call(
        functools.partial(_combine_kernel, tc=tc, n_tokens=n_tokens, tok0=tok0),
        grid_spec=pltpu.PrefetchScalarGridSpec(
            num_scalar_prefetch=1, grid=(t // tc,),
            in_specs=[pl.BlockSpec((tc, D_MODEL), row), pl.BlockSpec((tc, LANES), row),
                      pl.BlockSpec((1, 1, N_MOD * D_MODEL), lambda i, *_: (row_of_tile(i), 0, 0)),
                      pl.BlockSpec(memory_space=pl.ANY)],
            out_specs=pl.BlockSpec((tc, D_MODEL), row),
            scratch_shapes=[pltpu.VMEM((2, tc, D_MODEL), F32), pltpu.SemaphoreType.DMA]),
        out_shape=jax.ShapeDtypeStruct((t, D_MODEL), F32),
        compiler_params=_cparams("arbitrary"), name="combine",
    )(pos, x1, route, mod3, ys)


def _rope_tables(seq):
    pos = np.arange(seq)
    n_freq = ATT_HEAD_DIM // 4
    inv = ROPE_THETA ** (-np.arange(n_freq, dtype=np.float32) / n_freq)
    ang = np.concatenate([(pos // GRID_W).astype(np.float32)[:, None] * inv,
                          (pos % GRID_W).astype(np.float32)[:, None] * inv], axis=-1).astype(np.float32)
    ang = jnp.asarray(ang)
    cos, sin = jnp.cos(ang), jnp.sin(ang)
    cos_t = jnp.repeat(cos, 2, axis=1)
    sin_t = jnp.stack([-sin, sin], axis=-1).reshape(seq, ATT_HEAD_DIM)
    return cos_t, sin_t


def _pad_lanes(a):
    return jnp.pad(a, ((0, 0), (0, LANES - a.shape[1])))


def _layer(x2d, n_batch, seq, mod3, row_of_tile, lw, *, rope_tabs, cache, state, emit, h2_all, tile0, t_total):
    (n1, n2, w_main, wg_cat, bg, qg, kg, mg, w_out, rw_cat, rb) = lw
    cos_t, sin_t = rope_tabs
    outs = _project(x2d, mod3, row_of_tile(256), n1, w_main, wg_cat, bg, qg, kg, cos_t, sin_t,
                    rope=cache is not None, emit_kv=emit)
    q, k, v, qm, kmt, vm, om, g = outs[:8]
    rowq, colq = _gate_scans(g)
    attn = _attention(q, k, v, n_batch, seq, cache=cache)
    ml = _mlstm(qm, kmt, vm, rowq, colq, n_batch, seq, state=state, emit_state=emit)
    mixed = _mix_out(attn, ml[0], om, x2d, mod3, row_of_tile(MIX_TILE), mg, w_out, n2, rw_cat, rb,
                     h2_all, tile0, t_total, tm=MIX_TILE)
    return mixed, outs[8:], ml[1:]


def kernel(x_prompt, x_sample, cache_k, cache_v, state_C, state_n, state_m, c, c_ctx, mod_w, mod_b, norm1_g, norm2_g,
           w_in, b_gates, q_norm_g, k_norm_g, mlstm_norm_g, w_out, router_group_w, router_group_b, router_expert_w,
           router_expert_b, expert_w_gate, expert_w_up, expert_w_down):
    assert mod_w.shape[0] == 1, "single-layer stack"
    n_ctx, s_ctx, _ = x_prompt.shape
    n_lat, s_lat, _ = x_sample.shape
    t_ctx, t_lat = n_ctx * s_ctx, n_lat * s_lat
    t_all = t_ctx + t_lat
    ctx_row = n_lat

    cond = jnp.concatenate([c, c_ctx[None], jnp.zeros((8 - n_lat - 1, D_MODEL), F32)], axis=0)
    mod3 = _modulation(cond, mod_w[0], mod_b[0][None]).reshape(8, 1, N_MOD * D_MODEL)

    perm = np.concatenate([np.arange(0, 4), np.arange(8, 12), np.arange(4, 8), np.arange(12, 16)])
    wg_cat = _hi_lo_cat(_pad_lanes(w_in[0][:, MAIN_WIDTH:][:, perm]))
    rw = jnp.concatenate([router_group_w[0], jnp.moveaxis(router_expert_w[0], 0, 1).reshape(D_MODEL, N_EXPERTS)], axis=1)
    rb = _pad_lanes(jnp.concatenate([router_group_b[0], router_expert_b[0].reshape(-1)])[None])
    lw = (norm1_g, norm2_g, _cast_bf16(w_in, MAIN_WIDTH), wg_cat, _pad_lanes(b_gates[0][perm][None]),
          q_norm_g, k_norm_g, mlstm_norm_g, _cast_bf16(w_out, D_MODEL), _hi_lo_cat(_pad_lanes(rw)), rb)
    rope_tabs = _rope_tables(s_lat)

    ctx_rows = lambda tm: (lambda i: ctx_row)
    lat_rows = lambda tm: (lambda i: i // (s_lat // tm))
    (x1p, h2_all, routep), (ka, va), (s_c, s_n, s_m) = _layer(
        x_prompt.reshape(t_ctx, D_MODEL), n_ctx, s_ctx, mod3, ctx_rows, lw,
        rope_tabs=rope_tabs, cache=None, state=None, emit=True, h2_all=None, tile0=0, t_total=t_all)

    caug0 = jnp.concatenate([state_C[:, 0], state_n[:, 0][..., None],
                             jnp.zeros(state_n[:, 0].shape + (LANES - 1,), F32)], axis=-1)
    caug0 = caug0.reshape(n_lat, N_CHAINS, MLSTM_HEAD_DIM, AUG)
    past = cache_k.shape[2]
    cache = (cache_k[:, 0].reshape(n_lat * past, KV_WIDTH), cache_v[:, 0].reshape(n_lat * past, KV_WIDTH))
    (x1s, h2_all, routes), _, _ = _layer(
        x_sample.reshape(t_lat, D_MODEL), n_lat, s_lat, mod3, lat_rows, lw,
        rope_tabs=rope_tabs, cache=cache, state=(caug0, state_m[:, 0].reshape(-1)), emit=False,
        h2_all=h2_all, tile0=t_ctx // MIX_TILE, t_total=t_all)

    route = jnp.concatenate([routep, routes], axis=0)
    ranks, counts = _ranks(route)
    pos, tile_expert, n_used, tails, n_rows = _routing_plan(route, ranks, counts)
    xs = _dispatch(pos, tails, h2_all, n_rows)
    y_sorted = _experts(tile_expert, n_used, xs, expert_w_gate[0], expert_w_up[0], expert_w_down[0])
    yp = _combine(pos, x1p, routep, mod3, ctx_rows(256), y_sorted, 0, t_all)
    ys = _combine(pos, x1s, routes, mod3, lat_rows(256), y_sorted, t_ctx, t_all)

    kv_shape = (n_ctx, 1, s_ctx, ATT_KV_HEADS, ATT_HEAD_DIM)
    return (yp.reshape(x_prompt.shape), ys.reshape(x_sample.shape), ka.reshape(kv_shape), va.reshape(kv_shape),
            s_c.reshape(n_ctx, 1, 2, MLSTM_HEADS, MLSTM_HEAD_DIM, MLSTM_HEAD_DIM),
            s_n.reshape(n_ctx, 1, 2, MLSTM_HEADS, MLSTM_HEAD_DIM), s_m[..., 0, 0].reshape(n_ctx, 1, 2, MLSTM_HEADS))
```

```python
import functools

import numpy as np
import jax
import jax.numpy as jnp
from jax import lax
from jax.experimental import pallas as pl
from jax.experimental.pallas import tpu as pltpu

F32 = jnp.float32
BF16 = jnp.bfloat16

D_MODEL = 2048
GRID_W = 64
ATT_HEADS = 8
ATT_KV_HEADS = 2
ATT_HEAD_DIM = 128
ATT_GROUP = ATT_HEADS // ATT_KV_HEADS
ATT_WIDTH = ATT_HEADS * ATT_HEAD_DIM
KV_WIDTH = ATT_KV_HEADS * ATT_HEAD_DIM
ROPE_THETA = 10000.0
MLSTM_HEADS = 4
MLSTM_HEAD_DIM = 256
MLSTM_WIDTH = MLSTM_HEADS * MLSTM_HEAD_DIM
CHUNK = 256
N_GATES = 4 * MLSTM_HEADS
N_CHAINS = 2 * MLSTM_HEADS
MAIN_WIDTH = ATT_WIDTH + 2 * KV_WIDTH + 4 * MLSTM_WIDTH
N_GROUPS = 4
EXPERTS_PER_GROUP = 8
N_EXPERTS = N_GROUPS * EXPERTS_PER_GROUP
D_EXPERT = 512
N_MOD = 6
TM_MOE = 256
MIX_TILE = 512
ATT_KEY_CHUNK = 512
Q_SCALE = ATT_HEAD_DIM ** -0.5 * float(np.log2(np.e))
EPS = 1e-6

LANES = 128
AUG = MLSTM_HEAD_DIM + LANES
N_SCAN_ROWS = 5 * N_CHAINS
ROUTER_LANE0 = N_GROUPS
VMEM_LIMIT = 56 * 1024 * 1024


def _cparams(*sem):
    return pltpu.CompilerParams(dimension_semantics=sem, vmem_limit_bytes=VMEM_LIMIT)


def _const_spec(shape):
    nd = len(shape)
    return pl.BlockSpec(shape, lambda *_: (0,) * nd, pipeline_mode=pl.Buffered(1))


def _split_hi_lo(x):
    hi = x.astype(BF16)
    lo = (x - hi.astype(F32)).astype(BF16)
    return hi, lo


def _dot(a, b):
    return jnp.dot(a, b, preferred_element_type=F32)


def _hi_lo_cat(w):
    return jnp.concatenate(_split_hi_lo(w), axis=1)


def _dot_hi_lo(a_hi, a_lo, w_ref):
    r = _dot(a_hi, w_ref[...])
    return r[:, :LANES] + r[:, LANES:] + _dot(a_lo, w_ref[:, :LANES])


def _rms(x, g):
    return x * lax.rsqrt(jnp.mean(x * x, axis=-1, keepdims=True) + EPS) * g


def _mod_kernel(c_ref, w_ref, b_ref, o_ref):
    c = c_ref[...]
    s = c / (1.0 + jnp.exp(-c))
    s_hi = s.astype(BF16).astype(F32)
    lhs = jnp.concatenate([s_hi, s - s_hi], axis=0).astype(BF16)
    w_hi, w_lo = _split_hi_lo(w_ref[...])
    r = _dot(lhs, w_hi)
    r2 = _dot(lhs, w_lo)
    o_ref[...] = r[:8] + r[8:] + r2[:8] + b_ref[...]


def _modulation(cond, mod_w, mod_b):
    n = mod_w.shape[1]
    tn = 512
    return pl.pallas_call(
        _mod_kernel,
        grid=(n // tn,),
        in_specs=[pl.BlockSpec((8, D_MODEL), lambda j: (0, 0)),
                  pl.BlockSpec((D_MODEL, tn), lambda j: (0, j)),
                  pl.BlockSpec((1, tn), lambda j: (0, j))],
        out_specs=pl.BlockSpec((8, tn), lambda j: (0, j)),
        out_shape=jax.ShapeDtypeStruct((8, n), F32),
        compiler_params=_cparams("parallel"),
        name="modulation",
    )(cond, mod_w, mod_b)


def _cast_kernel(w_ref, o_ref):
    o_ref[...] = w_ref[0].astype(BF16)


def _cast_bf16(w, n_cols, tn=512):
    rows = w.shape[1]
    return pl.pallas_call(
        _cast_kernel, grid=(n_cols // tn,),
        in_specs=[pl.BlockSpec((1, rows, tn), lambda j: (0, 0, j))],
        out_specs=pl.BlockSpec((rows, tn), lambda j: (0, j)),
        out_shape=jax.ShapeDtypeStruct((rows, n_cols), BF16),
        compiler_params=_cparams("parallel"), name="cast_bf16",
    )(w)


def _gate_cols_kernel(w_ref, o_ref):
    lane = lax.broadcasted_iota(jnp.int32, w_ref.shape[1:], 1)
    hi, lo = _split_hi_lo(jnp.where(lane < N_GATES, w_ref[0], 0.0))
    o_ref[:, :LANES] = hi
    o_ref[:, LANES:] = lo


def _gate_cols(w_in):
    rows = w_in.shape[1]
    return pl.pallas_call(
        _gate_cols_kernel, grid=(1,),
        in_specs=[pl.BlockSpec((1, rows, LANES), lambda i: (0, 0, MAIN_WIDTH // LANES))],
        out_specs=pl.BlockSpec((rows, 2 * LANES), lambda i: (0, 0)),
        out_shape=jax.ShapeDtypeStruct((rows, 2 * LANES), BF16),
        compiler_params=_cparams("arbitrary"), name="gate_cols",
    )(w_in)


def _pair_swap(x):
    lane = lax.broadcasted_iota(jnp.int32, x.shape, 1)
    return jnp.where((lane & 1) == 0, pltpu.roll(x, LANES - 1, 1), pltpu.roll(x, 1, 1))


def _proj_kernel(x_ref, mod_ref, n1_ref, w_ref, wg_ref, bg_ref, qg_ref, kg_ref, cos_ref, sin_ref,
                 q_ref, k_ref, v_ref, qm_ref, kmt_ref, vm_ref, om_ref, g_ref, *kv_refs, rope, tm):
    mod = mod_ref[0]
    shift, scale = mod[:, :D_MODEL], mod[:, D_MODEL:2 * D_MODEL]
    h = _rms(x_ref[...], n1_ref[...]) * (1.0 + scale) + shift
    h_hi, h_lo = _split_hi_lo(h)

    def rot(seg):
        return seg * cos_ref[...] + _pair_swap(seg) * sin_ref[...] if rope else seg

    qa = _dot(h_hi, w_ref[:, :ATT_WIDTH])
    for hh in range(ATT_HEADS):
        sl = slice(hh * ATT_HEAD_DIM, (hh + 1) * ATT_HEAD_DIM)
        seg = rot(_rms(qa[:, sl], qg_ref[...]))
        q_ref[:, sl] = (seg * Q_SCALE).astype(BF16)

    kv = _dot(h_hi, w_ref[:, ATT_WIDTH:ATT_WIDTH + 2 * KV_WIDTH])
    for hh in range(ATT_KV_HEADS):
        sl = slice(hh * ATT_HEAD_DIM, (hh + 1) * ATT_HEAD_DIM)
        seg = _rms(kv[:, sl], kg_ref[...])
        if kv_refs:
            kv_refs[0][:, sl] = seg
        k_ref[:, sl] = rot(seg).astype(BF16)
    va = kv[:, KV_WIDTH:]
    if kv_refs:
        kv_refs[1][...] = va
    v_ref[...] = va.astype(BF16)

    c0 = ATT_WIDTH + 2 * KV_WIDTH
    qm_ref[...] = (_dot(h_hi, w_ref[:, c0:c0 + MLSTM_WIDTH]) * MLSTM_HEAD_DIM ** -0.5).astype(BF16)
    km = _dot(h_hi, w_ref[:, c0 + MLSTM_WIDTH:c0 + 2 * MLSTM_WIDTH])
    for cc in range(tm // CHUNK):
        kmt_ref[cc] = km[cc * CHUNK:(cc + 1) * CHUNK, :].T.astype(BF16)
    vm_ref[...] = _dot(h_hi, w_ref[:, c0 + 2 * MLSTM_WIDTH:c0 + 3 * MLSTM_WIDTH]).astype(BF16)
    om_ref[...] = _dot(h_hi, w_ref[:, c0 + 3 * MLSTM_WIDTH:c0 + 4 * MLSTM_WIDTH])
    g_ref[...] = _dot_hi_lo(h_hi, h_lo, wg_ref) + bg_ref[...]


def _project(x2d, mod3, row_of_tile, n1, w_main, wg_cat, bg, qg, kg, cos_t, sin_t, *, rope, emit_kv, tm=256):
    t = x2d.shape[0]
    n_pos = cos_t.shape[0] // tm
    row = lambda i: (i, 0)
    in_specs = [pl.BlockSpec((tm, D_MODEL), row),
                pl.BlockSpec((1, 1, N_MOD * D_MODEL), lambda i: (row_of_tile(i), 0, 0)),
                _const_spec((1, D_MODEL)),
                _const_spec((D_MODEL, MAIN_WIDTH)),
                _const_spec((D_MODEL, 2 * LANES)), _const_spec((1, LANES)),
                _const_spec((1, ATT_HEAD_DIM)), _const_spec((1, ATT_HEAD_DIM)),
                pl.BlockSpec((tm, ATT_HEAD_DIM), lambda i: (i % n_pos, 0)),
                pl.BlockSpec((tm, ATT_HEAD_DIM), lambda i: (i % n_pos, 0))]
    out_shape = [jax.ShapeDtypeStruct((t, ATT_WIDTH), BF16), jax.ShapeDtypeStruct((t, KV_WIDTH), BF16),
                 jax.ShapeDtypeStruct((t, KV_WIDTH), BF16), jax.ShapeDtypeStruct((t, MLSTM_WIDTH), BF16),
                 jax.ShapeDtypeStruct((t // CHUNK, MLSTM_WIDTH, CHUNK), BF16),
                 jax.ShapeDtypeStruct((t, MLSTM_WIDTH), BF16), jax.ShapeDtypeStruct((t, MLSTM_WIDTH), F32),
                 jax.ShapeDtypeStruct((t, LANES), F32)]
    out_specs = [pl.BlockSpec((tm, ATT_WIDTH), row), pl.BlockSpec((tm, KV_WIDTH), row),
                 pl.BlockSpec((tm, KV_WIDTH), row), pl.BlockSpec((tm, MLSTM_WIDTH), row),
                 pl.BlockSpec((tm // CHUNK, MLSTM_WIDTH, CHUNK), lambda i: (i, 0, 0)),
                 pl.BlockSpec((tm, MLSTM_WIDTH), row), pl.BlockSpec((tm, MLSTM_WIDTH), row),
                 pl.BlockSpec((tm, LANES), row)]
    if emit_kv:
        out_shape += [jax.ShapeDtypeStruct((t, KV_WIDTH), F32)] * 2
        out_specs += [pl.BlockSpec((tm, KV_WIDTH), row)] * 2
    return pl.pallas_call(
        functools.partial(_proj_kernel, rope=rope, tm=tm),
        grid=(t // tm,), in_specs=in_specs, out_specs=out_specs, out_shape=out_shape,
        compiler_params=_cparams("parallel"), name="in_proj",
    )(x2d, mod3, n1, w_main, wg_cat, bg, qg, kg, cos_t, sin_t)


def _lane_scan(x, op, fill, is_fwd, lane):
    s = 1
    while s < CHUNK:
        from_left = jnp.where(lane >= s, pltpu.roll(x, s, 1), fill)
        from_right = jnp.where(lane < CHUNK - s, pltpu.roll(x, CHUNK - s, 1), fill)
        x = op(x, jnp.where(is_fwd, from_left, from_right))
        s *= 2
    return x


def _scan_kernel(g_ref, row_ref, col_ref, *, n_chunks):
    lane = lax.broadcasted_iota(jnp.int32, (N_CHAINS, CHUNK), 1)
    is_fwd = lax.broadcasted_iota(jnp.int32, (N_CHAINS, CHUNK), 0) < MLSTM_HEADS
    for cc in range(n_chunks):
        gt = g_ref[cc * CHUNK:(cc + 1) * CHUNK, :].T
        fwd, bwd = gt[:N_CHAINS], gt[N_CHAINS:2 * N_CHAINS]
        li = jnp.where(is_fwd, fwd, pltpu.roll(bwd, MLSTM_HEADS, 0))
        f = jnp.where(is_fwd, pltpu.roll(fwd, MLSTM_HEADS, 0), bwd)
        lf = jnp.minimum(f, 0.0) - jnp.log(1.0 + jnp.exp(-jnp.abs(f)))
        b = _lane_scan(lf, jnp.add, 0.0, is_fwd, lane)
        a = li - b
        run_max = _lane_scan(a, jnp.maximum, -jnp.inf, is_fwd, lane)
        all_max = jnp.broadcast_to(jnp.max(a, axis=1, keepdims=True), a.shape)
        last = jnp.where(is_fwd, CHUNK - 1, 0)
        total = jnp.broadcast_to(jnp.sum(jnp.where(lane == last, b, 0.0), axis=1, keepdims=True), a.shape)
        rows = jnp.concatenate([a, b, run_max, all_max, total], axis=0)
        row_ref[cc] = rows
        padded = jnp.concatenate([rows, jnp.zeros((LANES - N_SCAN_ROWS, CHUNK), F32)], axis=0)
        col_ref[cc * CHUNK:(cc + 1) * CHUNK, :] = padded.T


def _gate_scans(g, tb=1024):
    t = g.shape[0]
    tb = min(tb, t)
    return pl.pallas_call(
        functools.partial(_scan_kernel, n_chunks=tb // CHUNK),
        grid=(t // tb,),
        in_specs=[pl.BlockSpec((tb, LANES), lambda i: (i, 0))],
        out_specs=[pl.BlockSpec((tb // CHUNK, N_SCAN_ROWS, CHUNK), lambda i: (i, 0, 0)),
                   pl.BlockSpec((tb, LANES), lambda i: (i, 0))],
        out_shape=[jax.ShapeDtypeStruct((t // CHUNK, N_SCAN_ROWS, CHUNK), F32),
                   jax.ShapeDtypeStruct((t, LANES), F32)],
        compiler_params=_cparams("parallel"), name="gate_scans",
    )(g)


def _mlstm_kernel(*refs, n_chunks, has_state, emit_state):
    it = iter(refs)
    q_ref, kt_ref, v_ref, row_ref, col_ref = [next(it) for _ in range(5)]
    c0_ref, m0_ref = (next(it), next(it)) if has_state else (None, None)
    o_ref = next(it)
    c_out, n_out, m_out = (next(it), next(it), next(it)) if emit_state else (None, None, None)
    caug, hb = next(it), next(it)

    b = pl.program_id(0)
    if has_state:
        caug[...] = c0_ref[0]
        m_init = tuple(jnp.full((1, 1), m0_ref[b * N_CHAINS + r], F32) for r in range(N_CHAINS))
    else:
        caug[...] = jnp.zeros(caug.shape, F32)
        m_init = tuple(jnp.zeros((1, 1), F32) for _ in range(N_CHAINS))

    sub = lax.broadcasted_iota(jnp.int32, (CHUNK, CHUNK), 0)
    lane = lax.broadcasted_iota(jnp.int32, (CHUNK, CHUNK), 1)
    ones_col = (lax.broadcasted_iota(jnp.int32, (CHUNK, LANES), 1) == 0).astype(BF16)

    def chain(r, cc, m_prev):
        d, hd = divmod(r, MLSTM_HEADS)
        t0 = pl.multiple_of(cc * CHUNK, CHUNK)
        hs = slice(hd * MLSTM_HEAD_DIM, (hd + 1) * MLSTM_HEAD_DIM)
        rows = row_ref[cc]
        cols = col_ref[pl.ds(t0, CHUNK), :]
        row = lambda k: rows[k * N_CHAINS + r:k * N_CHAINS + r + 1, :]
        col = lambda k: cols[:, k * N_CHAINS + r:k * N_CHAINS + r + 1]
        q = q_ref[pl.ds(t0, CHUNK), hs]
        kt = kt_ref[cc, hs, :]
        vaug = jnp.concatenate([v_ref[pl.ds(t0, CHUNK), hs], ones_col], axis=1)

        m_col = jnp.maximum(m_prev, col(2))
        keep = (lane <= sub) if d == 0 else (lane >= sub)
        w = jnp.where(keep, jnp.exp(row(0) - m_col), 0.0)
        w_inter = jnp.exp(m_prev - m_col)
        p = (_dot(q, kt) * w).astype(BF16)
        intra = _dot(p, vaug)
        inter = _dot(q, caug[r].astype(BF16))
        num = intra[:, :MLSTM_HEAD_DIM] + w_inter * inter[:, :MLSTM_HEAD_DIM]
        den = intra[:, MLSTM_HEAD_DIM:MLSTM_HEAD_DIM + 1] + w_inter * inter[:, MLSTM_HEAD_DIM:MLSTM_HEAD_DIM + 1]
        h_out = num / jnp.maximum(jnp.abs(den), jnp.exp(-(col(1) + m_col)))

        m_last = jnp.maximum(m_prev, row(3))
        kw = (kt.astype(F32) * jnp.exp(row(0) - m_last)).astype(BF16)
        caug[r] = jnp.exp(m_prev - m_last[:, :1]) * caug[r] + _dot(kw, vaug)
        return h_out, (row(4) + m_last)[:, :1]

    def body(c, ms):
        new = []
        for r in range(N_CHAINS):
            cc = c if r < MLSTM_HEADS else n_chunks - 1 - c
            h_out, m_new = chain(r, cc, ms[r])
            hd = r % MLSTM_HEADS
            dst = o_ref if r < MLSTM_HEADS else hb
            dst[pl.ds(pl.multiple_of(cc * CHUNK, CHUNK), CHUNK), hd * MLSTM_HEAD_DIM:(hd + 1) * MLSTM_HEAD_DIM] = h_out
            new.append(m_new)
        return tuple(new)

    ms = lax.fori_loop(0, n_chunks, body, m_init)
    o_ref[...] += hb[...]
    if emit_state:
        for r in range(N_CHAINS):
            c_out[0, r] = caug[r, :, :MLSTM_HEAD_DIM]
            n_out[0, r] = caug[r, :, MLSTM_HEAD_DIM:].T[:1, :]
            m_out[0, r] = jnp.broadcast_to(ms[r], (1, LANES))


def _mlstm(qm, kmt, vm, rowq, colq, n_batch, seq, state=None, emit_state=False):
    nc = seq // CHUNK
    mode = dict(pipeline_mode=pl.Buffered(1)) if seq * MLSTM_WIDTH * 4 > (2 << 20) else {}
    in_specs = [pl.BlockSpec((seq, MLSTM_WIDTH), lambda b: (b, 0), **mode),
                pl.BlockSpec((nc, MLSTM_WIDTH, CHUNK), lambda b: (b, 0, 0), **mode),
                pl.BlockSpec((seq, MLSTM_WIDTH), lambda b: (b, 0), **mode),
                pl.BlockSpec((nc, N_SCAN_ROWS, CHUNK), lambda b: (b, 0, 0), **mode),
                pl.BlockSpec((seq, LANES), lambda b: (b, 0), **mode)]
    args = [qm, kmt, vm, rowq, colq]
    if state is not None:
        in_specs += [pl.BlockSpec((1, N_CHAINS, MLSTM_HEAD_DIM, AUG), lambda b: (b, 0, 0, 0), **mode),
                     pl.BlockSpec(memory_space=pltpu.SMEM)]
        args += list(state)
    out_shape = [jax.ShapeDtypeStruct((n_batch * seq, MLSTM_WIDTH), F32)]
    out_specs = [pl.BlockSpec((seq, MLSTM_WIDTH), lambda b: (b, 0))]
    if emit_state:
        out_shape += [jax.ShapeDtypeStruct((n_batch, N_CHAINS, MLSTM_HEAD_DIM, MLSTM_HEAD_DIM), F32),
                      jax.ShapeDtypeStruct((n_batch, N_CHAINS, 1, MLSTM_HEAD_DIM), F32),
                      jax.ShapeDtypeStruct((n_batch, N_CHAINS, 1, LANES), F32)]
        out_specs += [pl.BlockSpec((1, N_CHAINS, MLSTM_HEAD_DIM, MLSTM_HEAD_DIM), lambda b: (b, 0, 0, 0)),
                      pl.BlockSpec((1, N_CHAINS, 1, MLSTM_HEAD_DIM), lambda b: (b, 0, 0, 0)),
                      pl.BlockSpec((1, N_CHAINS, 1, LANES), lambda b: (b, 0, 0, 0))]
    return pl.pallas_call(
        functools.partial(_mlstm_kernel, n_chunks=nc, has_state=state is not None, emit_state=emit_state),
        grid=(n_batch,), in_specs=in_specs, out_specs=out_specs, out_shape=out_shape,
        scratch_shapes=[pltpu.VMEM((N_CHAINS, MLSTM_HEAD_DIM, AUG), F32), pltpu.VMEM((seq, MLSTM_WIDTH), F32)],
        compiler_params=_cparams("parallel"), name="mlstm",
    )(*args)


def _attn_kernel(*refs, tq, has_cache):
    if has_cache:
        q_ref, k_ref, v_ref, ck_ref, cv_ref, o_ref = refs
    else:
        q_ref, k_ref, v_ref, o_ref = refs
    q = q_ref[...]
    qs = jnp.concatenate([q[:, g * ATT_HEAD_DIM:(g + 1) * ATT_HEAD_DIM] for g in range(ATT_GROUP)], axis=0)
    nt = (((1,), (1,)), ((), ()))
    seq = k_ref.shape[0]
    kc = min(seq, ATT_KEY_CHUNK)
    chunks = [(k_ref, v_ref, c * kc) for c in range(seq // kc)]
    if has_cache:
        chunks.insert(0, (ck_ref, cv_ref, None))
    m = l = o = None
    for kr, vr, start in chunks:
        kk, vv = (kr[...], vr[...]) if start is None else (kr[start:start + kc, :], vr[start:start + kc, :])
        s = lax.dot_general(qs, kk.astype(BF16), nt, preferred_element_type=F32)
        mc = jnp.max(s, axis=-1, keepdims=True)
        if m is None:
            m = mc
            p = jnp.exp2(s - m)
            l = jnp.sum(p, axis=-1, keepdims=True)
            o = _dot(p.astype(BF16), vv.astype(BF16))
        else:
            m_new = jnp.maximum(m, mc)
            alpha = jnp.exp2(m - m_new)
            p = jnp.exp2(s - m_new)
            l = alpha * l + jnp.sum(p, axis=-1, keepdims=True)
            o = alpha * o + _dot(p.astype(BF16), vv.astype(BF16))
            m = m_new
    o = o / l
    for g in range(ATT_GROUP):
        o_ref[:, g * ATT_HEAD_DIM:(g + 1) * ATT_HEAD_DIM] = o[g * tq:(g + 1) * tq].astype(BF16)


def _attention(q, k, v, n_batch, seq, cache=None, tq=256):
    nqb = seq // tq
    gw = ATT_GROUP * ATT_HEAD_DIM
    in_specs = [pl.BlockSpec((tq, gw), lambda b, h, i: (b * nqb + i, h)),
                pl.BlockSpec((seq, ATT_HEAD_DIM), lambda b, h, i: (b, h)),
                pl.BlockSpec((seq, ATT_HEAD_DIM), lambda b, h, i: (b, h))]
    args = [q, k, v]
    if cache is not None:
        past = cache[0].shape[0] // n_batch
        in_specs += [pl.BlockSpec((past, ATT_HEAD_DIM), lambda b, h, i: (b, h))] * 2
        args += list(cache)
    return pl.pallas_call(
        functools.partial(_attn_kernel, tq=tq, has_cache=cache is not None),
        grid=(n_batch, ATT_KV_HEADS, nqb), in_specs=in_specs,
        out_specs=pl.BlockSpec((tq, gw), lambda b, h, i: (b * nqb + i, h)),
        out_shape=jax.ShapeDtypeStruct((n_batch * seq, ATT_WIDTH), BF16),
        compiler_params=_cparams("parallel", "parallel", "parallel"), name="attention",
    )(*args)


def _mix_kernel(attn_ref, hm_ref, om_ref, x_ref, mod_ref, mg_ref, wo_ref, n2_ref, rw_ref, rb_ref, *rest, sub):
    for r0 in range(0, x_ref.shape[0], sub):
        _mix_rows(slice(r0, r0 + sub), attn_ref, hm_ref, om_ref, x_ref, mod_ref, mg_ref, wo_ref, n2_ref, rw_ref,
                  rb_ref, *rest[-3:])


def _mix_rows(rows, attn_ref, hm_ref, om_ref, x_ref, mod_ref, mg_ref, wo_ref, n2_ref, rw_ref, rb_ref,
              x1_ref, h2_ref, route_ref):
    mod = mod_ref[0]
    gate1 = mod[:, 2 * D_MODEL:3 * D_MODEL]
    shift2, scale2 = mod[:, 3 * D_MODEL:4 * D_MODEL], mod[:, 4 * D_MODEL:5 * D_MODEL]
    hm = hm_ref[rows, :]
    mg = mg_ref[...]
    parts = []
    for hd in range(MLSTM_HEADS):
        sl = slice(hd * MLSTM_HEAD_DIM, (hd + 1) * MLSTM_HEAD_DIM)
        parts.append(_rms(hm[:, sl], mg[:, sl]))
    om = om_ref[rows, :]
    hmg = jnp.concatenate(parts, axis=1) * (1.0 / (1.0 + jnp.exp(-om)))
    y = _dot(attn_ref[rows, :], wo_ref[:ATT_WIDTH, :]) + _dot(hmg.astype(BF16), wo_ref[ATT_WIDTH:, :])
    x1 = x_ref[rows, :] + gate1 * y
    x1_ref[rows, :] = x1
    h2 = _rms(x1, n2_ref[...]) * (1.0 + scale2) + shift2
    h2_hi, h2_lo = _split_hi_lo(h2)
    h2_ref[rows, :] = h2

    lg = _dot_hi_lo(h2_hi, h2_lo, rw_ref) + rb_ref[...]
    lane = lax.broadcasted_iota(jnp.int32, lg.shape, 1).astype(F32)
    neg = -jnp.inf
    first = lambda hit: jnp.min(jnp.where(hit, lane, float(LANES)), axis=-1, keepdims=True)
    gl = jnp.where(lane < N_GROUPS, lg, neg)
    gmax = jnp.max(gl, axis=-1, keepdims=True)
    grp = first(gl == gmax)
    p_grp = 1.0 / jnp.sum(jnp.exp(gl - gmax), axis=-1, keepdims=True)
    lo = ROUTER_LANE0 + grp * EXPERTS_PER_GROUP
    el = jnp.where((lane >= lo) & (lane < lo + EXPERTS_PER_GROUP), lg, neg)
    m1 = jnp.max(el, axis=-1, keepdims=True)
    i1 = first(el == m1)
    el2 = jnp.where(lane == i1, neg, el)
    m2 = jnp.max(el2, axis=-1, keepdims=True)
    i2 = first(el2 == m2)
    r = jnp.exp(m2 - m1)
    w1 = p_grp / (1.0 + r)
    w2 = w1 * r
    route_ref[rows, :] = jnp.where(lane == 0.0, i1 - ROUTER_LANE0, jnp.where(lane == 1.0, i2 - ROUTER_LANE0,
                                   jnp.where(lane == 2.0, w1, jnp.where(lane == 3.0, w2, 0.0))))


def _mix_out(attn, hm, om, x2d, mod3, row_of_tile, mg, w_out, n2, rw_cat, rb, h2_all, tile0, t_total, tm=512):
    t = x2d.shape[0]
    row = lambda i: (i, 0)
    in_specs = [pl.BlockSpec((tm, ATT_WIDTH), row), pl.BlockSpec((tm, MLSTM_WIDTH), row),
                pl.BlockSpec((tm, MLSTM_WIDTH), row), pl.BlockSpec((tm, D_MODEL), row),
                pl.BlockSpec((1, 1, N_MOD * D_MODEL), lambda i: (row_of_tile(i), 0, 0)),
                _const_spec((1, MLSTM_WIDTH)), _const_spec((D_MODEL, D_MODEL)), _const_spec((1, D_MODEL)),
                _const_spec((D_MODEL, 2 * LANES)), _const_spec((1, LANES))]
    args = [attn, hm, om, x2d, mod3, mg, w_out, n2, rw_cat, rb]
    aliases = {}
    if h2_all is not None:
        aliases = {len(args): 1}
        in_specs.append(pl.BlockSpec(memory_space=pl.ANY))
        args.append(h2_all)
    return pl.pallas_call(
        functools.partial(_mix_kernel, sub=256), grid=(t // tm,), in_specs=in_specs,
        out_specs=[pl.BlockSpec((tm, D_MODEL), row), pl.BlockSpec((tm, D_MODEL), lambda i: (tile0 + i, 0)),
                   pl.BlockSpec((tm, LANES), row)],
        out_shape=[jax.ShapeDtypeStruct((t, D_MODEL), F32), jax.ShapeDtypeStruct((t_total, D_MODEL), F32),
                   jax.ShapeDtypeStruct((t, LANES), F32)],
        input_output_aliases=aliases,
        compiler_params=_cparams("parallel"), name="mix_out",
    )(*args)


def _rank_kernel(route_ref, rank_ref, cnt_ref, run_ref, tri_ref):
    tr = route_ref.shape[0]

    @pl.when(pl.program_id(0) == 0)
    def _():
        run_ref[...] = jnp.zeros(run_ref.shape, F32)
        tri_ref[...] = (lax.broadcasted_iota(jnp.int32, (tr, tr), 1)
                        < lax.broadcasted_iota(jnp.int32, (tr, tr), 0)).astype(BF16)

    route = route_ref[...]
    lane = lax.broadcasted_iota(jnp.int32, route.shape, 1).astype(F32)
    hit1, hit2 = lane == route[:, 0:1], lane == route[:, 1:2]
    onehot = jnp.where(hit1, 1.0, jnp.where(hit2, 1.0, 0.0))
    before = _dot(tri_ref[...], onehot.astype(BF16)) + run_ref[0:1, :]
    r1 = jnp.sum(jnp.where(hit1, before, 0.0), axis=-1, keepdims=True)
    r2 = jnp.sum(jnp.where(hit2, before, 0.0), axis=-1, keepdims=True)
    rank_ref[...] = jnp.where(lane == 0.0, r1, jnp.where(lane == 1.0, r2, 0.0))
    run_ref[...] = run_ref[...] + jnp.sum(onehot, axis=0, keepdims=True)
    cnt_ref[...] = run_ref[...]


def _ranks(route, tr=512):
    t = route.shape[0]
    return pl.pallas_call(
        _rank_kernel, grid=(t // tr,),
        in_specs=[pl.BlockSpec((tr, LANES), lambda i: (i, 0))],
        out_specs=[pl.BlockSpec((tr, LANES), lambda i: (i, 0)), pl.BlockSpec((8, LANES), lambda i: (0, 0))],
        out_shape=[jax.ShapeDtypeStruct((t, LANES), F32), jax.ShapeDtypeStruct((8, LANES), F32)],
        scratch_shapes=[pltpu.VMEM((8, LANES), F32), pltpu.VMEM((tr, tr), BF16)],
        compiler_params=_cparams("arbitrary"), name="expert_ranks",
    )(route)


def _pos_kernel(route_ref, rank_ref, start_ref, o_ref):
    route, rank = route_ref[...], rank_ref[...]
    lane = lax.broadcasted_iota(jnp.int32, route.shape, 1).astype(F32)
    start = start_ref[...]
    first = lambda col: jnp.sum(jnp.where(lane == route[:, col:col + 1], start, 0.0), axis=-1, keepdims=True)
    p1 = first(0) + rank[:, 0:1]
    p2 = first(1) + rank[:, 1:2]
    tile = jnp.where(lane == 0.0, p1, jnp.where(lane == 1.0, p2, 0.0))
    o_ref[...] = tile.T[:8, :].astype(jnp.int32)


def _positions(route, ranks, starts, tr=512):
    t = route.shape[0]
    pos = pl.pallas_call(
        _pos_kernel, grid=(t // tr,),
        in_specs=[pl.BlockSpec((tr, LANES), lambda i: (i, 0)), pl.BlockSpec((tr, LANES), lambda i: (i, 0)),
                  pl.BlockSpec((1, LANES), lambda i: (0, 0))],
        out_specs=pl.BlockSpec((8, tr), lambda i: (0, i)),
        out_shape=jax.ShapeDtypeStruct((8, t), jnp.int32),
        compiler_params=_cparams("parallel"), name="pair_rows",
    )(route, ranks, starts)
    return pos[:2].reshape(-1)


def _routing_plan(route, ranks, counts):
    t = route.shape[0]
    n_tiles = 2 * t // TM_MOE + N_EXPERTS
    cnt = counts[0, :N_EXPERTS].astype(jnp.int32)
    padded = (cnt + TM_MOE - 1) // TM_MOE * TM_MOE
    ends = jnp.cumsum(padded)
    pos = _positions(route, ranks, _pad_lanes((ends - padded).astype(F32)[None]))
    n_used = (ends[-1:] // TM_MOE).astype(jnp.int32)
    tile_start = jnp.arange(n_tiles, dtype=jnp.int32) * TM_MOE
    tile_expert = jnp.sum((ends[None, :] <= tile_start[:, None]).astype(jnp.int32), axis=1)
    tile_expert = jnp.minimum(tile_expert, tile_expert[n_used[0] - 1]).astype(jnp.int32)
    tails = jnp.where(padded > 0, ends - TM_MOE, -1).astype(jnp.int32)
    return pos, tile_expert, n_used, tails, n_tiles * TM_MOE


def _row_copies(pos_ref, n_pairs, tok0, n_rows, make_copy):
    def body(r, carry):
        for k in range(2):
            make_copy(k, r, pos_ref[k * n_pairs + tok0 + r]).start()
        return carry
    lax.fori_loop(0, n_rows, body, 0, unroll=8)


def _dispatch_kernel(pos_ref, tail_ref, h_ref, xs_ref, zero_ref, sem, *, td, n_tokens):
    i = pl.program_id(0)

    @pl.when(i == 0)
    def _():
        zero_ref[...] = jnp.zeros(zero_ref.shape, F32)
        tail_copy = lambda e: pltpu.make_async_copy(
            zero_ref, xs_ref.at[pl.ds(pl.multiple_of(tail_ref[e], TM_MOE), TM_MOE)], sem)
        for e in range(N_EXPERTS):
            pl.when(tail_ref[e] >= 0)(lambda e=e: tail_copy(e).start())
        for e in range(N_EXPERTS):
            pl.when(tail_ref[e] >= 0)(lambda e=e: tail_copy(e).wait())

    _row_copies(pos_ref, n_tokens, i * td, td,
                lambda k, r, p: pltpu.make_async_copy(h_ref.at[pl.ds(r, 1)], xs_ref.at[pl.ds(p, 1)], sem))
    for _ in range(2):
        pltpu.make_async_copy(h_ref, xs_ref.at[pl.ds(0, td)], sem).wait()


def _dispatch(pos, tails, h2, n_rows, td=256):
    t = h2.shape[0]
    return pl.pallas_call(
        functools.partial(_dispatch_kernel, td=td, n_tokens=t),
        grid_spec=pltpu.PrefetchScalarGridSpec(
            num_scalar_prefetch=2, grid=(t // td,),
            in_specs=[pl.BlockSpec((td, D_MODEL), lambda i, *_: (i, 0))],
            out_specs=pl.BlockSpec(memory_space=pl.ANY),
            scratch_shapes=[pltpu.VMEM((TM_MOE, D_MODEL), F32), pltpu.SemaphoreType.DMA]),
        out_shape=jax.ShapeDtypeStruct((n_rows, D_MODEL), F32),
        compiler_params=_cparams("arbitrary"), name="dispatch",
    )(pos, tails, h2)


def _expert_kernel(te_ref, nu_ref, xs_ref, wg_ref, wu_ref, wd_ref, ys_ref, wgb, wub, wdb):
    j = pl.program_id(0)

    @pl.when(j < nu_ref[0])
    def _():
        @pl.when((j == 0) | (te_ref[j] != te_ref[jnp.maximum(j - 1, 0)]))
        def _():
            wgb[...] = wg_ref[0].astype(BF16)
            wub[...] = wu_ref[0].astype(BF16)
            wdb[...] = wd_ref[0].astype(BF16)

        x = xs_ref[...].astype(BF16)
        g = _dot(x, wgb[...])
        u = _dot(x, wub[...])
        a = (g / (1.0 + jnp.exp(-g))) * u
        ys_ref[...] = _dot(a.astype(BF16), wdb[...])


def _experts(tile_expert, n_used, xs, wg, wu, wd):
    n_tiles = xs.shape[0] // TM_MOE
    tile = lambda j, te, nu: (jnp.minimum(j, nu[0] - 1), 0)
    wspec = lambda shape: pl.BlockSpec((1,) + shape, lambda j, te, nu: (te[j], 0, 0))
    return pl.pallas_call(
        _expert_kernel,
        grid_spec=pltpu.PrefetchScalarGridSpec(
            num_scalar_prefetch=2, grid=(n_tiles,),
            in_specs=[pl.BlockSpec((TM_MOE, D_MODEL), tile), wspec((D_MODEL, D_EXPERT)), wspec((D_MODEL, D_EXPERT)),
                      wspec((D_EXPERT, D_MODEL))],
            out_specs=pl.BlockSpec((TM_MOE, D_MODEL), tile),
            scratch_shapes=[pltpu.VMEM((D_MODEL, D_EXPERT), BF16), pltpu.VMEM((D_MODEL, D_EXPERT), BF16),
                            pltpu.VMEM((D_EXPERT, D_MODEL), BF16)]),
        out_shape=jax.ShapeDtypeStruct(xs.shape, F32),
        compiler_params=_cparams("arbitrary"), name="experts",
    )(tile_expert, n_used, xs, wg, wu, wd)


def _combine_kernel(pos_ref, x1_ref, route_ref, mod_ref, ys_ref, o_ref, ybuf, sem, *, tc, n_tokens, tok0):
    _row_copies(pos_ref, n_tokens, tok0 + pl.program_id(0) * tc, tc,
                lambda k, r, p: pltpu.make_async_copy(ys_ref.at[pl.ds(p, 1)], ybuf.at[k, pl.ds(r, 1)], sem))
    for k in range(2):
        pltpu.make_async_copy(ys_ref.at[pl.ds(0, tc)], ybuf.at[k], sem).wait()
    route = route_ref[...]
    gate2 = mod_ref[0][:, 5 * D_MODEL:]
    o_ref[...] = x1_ref[...] + gate2 * (route[:, 2:3] * ybuf[0] + route[:, 3:4] * ybuf[1])


def _combine(pos, x1, route, mod3, row_of_tile, ys, tok0, n_tokens, tc=256):
    t = x1.shape[0]
    row = lambda i, *_: (i, 0)
    return pl.pallas_call(
        functools.partial(_combine_kernel, tc=tc, n_tokens=n_tokens, tok0=tok0),
        grid_spec=pltpu.PrefetchScalarGridSpec(
            num_scalar_prefetch=1, grid=(t // tc,),
            in_specs=[pl.BlockSpec((tc, D_MODEL), row), pl.BlockSpec((tc, LANES), row),
                      pl.BlockSpec((1, 1, N_MOD * D_MODEL), lambda i, *_: (row_of_tile(i), 0, 0)),
                      pl.BlockSpec(memory_space=pl.ANY)],
            out_specs=pl.BlockSpec((tc, D_MODEL), row),
            scratch_shapes=[pltpu.VMEM((2, tc, D_MODEL), F32), pltpu.SemaphoreType.DMA]),
        out_shape=jax.ShapeDtypeStruct((t, D_MODEL), F32),
        compiler_params=_cparams("arbitrary"), name="combine",
    )(pos, x1, route, mod3, ys)


def _rope_tables(seq):
    pos = np.arange(seq)
    n_freq = ATT_HEAD_DIM // 4
    inv = ROPE_THETA ** (-np.arange(n_freq, dtype=np.float32) / n_freq)
    ang = np.concatenate([(pos // GRID_W).astype(np.float32)[:, None] * inv,
                          (pos % GRID_W).astype(np.float32)[:, None] * inv], axis=-1).astype(np.float32)
    ang = jnp.asarray(ang)
    cos, sin = jnp.cos(ang), jnp.sin(ang)
    cos_t = jnp.repeat(cos, 2, axis=1)
    sin_t = jnp.stack([-sin, sin], axis=-1).reshape(seq, ATT_HEAD_DIM)
    return cos_t, sin_t


def _pad_lanes(a):
    return jnp.pad(a, ((0, 0), (0, LANES - a.shape[1])))


def _layer(x2d, n_batch, seq, mod3, row_of_tile, lw, *, rope_tabs, cache, state, emit, h2_all, tile0, t_total):
    (n1, n2, w_main, wg_cat, bg, qg, kg, mg, w_out, rw_cat, rb) = lw
    cos_t, sin_t = rope_tabs
    outs = _project(x2d, mod3, row_of_tile(256), n1, w_main, wg_cat, bg, qg, kg, cos_t, sin_t,
                    rope=cache is not None, emit_kv=emit)
    q, k, v, qm, kmt, vm, om, g = outs[:8]
    rowq, colq = _gate_scans(g)
    attn = _attention(q, k, v, n_batch, seq, cache=cache)
    ml = _mlstm(qm, kmt, vm, rowq, colq, n_batch, seq, state=state, emit_state=emit)
    mixed = _mix_out(attn, ml[0], om, x2d, mod3, row_of_tile(MIX_TILE), mg, w_out, n2, rw_cat, rb,
                     h2_all, tile0, t_total, tm=MIX_TILE)
    return mixed, outs[8:], ml[1:]


def kernel(x_prompt, x_sample, cache_k, cache_v, state_C, state_n, state_m, c, c_ctx, mod_w, mod_b, norm1_g, norm2_g,
           w_in, b_gates, q_norm_g, k_norm_g, mlstm_norm_g, w_out, router_group_w, router_group_b, router_expert_w,
           router_expert_b, expert_w_gate, expert_w_up, expert_w_down):
    assert mod_w.shape[0] == 1, "single-layer stack"
    n_ctx, s_ctx, _ = x_prompt.shape
    n_lat, s_lat, _ = x_sample.shape
    t_ctx, t_lat = n_ctx * s_ctx, n_lat * s_lat
    t_all = t_ctx + t_lat
    ctx_row = n_lat

    cond = jnp.concatenate([c, c_ctx[None], jnp.zeros((8 - n_lat - 1, D_MODEL), F32)], axis=0)
    mod3 = _modulation(cond, mod_w[0], mod_b[0][None]).reshape(8, 1, N_MOD * D_MODEL)

    rw =jnp.concatenate([router_group_w[0], jnp.moveaxis(router_expert_w[0], 0, 1).reshape(D_MODEL, N_EXPERTS)], axis=1)
    rb = _pad_lanes(jnp.concatenate([router_group_b[0], router_expert_b[0].reshape(-1)])[None])
    lw = (norm1_g, norm2_g, _cast_bf16(w_in, MAIN_WIDTH), _gate_cols(w_in), _pad_lanes(b_gates),
          q_norm_g, k_norm_g, mlstm_norm_g, _cast_bf16(w_out, D_MODEL), _hi_lo_cat(_pad_lanes(rw)), rb)
    rope_tabs = _rope_tables(s_lat)

    ctx_rows = lambda tm: (lambda i: ctx_row)
    lat_rows = lambda tm: (lambda i: i // (s_lat // tm))
    (x1p, h2_all, routep), (ka, va), (s_c, s_n, s_m) = _layer(
        x_prompt.reshape(t_ctx, D_MODEL), n_ctx, s_ctx, mod3, ctx_rows, lw,
        rope_tabs=rope_tabs, cache=None, state=None, emit=True, h2_all=None, tile0=0, t_total=t_all)

    caug0 = jnp.concatenate([state_C[:, 0], state_n[:, 0][..., None],
                             jnp.zeros(state_n[:, 0].shape + (LANES - 1,), F32)], axis=-1)
    caug0 = caug0.reshape(n_lat, N_CHAINS, MLSTM_HEAD_DIM, AUG)
    past = cache_k.shape[2]
    cache = (cache_k[:, 0].reshape(n_lat * past, KV_WIDTH), cache_v[:, 0].reshape(n_lat * past, KV_WIDTH))
    (x1s, h2_all, routes), _, _ = _layer(
        x_sample.reshape(t_lat, D_MODEL), n_lat, s_lat, mod3, lat_rows, lw,
        rope_tabs=rope_tabs, cache=cache, state=(caug0, state_m[:, 0].reshape(-1)), emit=False,
        h2_all=h2_all, tile0=t_ctx // MIX_TILE, t_total=t_all)

    route = jnp.concatenate([routep, routes], axis=0)
    ranks, counts = _ranks(route)
    pos, tile_expert, n_used, tails, n_rows = _routing_plan(route, ranks, counts)
    xs = _dispatch(pos, tails, h2_all, n_rows)
    y_sorted = _experts(tile_expert, n_used, xs, expert_w_gate[0], expert_w_up[0], expert_w_down[0])
    yp = _combine(pos, x1p, routep, mod3, ctx_rows(256), y_sorted, 0, t_all)
    ys = _combine(pos, x1s, routes, mod3, lat_rows(256), y_sorted, t_ctx, t_all)

    kv_shape = (n_ctx, 1, s_ctx, ATT_KV_HEADS, ATT_HEAD_DIM)
    return (yp.reshape(x_prompt.shape), ys.reshape(x_sample.shape), ka.reshape(kv_shape), va.reshape(kv_shape),
            s_c.reshape(n_ctx, 1, 2, MLSTM_HEADS, MLSTM_HEAD_DIM, MLSTM_HEAD_DIM),
            s_n.reshape(n_ctx, 1, 2, MLSTM_HEADS, MLSTM_HEAD_DIM), s_m[..., 0, 0].reshape(n_ctx, 1, 2, MLSTM_HEADS))
```

```python
import functools

import numpy as np
import jax
import jax.numpy as jnp
from jax import lax
from jax.experimental import pallas as pl
from jax.experimental.pallas import tpu as pltpu

F32 = jnp.float32
BF16 = jnp.bfloat16

D_MODEL = 2048
GRID_W = 64
ATT_HEADS = 8
ATT_KV_HEADS = 2
ATT_HEAD_DIM = 128
ATT_GROUP = ATT_HEADS // ATT_KV_HEADS
ATT_WIDTH = ATT_HEADS * ATT_HEAD_DIM
KV_WIDTH = ATT_KV_HEADS * ATT_HEAD_DIM
ROPE_THETA = 10000.0
MLSTM_HEADS = 4
MLSTM_HEAD_DIM = 256
MLSTM_WIDTH = MLSTM_HEADS * MLSTM_HEAD_DIM
CHUNK = 256
N_GATES = 4 * MLSTM_HEADS
N_CHAINS = 2 * MLSTM_HEADS
MAIN_WIDTH = ATT_WIDTH + 2 * KV_WIDTH + 4 * MLSTM_WIDTH
N_GROUPS = 4
EXPERTS_PER_GROUP = 8
N_EXPERTS = N_GROUPS * EXPERTS_PER_GROUP
D_EXPERT = 512
N_MOD = 6
TM_MOE = 256
MIX_TILE = 512
ATT_KEY_CHUNK = 512
Q_SCALE = ATT_HEAD_DIM ** -0.5 * float(np.log2(np.e))
EPS = 1e-6

LANES = 128
AUG = MLSTM_HEAD_DIM + LANES
N_SCAN_ROWS = 5 * N_CHAINS
ROUTER_LANE0 = N_GROUPS
VMEM_LIMIT = 56 * 1024 * 1024


def _cparams(*sem):
    return pltpu.CompilerParams(dimension_semantics=sem, vmem_limit_bytes=VMEM_LIMIT)


def _const_spec(shape):
    nd = len(shape)
    return pl.BlockSpec(shape, lambda *_: (0,) * nd, pipeline_mode=pl.Buffered(1))


def _split_hi_lo(x):
    hi = x.astype(BF16)
    lo = (x - hi.astype(F32)).astype(BF16)
    return hi, lo


def _dot(a, b):
    return jnp.dot(a, b, preferred_element_type=F32)


def _hi_lo_cat(w):
    return jnp.concatenate(_split_hi_lo(w), axis=1)


def _dot_hi_lo(a_hi, a_lo, w_ref):
    r = _dot(a_hi, w_ref[...])
    return r[:, :LANES] + r[:, LANES:] + _dot(a_lo, w_ref[:, :LANES])


def _rms(x, g):
    return x * lax.rsqrt(jnp.mean(x * x, axis=-1, keepdims=True) + EPS) * g


def _mod_kernel(c_ref, w_ref, b_ref, o_ref):
    c = c_ref[...]
    s = c / (1.0 + jnp.exp(-c))
    s_hi = s.astype(BF16).astype(F32)
    lhs = jnp.concatenate([s_hi, s - s_hi], axis=0).astype(BF16)
    w_hi, w_lo = _split_hi_lo(w_ref[...])
    r = _dot(lhs, w_hi)
    r2 = _dot(lhs, w_lo)
    o_ref[...] = r[:8] + r[8:] + r2[:8] + b_ref[...]


def _modulation(cond, mod_w, mod_b):
    n = mod_w.shape[1]
    tn = 512
    return pl.pallas_call(
        _mod_kernel,
        grid=(n // tn,),
        in_specs=[pl.BlockSpec((8, D_MODEL), lambda j: (0, 0)),
                  pl.BlockSpec((D_MODEL, tn), lambda j: (0, j)),
                  pl.BlockSpec((1, tn), lambda j: (0, j))],
        out_specs=pl.BlockSpec((8, tn), lambda j: (0, j)),
        out_shape=jax.ShapeDtypeStruct((8, n), F32),
        compiler_params=_cparams("parallel"),
        name="modulation",
    )(cond, mod_w, mod_b)


def _cast_kernel(w_ref, o_ref):
    o_ref[...] = w_ref[0].astype(BF16)


def _cast_bf16(w, n_cols, tn=512):
    rows = w.shape[1]
    return pl.pallas_call(
        _cast_kernel, grid=(n_cols // tn,),
        in_specs=[pl.BlockSpec((1, rows, tn), lambda j: (0, 0, j))],
        out_specs=pl.BlockSpec((rows, tn), lambda j: (0, j)),
        out_shape=jax.ShapeDtypeStruct((rows, n_cols), BF16),
        compiler_params=_cparams("parallel"), name="cast_bf16",
    )(w)


def _cast_t_kernel(wt_ref, o_ref):
    o_ref[...] = wt_ref[...].T.astype(BF16)


def _cast_bf16_t(wt, n_cols, tn=512):
    rows = wt.shape[1]
    return pl.pallas_call(
        _cast_t_kernel, grid=(n_cols // tn,),
        in_specs=[pl.BlockSpec((tn, rows), lambda j: (j, 0))],
        out_specs=pl.BlockSpec((rows, tn), lambda j: (0, j)),
        out_shape=jax.ShapeDtypeStruct((rows, n_cols), BF16),
        compiler_params=_cparams("parallel"), name="cast_bf16_t",
    )(wt)


def _gate_cols_kernel(wt_ref, o_ref):
    sub = lax.broadcasted_iota(jnp.int32, wt_ref.shape, 0)
    hi, lo = _split_hi_lo(jnp.where(sub < N_GATES, wt_ref[...], 0.0).T)
    o_ref[:, :LANES] = hi
    o_ref[:, LANES:] = lo


def _gate_cols(wt):
    rows = wt.shape[1]
    return pl.pallas_call(
        _gate_cols_kernel, grid=(1,),
        in_specs=[pl.BlockSpec((LANES, rows), lambda i: (MAIN_WIDTH // LANES, 0))],
        out_specs=pl.BlockSpec((rows, 2 * LANES), lambda i: (0, 0)),
        out_shape=jax.ShapeDtypeStruct((rows, 2 * LANES), BF16),
        compiler_params=_cparams("arbitrary"), name="gate_cols",
    )(wt)


def _pair_swap(x):
    lane = lax.broadcasted_iota(jnp.int32, x.shape, 1)
    return jnp.where((lane & 1) == 0, pltpu.roll(x, LANES - 1, 1), pltpu.roll(x, 1, 1))


def _proj_kernel(x_ref, mod_ref, n1_ref, w_ref, wg_ref, bg_ref, qg_ref, kg_ref, cos_ref, sin_ref,
                 q_ref, k_ref, v_ref, qm_ref, kmt_ref, vm_ref, om_ref, g_ref, *kv_refs, rope, tm):
    mod = mod_ref[0]
    shift, scale = mod[:, :D_MODEL], mod[:, D_MODEL:2 * D_MODEL]
    h = _rms(x_ref[...], n1_ref[...]) * (1.0 + scale) + shift
    h_hi, h_lo = _split_hi_lo(h)

    def rot(seg):
        return seg * cos_ref[...] + _pair_swap(seg) * sin_ref[...] if rope else seg

    qa = _dot(h_hi, w_ref[:, :ATT_WIDTH])
    for hh in range(ATT_HEADS):
        sl = slice(hh * ATT_HEAD_DIM, (hh + 1) * ATT_HEAD_DIM)
        seg = rot(_rms(qa[:, sl], qg_ref[...]))
        q_ref[:, sl] = (seg * Q_SCALE).astype(BF16)

    kv = _dot(h_hi, w_ref[:, ATT_WIDTH:ATT_WIDTH + 2 * KV_WIDTH])
    for hh in range(ATT_KV_HEADS):
        sl = slice(hh * ATT_HEAD_DIM, (hh + 1) * ATT_HEAD_DIM)
        seg = _rms(kv[:, sl], kg_ref[...])
        if kv_refs:
            kv_refs[0][:, sl] = seg
        k_ref[:, sl] = rot(seg).astype(BF16)
    va = kv[:, KV_WIDTH:]
    if kv_refs:
        kv_refs[1][...] = va
    v_ref[...] = va.astype(BF16)

    c0 = ATT_WIDTH + 2 * KV_WIDTH
    qm_ref[...] = (_dot(h_hi, w_ref[:, c0:c0 + MLSTM_WIDTH]) * MLSTM_HEAD_DIM ** -0.5).astype(BF16)
    km = _dot(h_hi, w_ref[:, c0 + MLSTM_WIDTH:c0 + 2 * MLSTM_WIDTH])
    for cc in range(tm // CHUNK):
        kmt_ref[cc] = km[cc * CHUNK:(cc + 1) * CHUNK, :].T.astype(BF16)
    vm_ref[...] = _dot(h_hi, w_ref[:, c0 + 2 * MLSTM_WIDTH:c0 + 3 * MLSTM_WIDTH]).astype(BF16)
    om_ref[...] = _dot(h_hi, w_ref[:, c0 + 3 * MLSTM_WIDTH:c0 + 4 * MLSTM_WIDTH])
    g_ref[...] = _dot_hi_lo(h_hi, h_lo, wg_ref) + bg_ref[...]


def _project(x2d, mod3, row_of_tile, n1, w_main, wg_cat, bg, qg, kg, cos_t, sin_t, *, rope, emit_kv, tm=256):
    t = x2d.shape[0]
    n_pos = cos_t.shape[0] // tm
    row = lambda i: (i, 0)
    in_specs = [pl.BlockSpec((tm, D_MODEL), row),
                pl.BlockSpec((1, 1, N_MOD * D_MODEL), lambda i: (row_of_tile(i), 0, 0)),
                _const_spec((1, D_MODEL)),
                _const_spec((D_MODEL, MAIN_WIDTH)),
                _const_spec((D_MODEL, 2 * LANES)), _const_spec((1, LANES)),
                _const_spec((1, ATT_HEAD_DIM)), _const_spec((1, ATT_HEAD_DIM)),
                pl.BlockSpec((tm, ATT_HEAD_DIM), lambda i: (i % n_pos, 0)),
                pl.BlockSpec((tm, ATT_HEAD_DIM), lambda i: (i % n_pos, 0))]
    out_shape = [jax.ShapeDtypeStruct((t, ATT_WIDTH), BF16), jax.ShapeDtypeStruct((t, KV_WIDTH), BF16),
                 jax.ShapeDtypeStruct((t, KV_WIDTH), BF16), jax.ShapeDtypeStruct((t, MLSTM_WIDTH), BF16),
                 jax.ShapeDtypeStruct((t // CHUNK, MLSTM_WIDTH, CHUNK), BF16),
                 jax.ShapeDtypeStruct((t, MLSTM_WIDTH), BF16), jax.ShapeDtypeStruct((t, MLSTM_WIDTH), F32),
                 jax.ShapeDtypeStruct((t, LANES), F32)]
    out_specs = [pl.BlockSpec((tm, ATT_WIDTH), row), pl.BlockSpec((tm, KV_WIDTH), row),
                 pl.BlockSpec((tm, KV_WIDTH), row), pl.BlockSpec((tm, MLSTM_WIDTH), row),
                 pl.BlockSpec((tm // CHUNK, MLSTM_WIDTH, CHUNK), lambda i: (i, 0, 0)),
                 pl.BlockSpec((tm, MLSTM_WIDTH), row), pl.BlockSpec((tm, MLSTM_WIDTH), row),
                 pl.BlockSpec((tm, LANES), row)]
    if emit_kv:
        out_shape += [jax.ShapeDtypeStruct((t, KV_WIDTH), F32)] * 2
        out_specs += [pl.BlockSpec((tm, KV_WIDTH), row)] * 2
    return pl.pallas_call(
        functools.partial(_proj_kernel, rope=rope, tm=tm),
        grid=(t // tm,), in_specs=in_specs, out_specs=out_specs, out_shape=out_shape,
        compiler_params=_cparams("parallel"), name="in_proj",
    )(x2d, mod3, n1, w_main, wg_cat, bg, qg, kg, cos_t, sin_t)


def _lane_scan(x, op, fill, is_fwd, lane):
    s = 1
    while s < CHUNK:
        from_left = jnp.where(lane >= s, pltpu.roll(x, s, 1), fill)
        from_right = jnp.where(lane < CHUNK - s, pltpu.roll(x, CHUNK - s, 1), fill)
        x = op(x, jnp.where(is_fwd, from_left, from_right))
        s *= 2
    return x


def _scan_kernel(g_ref, row_ref, col_ref, *, n_chunks):
    lane = lax.broadcasted_iota(jnp.int32, (N_CHAINS, CHUNK), 1)
    is_fwd = lax.broadcasted_iota(jnp.int32, (N_CHAINS, CHUNK), 0) < MLSTM_HEADS
    for cc in range(n_chunks):
        gt = g_ref[cc * CHUNK:(cc + 1) * CHUNK, :].T
        fwd, bwd = gt[:N_CHAINS], gt[N_CHAINS:2 * N_CHAINS]
        li = jnp.where(is_fwd, fwd, pltpu.roll(bwd, MLSTM_HEADS, 0))
        f = jnp.where(is_fwd, pltpu.roll(fwd, MLSTM_HEADS, 0), bwd)
        lf = jnp.minimum(f, 0.0) - jnp.log(1.0 + jnp.exp(-jnp.abs(f)))
        b = _lane_scan(lf, jnp.add, 0.0, is_fwd, lane)
        a = li - b
        run_max = _lane_scan(a, jnp.maximum, -jnp.inf, is_fwd, lane)
        all_max = jnp.broadcast_to(jnp.max(a, axis=1, keepdims=True), a.shape)
        last = jnp.where(is_fwd, CHUNK - 1, 0)
        total = jnp.broadcast_to(jnp.sum(jnp.where(lane == last, b, 0.0), axis=1, keepdims=True), a.shape)
        rows = jnp.concatenate([a, b, run_max, all_max, total], axis=0)
        row_ref[cc] = rows
        padded = jnp.concatenate([rows, jnp.zeros((LANES - N_SCAN_ROWS, CHUNK), F32)], axis=0)
        col_ref[cc * CHUNK:(cc + 1) * CHUNK, :] = padded.T


def _gate_scans(g, tb=1024):
    t = g.shape[0]
    tb = min(tb, t)
    return pl.pallas_call(
        functools.partial(_scan_kernel, n_chunks=tb // CHUNK),
        grid=(t // tb,),
        in_specs=[pl.BlockSpec((tb, LANES), lambda i: (i, 0))],
        out_specs=[pl.BlockSpec((tb // CHUNK, N_SCAN_ROWS, CHUNK), lambda i: (i, 0, 0)),
                   pl.BlockSpec((tb, LANES), lambda i: (i, 0))],
        out_shape=[jax.ShapeDtypeStruct((t // CHUNK, N_SCAN_ROWS, CHUNK), F32),
                   jax.ShapeDtypeStruct((t, LANES), F32)],
        compiler_params=_cparams("parallel"), name="gate_scans",
    )(g)


def _mlstm_kernel(*refs, n_chunks, has_state, emit_state):
    it = iter(refs)
    q_ref, kt_ref, v_ref, row_ref, col_ref = [next(it) for _ in range(5)]
    c0_ref, m0_ref = (next(it), next(it)) if has_state else (None, None)
    o_ref = next(it)
    c_out, n_out, m_out = (next(it), next(it), next(it)) if emit_state else (None, None, None)
    caug, hb = next(it), next(it)

    b = pl.program_id(0)
    if has_state:
        caug[...] = c0_ref[0]
        m_init = tuple(jnp.full((1, 1), m0_ref[b * N_CHAINS + r], F32) for r in range(N_CHAINS))
    else:
        caug[...] = jnp.zeros(caug.shape, F32)
        m_init = tuple(jnp.zeros((1, 1), F32) for _ in range(N_CHAINS))

    sub = lax.broadcasted_iota(jnp.int32, (CHUNK, CHUNK), 0)
    lane = lax.broadcasted_iota(jnp.int32, (CHUNK, CHUNK), 1)
    ones_col = (lax.broadcasted_iota(jnp.int32, (CHUNK, LANES), 1) == 0).astype(BF16)

    def chain(r, cc, m_prev):
        d, hd = divmod(r, MLSTM_HEADS)
        t0 = pl.multiple_of(cc * CHUNK, CHUNK)
        hs = slice(hd * MLSTM_HEAD_DIM, (hd + 1) * MLSTM_HEAD_DIM)
        rows = row_ref[cc]
        cols = col_ref[pl.ds(t0, CHUNK), :]
        row = lambda k: rows[k * N_CHAINS + r:k * N_CHAINS + r + 1, :]
        col = lambda k: cols[:, k * N_CHAINS + r:k * N_CHAINS + r + 1]
        q = q_ref[pl.ds(t0, CHUNK), hs]
        kt = kt_ref[cc, hs, :]
        vaug = jnp.concatenate([v_ref[pl.ds(t0, CHUNK), hs], ones_col], axis=1)

        m_col = jnp.maximum(m_prev, col(2))
        keep = (lane <= sub) if d == 0 else (lane >= sub)
        w = jnp.where(keep, jnp.exp(row(0) - m_col), 0.0)
        w_inter = jnp.exp(m_prev - m_col)
        p = (_dot(q, kt) * w).astype(BF16)
        intra = _dot(p, vaug)
        inter = _dot(q, caug[r].astype(BF16))
        num = intra[:, :MLSTM_HEAD_DIM] + w_inter * inter[:, :MLSTM_HEAD_DIM]
        den = intra[:, MLSTM_HEAD_DIM:MLSTM_HEAD_DIM + 1] + w_inter * inter[:, MLSTM_HEAD_DIM:MLSTM_HEAD_DIM + 1]
        h_out = num / jnp.maximum(jnp.abs(den), jnp.exp(-(col(1) + m_col)))

        m_last = jnp.maximum(m_prev, row(3))
        kw = (kt.astype(F32) * jnp.exp(row(0) - m_last)).astype(BF16)
        caug[r] = jnp.exp(m_prev - m_last[:, :1]) * caug[r] + _dot(kw, vaug)
        return h_out, (row(4) + m_last)[:, :1]

    def body(c, ms):
        new = []
        for r in range(N_CHAINS):
            cc = c if r < MLSTM_HEADS else n_chunks - 1 - c
            h_out, m_new = chain(r, cc, ms[r])
            hd = r % MLSTM_HEADS
            dst = o_ref if r < MLSTM_HEADS else hb
            dst[pl.ds(pl.multiple_of(cc * CHUNK, CHUNK), CHUNK), hd * MLSTM_HEAD_DIM:(hd + 1) * MLSTM_HEAD_DIM] = h_out
            new.append(m_new)
        return tuple(new)

    ms = lax.fori_loop(0, n_chunks, body, m_init)
    o_ref[...] += hb[...]
    if emit_state:
        for r in range(N_CHAINS):
            c_out[0, r] = caug[r, :, :MLSTM_HEAD_DIM]
            n_out[0, r] = caug[r, :, MLSTM_HEAD_DIM:].T[:1, :]
            m_out[0, r] = jnp.broadcast_to(ms[r], (1, LANES))


def _mlstm(qm, kmt, vm, rowq, colq, n_batch, seq, state=None, emit_state=False):
    nc = seq // CHUNK
    mode = dict(pipeline_mode=pl.Buffered(1)) if seq * MLSTM_WIDTH * 4 > (2 << 20) else {}
    in_specs = [pl.BlockSpec((seq, MLSTM_WIDTH), lambda b: (b, 0), **mode),
                pl.BlockSpec((nc, MLSTM_WIDTH, CHUNK), lambda b: (b, 0, 0), **mode),
                pl.BlockSpec((seq, MLSTM_WIDTH), lambda b: (b, 0), **mode),
                pl.BlockSpec((nc, N_SCAN_ROWS, CHUNK), lambda b: (b, 0, 0), **mode),
                pl.BlockSpec((seq, LANES), lambda b: (b, 0), **mode)]
    args = [qm, kmt, vm, rowq, colq]
    if state is not None:
        in_specs += [pl.BlockSpec((1, N_CHAINS, MLSTM_HEAD_DIM, AUG), lambda b: (b, 0, 0, 0), **mode),
                     pl.BlockSpec(memory_space=pltpu.SMEM)]
        args += list(state)
    out_shape = [jax.ShapeDtypeStruct((n_batch * seq, MLSTM_WIDTH), F32)]
    out_specs = [pl.BlockSpec((seq, MLSTM_WIDTH), lambda b: (b, 0))]
    if emit_state:
        out_shape += [jax.ShapeDtypeStruct((n_batch, N_CHAINS, MLSTM_HEAD_DIM, MLSTM_HEAD_DIM), F32),
                      jax.ShapeDtypeStruct((n_batch, N_CHAINS, 1, MLSTM_HEAD_DIM), F32),
                      jax.ShapeDtypeStruct((n_batch, N_CHAINS, 1, LANES), F32)]
        out_specs += [pl.BlockSpec((1, N_CHAINS, MLSTM_HEAD_DIM, MLSTM_HEAD_DIM), lambda b: (b, 0, 0, 0)),
                      pl.BlockSpec((1, N_CHAINS, 1, MLSTM_HEAD_DIM), lambda b: (b, 0, 0, 0)),
                      pl.BlockSpec((1, N_CHAINS, 1, LANES), lambda b: (b, 0, 0, 0))]
    return pl.pallas_call(
        functools.partial(_mlstm_kernel, n_chunks=nc, has_state=state is not None, emit_state=emit_state),
        grid=(n_batch,), in_specs=in_specs, out_specs=out_specs, out_shape=out_shape,
        scratch_shapes=[pltpu.VMEM((N_CHAINS, MLSTM_HEAD_DIM, AUG), F32), pltpu.VMEM((seq, MLSTM_WIDTH), F32)],
        compiler_params=_cparams("parallel"), name="mlstm",
    )(*args)


def _attn_kernel(*refs, tq, has_cache):
    if has_cache:
        q_ref, k_ref, v_ref, ck_ref, cv_ref, o_ref = refs
    else:
        q_ref, k_ref, v_ref, o_ref = refs
    q = q_ref[...]
    qs = jnp.concatenate([q[:, g * ATT_HEAD_DIM:(g + 1) * ATT_HEAD_DIM] for g in range(ATT_GROUP)], axis=0)
    nt = (((1,), (1,)), ((), ()))
    seq = k_ref.shape[0]
    kc = min(seq, ATT_KEY_CHUNK)
    chunks = [(k_ref, v_ref, c * kc) for c in range(seq // kc)]
    if has_cache:
        chunks.insert(0, (ck_ref, cv_ref, None))
    m = l = o = None
    for kr, vr, start in chunks:
        kk, vv = (kr[...], vr[...]) if start is None else (kr[start:start + kc, :], vr[start:start + kc, :])
        s = lax.dot_general(qs, kk.astype(BF16), nt, preferred_element_type=F32)
        mc = jnp.max(s, axis=-1, keepdims=True)
        if m is None:
            m = mc
            p = jnp.exp2(s - m)
            l = jnp.sum(p, axis=-1, keepdims=True)
            o = _dot(p.astype(BF16), vv.astype(BF16))
        else:
            m_new = jnp.maximum(m, mc)
            alpha = jnp.exp2(m - m_new)
            p = jnp.exp2(s - m_new)
            l = alpha * l + jnp.sum(p, axis=-1, keepdims=True)
            o = alpha * o + _dot(p.astype(BF16), vv.astype(BF16))
            m = m_new
    o = o / l
    for g in range(ATT_GROUP):
        o_ref[:, g * ATT_HEAD_DIM:(g + 1) * ATT_HEAD_DIM] = o[g * tq:(g + 1) * tq].astype(BF16)


def _attention(q, k, v, n_batch, seq, cache=None, tq=256):
    nqb = seq // tq
    gw = ATT_GROUP * ATT_HEAD_DIM
    in_specs = [pl.BlockSpec((tq, gw), lambda b, h, i: (b * nqb + i, h)),
                pl.BlockSpec((seq, ATT_HEAD_DIM), lambda b, h, i: (b, h)),
                pl.BlockSpec((seq, ATT_HEAD_DIM), lambda b, h, i: (b, h))]
    args = [q, k, v]
    if cache is not None:
        past = cache[0].shape[0] // n_batch
        in_specs += [pl.BlockSpec((past, ATT_HEAD_DIM), lambda b, h, i: (b, h))] * 2
        args += list(cache)
    return pl.pallas_call(
        functools.partial(_attn_kernel, tq=tq, has_cache=cache is not None),
        grid=(n_batch, ATT_KV_HEADS, nqb), in_specs=in_specs,
        out_specs=pl.BlockSpec((tq, gw), lambda b, h, i: (b * nqb + i, h)),
        out_shape=jax.ShapeDtypeStruct((n_batch * seq, ATT_WIDTH), BF16),
        compiler_params=_cparams("parallel", "parallel", "parallel"), name="attention",
    )(*args)


def _mix_kernel(attn_ref, hm_ref, om_ref, x_ref, mod_ref, mg_ref, wo_ref, n2_ref, rw_ref, rb_ref, *rest, sub):
    for r0 in range(0, x_ref.shape[0], sub):
        _mix_rows(slice(r0, r0 + sub), attn_ref, hm_ref, om_ref, x_ref, mod_ref, mg_ref, wo_ref, n2_ref, rw_ref,
                  rb_ref, *rest[-3:])


def _mix_rows(rows, attn_ref, hm_ref, om_ref, x_ref, mod_ref, mg_ref, wo_ref, n2_ref, rw_ref, rb_ref,
              x1_ref, h2_ref, route_ref):
    mod = mod_ref[0]
    gate1 = mod[:, 2 * D_MODEL:3 * D_MODEL]
    shift2, scale2 = mod[:, 3 * D_MODEL:4 * D_MODEL], mod[:, 4 * D_MODEL:5 * D_MODEL]
    hm = hm_ref[rows, :]
    mg = mg_ref[...]
    parts = []
    for hd in range(MLSTM_HEADS):
        sl = slice(hd * MLSTM_HEAD_DIM, (hd + 1) * MLSTM_HEAD_DIM)
        parts.append(_rms(hm[:, sl], mg[:, sl]))
    om = om_ref[rows, :]
    hmg = jnp.concatenate(parts, axis=1) * (1.0 / (1.0 + jnp.exp(-om)))
    y = _dot(attn_ref[rows, :], wo_ref[:ATT_WIDTH, :]) + _dot(hmg.astype(BF16), wo_ref[ATT_WIDTH:, :])
    x1 = x_ref[rows, :] + gate1 * y
    x1_ref[rows, :] = x1
    h2 = _rms(x1, n2_ref[...]) * (1.0 + scale2) + shift2
    h2_hi, h2_lo = _split_hi_lo(h2)
    h2_ref[rows, :] = h2

    lg = _dot_hi_lo(h2_hi, h2_lo, rw_ref) + rb_ref[...]
    lane = lax.broadcasted_iota(jnp.int32, lg.shape, 1).astype(F32)
    neg = -jnp.inf
    first = lambda hit: jnp.min(jnp.where(hit, lane, float(LANES)), axis=-1, keepdims=True)
    gl = jnp.where(lane < N_GROUPS, lg, neg)
    gmax = jnp.max(gl, axis=-1, keepdims=True)
    grp = first(gl == gmax)
    p_grp = 1.0 / jnp.sum(jnp.exp(gl - gmax), axis=-1, keepdims=True)
    lo = ROUTER_LANE0 + grp * EXPERTS_PER_GROUP
    el = jnp.where((lane >= lo) & (lane < lo + EXPERTS_PER_GROUP), lg, neg)
    m1 = jnp.max(el, axis=-1, keepdims=True)
    i1 = first(el == m1)
    el2 = jnp.where(lane == i1, neg, el)
    m2 = jnp.max(el2, axis=-1, keepdims=True)
    i2 = first(el2 == m2)
    r = jnp.exp(m2 - m1)
    w1 = p_grp / (1.0 + r)
    w2 = w1 * r
    route_ref[rows, :] = jnp.where(lane == 0.0, i1 - ROUTER_LANE0, jnp.where(lane == 1.0, i2 - ROUTER_LANE0,
                                   jnp.where(lane == 2.0, w1, jnp.where(lane == 3.0, w2, 0.0))))


def _mix_out(attn, hm, om, x2d, mod3, row_of_tile, mg, w_out, n2, rw_cat, rb, h2_all, tile0, t_total, tm=512):
    t = x2d.shape[0]
    row = lambda i: (i, 0)
    in_specs = [pl.BlockSpec((tm, ATT_WIDTH), row), pl.BlockSpec((tm, MLSTM_WIDTH), row),
                pl.BlockSpec((tm, MLSTM_WIDTH), row), pl.BlockSpec((tm, D_MODEL), row),
                pl.BlockSpec((1, 1, N_MOD * D_MODEL), lambda i: (row_of_tile(i), 0, 0)),
                _const_spec((1, MLSTM_WIDTH)), _const_spec((D_MODEL, D_MODEL)), _const_spec((1, D_MODEL)),
                _const_spec((D_MODEL, 2 * LANES)), _const_spec((1, LANES))]
    args = [attn, hm, om, x2d, mod3, mg, w_out, n2, rw_cat, rb]
    aliases = {}
    if h2_all is not None:
        aliases = {len(args): 1}
        in_specs.append(pl.BlockSpec(memory_space=pl.ANY))
        args.append(h2_all)
    return pl.pallas_call(
        functools.partial(_mix_kernel, sub=256), grid=(t // tm,), in_specs=in_specs,
        out_specs=[pl.BlockSpec((tm, D_MODEL), row), pl.BlockSpec((tm, D_MODEL), lambda i: (tile0 + i, 0)),
                   pl.BlockSpec((tm, LANES), row)],
        out_shape=[jax.ShapeDtypeStruct((t, D_MODEL), F32), jax.ShapeDtypeStruct((t_total, D_MODEL), F32),
                   jax.ShapeDtypeStruct((t, LANES), F32)],
        input_output_aliases=aliases,
        compiler_params=_cparams("parallel"), name="mix_out",
    )(*args)


def _rank_kernel(route_ref, rank_ref, cnt_ref, run_ref, tri_ref):
    tr = route_ref.shape[0]

    @pl.when(pl.program_id(0) == 0)
    def _():
        run_ref[...] = jnp.zeros(run_ref.shape, F32)
        tri_ref[...] = (lax.broadcasted_iota(jnp.int32, (tr, tr), 1)
                        < lax.broadcasted_iota(jnp.int32, (tr, tr), 0)).astype(BF16)

    route = route_ref[...]
    lane = lax.broadcasted_iota(jnp.int32, route.shape, 1).astype(F32)
    hit1, hit2 = lane == route[:, 0:1], lane == route[:, 1:2]
    onehot = jnp.where(hit1, 1.0, jnp.where(hit2, 1.0, 0.0))
    before = _dot(tri_ref[...], onehot.astype(BF16)) + run_ref[0:1, :]
    r1 = jnp.sum(jnp.where(hit1, before, 0.0), axis=-1, keepdims=True)
    r2 = jnp.sum(jnp.where(hit2, before, 0.0), axis=-1, keepdims=True)
    rank_ref[...] = jnp.where(lane == 0.0, r1, jnp.where(lane == 1.0, r2, 0.0))
    run_ref[...] = run_ref[...] + jnp.sum(onehot, axis=0, keepdims=True)
    cnt_ref[...] = run_ref[...]


def _ranks(route, tr=512):
    t = route.shape[0]
    return pl.pallas_call(
        _rank_kernel, grid=(t // tr,),
        in_specs=[pl.BlockSpec((tr, LANES), lambda i: (i, 0))],
        out_specs=[pl.BlockSpec((tr, LANES), lambda i: (i, 0)), pl.BlockSpec((8, LANES), lambda i: (0, 0))],
        out_shape=[jax.ShapeDtypeStruct((t, LANES), F32), jax.ShapeDtypeStruct((8, LANES), F32)],
        scratch_shapes=[pltpu.VMEM((8, LANES), F32), pltpu.VMEM((tr, tr), BF16)],
        compiler_params=_cparams("arbitrary"), name="expert_ranks",
    )(route)


def _pos_kernel(route_ref, rank_ref, start_ref, o_ref):
    route, rank = route_ref[...], rank_ref[...]
    lane = lax.broadcasted_iota(jnp.int32, route.shape, 1).astype(F32)
    start = start_ref[...]
    first = lambda col: jnp.sum(jnp.where(lane == route[:, col:col + 1], start, 0.0), axis=-1, keepdims=True)
    p1 = first(0) + rank[:, 0:1]
    p2 = first(1) + rank[:, 1:2]
    tile = jnp.where(lane == 0.0, p1, jnp.where(lane == 1.0, p2, 0.0))
    o_ref[...] = tile.T[:8, :].astype(jnp.int32)


def _positions(route, ranks, starts, tr=512):
    t = route.shape[0]
    pos = pl.pallas_call(
        _pos_kernel, grid=(t // tr,),
        in_specs=[pl.BlockSpec((tr, LANES), lambda i: (i, 0)), pl.BlockSpec((tr, LANES), lambda i: (i, 0)),
                  pl.BlockSpec((1, LANES), lambda i: (0, 0))],
        out_specs=pl.BlockSpec((8, tr), lambda i: (0, i)),
        out_shape=jax.ShapeDtypeStruct((8, t), jnp.int32),
        compiler_params=_cparams("parallel"), name="pair_rows",
    )(route, ranks, starts)
    return pos[:2].reshape(-1)


def _routing_plan(route, ranks, counts):
    t = route.shape[0]
    n_tiles = 2 * t // TM_MOE + N_EXPERTS
    cnt = counts[0, :N_EXPERTS].astype(jnp.int32)
    padded = (cnt + TM_MOE - 1) // TM_MOE * TM_MOE
    ends = jnp.cumsum(padded)
    starts = ends - padded
    pos = _positions(route, ranks, _pad_lanes(starts.astype(F32)[None]))
    used = padded > 0
    ids = jnp.arange(N_EXPERTS, dtype=jnp.int32)
    last_used = jnp.max(jnp.where(used[None, :] & (ids[None, :] <= ids[:, None]), ids[None, :], -1), axis=1)
    weight_of = jnp.maximum(last_used, jnp.min(jnp.where(used, ids, N_EXPERTS - 1)))
    tails = jnp.where(used, ends - TM_MOE, -1).astype(jnp.int32)
    return pos, starts // TM_MOE, padded // TM_MOE, weight_of, tails, n_tiles * TM_MOE


def _row_copies(pos_ref, n_pairs, tok0, n_rows, make_copy):
    def body(r, carry):
        for k in range(2):
            make_copy(k, r, pos_ref[k * n_pairs + tok0 + r]).start()
        return carry
    lax.fori_loop(0, n_rows, body, 0, unroll=8)


def _dispatch_kernel(pos_ref, tail_ref, h_ref, xs_ref, zero_ref, sem, *, td, n_tokens):
    i = pl.program_id(0)

    @pl.when(i == 0)
    def _():
        zero_ref[...] = jnp.zeros(zero_ref.shape, F32)
        tail_copy = lambda e: pltpu.make_async_copy(
            zero_ref, xs_ref.at[pl.ds(pl.multiple_of(tail_ref[e], TM_MOE), TM_MOE)], sem)
        for e in range(N_EXPERTS):
            pl.when(tail_ref[e] >= 0)(lambda e=e: tail_copy(e).start())
        for e in range(N_EXPERTS):
            pl.when(tail_ref[e] >= 0)(lambda e=e: tail_copy(e).wait())

    _row_copies(pos_ref, n_tokens, i * td, td,
                lambda k, r, p: pltpu.make_async_copy(h_ref.at[pl.ds(r, 1)], xs_ref.at[pl.ds(p, 1)], sem))
    for _ in range(2):
        pltpu.make_async_copy(h_ref, xs_ref.at[pl.ds(0, td)], sem).wait()


def _dispatch(pos, tails, h2, n_rows, td=256):
    t = h2.shape[0]
    return pl.pallas_call(
        functools.partial(_dispatch_kernel, td=td, n_tokens=t),
        grid_spec=pltpu.PrefetchScalarGridSpec(
            num_scalar_prefetch=2, grid=(t // td,),
            in_specs=[pl.BlockSpec((td, D_MODEL), lambda i, *_: (i, 0))],
            out_specs=pl.BlockSpec(memory_space=pl.ANY),
            scratch_shapes=[pltpu.VMEM((TM_MOE, D_MODEL), F32), pltpu.SemaphoreType.DMA]),
        out_shape=jax.ShapeDtypeStruct((n_rows, D_MODEL), F32),
        compiler_params=_cparams("arbitrary"), name="dispatch",
    )(pos, tails, h2)


def _expert_kernel(first_ref, count_ref, wsel_ref, xs_ref, wg_ref, wu_ref, wd_ref, ys_ref,
                   xbuf, ybuf, wgb, wub, wdb, xsem, ysem):
    e = pl.program_id(0)
    t0, n = first_ref[e], count_ref[e]
    rows = lambda tile: pl.ds(pl.multiple_of(tile * TM_MOE, TM_MOE), TM_MOE)
    x_copy = lambda tile, s: pltpu.make_async_copy(xs_ref.at[rows(tile)], xbuf.at[s], xsem.at[s])
    y_copy = lambda tile, s: pltpu.make_async_copy(ybuf.at[s], ys_ref.at[rows(tile)], ysem.at[s])

    @pl.when(n > 0)
    def _():
        x_copy(t0, 0).start()
        wgb[...] = wg_ref[0].astype(BF16)
        wub[...] = wu_ref[0].astype(BF16)
        wdb[...] = wd_ref[0].astype(BF16)

        def body(i, carry):
            s = i % 2

            @pl.when(i + 1 < n)
            def _():
                x_copy(t0 + i + 1, 1 - s).start()

            x_copy(t0 + i, s).wait()

            @pl.when(i >= 2)
            def _():
                y_copy(t0 + i - 2, s).wait()

            x = xbuf[s].astype(BF16)
            g = _dot(x, wgb[...])
            u = _dot(x, wub[...])
            a = (g / (1.0 + jnp.exp(-g))) * u
            ybuf[s] = _dot(a.astype(BF16), wdb[...])
            y_copy(t0 + i, s).start()
            return carry

        lax.fori_loop(0, n, body, 0)

        @pl.when(n >= 2)
        def _():
            y_copy(t0 + n - 2, n % 2).wait()

        y_copy(t0 + n - 1, (n - 1) % 2).wait()


def _experts(first_tile, tile_count, weight_of, xs, wg, wu, wd):
    wspec = lambda shape: pl.BlockSpec((1,) + shape, lambda e, ft, tc, ws: (ws[e], 0, 0))
    return pl.pallas_call(
        _expert_kernel,
        grid_spec=pltpu.PrefetchScalarGridSpec(
            num_scalar_prefetch=3, grid=(N_EXPERTS,),
            in_specs=[pl.BlockSpec(memory_space=pl.ANY), wspec((D_MODEL, D_EXPERT)), wspec((D_MODEL, D_EXPERT)),
                      wspec((D_EXPERT, D_MODEL))],
            out_specs=pl.BlockSpec(memory_space=pl.ANY),
            scratch_shapes=[pltpu.VMEM((2, TM_MOE, D_MODEL), F32), pltpu.VMEM((2, TM_MOE, D_MODEL), F32),
                            pltpu.VMEM((D_MODEL, D_EXPERT), BF16), pltpu.VMEM((D_MODEL, D_EXPERT), BF16),
                            pltpu.VMEM((D_EXPERT, D_MODEL), BF16),
                            pltpu.SemaphoreType.DMA((2,)), pltpu.SemaphoreType.DMA((2,))]),
        out_shape=jax.ShapeDtypeStruct(xs.shape, F32),
        compiler_params=_cparams("arbitrary"), name="experts",
    )(first_tile, tile_count, weight_of, xs, wg, wu, wd)


def _combine_kernel(pos_ref, x1_ref, route_ref, mod_ref, ys_ref, o_ref, ybuf, sem, *, tc, n_tokens, tok0):
    _row_copies(pos_ref, n_tokens, tok0 + pl.program_id(0) * tc, tc,
                lambda k, r, p: pltpu.make_async_copy(ys_ref.at[pl.ds(p, 1)], ybuf.at[k, pl.ds(r, 1)], sem))
    for k in range(2):
        pltpu.make_async_copy(ys_ref.at[pl.ds(0, tc)], ybuf.at[k], sem).wait()
    route = route_ref[...]
    gate2 = mod_ref[0][:, 5 * D_MODEL:]
    o_ref[...] = x1_ref[...] + gate2 * (route[:, 2:3] * ybuf[0] + route[:, 3:4] * ybuf[1])


def _combine(pos, x1, route, mod3, row_of_tile, ys, tok0, n_tokens, tc=256):
    t = x1.shape[0]
    row = lambda i, *_: (i, 0)
    return pl.pallas_call(
        functools.partial(_combine_kernel, tc=tc, n_tokens=n_tokens, tok0=tok0),
        grid_spec=pltpu.PrefetchScalarGridSpec(
            num_scalar_prefetch=1, grid=(t // tc,),
            in_specs=[pl.BlockSpec((tc, D_MODEL), row), pl.BlockSpec((tc, LANES), row),
                      pl.BlockSpec((1, 1, N_MOD * D_MODEL), lambda i, *_: (row_of_tile(i), 0, 0)),
                      pl.BlockSpec(memory_space=pl.ANY)],
            out_specs=pl.BlockSpec((tc, D_MODEL), row),
            scratch_shapes=[pltpu.VMEM((2, tc, D_MODEL), F32), pltpu.SemaphoreType.DMA]),
        out_shape=jax.ShapeDtypeStruct((t, D_MODEL), F32),
        compiler_params=_cparams("arbitrary"), name="combine",
    )(pos, x1, route, mod3, ys)


def _rope_tables(seq):
    pos = np.arange(seq)
    n_freq = ATT_HEAD_DIM // 4
    inv = ROPE_THETA ** (-np.arange(n_freq, dtype=np.float32) / n_freq)
    ang = np.concatenate([(pos // GRID_W).astype(np.float32)[:, None] * inv,
                          (pos % GRID_W).astype(np.float32)[:, None] * inv], axis=-1).astype(np.float32)
    ang = jnp.asarray(ang)
    cos, sin = jnp.cos(ang), jnp.sin(ang)
    cos_t = jnp.repeat(cos, 2, axis=1)
    sin_t = jnp.stack([-sin, sin], axis=-1).reshape(seq, ATT_HEAD_DIM)
    return cos_t, sin_t


def _pad_lanes(a):
    return jnp.pad(a, ((0, 0), (0, LANES - a.shape[1])))


def _layer(x2d, n_batch, seq, mod3, row_of_tile, lw, *, rope_tabs, cache, state, emit, h2_all, tile0, t_total):
    (n1, n2, w_main, wg_cat, bg, qg, kg, mg, w_out, rw_cat, rb) = lw
    cos_t, sin_t = rope_tabs
    outs = _project(x2d, mod3, row_of_tile(256), n1, w_main, wg_cat, bg, qg, kg, cos_t, sin_t,
                    rope=cache is not None, emit_kv=emit)
    q, k, v, qm, kmt, vm, om, g = outs[:8]
    rowq, colq = _gate_scans(g)
    attn = _attention(q, k, v, n_batch, seq, cache=cache)
    ml = _mlstm(qm, kmt, vm, rowq, colq, n_batch, seq, state=state, emit_state=emit)
    mixed = _mix_out(attn, ml[0], om, x2d, mod3, row_of_tile(MIX_TILE), mg, w_out, n2, rw_cat, rb,
                     h2_all, tile0, t_total, tm=MIX_TILE)
    return mixed, outs[8:], ml[1:]


def kernel(x_prompt, x_sample, cache_k, cache_v, state_C, state_n, state_m, c, c_ctx, mod_w, mod_b, norm1_g, norm2_g,
           w_in, b_gates, q_norm_g, k_norm_g, mlstm_norm_g, w_out, router_group_w, router_group_b, router_expert_w,
           router_expert_b, expert_w_gate, expert_w_up, expert_w_down):
    assert mod_w.shape[0] == 1, "single-layer stack"
    n_ctx, s_ctx, _ = x_prompt.shape
    n_lat, s_lat, _ = x_sample.shape
    t_ctx, t_lat = n_ctx * s_ctx, n_lat * s_lat
    t_all = t_ctx + t_lat
    ctx_row = n_lat

    cond = jnp.concatenate([c, c_ctx[None], jnp.zeros((8 - n_lat - 1, D_MODEL), F32)], axis=0)
    mod3 = _modulation(cond, mod_w[0], mod_b[0][None]).reshape(8, 1, N_MOD * D_MODEL)

    rw =jnp.concatenate([router_group_w[0], jnp.moveaxis(router_expert_w[0], 0, 1).reshape(D_MODEL, N_EXPERTS)], axis=1)
    rb = _pad_lanes(jnp.concatenate([router_group_b[0], router_expert_b[0].reshape(-1)])[None])
    w_in_t = w_in[0].T
    lw = (norm1_g, norm2_g, _cast_bf16_t(w_in_t, MAIN_WIDTH), _gate_cols(w_in_t), _pad_lanes(b_gates),
          q_norm_g, k_norm_g, mlstm_norm_g, _cast_bf16(w_out, D_MODEL), _hi_lo_cat(_pad_lanes(rw)), rb)
    rope_tabs = _rope_tables(s_lat)

    ctx_rows = lambda tm: (lambda i: ctx_row)
    lat_rows = lambda tm: (lambda i: i // (s_lat // tm))
    (x1p, h2_all, routep), (ka, va), (s_c, s_n, s_m) = _layer(
        x_prompt.reshape(t_ctx, D_MODEL), n_ctx, s_ctx, mod3, ctx_rows, lw,
        rope_tabs=rope_tabs, cache=None, state=None, emit=True, h2_all=None, tile0=0, t_total=t_all)

    caug0 = jnp.concatenate([state_C[:, 0], state_n[:, 0][..., None],
                             jnp.zeros(state_n[:, 0].shape + (LANES - 1,), F32)], axis=-1)
    caug0 = caug0.reshape(n_lat, N_CHAINS, MLSTM_HEAD_DIM, AUG)
    past = cache_k.shape[2]
    cache = (cache_k[:, 0].reshape(n_lat * past, KV_WIDTH), cache_v[:, 0].reshape(n_lat * past, KV_WIDTH))
    (x1s, h2_all, routes), _, _ = _layer(
        x_sample.reshape(t_lat, D_MODEL), n_lat, s_lat, mod3, lat_rows, lw,
        rope_tabs=rope_tabs, cache=cache, state=(caug0, state_m[:, 0].reshape(-1)), emit=False,
        h2_all=h2_all, tile0=t_ctx // MIX_TILE, t_total=t_all)

    route = jnp.concatenate([routep, routes], axis=0)
    ranks, counts = _ranks(route)
    pos, first_tile, tile_count, weight_of, tails, n_rows = _routing_plan(route, ranks, counts)
    xs = _dispatch(pos, tails, h2_all, n_rows)
    y_sorted = _experts(first_tile, tile_count, weight_of, xs,
                        expert_w_gate[0], expert_w_up[0], expert_w_down[0])
    yp = _combine(pos, x1p, routep, mod3, ctx_rows(256), y_sorted, 0, t_all)
    ys = _combine(pos, x1s, routes, mod3, lat_rows(256), y_sorted, t_ctx, t_all)

    kv_shape = (n_ctx, 1, s_ctx, ATT_KV_HEADS, ATT_HEAD_DIM)
    return (yp.reshape(x_prompt.shape), ys.reshape(x_sample.shape), ka.reshape(kv_shape), va.reshape(kv_shape),
            s_c.reshape(n_ctx, 1, 2, MLSTM_HEADS, MLSTM_HEAD_DIM, MLSTM_HEAD_DIM),
            s_n.reshape(n_ctx, 1, 2, MLSTM_HEADS, MLSTM_HEAD_DIM), s_m[..., 0, 0].reshape(n_ctx, 1, 2, MLSTM_HEADS))
```

```python
import functools

import numpy as np
import jax
import jax.numpy as jnp
from jax import lax
from jax.experimental import pallas as pl
from jax.experimental.pallas import tpu as pltpu

F32 = jnp.float32
BF16 = jnp.bfloat16

D_MODEL = 2048
GRID_W = 64
ATT_HEADS = 8
ATT_KV_HEADS = 2
ATT_HEAD_DIM = 128
ATT_GROUP = ATT_HEADS // ATT_KV_HEADS
ATT_WIDTH = ATT_HEADS * ATT_HEAD_DIM
KV_WIDTH = ATT_KV_HEADS * ATT_HEAD_DIM
ROPE_THETA = 10000.0
MLSTM_HEADS = 4
MLSTM_HEAD_DIM = 256
MLSTM_WIDTH = MLSTM_HEADS * MLSTM_HEAD_DIM
CHUNK = 256
N_GATES = 4 * MLSTM_HEADS
N_CHAINS = 2 * MLSTM_HEADS
MAIN_WIDTH = ATT_WIDTH + 2 * KV_WIDTH + 4 * MLSTM_WIDTH
N_GROUPS = 4
EXPERTS_PER_GROUP = 8
N_EXPERTS = N_GROUPS * EXPERTS_PER_GROUP
D_EXPERT = 512
N_MOD = 6
TM_MOE = 256
MIX_TILE = 512
ATT_KEY_CHUNK = 512
Q_SCALE = ATT_HEAD_DIM ** -0.5 * float(np.log2(np.e))
EPS = 1e-6

LANES = 128
AUG = MLSTM_HEAD_DIM + LANES
N_SCAN_ROWS = 5 * N_CHAINS
ROUTER_LANE0 = N_GROUPS
VMEM_LIMIT = 56 * 1024 * 1024


def _cparams(*sem):
    return pltpu.CompilerParams(dimension_semantics=sem, vmem_limit_bytes=VMEM_LIMIT)


def _const_spec(shape):
    nd = len(shape)
    return pl.BlockSpec(shape, lambda *_: (0,) * nd, pipeline_mode=pl.Buffered(1))


def _split_hi_lo(x):
    hi = x.astype(BF16)
    lo = (x - hi.astype(F32)).astype(BF16)
    return hi, lo


def _dot(a, b):
    return jnp.dot(a, b, preferred_element_type=F32)


def _hi_lo_cat(w):
    return jnp.concatenate(_split_hi_lo(w), axis=1)


def _dot_hi_lo(a_hi, a_lo, w_ref):
    r = _dot(a_hi, w_ref[...])
    return r[:, :LANES] + r[:, LANES:] + _dot(a_lo, w_ref[:, :LANES])


def _rms(x, g):
    return x * lax.rsqrt(jnp.mean(x * x, axis=-1, keepdims=True) + EPS) * g


def _mod_kernel(c_ref, w_ref, b_ref, o_ref):
    c = c_ref[...]
    s = c / (1.0 + jnp.exp(-c))
    s_hi = s.astype(BF16).astype(F32)
    lhs = jnp.concatenate([s_hi, s - s_hi], axis=0).astype(BF16)
    w_hi, w_lo = _split_hi_lo(w_ref[...])
    r = _dot(lhs, w_hi)
    r2 = _dot(lhs, w_lo)
    o_ref[...] = r[:8] + r[8:] + r2[:8] + b_ref[...]


def _modulation(cond, mod_w, mod_b):
    n = mod_w.shape[1]
    tn = 512
    return pl.pallas_call(
        _mod_kernel,
        grid=(n // tn,),
        in_specs=[pl.BlockSpec((8, D_MODEL), lambda j: (0, 0)),
                  pl.BlockSpec((D_MODEL, tn), lambda j: (0, j)),
                  pl.BlockSpec((1, tn), lambda j: (0, j))],
        out_specs=pl.BlockSpec((8, tn), lambda j: (0, j)),
        out_shape=jax.ShapeDtypeStruct((8, n), F32),
        compiler_params=_cparams("parallel"),
        name="modulation",
    )(cond, mod_w, mod_b)


def _cast_kernel(w_ref, o_ref):
    o_ref[...] = w_ref[0].astype(BF16)


def _cast_bf16(w, n_cols, tn=512):
    rows = w.shape[1]
    return pl.pallas_call(
        _cast_kernel, grid=(n_cols // tn,),
        in_specs=[pl.BlockSpec((1, rows, tn), lambda j: (0, 0, j))],
        out_specs=pl.BlockSpec((rows, tn), lambda j: (0, j)),
        out_shape=jax.ShapeDtypeStruct((rows, n_cols), BF16),
        compiler_params=_cparams("parallel"), name="cast_bf16",
    )(w)


def _cast_t_kernel(wt_ref, o_ref):
    o_ref[...] = wt_ref[...].T.astype(BF16)


def _cast_bf16_t(wt, n_cols, tn=512):
    rows = wt.shape[1]
    return pl.pallas_call(
        _cast_t_kernel, grid=(n_cols // tn,),
        in_specs=[pl.BlockSpec((tn, rows), lambda j: (j, 0))],
        out_specs=pl.BlockSpec((rows, tn), lambda j: (0, j)),
        out_shape=jax.ShapeDtypeStruct((rows, n_cols), BF16),
        compiler_params=_cparams("parallel"), name="cast_bf16_t",
    )(wt)


def _gate_cols_kernel(wt_ref, o_ref):
    sub = lax.broadcasted_iota(jnp.int32, wt_ref.shape, 0)
    hi, lo = _split_hi_lo(jnp.where(sub < N_GATES, wt_ref[...], 0.0).T)
    o_ref[:, :LANES] = hi
    o_ref[:, LANES:] = lo


def _gate_cols(wt):
    rows = wt.shape[1]
    return pl.pallas_call(
        _gate_cols_kernel, grid=(1,),
        in_specs=[pl.BlockSpec((LANES, rows), lambda i: (MAIN_WIDTH // LANES, 0))],
        out_specs=pl.BlockSpec((rows, 2 * LANES), lambda i: (0, 0)),
        out_shape=jax.ShapeDtypeStruct((rows, 2 * LANES), BF16),
        compiler_params=_cparams("arbitrary"), name="gate_cols",
    )(wt)


def _pair_swap(x):
    lane = lax.broadcasted_iota(jnp.int32, x.shape, 1)
    return jnp.where((lane & 1) == 0, pltpu.roll(x, LANES - 1, 1), pltpu.roll(x, 1, 1))


def _proj_kernel(x_ref, mod_ref, n1_ref, w_ref, wg_ref, bg_ref, qg_ref, kg_ref, cos_ref, sin_ref,
                 q_ref, k_ref, v_ref, qm_ref, kmt_ref, vm_ref, om_ref, g_ref, *kv_refs, rope, tm):
    mod = mod_ref[0]
    shift, scale = mod[:, :D_MODEL], mod[:, D_MODEL:2 * D_MODEL]
    h = _rms(x_ref[...], n1_ref[...]) * (1.0 + scale) + shift
    h_hi, h_lo = _split_hi_lo(h)

    def rot(seg):
        return seg * cos_ref[...] + _pair_swap(seg) * sin_ref[...] if rope else seg

    qa = _dot(h_hi, w_ref[:, :ATT_WIDTH])
    for hh in range(ATT_HEADS):
        sl = slice(hh * ATT_HEAD_DIM, (hh + 1) * ATT_HEAD_DIM)
        seg = rot(_rms(qa[:, sl], qg_ref[...]))
        q_ref[:, sl] = (seg * Q_SCALE).astype(BF16)

    kv = _dot(h_hi, w_ref[:, ATT_WIDTH:ATT_WIDTH + 2 * KV_WIDTH])
    for hh in range(ATT_KV_HEADS):
        sl = slice(hh * ATT_HEAD_DIM, (hh + 1) * ATT_HEAD_DIM)
        seg = _rms(kv[:, sl], kg_ref[...])
        if kv_refs:
            kv_refs[0][:, sl] = seg
        k_ref[:, sl] = rot(seg).astype(BF16)
    va = kv[:, KV_WIDTH:]
    if kv_refs:
        kv_refs[1][...] = va
    v_ref[...] = va.astype(BF16)

    c0 = ATT_WIDTH + 2 * KV_WIDTH
    qm_ref[...] = (_dot(h_hi, w_ref[:, c0:c0 + MLSTM_WIDTH]) * MLSTM_HEAD_DIM ** -0.5).astype(BF16)
    km = _dot(h_hi, w_ref[:, c0 + MLSTM_WIDTH:c0 + 2 * MLSTM_WIDTH])
    for cc in range(tm // CHUNK):
        kmt_ref[cc] = km[cc * CHUNK:(cc + 1) * CHUNK, :].T.astype(BF16)
    vm_ref[...] = _dot(h_hi, w_ref[:, c0 + 2 * MLSTM_WIDTH:c0 + 3 * MLSTM_WIDTH]).astype(BF16)
    om_ref[...] = _dot(h_hi, w_ref[:, c0 + 3 * MLSTM_WIDTH:c0 + 4 * MLSTM_WIDTH])
    g_ref[...] = _dot_hi_lo(h_hi, h_lo, wg_ref) + bg_ref[...]


def _project(x2d, mod3, row_of_tile, n1, w_main, wg_cat, bg, qg, kg, cos_t, sin_t, *, rope, emit_kv, tm=256):
    t = x2d.shape[0]
    n_pos = cos_t.shape[0] // tm
    row = lambda i: (i, 0)
    in_specs = [pl.BlockSpec((tm, D_MODEL), row),
                pl.BlockSpec((1, 1, N_MOD * D_MODEL), lambda i: (row_of_tile(i), 0, 0)),
                _const_spec((1, D_MODEL)),
                _const_spec((D_MODEL, MAIN_WIDTH)),
                _const_spec((D_MODEL, 2 * LANES)), _const_spec((1, LANES)),
                _const_spec((1, ATT_HEAD_DIM)), _const_spec((1, ATT_HEAD_DIM)),
                pl.BlockSpec((tm, ATT_HEAD_DIM), lambda i: (i % n_pos, 0)),
                pl.BlockSpec((tm, ATT_HEAD_DIM), lambda i: (i % n_pos, 0))]
    out_shape = [jax.ShapeDtypeStruct((t, ATT_WIDTH), BF16), jax.ShapeDtypeStruct((t, KV_WIDTH), BF16),
                 jax.ShapeDtypeStruct((t, KV_WIDTH), BF16), jax.ShapeDtypeStruct((t, MLSTM_WIDTH), BF16),
                 jax.ShapeDtypeStruct((t // CHUNK, MLSTM_WIDTH, CHUNK), BF16),
                 jax.ShapeDtypeStruct((t, MLSTM_WIDTH), BF16), jax.ShapeDtypeStruct((t, MLSTM_WIDTH), F32),
                 jax.ShapeDtypeStruct((t, LANES), F32)]
    out_specs = [pl.BlockSpec((tm, ATT_WIDTH), row), pl.BlockSpec((tm, KV_WIDTH), row),
                 pl.BlockSpec((tm, KV_WIDTH), row), pl.BlockSpec((tm, MLSTM_WIDTH), row),
                 pl.BlockSpec((tm // CHUNK, MLSTM_WIDTH, CHUNK), lambda i: (i, 0, 0)),
                 pl.BlockSpec((tm, MLSTM_WIDTH), row), pl.BlockSpec((tm, MLSTM_WIDTH), row),
                 pl.BlockSpec((tm, LANES), row)]
    if emit_kv:
        out_shape += [jax.ShapeDtypeStruct((t, KV_WIDTH), F32)] * 2
        out_specs += [pl.BlockSpec((tm, KV_WIDTH), row)] * 2
    return pl.pallas_call(
        functools.partial(_proj_kernel, rope=rope, tm=tm),
        grid=(t // tm,), in_specs=in_specs, out_specs=out_specs, out_shape=out_shape,
        compiler_params=_cparams("parallel"), name="in_proj",
    )(x2d, mod3, n1, w_main, wg_cat, bg, qg, kg, cos_t, sin_t)


def _lane_scan(x, op, fill, is_fwd, lane):
    s = 1
    while s < CHUNK:
        from_left = jnp.where(lane >= s, pltpu.roll(x, s, 1), fill)
        from_right = jnp.where(lane < CHUNK - s, pltpu.roll(x, CHUNK - s, 1), fill)
        x = op(x, jnp.where(is_fwd, from_left, from_right))
        s *= 2
    return x


def _scan_kernel(g_ref, row_ref, col_ref, *, n_chunks):
    lane = lax.broadcasted_iota(jnp.int32, (N_CHAINS, CHUNK), 1)
    is_fwd = lax.broadcasted_iota(jnp.int32, (N_CHAINS, CHUNK), 0) < MLSTM_HEADS
    for cc in range(n_chunks):
        gt = g_ref[cc * CHUNK:(cc + 1) * CHUNK, :].T
        fwd, bwd = gt[:N_CHAINS], gt[N_CHAINS:2 * N_CHAINS]
        li = jnp.where(is_fwd, fwd, pltpu.roll(bwd, MLSTM_HEADS, 0))
        f = jnp.where(is_fwd, pltpu.roll(fwd, MLSTM_HEADS, 0), bwd)
        lf = jnp.minimum(f, 0.0) - jnp.log(1.0 + jnp.exp(-jnp.abs(f)))
        b = _lane_scan(lf, jnp.add, 0.0, is_fwd, lane)
        a = li - b
        run_max = _lane_scan(a, jnp.maximum, -jnp.inf, is_fwd, lane)
        all_max = jnp.broadcast_to(jnp.max(a, axis=1, keepdims=True), a.shape)
        last = jnp.where(is_fwd, CHUNK - 1, 0)
        total = jnp.broadcast_to(jnp.sum(jnp.where(lane == last, b, 0.0), axis=1, keepdims=True), a.shape)
        rows = jnp.concatenate([a, b, run_max, all_max, total], axis=0)
        row_ref[cc] = rows
        padded = jnp.concatenate([rows, jnp.zeros((LANES - N_SCAN_ROWS, CHUNK), F32)], axis=0)
        col_ref[cc * CHUNK:(cc + 1) * CHUNK, :] = padded.T


def _gate_scans(g, tb=1024):
    t = g.shape[0]
    tb = min(tb, t)
    return pl.pallas_call(
        functools.partial(_scan_kernel, n_chunks=tb // CHUNK),
        grid=(t // tb,),
        in_specs=[pl.BlockSpec((tb, LANES), lambda i: (i, 0))],
        out_specs=[pl.BlockSpec((tb // CHUNK, N_SCAN_ROWS, CHUNK), lambda i: (i, 0, 0)),
                   pl.BlockSpec((tb, LANES), lambda i: (i, 0))],
        out_shape=[jax.ShapeDtypeStruct((t // CHUNK, N_SCAN_ROWS, CHUNK), F32),
                   jax.ShapeDtypeStruct((t, LANES), F32)],
        compiler_params=_cparams("parallel"), name="gate_scans",
    )(g)


def _mlstm_kernel(*refs, n_chunks, has_state, emit_state):
    it = iter(refs)
    q_ref, kt_ref, v_ref, row_ref, col_ref = [next(it) for _ in range(5)]
    c0_ref, m0_ref = (next(it), next(it)) if has_state else (None, None)
    o_ref = next(it)
    c_out, n_out, m_out = (next(it), next(it), next(it)) if emit_state else (None, None, None)
    caug, hb = next(it), next(it)

    b = pl.program_id(0)
    if has_state:
        caug[...] = c0_ref[0]
        m_init = tuple(jnp.full((1, 1), m0_ref[b * N_CHAINS + r], F32) for r in range(N_CHAINS))
    else:
        caug[...] = jnp.zeros(caug.shape, F32)
        m_init = tuple(jnp.zeros((1, 1), F32) for _ in range(N_CHAINS))

    sub = lax.broadcasted_iota(jnp.int32, (CHUNK, CHUNK), 0)
    lane = lax.broadcasted_iota(jnp.int32, (CHUNK, CHUNK), 1)
    ones_col = (lax.broadcasted_iota(jnp.int32, (CHUNK, LANES), 1) == 0).astype(BF16)

    def chain(r, cc, m_prev):
        d, hd = divmod(r, MLSTM_HEADS)
        t0 = pl.multiple_of(cc * CHUNK, CHUNK)
        hs = slice(hd * MLSTM_HEAD_DIM, (hd + 1) * MLSTM_HEAD_DIM)
        rows = row_ref[cc]
        cols = col_ref[pl.ds(t0, CHUNK), :]
        row = lambda k: rows[k * N_CHAINS + r:k * N_CHAINS + r + 1, :]
        col = lambda k: cols[:, k * N_CHAINS + r:k * N_CHAINS + r + 1]
        q = q_ref[pl.ds(t0, CHUNK), hs]
        kt = kt_ref[cc, hs, :]
        vaug = jnp.concatenate([v_ref[pl.ds(t0, CHUNK), hs], ones_col], axis=1)

        m_col = jnp.maximum(m_prev, col(2))
        keep = (lane <= sub) if d == 0 else (lane >= sub)
        w = jnp.where(keep, jnp.exp(row(0) - m_col), 0.0)
        w_inter = jnp.exp(m_prev - m_col)
        p = (_dot(q, kt) * w).astype(BF16)
        intra = _dot(p, vaug)
        inter = _dot(q, caug[r].astype(BF16))
        num = intra[:, :MLSTM_HEAD_DIM] + w_inter * inter[:, :MLSTM_HEAD_DIM]
        den = intra[:, MLSTM_HEAD_DIM:MLSTM_HEAD_DIM + 1] + w_inter * inter[:, MLSTM_HEAD_DIM:MLSTM_HEAD_DIM + 1]
        h_out = num / jnp.maximum(jnp.abs(den), jnp.exp(-(col(1) + m_col)))

        m_last = jnp.maximum(m_prev, row(3))
        kw = (kt.astype(F32) * jnp.exp(row(0) - m_last)).astype(BF16)
        caug[r] = jnp.exp(m_prev - m_last[:, :1]) * caug[r] + _dot(kw, vaug)
        return h_out, (row(4) + m_last)[:, :1]

    def body(c, ms):
        new = []
        for r in range(N_CHAINS):
            cc = c if r < MLSTM_HEADS else n_chunks - 1 - c
            h_out, m_new = chain(r, cc, ms[r])
            hd = r % MLSTM_HEADS
            dst = o_ref if r < MLSTM_HEADS else hb
            dst[pl.ds(pl.multiple_of(cc * CHUNK, CHUNK), CHUNK), hd * MLSTM_HEAD_DIM:(hd + 1) * MLSTM_HEAD_DIM] = h_out
            new.append(m_new)
        return tuple(new)

    ms = lax.fori_loop(0, n_chunks, body, m_init)
    o_ref[...] += hb[...]
    if emit_state:
        for r in range(N_CHAINS):
            c_out[0, r] = caug[r, :, :MLSTM_HEAD_DIM]
            n_out[0, r] = caug[r, :, MLSTM_HEAD_DIM:].T[:1, :]
            m_out[0, r] = jnp.broadcast_to(ms[r], (1, LANES))


def _mlstm(qm, kmt, vm, rowq, colq, n_batch, seq, state=None, emit_state=False):
    nc = seq // CHUNK
    mode = dict(pipeline_mode=pl.Buffered(1)) if seq * MLSTM_WIDTH * 4 > (2 << 20) else {}
    in_specs = [pl.BlockSpec((seq, MLSTM_WIDTH), lambda b: (b, 0), **mode),
                pl.BlockSpec((nc, MLSTM_WIDTH, CHUNK), lambda b: (b, 0, 0), **mode),
                pl.BlockSpec((seq, MLSTM_WIDTH), lambda b: (b, 0), **mode),
                pl.BlockSpec((nc, N_SCAN_ROWS, CHUNK), lambda b: (b, 0, 0), **mode),
                pl.BlockSpec((seq, LANES), lambda b: (b, 0), **mode)]
    args = [qm, kmt, vm, rowq, colq]
    if state is not None:
        in_specs += [pl.BlockSpec((1, N_CHAINS, MLSTM_HEAD_DIM, AUG), lambda b: (b, 0, 0, 0), **mode),
                     pl.BlockSpec(memory_space=pltpu.SMEM)]
        args += list(state)
    out_shape = [jax.ShapeDtypeStruct((n_batch * seq, MLSTM_WIDTH), F32)]
    out_specs = [pl.BlockSpec((seq, MLSTM_WIDTH), lambda b: (b, 0))]
    if emit_state:
        out_shape += [jax.ShapeDtypeStruct((n_batch, N_CHAINS, MLSTM_HEAD_DIM, MLSTM_HEAD_DIM), F32),
                      jax.ShapeDtypeStruct((n_batch, N_CHAINS, 1, MLSTM_HEAD_DIM), F32),
                      jax.ShapeDtypeStruct((n_batch, N_CHAINS, 1, LANES), F32)]
        out_specs += [pl.BlockSpec((1, N_CHAINS, MLSTM_HEAD_DIM, MLSTM_HEAD_DIM), lambda b: (b, 0, 0, 0)),
                      pl.BlockSpec((1, N_CHAINS, 1, MLSTM_HEAD_DIM), lambda b: (b, 0, 0, 0)),
                      pl.BlockSpec((1, N_CHAINS, 1, LANES), lambda b: (b, 0, 0, 0))]
    return pl.pallas_call(
        functools.partial(_mlstm_kernel, n_chunks=nc, has_state=state is not None, emit_state=emit_state),
        grid=(n_batch,), in_specs=in_specs, out_specs=out_specs, out_shape=out_shape,
        scratch_shapes=[pltpu.VMEM((N_CHAINS, MLSTM_HEAD_DIM, AUG), F32), pltpu.VMEM((seq, MLSTM_WIDTH), F32)],
        compiler_params=_cparams("parallel"), name="mlstm",
    )(*args)


def _attn_kernel(*refs, tq, has_cache):
    if has_cache:
        q_ref, k_ref, v_ref, ck_ref, cv_ref, o_ref = refs
    else:
        q_ref, k_ref, v_ref, o_ref = refs
    q = q_ref[...]
    qs = jnp.concatenate([q[:, g * ATT_HEAD_DIM:(g + 1) * ATT_HEAD_DIM] for g in range(ATT_GROUP)], axis=0)
    nt = (((1,), (1,)), ((), ()))
    seq = k_ref.shape[0]
    kc = min(seq, ATT_KEY_CHUNK)
    chunks = [(k_ref, v_ref, c * kc) for c in range(seq // kc)]
    if has_cache:
        chunks.insert(0, (ck_ref, cv_ref, None))
    m = l = o = None
    for kr, vr, start in chunks:
        kk, vv = (kr[...], vr[...]) if start is None else (kr[start:start + kc, :], vr[start:start + kc, :])
        s = lax.dot_general(qs, kk.astype(BF16), nt, preferred_element_type=F32)
        mc = jnp.max(s, axis=-1, keepdims=True)
        if m is None:
            m = mc
            p = jnp.exp2(s - m)
            l = jnp.sum(p, axis=-1, keepdims=True)
            o = _dot(p.astype(BF16), vv.astype(BF16))
        else:
            m_new = jnp.maximum(m, mc)
            alpha = jnp.exp2(m - m_new)
            p = jnp.exp2(s - m_new)
            l = alpha * l + jnp.sum(p, axis=-1, keepdims=True)
            o = alpha * o + _dot(p.astype(BF16), vv.astype(BF16))
            m = m_new
    o = o / l
    for g in range(ATT_GROUP):
        o_ref[:, g * ATT_HEAD_DIM:(g + 1) * ATT_HEAD_DIM] = o[g * tq:(g + 1) * tq].astype(BF16)


def _attention(q, k, v, n_batch, seq, cache=None, tq=256):
    nqb = seq // tq
    gw = ATT_GROUP * ATT_HEAD_DIM
    in_specs = [pl.BlockSpec((tq, gw), lambda b, h, i: (b * nqb + i, h)),
                pl.BlockSpec((seq, ATT_HEAD_DIM), lambda b, h, i: (b, h)),
                pl.BlockSpec((seq, ATT_HEAD_DIM), lambda b, h, i: (b, h))]
    args = [q, k, v]
    if cache is not None:
        past = cache[0].shape[0] // n_batch
        in_specs += [pl.BlockSpec((past, ATT_HEAD_DIM), lambda b, h, i: (b, h))] * 2
        args += list(cache)
    return pl.pallas_call(
        functools.partial(_attn_kernel, tq=tq, has_cache=cache is not None),
        grid=(n_batch, ATT_KV_HEADS, nqb), in_specs=in_specs,
        out_specs=pl.BlockSpec((tq, gw), lambda b, h, i: (b * nqb + i, h)),
        out_shape=jax.ShapeDtypeStruct((n_batch * seq, ATT_WIDTH), BF16),
        compiler_params=_cparams("parallel", "parallel", "parallel"), name="attention",
    )(*args)


def _mix_kernel(attn_ref, hm_ref, om_ref, x_ref, mod_ref, mg_ref, wo_ref, n2_ref, rw_ref, rb_ref, *rest, sub):
    for r0 in range(0, x_ref.shape[0], sub):
        _mix_rows(slice(r0, r0 + sub), attn_ref, hm_ref, om_ref, x_ref, mod_ref, mg_ref, wo_ref, n2_ref, rw_ref,
                  rb_ref, *rest[-3:])


def _mix_rows(rows, attn_ref, hm_ref, om_ref, x_ref, mod_ref, mg_ref, wo_ref, n2_ref, rw_ref, rb_ref,
              x1_ref, h2_ref, route_ref):
    mod = mod_ref[0]
    gate1 = mod[:, 2 * D_MODEL:3 * D_MODEL]
    shift2, scale2 = mod[:, 3 * D_MODEL:4 * D_MODEL], mod[:, 4 * D_MODEL:5 * D_MODEL]
    hm = hm_ref[rows, :]
    mg = mg_ref[...]
    parts = []
    for hd in range(MLSTM_HEADS):
        sl = slice(hd * MLSTM_HEAD_DIM, (hd + 1) * MLSTM_HEAD_DIM)
        parts.append(_rms(hm[:, sl], mg[:, sl]))
    om = om_ref[rows, :]
    hmg = jnp.concatenate(parts, axis=1) * (1.0 / (1.0 + jnp.exp(-om)))
    y = _dot(attn_ref[rows, :], wo_ref[:ATT_WIDTH, :]) + _dot(hmg.astype(BF16), wo_ref[ATT_WIDTH:, :])
    x1 = x_ref[rows, :] + gate1 * y
    x1_ref[rows, :] = x1
    h2 = _rms(x1, n2_ref[...]) * (1.0 + scale2) + shift2
    h2_hi, h2_lo = _split_hi_lo(h2)
    h2_ref[rows, :] = _pack_halves(h2)

    lg = _dot_hi_lo(h2_hi, h2_lo, rw_ref) + rb_ref[...]
    lane = lax.broadcasted_iota(jnp.int32, lg.shape, 1).astype(F32)
    neg = -jnp.inf
    first = lambda hit: jnp.min(jnp.where(hit, lane, float(LANES)), axis=-1, keepdims=True)
    gl = jnp.where(lane < N_GROUPS, lg, neg)
    gmax = jnp.max(gl, axis=-1, keepdims=True)
    grp = first(gl == gmax)
    p_grp = 1.0 / jnp.sum(jnp.exp(gl - gmax), axis=-1, keepdims=True)
    lo = ROUTER_LANE0 + grp * EXPERTS_PER_GROUP
    el = jnp.where((lane >= lo) & (lane < lo + EXPERTS_PER_GROUP), lg, neg)
    m1 = jnp.max(el, axis=-1, keepdims=True)
    i1 = first(el == m1)
    el2 = jnp.where(lane == i1, neg, el)
    m2 = jnp.max(el2, axis=-1, keepdims=True)
    i2 = first(el2 == m2)
    r = jnp.exp(m2 - m1)
    w1 = p_grp / (1.0 + r)
    w2 = w1 * r
    route_ref[rows, :] = jnp.where(lane == 0.0, i1 - ROUTER_LANE0, jnp.where(lane == 1.0, i2 - ROUTER_LANE0,
                                   jnp.where(lane == 2.0, w1, jnp.where(lane == 3.0, w2, 0.0))))


def _mix_out(attn, hm, om, x2d, mod3, row_of_tile, mg, w_out, n2, rw_cat, rb, h2_all, tile0, t_total, tm=512):
    t = x2d.shape[0]
    row = lambda i: (i, 0)
    in_specs = [pl.BlockSpec((tm, ATT_WIDTH), row), pl.BlockSpec((tm, MLSTM_WIDTH), row),
                pl.BlockSpec((tm, MLSTM_WIDTH), row), pl.BlockSpec((tm, D_MODEL), row),
                pl.BlockSpec((1, 1, N_MOD * D_MODEL), lambda i: (row_of_tile(i), 0, 0)),
                _const_spec((1, MLSTM_WIDTH)), _const_spec((D_MODEL, D_MODEL)), _const_spec((1, D_MODEL)),
                _const_spec((D_MODEL, 2 * LANES)), _const_spec((1, LANES))]
    args = [attn, hm, om, x2d, mod3, mg, w_out, n2, rw_cat, rb]
    aliases = {}
    if h2_all is not None:
        aliases = {len(args): 1}
        in_specs.append(pl.BlockSpec(memory_space=pl.ANY))
        args.append(h2_all)
    return pl.pallas_call(
        functools.partial(_mix_kernel, sub=256), grid=(t // tm,), in_specs=in_specs,
        out_specs=[pl.BlockSpec((tm, D_MODEL), row), pl.BlockSpec((tm, D_MODEL // 2), lambda i: (tile0 + i, 0)),
                   pl.BlockSpec((tm, LANES), row)],
        out_shape=[jax.ShapeDtypeStruct((t, D_MODEL), F32), jax.ShapeDtypeStruct((t_total, D_MODEL // 2), jnp.uint32),
                   jax.ShapeDtypeStruct((t, LANES), F32)],
        input_output_aliases=aliases,
        compiler_params=_cparams("parallel"), name="mix_out",
    )(*args)


def _rank_kernel(route_ref, rank_ref, cnt_ref, run_ref, tri_ref):
    tr = route_ref.shape[0]

    @pl.when(pl.program_id(0) == 0)
    def _():
        run_ref[...] = jnp.zeros(run_ref.shape, F32)
        tri_ref[...] = (lax.broadcasted_iota(jnp.int32, (tr, tr), 1)
                        < lax.broadcasted_iota(jnp.int32, (tr, tr), 0)).astype(BF16)

    route = route_ref[...]
    lane = lax.broadcasted_iota(jnp.int32, route.shape, 1).astype(F32)
    hit1, hit2 = lane == route[:, 0:1], lane == route[:, 1:2]
    onehot = jnp.where(hit1, 1.0, jnp.where(hit2, 1.0, 0.0))
    before = _dot(tri_ref[...], onehot.astype(BF16)) + run_ref[0:1, :]
    r1 = jnp.sum(jnp.where(hit1, before, 0.0), axis=-1, keepdims=True)
    r2 = jnp.sum(jnp.where(hit2, before, 0.0), axis=-1, keepdims=True)
    rank_ref[...] = jnp.where(lane == 0.0, r1, jnp.where(lane == 1.0, r2, 0.0))
    run_ref[...] = run_ref[...] + jnp.sum(onehot, axis=0, keepdims=True)
    cnt_ref[...] = run_ref[...]


def _ranks(route, tr=512):
    t = route.shape[0]
    return pl.pallas_call(
        _rank_kernel, grid=(t // tr,),
        in_specs=[pl.BlockSpec((tr, LANES), lambda i: (i, 0))],
        out_specs=[pl.BlockSpec((tr, LANES), lambda i: (i, 0)), pl.BlockSpec((8, LANES), lambda i: (0, 0))],
        out_shape=[jax.ShapeDtypeStruct((t, LANES), F32), jax.ShapeDtypeStruct((8, LANES), F32)],
        scratch_shapes=[pltpu.VMEM((8, LANES), F32), pltpu.VMEM((tr, tr), BF16)],
        compiler_params=_cparams("arbitrary"), name="expert_ranks",
    )(route)


def _pos_kernel(route_ref, rank_ref, start_ref, o_ref):
    route, rank = route_ref[...], rank_ref[...]
    lane = lax.broadcasted_iota(jnp.int32, route.shape, 1).astype(F32)
    start = start_ref[...]
    first = lambda col: jnp.sum(jnp.where(lane == route[:, col:col + 1], start, 0.0), axis=-1, keepdims=True)
    p1 = first(0) + rank[:, 0:1]
    p2 = first(1) + rank[:, 1:2]
    tile = jnp.where(lane == 0.0, p1, jnp.where(lane == 1.0, p2, 0.0))
    o_ref[...] = tile.T[:8, :].astype(jnp.int32)


def _positions(route, ranks, starts, tr=512):
    t = route.shape[0]
    pos = pl.pallas_call(
        _pos_kernel, grid=(t // tr,),
        in_specs=[pl.BlockSpec((tr, LANES), lambda i: (i, 0)), pl.BlockSpec((tr, LANES), lambda i: (i, 0)),
                  pl.BlockSpec((1, LANES), lambda i: (0, 0))],
        out_specs=pl.BlockSpec((8, tr), lambda i: (0, i)),
        out_shape=jax.ShapeDtypeStruct((8, t), jnp.int32),
        compiler_params=_cparams("parallel"), name="pair_rows",
    )(route, ranks, starts)
    return pos[:2].reshape(-1)


def _routing_plan(route, ranks, counts):
    t = route.shape[0]
    n_tiles = 2 * t // TM_MOE + N_EXPERTS
    cnt = counts[0, :N_EXPERTS].astype(jnp.int32)
    padded = (cnt + TM_MOE - 1) // TM_MOE * TM_MOE
    ends = jnp.cumsum(padded)
    pos = _positions(route, ranks, _pad_lanes((ends - padded).astype(F32)[None]))
    n_used = (ends[-1:] // TM_MOE).astype(jnp.int32)
    tile_start = jnp.arange(n_tiles, dtype=jnp.int32) * TM_MOE
    tile_expert = jnp.sum((ends[None, :] <= tile_start[:, None]).astype(jnp.int32), axis=1)
    tile_expert = jnp.minimum(tile_expert, tile_expert[n_used[0] - 1]).astype(jnp.int32)
    tails = jnp.where(padded > 0, ends - TM_MOE, -1).astype(jnp.int32)
    return pos, tile_expert, n_used, tails, n_tiles * TM_MOE


def _row_copies(pos_ref, n_pairs, tok0, n_rows, make_copy):
    def body(r, carry):
        for k in range(2):
            make_copy(k, r, pos_ref[k * n_pairs + tok0 + r]).start()
        return carry
    lax.fori_loop(0, n_rows, body, 0, unroll=8)


def _dispatch_kernel(pos_ref, tail_ref, h_ref, xs_ref, zero_ref, sem, *, td, n_tokens):
    i = pl.program_id(0)

    @pl.when(i == 0)
    def _():
        zero_ref[...] = jnp.zeros(zero_ref.shape, zero_ref.dtype)
        tail_copy = lambda e: pltpu.make_async_copy(
            zero_ref, xs_ref.at[pl.ds(pl.multiple_of(tail_ref[e], TM_MOE), TM_MOE)], sem)
        for e in range(N_EXPERTS):
            pl.when(tail_ref[e] >= 0)(lambda e=e: tail_copy(e).start())
        for e in range(N_EXPERTS):
            pl.when(tail_ref[e] >= 0)(lambda e=e: tail_copy(e).wait())

    _row_copies(pos_ref, n_tokens, i * td, td,
                lambda k, r, p: pltpu.make_async_copy(h_ref.at[pl.ds(r, 1)], xs_ref.at[pl.ds(p, 1)], sem))
    for _ in range(2):
        pltpu.make_async_copy(h_ref, xs_ref.at[pl.ds(0, td)], sem).wait()


def _dispatch(pos, tails, h2, n_rows, td=256):
    t = h2.shape[0]
    return pl.pallas_call(
        functools.partial(_dispatch_kernel, td=td, n_tokens=t),
        grid_spec=pltpu.PrefetchScalarGridSpec(
            num_scalar_prefetch=2, grid=(t // td,),
            in_specs=[pl.BlockSpec((td, h2.shape[1]), lambda i, *_: (i, 0))],
            out_specs=pl.BlockSpec(memory_space=pl.ANY),
            scratch_shapes=[pltpu.VMEM((TM_MOE, h2.shape[1]), h2.dtype), pltpu.SemaphoreType.DMA]),
        out_shape=jax.ShapeDtypeStruct((n_rows, h2.shape[1]), h2.dtype),
        compiler_params=_cparams("arbitrary"), name="dispatch",
    )(pos, tails, h2)


def _pack_halves(x):
    half = x.shape[1] // 2
    return pltpu.pack_elementwise([x[:, :half], x[:, half:]], packed_dtype=BF16)


def _unpack_halves(p, dtype):
    return tuple(pltpu.unpack_elementwise(p, index=i, packed_dtype=BF16, unpacked_dtype=F32).astype(dtype)
                 for i in range(2))


def _expert_kernel(te_ref, nu_ref, xs_ref, wg_ref, wu_ref, wd_ref, ys_ref, wgb, wub, wdb):
    j = pl.program_id(0)

    @pl.when(j < nu_ref[0])
    def _():
        @pl.when((j == 0) | (te_ref[j] != te_ref[jnp.maximum(j - 1, 0)]))
        def _():
            wgb[...] = wg_ref[0].astype(BF16)
            wub[...] = wu_ref[0].astype(BF16)
            wdb[...] = wd_ref[0].astype(BF16)

        xa, xb = _unpack_halves(xs_ref[...], BF16)
        half = D_MODEL // 2
        g = _dot(xa, wgb[:half, :]) + _dot(xb, wgb[half:, :])
        u = _dot(xa, wub[:half, :]) + _dot(xb, wub[half:, :])
        a = (g / (1.0 + jnp.exp(-g))) * u
        ys_ref[...] = _pack_halves(_dot(a.astype(BF16), wdb[...]))


def _experts(tile_expert, n_used, xs, wg, wu, wd):
    n_tiles = xs.shape[0] // TM_MOE
    tile = lambda j, te, nu: (jnp.minimum(j, nu[0] - 1), 0)
    wspec = lambda shape: pl.BlockSpec((1,) + shape, lambda j, te, nu: (te[j], 0, 0))
    return pl.pallas_call(
        _expert_kernel,
        grid_spec=pltpu.PrefetchScalarGridSpec(
            num_scalar_prefetch=2, grid=(n_tiles,),
            in_specs=[pl.BlockSpec((TM_MOE, D_MODEL // 2), tile), wspec((D_MODEL, D_EXPERT)),
                      wspec((D_MODEL, D_EXPERT)), wspec((D_EXPERT, D_MODEL))],
            out_specs=pl.BlockSpec((TM_MOE, D_MODEL // 2), tile),
            scratch_shapes=[pltpu.VMEM((D_MODEL, D_EXPERT), BF16), pltpu.VMEM((D_MODEL, D_EXPERT), BF16),
                            pltpu.VMEM((D_EXPERT, D_MODEL), BF16)]),
        out_shape=jax.ShapeDtypeStruct(xs.shape, jnp.uint32),
        compiler_params=_cparams("arbitrary"), name="experts",
    )(tile_expert, n_used, xs, wg, wu, wd)


def _combine_kernel(pos_ref, x1_ref, route_ref, mod_ref, ys_ref, o_ref, ybuf, sem, *, tc, n_tokens, tok0):
    _row_copies(pos_ref, n_tokens, tok0 + pl.program_id(0) * tc, tc,
                lambda k, r, p: pltpu.make_async_copy(ys_ref.at[pl.ds(p, 1)], ybuf.at[k, pl.ds(r, 1)], sem))
    for k in range(2):
        pltpu.make_async_copy(ys_ref.at[pl.ds(0, tc)], ybuf.at[k], sem).wait()
    route = route_ref[...]
    w1, w2 = route[:, 2:3], route[:, 3:4]
    half = D_MODEL // 2
    for h, (y1, y2) in enumerate(zip(_unpack_halves(ybuf[0], F32), _unpack_halves(ybuf[1], F32))):
        cols = slice(h * half, (h + 1) * half)
        gate2 = mod_ref[0, :, 5 * D_MODEL + h * half:5 * D_MODEL + (h + 1) * half]
        o_ref[:, cols] = x1_ref[:, cols] + gate2 * (w1 * y1 + w2 * y2)


def _combine(pos, x1, route, mod3, row_of_tile, ys, tok0, n_tokens, tc=256):
    t = x1.shape[0]
    row = lambda i, *_: (i, 0)
    return pl.pallas_call(
        functools.partial(_combine_kernel, tc=tc, n_tokens=n_tokens, tok0=tok0),
        grid_spec=pltpu.PrefetchScalarGridSpec(
            num_scalar_prefetch=1, grid=(t // tc,),
            in_specs=[pl.BlockSpec((tc, D_MODEL), row), pl.BlockSpec((tc, LANES), row),
                      pl.BlockSpec((1, 1, N_MOD * D_MODEL), lambda i, *_: (row_of_tile(i), 0, 0)),
                      pl.BlockSpec(memory_space=pl.ANY)],
            out_specs=pl.BlockSpec((tc, D_MODEL), row),
            scratch_shapes=[pltpu.VMEM((2, tc, ys.shape[1]), ys.dtype), pltpu.SemaphoreType.DMA]),
        out_shape=jax.ShapeDtypeStruct((t, D_MODEL), F32),
        compiler_params=_cparams("arbitrary"), name="combine",
    )(pos, x1, route, mod3, ys)


def _rope_tables(seq):
    pos = np.arange(seq)
    n_freq = ATT_HEAD_DIM // 4
    inv = ROPE_THETA ** (-np.arange(n_freq, dtype=np.float32) / n_freq)
    ang = np.concatenate([(pos // GRID_W).astype(np.float32)[:, None] * inv,
                          (pos % GRID_W).astype(np.float32)[:, None] * inv], axis=-1).astype(np.float32)
    ang = jnp.asarray(ang)
    cos, sin = jnp.cos(ang), jnp.sin(ang)
    cos_t = jnp.repeat(cos, 2, axis=1)
    sin_t = jnp.stack([-sin, sin], axis=-1).reshape(seq, ATT_HEAD_DIM)
    return cos_t, sin_t


def _pad_lanes(a):
    return jnp.pad(a, ((0, 0), (0, LANES - a.shape[1])))


def _layer(x2d, n_batch, seq, mod3, row_of_tile, lw, *, rope_tabs, cache, state, emit, h2_all, tile0, t_total):
    (n1, n2, w_main, wg_cat, bg, qg, kg, mg, w_out, rw_cat, rb) = lw
    cos_t, sin_t = rope_tabs
    outs = _project(x2d, mod3, row_of_tile(256), n1, w_main, wg_cat, bg, qg, kg, cos_t, sin_t,
                    rope=cache is not None, emit_kv=emit)
    q, k, v, qm, kmt, vm, om, g = outs[:8]
    rowq, colq = _gate_scans(g)
    attn = _attention(q, k, v, n_batch, seq, cache=cache)
    ml = _mlstm(qm, kmt, vm, rowq, colq, n_batch, seq, state=state, emit_state=emit)
    mixed = _mix_out(attn, ml[0], om, x2d, mod3, row_of_tile(MIX_TILE), mg, w_out, n2, rw_cat, rb,
                     h2_all, tile0, t_total, tm=MIX_TILE)
    return mixed, outs[8:], ml[1:]


def kernel(x_prompt, x_sample, cache_k, cache_v, state_C, state_n, state_m, c, c_ctx, mod_w, mod_b, norm1_g, norm2_g,
           w_in, b_gates, q_norm_g, k_norm_g, mlstm_norm_g, w_out, router_group_w, router_group_b, router_expert_w,
           router_expert_b, expert_w_gate, expert_w_up, expert_w_down):
    assert mod_w.shape[0] == 1, "single-layer stack"
    n_ctx, s_ctx, _ = x_prompt.shape
    n_lat, s_lat, _ = x_sample.shape
    t_ctx, t_lat = n_ctx * s_ctx, n_lat * s_lat
    t_all = t_ctx + t_lat
    ctx_row = n_lat

    cond = jnp.concatenate([c, c_ctx[None], jnp.zeros((8 - n_lat - 1, D_MODEL), F32)], axis=0)
    mod3 = _modulation(cond, mod_w[0], mod_b[0][None]).reshape(8, 1, N_MOD * D_MODEL)

    rw =jnp.concatenate([router_group_w[0], jnp.moveaxis(router_expert_w[0], 0, 1).reshape(D_MODEL, N_EXPERTS)], axis=1)
    rb = _pad_lanes(jnp.concatenate([router_group_b[0], router_expert_b[0].reshape(-1)])[None])
    w_in_t = w_in[0].T
    lw = (norm1_g, norm2_g, _cast_bf16_t(w_in_t, MAIN_WIDTH), _gate_cols(w_in_t), _pad_lanes(b_gates),
          q_norm_g, k_norm_g, mlstm_norm_g, _cast_bf16(w_out, D_MODEL), _hi_lo_cat(_pad_lanes(rw)), rb)
    rope_tabs = _rope_tables(s_lat)

    ctx_rows = lambda tm: (lambda i: ctx_row)
    lat_rows = lambda tm: (lambda i: i // (s_lat // tm))
    (x1p, h2_all, routep), (ka, va), (s_c, s_n, s_m) = _layer(
        x_prompt.reshape(t_ctx, D_MODEL), n_ctx, s_ctx, mod3, ctx_rows, lw,
        rope_tabs=rope_tabs, cache=None, state=None, emit=True, h2_all=None, tile0=0, t_total=t_all)

    caug0 = jnp.concatenate([state_C[:, 0], state_n[:, 0][..., None],
                             jnp.zeros(state_n[:, 0].shape + (LANES - 1,), F32)], axis=-1)
    caug0 = caug0.reshape(n_lat, N_CHAINS, MLSTM_HEAD_DIM, AUG)
    past = cache_k.shape[2]
    cache = (cache_k[:, 0].reshape(n_lat * past, KV_WIDTH), cache_v[:, 0].reshape(n_lat * past, KV_WIDTH))
    (x1s, h2_all, routes), _, _ = _layer(
        x_sample.reshape(t_lat, D_MODEL), n_lat, s_lat, mod3, lat_rows, lw,
        rope_tabs=rope_tabs, cache=cache, state=(caug0, state_m[:, 0].reshape(-1)), emit=False,
        h2_all=h2_all, tile0=t_ctx // MIX_TILE, t_total=t_all)

    route = jnp.concatenate([routep, routes], axis=0)
    ranks, counts = _ranks(route)
    pos, tile_expert, n_used, tails, n_rows = _routing_plan(route, ranks, counts)
    xs = _dispatch(pos, tails, h2_all, n_rows)
    y_sorted = _experts(tile_expert, n_used, xs, expert_w_gate[0], expert_w_up[0], expert_w_down[0])
    yp = _combine(pos, x1p, routep, mod3, ctx_rows(256), y_sorted, 0, t_all)
    ys = _combine(pos, x1s, routes, mod3, lat_rows(256), y_sorted, t_ctx, t_all)

    kv_shape = (n_ctx, 1, s_ctx, ATT_KV_HEADS, ATT_HEAD_DIM)
    return (yp.reshape(x_prompt.shape), ys.reshape(x_sample.shape), ka.reshape(kv_shape), va.reshape(kv_shape),
            s_c.reshape(n_ctx, 1, 2, MLSTM_HEADS, MLSTM_HEAD_DIM, MLSTM_HEAD_DIM),
            s_n.reshape(n_ctx, 1, 2, MLSTM_HEADS, MLSTM_HEAD_DIM), s_m[..., 0, 0].reshape(n_ctx, 1, 2, MLSTM_HEADS))
```

```python
import functools

import numpy as np
import jax
import jax.numpy as jnp
from jax import lax
from jax.experimental import pallas as pl
from jax.experimental.pallas import tpu as pltpu

F32 = jnp.float32
BF16 = jnp.bfloat16

D_MODEL = 2048
GRID_W = 64
ATT_HEADS = 8
ATT_KV_HEADS = 2
ATT_HEAD_DIM = 128
ATT_GROUP = ATT_HEADS // ATT_KV_HEADS
ATT_WIDTH = ATT_HEADS * ATT_HEAD_DIM
KV_WIDTH = ATT_KV_HEADS * ATT_HEAD_DIM
ROPE_THETA = 10000.0
MLSTM_HEADS = 4
MLSTM_HEAD_DIM = 256
MLSTM_WIDTH = MLSTM_HEADS * MLSTM_HEAD_DIM
CHUNK = 256
N_GATES = 4 * MLSTM_HEADS
N_CHAINS = 2 * MLSTM_HEADS
MAIN_WIDTH = ATT_WIDTH + 2 * KV_WIDTH + 4 * MLSTM_WIDTH
N_GROUPS = 4
EXPERTS_PER_GROUP = 8
N_EXPERTS = N_GROUPS * EXPERTS_PER_GROUP
D_EXPERT = 512
N_MOD = 6
TM_MOE = 256
MIX_TILE = 512
ATT_KEY_CHUNK = 512
Q_SCALE = ATT_HEAD_DIM ** -0.5 * float(np.log2(np.e))
EPS = 1e-6

LANES = 128
AUG = MLSTM_HEAD_DIM + LANES
N_SCAN_ROWS = 5 * N_CHAINS
ROUTER_LANE0 = N_GROUPS
VMEM_LIMIT = 56 * 1024 * 1024


def _cparams(*sem):
    return pltpu.CompilerParams(dimension_semantics=sem, vmem_limit_bytes=VMEM_LIMIT)


def _const_spec(shape):
    nd = len(shape)
    return pl.BlockSpec(shape, lambda *_: (0,) * nd, pipeline_mode=pl.Buffered(1))


def _split_hi_lo(x):
    hi = x.astype(BF16)
    lo = (x - hi.astype(F32)).astype(BF16)
    return hi, lo


def _dot(a, b):
    return jnp.dot(a, b, preferred_element_type=F32)


def _hi_lo_cat(w):
    return jnp.concatenate(_split_hi_lo(w), axis=1)


def _dot_hi_lo(a_hi, a_lo, w_ref):
    r = _dot(a_hi, w_ref[...])
    return r[:, :LANES] + r[:, LANES:] + _dot(a_lo, w_ref[:, :LANES])


def _rms(x, g):
    return x * lax.rsqrt(jnp.mean(x * x, axis=-1, keepdims=True) + EPS) * g


def _mod_kernel(c_ref, w_ref, b_ref, o_ref):
    c = c_ref[...]
    s = c / (1.0 + jnp.exp(-c))
    s_hi = s.astype(BF16).astype(F32)
    lhs = jnp.concatenate([s_hi, s - s_hi], axis=0).astype(BF16)
    w_hi, w_lo = _split_hi_lo(w_ref[...])
    r = _dot(lhs, w_hi)
    r2 = _dot(lhs, w_lo)
    o_ref[...] = r[:8] + r[8:] + r2[:8] + b_ref[...]


def _modulation(cond, mod_w, mod_b):
    n = mod_w.shape[1]
    tn = 512
    return pl.pallas_call(
        _mod_kernel,
        grid=(n // tn,),
        in_specs=[pl.BlockSpec((8, D_MODEL), lambda j: (0, 0)),
                  pl.BlockSpec((D_MODEL, tn), lambda j: (0, j)),
                  pl.BlockSpec((1, tn), lambda j: (0, j))],
        out_specs=pl.BlockSpec((8, tn), lambda j: (0, j)),
        out_shape=jax.ShapeDtypeStruct((8, n), F32),
        compiler_params=_cparams("parallel"),
        name="modulation",
    )(cond, mod_w, mod_b)


def _cast_kernel(w_ref, o_ref):
    o_ref[...] = w_ref[0].astype(BF16)


def _cast_bf16(w, n_cols, tn=512):
    rows = w.shape[1]
    return pl.pallas_call(
        _cast_kernel, grid=(n_cols // tn,),
        in_specs=[pl.BlockSpec((1, rows, tn), lambda j: (0, 0, j))],
        out_specs=pl.BlockSpec((rows, tn), lambda j: (0, j)),
        out_shape=jax.ShapeDtypeStruct((rows, n_cols), BF16),
        compiler_params=_cparams("parallel"), name="cast_bf16",
    )(w)


def _cast_t_kernel(wt_ref, o_ref):
    o_ref[...] = wt_ref[...].T.astype(BF16)


def _cast_bf16_t(wt, n_cols, tn=512):
    rows = wt.shape[1]
    return pl.pallas_call(
        _cast_t_kernel, grid=(n_cols // tn,),
        in_specs=[pl.BlockSpec((tn, rows), lambda j: (j, 0))],
        out_specs=pl.BlockSpec((rows, tn), lambda j: (0, j)),
        out_shape=jax.ShapeDtypeStruct((rows, n_cols), BF16),
        compiler_params=_cparams("parallel"), name="cast_bf16_t",
    )(wt)


def _gate_cols_kernel(wt_ref, o_ref):
    sub = lax.broadcasted_iota(jnp.int32, wt_ref.shape, 0)
    hi, lo = _split_hi_lo(jnp.where(sub < N_GATES, wt_ref[...], 0.0).T)
    o_ref[:, :LANES] = hi
    o_ref[:, LANES:] = lo


def _gate_cols(wt):
    rows = wt.shape[1]
    return pl.pallas_call(
        _gate_cols_kernel, grid=(1,),
        in_specs=[pl.BlockSpec((LANES, rows), lambda i: (MAIN_WIDTH // LANES, 0))],
        out_specs=pl.BlockSpec((rows, 2 * LANES), lambda i: (0, 0)),
        out_shape=jax.ShapeDtypeStruct((rows, 2 * LANES), BF16),
        compiler_params=_cparams("arbitrary"), name="gate_cols",
    )(wt)


def _pair_swap(x):
    lane = lax.broadcasted_iota(jnp.int32, x.shape, 1)
    return jnp.where((lane & 1) == 0, pltpu.roll(x, LANES - 1, 1), pltpu.roll(x, 1, 1))


def _proj_kernel(x_ref, mod_ref, n1_ref, w_ref, wg_ref, bg_ref, qg_ref, kg_ref, cos_ref, sin_ref,
                 q_ref, k_ref, v_ref, qm_ref, kmt_ref, vm_ref, om_ref, g_ref, *kv_refs, rope, tm):
    mod = mod_ref[0]
    shift, scale = mod[:, :D_MODEL], mod[:, D_MODEL:2 * D_MODEL]
    h = _rms(x_ref[...], n1_ref[...]) * (1.0 + scale) + shift
    h_hi, h_lo = _split_hi_lo(h)

    def rot(seg):
        return seg * cos_ref[...] + _pair_swap(seg) * sin_ref[...] if rope else seg

    qa = _dot(h_hi, w_ref[:, :ATT_WIDTH])
    for hh in range(ATT_HEADS):
        sl = slice(hh * ATT_HEAD_DIM, (hh + 1) * ATT_HEAD_DIM)
        seg = rot(_rms(qa[:, sl], qg_ref[...]))
        q_ref[:, sl] = (seg * Q_SCALE).astype(BF16)

    kv = _dot(h_hi, w_ref[:, ATT_WIDTH:ATT_WIDTH + 2 * KV_WIDTH])
    for hh in range(ATT_KV_HEADS):
        sl = slice(hh * ATT_HEAD_DIM, (hh + 1) * ATT_HEAD_DIM)
        seg = _rms(kv[:, sl], kg_ref[...])
        if kv_refs:
            kv_refs[0][:, sl] = seg
        k_ref[:, sl] = rot(seg).astype(BF16)
    va = kv[:, KV_WIDTH:]
    if kv_refs:
        kv_refs[1][...] = va
    v_ref[...] = va.astype(BF16)

    c0 = ATT_WIDTH + 2 * KV_WIDTH
    qm_ref[...] = (_dot(h_hi, w_ref[:, c0:c0 + MLSTM_WIDTH]) * MLSTM_HEAD_DIM ** -0.5).astype(BF16)
    km = _dot(h_hi, w_ref[:, c0 + MLSTM_WIDTH:c0 + 2 * MLSTM_WIDTH])
    for cc in range(tm // CHUNK):
        kmt_ref[cc] = km[cc * CHUNK:(cc + 1) * CHUNK, :].T.astype(BF16)
    vm_ref[...] = _dot(h_hi, w_ref[:, c0 + 2 * MLSTM_WIDTH:c0 + 3 * MLSTM_WIDTH]).astype(BF16)
    om_ref[...] = _dot(h_hi, w_ref[:, c0 + 3 * MLSTM_WIDTH:c0 + 4 * MLSTM_WIDTH])
    g_ref[...] = _dot_hi_lo(h_hi, h_lo, wg_ref) + bg_ref[...]


def _project(x2d, mod3, row_of_tile, n1, w_main, wg_cat, bg, qg, kg, cos_t, sin_t, *, rope, emit_kv, tm=256):
    t = x2d.shape[0]
    n_pos = cos_t.shape[0] // tm
    row = lambda i: (i, 0)
    in_specs = [pl.BlockSpec((tm, D_MODEL), row),
                pl.BlockSpec((1, 1, N_MOD * D_MODEL), lambda i: (row_of_tile(i), 0, 0)),
                _const_spec((1, D_MODEL)),
                _const_spec((D_MODEL, MAIN_WIDTH)),
                _const_spec((D_MODEL, 2 * LANES)), _const_spec((1, LANES)),
                _const_spec((1, ATT_HEAD_DIM)), _const_spec((1, ATT_HEAD_DIM)),
                pl.BlockSpec((tm, ATT_HEAD_DIM), lambda i: (i % n_pos, 0)),
                pl.BlockSpec((tm, ATT_HEAD_DIM), lambda i: (i % n_pos, 0))]
    out_shape = [jax.ShapeDtypeStruct((t, ATT_WIDTH), BF16), jax.ShapeDtypeStruct((t, KV_WIDTH), BF16),
                 jax.ShapeDtypeStruct((t, KV_WIDTH), BF16), jax.ShapeDtypeStruct((t, MLSTM_WIDTH), BF16),
                 jax.ShapeDtypeStruct((t // CHUNK, MLSTM_WIDTH, CHUNK), BF16),
                 jax.ShapeDtypeStruct((t, MLSTM_WIDTH), BF16), jax.ShapeDtypeStruct((t, MLSTM_WIDTH), F32),
                 jax.ShapeDtypeStruct((t, LANES), F32)]
    out_specs = [pl.BlockSpec((tm, ATT_WIDTH), row), pl.BlockSpec((tm, KV_WIDTH), row),
                 pl.BlockSpec((tm, KV_WIDTH), row), pl.BlockSpec((tm, MLSTM_WIDTH), row),
                 pl.BlockSpec((tm // CHUNK, MLSTM_WIDTH, CHUNK), lambda i: (i, 0, 0)),
                 pl.BlockSpec((tm, MLSTM_WIDTH), row), pl.BlockSpec((tm, MLSTM_WIDTH), row),
                 pl.BlockSpec((tm, LANES), row)]
    if emit_kv:
        out_shape += [jax.ShapeDtypeStruct((t, KV_WIDTH), F32)] * 2
        out_specs += [pl.BlockSpec((tm, KV_WIDTH), row)] * 2
    return pl.pallas_call(
        functools.partial(_proj_kernel, rope=rope, tm=tm),
        grid=(t // tm,), in_specs=in_specs, out_specs=out_specs, out_shape=out_shape,
        compiler_params=_cparams("parallel"), name="in_proj",
    )(x2d, mod3, n1, w_main, wg_cat, bg, qg, kg, cos_t, sin_t)


def _lane_scan(x, op, fill, is_fwd, lane):
    s = 1
    while s < CHUNK:
        from_left = jnp.where(lane >= s, pltpu.roll(x, s, 1), fill)
        from_right = jnp.where(lane < CHUNK - s, pltpu.roll(x, CHUNK - s, 1), fill)
        x = op(x, jnp.where(is_fwd, from_left, from_right))
        s *= 2
    return x


def _scan_kernel(g_ref, row_ref, col_ref, *, n_chunks):
    lane = lax.broadcasted_iota(jnp.int32, (N_CHAINS, CHUNK), 1)
    is_fwd = lax.broadcasted_iota(jnp.int32, (N_CHAINS, CHUNK), 0) < MLSTM_HEADS
    for cc in range(n_chunks):
        gt = g_ref[cc * CHUNK:(cc + 1) * CHUNK, :].T
        fwd, bwd = gt[:N_CHAINS], gt[N_CHAINS:2 * N_CHAINS]
        li = jnp.where(is_fwd, fwd, pltpu.roll(bwd, MLSTM_HEADS, 0))
        f = jnp.where(is_fwd, pltpu.roll(fwd, MLSTM_HEADS, 0), bwd)
        lf = jnp.minimum(f, 0.0) - jnp.log(1.0 + jnp.exp(-jnp.abs(f)))
        b = _lane_scan(lf, jnp.add, 0.0, is_fwd, lane)
        a = li - b
        run_max = _lane_scan(a, jnp.maximum, -jnp.inf, is_fwd, lane)
        all_max = jnp.broadcast_to(jnp.max(a, axis=1, keepdims=True), a.shape)
        last = jnp.where(is_fwd, CHUNK - 1, 0)
        total = jnp.broadcast_to(jnp.sum(jnp.where(lane == last, b, 0.0), axis=1, keepdims=True), a.shape)
        rows = jnp.concatenate([a, b, run_max, all_max, total], axis=0)
        row_ref[cc] = rows
        padded = jnp.concatenate([rows, jnp.zeros((LANES - N_SCAN_ROWS, CHUNK), F32)], axis=0)
        col_ref[cc * CHUNK:(cc + 1) * CHUNK, :] = padded.T


def _gate_scans(g, tb=1024):
    t = g.shape[0]
    tb = min(tb, t)
    return pl.pallas_call(
        functools.partial(_scan_kernel, n_chunks=tb // CHUNK),
        grid=(t // tb,),
        in_specs=[pl.BlockSpec((tb, LANES), lambda i: (i, 0))],
        out_specs=[pl.BlockSpec((tb // CHUNK, N_SCAN_ROWS, CHUNK), lambda i: (i, 0, 0)),
                   pl.BlockSpec((tb, LANES), lambda i: (i, 0))],
        out_shape=[jax.ShapeDtypeStruct((t // CHUNK, N_SCAN_ROWS, CHUNK), F32),
                   jax.ShapeDtypeStruct((t, LANES), F32)],
        compiler_params=_cparams("parallel"), name="gate_scans",
    )(g)


def _mlstm_kernel(*refs, n_chunks, has_state, emit_state):
    it = iter(refs)
    q_ref, kt_ref, v_ref, row_ref, col_ref = [next(it) for _ in range(5)]
    c0_ref, m0_ref = (next(it), next(it)) if has_state else (None, None)
    o_ref = next(it)
    c_out, n_out, m_out = (next(it), next(it), next(it)) if emit_state else (None, None, None)
    caug, hb = next(it), next(it)

    b = pl.program_id(0)
    if has_state:
        caug[...] = c0_ref[0]
        m_init = tuple(jnp.full((1, 1), m0_ref[b * N_CHAINS + r], F32) for r in range(N_CHAINS))
    else:
        caug[...] = jnp.zeros(caug.shape, F32)
        m_init = tuple(jnp.zeros((1, 1), F32) for _ in range(N_CHAINS))

    sub = lax.broadcasted_iota(jnp.int32, (CHUNK, CHUNK), 0)
    lane = lax.broadcasted_iota(jnp.int32, (CHUNK, CHUNK), 1)
    ones_col = (lax.broadcasted_iota(jnp.int32, (CHUNK, LANES), 1) == 0).astype(BF16)

    def chain(r, cc, m_prev):
        d, hd = divmod(r, MLSTM_HEADS)
        t0 = pl.multiple_of(cc * CHUNK, CHUNK)
        hs = slice(hd * MLSTM_HEAD_DIM, (hd + 1) * MLSTM_HEAD_DIM)
        rows = row_ref[cc]
        cols = col_ref[pl.ds(t0, CHUNK), :]
        row = lambda k: rows[k * N_CHAINS + r:k * N_CHAINS + r + 1, :]
        col = lambda k: cols[:, k * N_CHAINS + r:k * N_CHAINS + r + 1]
        q = q_ref[pl.ds(t0, CHUNK), hs]
        kt = kt_ref[cc, hs, :]
        vaug = jnp.concatenate([v_ref[pl.ds(t0, CHUNK), hs], ones_col], axis=1)

        m_col = jnp.maximum(m_prev, col(2))
        keep = (lane <= sub) if d == 0 else (lane >= sub)
        w = jnp.where(keep, jnp.exp(row(0) - m_col), 0.0)
        w_inter = jnp.exp(m_prev - m_col)
        p = (_dot(q, kt) * w).astype(BF16)
        intra = _dot(p, vaug)
        inter = _dot(q, caug[r].astype(BF16))
        num = intra[:, :MLSTM_HEAD_DIM] + w_inter * inter[:, :MLSTM_HEAD_DIM]
        den = intra[:, MLSTM_HEAD_DIM:MLSTM_HEAD_DIM + 1] + w_inter * inter[:, MLSTM_HEAD_DIM:MLSTM_HEAD_DIM + 1]
        h_out = num / jnp.maximum(jnp.abs(den), jnp.exp(-(col(1) + m_col)))

        m_last = jnp.maximum(m_prev, row(3))
        kw = (kt.astype(F32) * jnp.exp(row(0) - m_last)).astype(BF16)
        caug[r] = jnp.exp(m_prev - m_last[:, :1]) * caug[r] + _dot(kw, vaug)
        return h_out, (row(4) + m_last)[:, :1]

    def body(c, ms):
        new = []
        for r in range(N_CHAINS):
            cc = c if r < MLSTM_HEADS else n_chunks - 1 - c
            h_out, m_new = chain(r, cc, ms[r])
            hd = r % MLSTM_HEADS
            dst = o_ref if r < MLSTM_HEADS else hb
            dst[pl.ds(pl.multiple_of(cc * CHUNK, CHUNK), CHUNK), hd * MLSTM_HEAD_DIM:(hd + 1) * MLSTM_HEAD_DIM] = h_out
            new.append(m_new)
        return tuple(new)

    ms = lax.fori_loop(0, n_chunks, body, m_init)
    o_ref[...] += hb[...]
    if emit_state:
        for r in range(N_CHAINS):
            c_out[0, r] = caug[r, :, :MLSTM_HEAD_DIM]
            n_out[0, r] = caug[r, :, MLSTM_HEAD_DIM:].T[:1, :]
            m_out[0, r] = jnp.broadcast_to(ms[r], (1, LANES))


def _mlstm(qm, kmt, vm, rowq, colq, n_batch, seq, state=None, emit_state=False):
    nc = seq // CHUNK
    mode = dict(pipeline_mode=pl.Buffered(1)) if seq * MLSTM_WIDTH * 4 > (2 << 20) else {}
    in_specs = [pl.BlockSpec((seq, MLSTM_WIDTH), lambda b: (b, 0), **mode),
                pl.BlockSpec((nc, MLSTM_WIDTH, CHUNK), lambda b: (b, 0, 0), **mode),
                pl.BlockSpec((seq, MLSTM_WIDTH), lambda b: (b, 0), **mode),
                pl.BlockSpec((nc, N_SCAN_ROWS, CHUNK), lambda b: (b, 0, 0), **mode),
                pl.BlockSpec((seq, LANES), lambda b: (b, 0), **mode)]
    args = [qm, kmt, vm, rowq, colq]
    if state is not None:
        in_specs += [pl.BlockSpec((1, N_CHAINS, MLSTM_HEAD_DIM, AUG), lambda b: (b, 0, 0, 0), **mode),
                     pl.BlockSpec(memory_space=pltpu.SMEM)]
        args += list(state)
    out_shape = [jax.ShapeDtypeStruct((n_batch * seq, MLSTM_WIDTH), F32)]
    out_specs = [pl.BlockSpec((seq, MLSTM_WIDTH), lambda b: (b, 0))]
    if emit_state:
        out_shape += [jax.ShapeDtypeStruct((n_batch, N_CHAINS, MLSTM_HEAD_DIM, MLSTM_HEAD_DIM), F32),
                      jax.ShapeDtypeStruct((n_batch, N_CHAINS, 1, MLSTM_HEAD_DIM), F32),
                      jax.ShapeDtypeStruct((n_batch, N_CHAINS, 1, LANES), F32)]
        out_specs += [pl.BlockSpec((1, N_CHAINS, MLSTM_HEAD_DIM, MLSTM_HEAD_DIM), lambda b: (b, 0, 0, 0)),
                      pl.BlockSpec((1, N_CHAINS, 1, MLSTM_HEAD_DIM), lambda b: (b, 0, 0, 0)),
                      pl.BlockSpec((1, N_CHAINS, 1, LANES), lambda b: (b, 0, 0, 0))]
    return pl.pallas_call(
        functools.partial(_mlstm_kernel, n_chunks=nc, has_state=state is not None, emit_state=emit_state),
        grid=(n_batch,), in_specs=in_specs, out_specs=out_specs, out_shape=out_shape,
        scratch_shapes=[pltpu.VMEM((N_CHAINS, MLSTM_HEAD_DIM, AUG), F32), pltpu.VMEM((seq, MLSTM_WIDTH), F32)],
        compiler_params=_cparams("parallel"), name="mlstm",
    )(*args)


def _attn_kernel(*refs, tq, has_cache):
    if has_cache:
        q_ref, k_ref, v_ref, ck_ref, cv_ref, o_ref = refs
    else:
        q_ref, k_ref, v_ref, o_ref = refs
    q = q_ref[...]
    qs = jnp.concatenate([q[:, g * ATT_HEAD_DIM:(g + 1) * ATT_HEAD_DIM] for g in range(ATT_GROUP)], axis=0)
    nt = (((1,), (1,)), ((), ()))
    seq = k_ref.shape[0]
    kc = min(seq, ATT_KEY_CHUNK)
    chunks = [(k_ref, v_ref, c * kc) for c in range(seq // kc)]
    if has_cache:
        chunks.insert(0, (ck_ref, cv_ref, None))
    m = l = o = None
    for kr, vr, start in chunks:
        kk, vv = (kr[...], vr[...]) if start is None else (kr[start:start + kc, :], vr[start:start + kc, :])
        s = lax.dot_general(qs, kk.astype(BF16), nt, preferred_element_type=F32)
        mc = jnp.max(s, axis=-1, keepdims=True)
        if m is None:
            m = mc
            p = jnp.exp2(s - m)
            l = jnp.sum(p, axis=-1, keepdims=True)
            o = _dot(p.astype(BF16), vv.astype(BF16))
        else:
            m_new = jnp.maximum(m, mc)
            alpha = jnp.exp2(m - m_new)
            p = jnp.exp2(s - m_new)
            l = alpha * l + jnp.sum(p, axis=-1, keepdims=True)
            o = alpha * o + _dot(p.astype(BF16), vv.astype(BF16))
            m = m_new
    o = o / l
    for g in range(ATT_GROUP):
        o_ref[:, g * ATT_HEAD_DIM:(g + 1) * ATT_HEAD_DIM] = o[g * tq:(g + 1) * tq].astype(BF16)


def _attention(q, k, v, n_batch, seq, cache=None, tq=256):
    nqb = seq // tq
    gw = ATT_GROUP * ATT_HEAD_DIM
    in_specs = [pl.BlockSpec((tq, gw), lambda b, h, i: (b * nqb + i, h)),
                pl.BlockSpec((seq, ATT_HEAD_DIM), lambda b, h, i: (b, h)),
                pl.BlockSpec((seq, ATT_HEAD_DIM), lambda b, h, i: (b, h))]
    args = [q, k, v]
    if cache is not None:
        past = cache[0].shape[0] // n_batch
        in_specs += [pl.BlockSpec((past, ATT_HEAD_DIM), lambda b, h, i: (b, h))] * 2
        args += list(cache)
    return pl.pallas_call(
        functools.partial(_attn_kernel, tq=tq, has_cache=cache is not None),
        grid=(n_batch, ATT_KV_HEADS, nqb), in_specs=in_specs,
        out_specs=pl.BlockSpec((tq, gw), lambda b, h, i: (b * nqb + i, h)),
        out_shape=jax.ShapeDtypeStruct((n_batch * seq, ATT_WIDTH), BF16),
        compiler_params=_cparams("parallel", "parallel", "parallel"), name="attention",
    )(*args)


def _mix_kernel(attn_ref, hm_ref, om_ref, x_ref, mod_ref, mg_ref, wo_ref, n2_ref, rw_ref, rb_ref, *rest, sub):
    for r0 in range(0, x_ref.shape[0], sub):
        _mix_rows(slice(r0, r0 + sub), attn_ref, hm_ref, om_ref, x_ref, mod_ref, mg_ref, wo_ref, n2_ref, rw_ref,
                  rb_ref, *rest[-3:])


def _mix_rows(rows, attn_ref, hm_ref, om_ref, x_ref, mod_ref, mg_ref, wo_ref, n2_ref, rw_ref, rb_ref,
              x1_ref, h2_ref, route_ref):
    mod = mod_ref[0]
    gate1 = mod[:, 2 * D_MODEL:3 * D_MODEL]
    shift2, scale2 = mod[:, 3 * D_MODEL:4 * D_MODEL], mod[:, 4 * D_MODEL:5 * D_MODEL]
    hm = hm_ref[rows, :]
    mg = mg_ref[...]
    parts = []
    for hd in range(MLSTM_HEADS):
        sl = slice(hd * MLSTM_HEAD_DIM, (hd + 1) * MLSTM_HEAD_DIM)
        parts.append(_rms(hm[:, sl], mg[:, sl]))
    om = om_ref[rows, :]
    hmg = jnp.concatenate(parts, axis=1) * (1.0 / (1.0 + jnp.exp(-om)))
    y = _dot(attn_ref[rows, :], wo_ref[:ATT_WIDTH, :]) + _dot(hmg.astype(BF16), wo_ref[ATT_WIDTH:, :])
    x1 = x_ref[rows, :] + gate1 * y
    x1_ref[rows, :] = x1
    h2 = _rms(x1, n2_ref[...]) * (1.0 + scale2) + shift2
    h2_hi, h2_lo = _split_hi_lo(h2)
    h2_ref[rows, :] = _pack_halves(h2)

    lg = _dot_hi_lo(h2_hi, h2_lo, rw_ref) + rb_ref[...]
    lane = lax.broadcasted_iota(jnp.int32, lg.shape, 1).astype(F32)
    neg = -jnp.inf
    first = lambda hit: jnp.min(jnp.where(hit, lane, float(LANES)), axis=-1, keepdims=True)
    gl = jnp.where(lane < N_GROUPS, lg, neg)
    gmax = jnp.max(gl, axis=-1, keepdims=True)
    grp = first(gl == gmax)
    p_grp = 1.0 / jnp.sum(jnp.exp(gl - gmax), axis=-1, keepdims=True)
    lo = ROUTER_LANE0 + grp * EXPERTS_PER_GROUP
    el = jnp.where((lane >= lo) & (lane < lo + EXPERTS_PER_GROUP), lg, neg)
    m1 = jnp.max(el, axis=-1, keepdims=True)
    i1 = first(el == m1)
    el2 = jnp.where(lane == i1, neg, el)
    m2 = jnp.max(el2, axis=-1, keepdims=True)
    i2 = first(el2 == m2)
    r = jnp.exp(m2 - m1)
    w1 = p_grp / (1.0 + r)
    w2 = w1 * r
    route_ref[rows, :] = jnp.where(lane == 0.0, i1 - ROUTER_LANE0, jnp.where(lane == 1.0, i2 - ROUTER_LANE0,
                                   jnp.where(lane == 2.0, w1, jnp.where(lane == 3.0, w2, 0.0))))


def _mix_out(attn, hm, om, x2d, mod3, row_of_tile, mg, w_out, n2, rw_cat, rb, h2_all, tile0, t_total, tm=512):
    t = x2d.shape[0]
    row = lambda i: (i, 0)
    in_specs = [pl.BlockSpec((tm, ATT_WIDTH), row), pl.BlockSpec((tm, MLSTM_WIDTH), row),
                pl.BlockSpec((tm, MLSTM_WIDTH), row), pl.BlockSpec((tm, D_MODEL), row),
                pl.BlockSpec((1, 1, N_MOD * D_MODEL), lambda i: (row_of_tile(i), 0, 0)),
                _const_spec((1, MLSTM_WIDTH)), _const_spec((D_MODEL, D_MODEL)), _const_spec((1, D_MODEL)),
                _const_spec((D_MODEL, 2 * LANES)), _const_spec((1, LANES))]
    args = [attn, hm, om, x2d, mod3, mg, w_out, n2, rw_cat, rb]
    aliases = {}
    if h2_all is not None:
        aliases = {len(args): 1}
        in_specs.append(pl.BlockSpec(memory_space=pl.ANY))
        args.append(h2_all)
    return pl.pallas_call(
        functools.partial(_mix_kernel, sub=256), grid=(t // tm,), in_specs=in_specs,
        out_specs=[pl.BlockSpec((tm, D_MODEL), row), pl.BlockSpec((tm, D_MODEL // 2), lambda i: (tile0 + i, 0)),
                   pl.BlockSpec((tm, LANES), row)],
        out_shape=[jax.ShapeDtypeStruct((t, D_MODEL), F32), jax.ShapeDtypeStruct((t_total, D_MODEL // 2), jnp.uint32),
                   jax.ShapeDtypeStruct((t, LANES), F32)],
        input_output_aliases=aliases,
        compiler_params=_cparams("parallel"), name="mix_out",
    )(*args)


def _rank_kernel(route_ref, rank_ref, cnt_ref, run_ref, tri_ref):
    tr = route_ref.shape[0]

    @pl.when(pl.program_id(0) == 0)
    def _():
        run_ref[...] = jnp.zeros(run_ref.shape, F32)
        tri_ref[...] = (lax.broadcasted_iota(jnp.int32, (tr, tr), 1)
                        < lax.broadcasted_iota(jnp.int32, (tr, tr), 0)).astype(BF16)

    route = route_ref[...]
    lane = lax.broadcasted_iota(jnp.int32, route.shape, 1).astype(F32)
    hit1, hit2 = lane == route[:, 0:1], lane == route[:, 1:2]
    onehot = jnp.where(hit1, 1.0, jnp.where(hit2, 1.0, 0.0))
    before = _dot(tri_ref[...], onehot.astype(BF16)) + run_ref[0:1, :]
    r1 = jnp.sum(jnp.where(hit1, before, 0.0), axis=-1, keepdims=True)
    r2 = jnp.sum(jnp.where(hit2, before, 0.0), axis=-1, keepdims=True)
    rank_ref[...] = jnp.where(lane == 0.0, r1, jnp.where(lane == 1.0, r2, 0.0))
    run_ref[...] = run_ref[...] + jnp.sum(onehot, axis=0, keepdims=True)
    cnt_ref[...] = run_ref[...]


def _ranks(route, tr=512):
    t = route.shape[0]
    return pl.pallas_call(
        _rank_kernel, grid=(t // tr,),
        in_specs=[pl.BlockSpec((tr, LANES), lambda i: (i, 0))],
        out_specs=[pl.BlockSpec((tr, LANES), lambda i: (i, 0)), pl.BlockSpec((8, LANES), lambda i: (0, 0))],
        out_shape=[jax.ShapeDtypeStruct((t, LANES), F32), jax.ShapeDtypeStruct((8, LANES), F32)],
        scratch_shapes=[pltpu.VMEM((8, LANES), F32), pltpu.VMEM((tr, tr), BF16)],
        compiler_params=_cparams("arbitrary"), name="expert_ranks",
    )(route)


def _pos_kernel(route_ref, rank_ref, start_ref, o_ref):
    route, rank = route_ref[...], rank_ref[...]
    lane = lax.broadcasted_iota(jnp.int32, route.shape, 1).astype(F32)
    start = start_ref[...]
    first = lambda col: jnp.sum(jnp.where(lane == route[:, col:col + 1], start, 0.0), axis=-1, keepdims=True)
    p1 = first(0) + rank[:, 0:1]
    p2 = first(1) + rank[:, 1:2]
    tile = jnp.where(lane == 0.0, p1, jnp.where(lane == 1.0, p2, 0.0))
    o_ref[...] = tile.T[:8, :].astype(jnp.int32)


def _positions(route, ranks, starts, tr=512):
    t = route.shape[0]
    pos = pl.pallas_call(
        _pos_kernel, grid=(t // tr,),
        in_specs=[pl.BlockSpec((tr, LANES), lambda i: (i, 0)), pl.BlockSpec((tr, LANES), lambda i: (i, 0)),
                  pl.BlockSpec((1, LANES), lambda i: (0, 0))],
        out_specs=pl.BlockSpec((8, tr), lambda i: (0, i)),
        out_shape=jax.ShapeDtypeStruct((8, t), jnp.int32),
        compiler_params=_cparams("parallel"), name="pair_rows",
    )(route, ranks, starts)
    return pos[:2].reshape(-1)


def _routing_plan(route, ranks, counts):
    t = route.shape[0]
    n_tiles = 2 * t // TM_MOE + N_EXPERTS
    cnt = counts[0, :N_EXPERTS].astype(jnp.int32)
    padded = (cnt + TM_MOE - 1) // TM_MOE * TM_MOE
    ends = jnp.cumsum(padded)
    pos = _positions(route, ranks, _pad_lanes((ends - padded).astype(F32)[None]))
    n_used = (ends[-1:] // TM_MOE).astype(jnp.int32)
    tile_start = jnp.arange(n_tiles, dtype=jnp.int32) * TM_MOE
    tile_expert = jnp.sum((ends[None, :] <= tile_start[:, None]).astype(jnp.int32), axis=1)
    tile_expert = jnp.minimum(tile_expert, tile_expert[n_used[0] - 1]).astype(jnp.int32)
    tails = jnp.where(padded > 0, ends - TM_MOE, -1).astype(jnp.int32)
    change = jnp.concatenate([jnp.ones((1,), jnp.int32), (tile_expert[1:] != tile_expert[:-1]).astype(jnp.int32)])
    weight_slot = (jnp.cumsum(change) - 1) % 2
    ids = jnp.arange(N_EXPERTS, dtype=jnp.int32)
    later_used = (padded > 0)[None, :] & (ids[None, :] > ids[:, None])
    next_used = jnp.min(jnp.where(later_used, ids[None, :], N_EXPERTS), axis=1)
    next_used = jnp.where(next_used < N_EXPERTS, next_used, -1)
    next_expert = jnp.sum(jnp.where(tile_expert[:, None] == ids[None, :], next_used[None, :], 0), axis=1)
    return pos, (tile_expert, n_used, weight_slot.astype(jnp.int32), next_expert.astype(jnp.int32)), tails, \
        n_tiles * TM_MOE


def _row_copies(pos_ref, n_pairs, tok0, n_rows, make_copy):
    def body(r, carry):
        for k in range(2):
            make_copy(k, r, pos_ref[k * n_pairs + tok0 + r]).start()
        return carry
    lax.fori_loop(0, n_rows, body, 0, unroll=8)


def _dispatch_kernel(pos_ref, tail_ref, h_ref, xs_ref, zero_ref, sem, *, td, n_tokens):
    i = pl.program_id(0)

    @pl.when(i == 0)
    def _():
        zero_ref[...] = jnp.zeros(zero_ref.shape, zero_ref.dtype)
        tail_copy = lambda e: pltpu.make_async_copy(
            zero_ref, xs_ref.at[pl.ds(pl.multiple_of(tail_ref[e], TM_MOE), TM_MOE)], sem)
        for e in range(N_EXPERTS):
            pl.when(tail_ref[e] >= 0)(lambda e=e: tail_copy(e).start())
        for e in range(N_EXPERTS):
            pl.when(tail_ref[e] >= 0)(lambda e=e: tail_copy(e).wait())

    _row_copies(pos_ref, n_tokens, i * td, td,
                lambda k, r, p: pltpu.make_async_copy(h_ref.at[pl.ds(r, 1)], xs_ref.at[pl.ds(p, 1)], sem))
    for _ in range(2):
        pltpu.make_async_copy(h_ref, xs_ref.at[pl.ds(0, td)], sem).wait()


def _dispatch(pos, tails, h2, n_rows, td=256):
    t = h2.shape[0]
    return pl.pallas_call(
        functools.partial(_dispatch_kernel, td=td, n_tokens=t),
        grid_spec=pltpu.PrefetchScalarGridSpec(
            num_scalar_prefetch=2, grid=(t // td,),
            in_specs=[pl.BlockSpec((td, h2.shape[1]), lambda i, *_: (i, 0))],
            out_specs=pl.BlockSpec(memory_space=pl.ANY),
            scratch_shapes=[pltpu.VMEM((TM_MOE, h2.shape[1]), h2.dtype), pltpu.SemaphoreType.DMA]),
        out_shape=jax.ShapeDtypeStruct((n_rows, h2.shape[1]), h2.dtype),
        compiler_params=_cparams("arbitrary"), name="dispatch",
    )(pos, tails, h2)


def _pack_halves(x):
    half = x.shape[1] // 2
    return pltpu.pack_elementwise([x[:, :half], x[:, half:]], packed_dtype=BF16)


def _unpack_halves(p, dtype):
    return tuple(pltpu.unpack_elementwise(p, index=i, packed_dtype=BF16, unpacked_dtype=F32).astype(dtype)
                 for i in range(2))


def _expert_kernel(te_ref, nu_ref, slot_ref, next_ref, xs_ref, wg_ref, wu_ref, wd_ref, ys_ref,
                   wgf, wuf, wdf, wgb, wub, wdb, wsem):
    j = pl.program_id(0)

    def weight_copies(e, s):
        return [pltpu.make_async_copy(w_ref.at[e], buf.at[s], wsem.at[s])
                for w_ref, buf in ((wg_ref, wgf), (wu_ref, wuf), (wd_ref, wdf))]

    @pl.when(j < nu_ref[0])
    def _():
        e, s = te_ref[j], slot_ref[j]

        @pl.when(j == 0)
        def _():
            for cp in weight_copies(e, s):
                cp.start()

        @pl.when((j == 0) | (e != te_ref[jnp.maximum(j - 1, 0)]))
        def _():
            for cp in weight_copies(e, s):
                cp.wait()
            nxt = next_ref[j]

            @pl.when(nxt >= 0)
            def _():
                for cp in weight_copies(nxt, 1 - s):
                    cp.start()

            wgb[...] = wgf[s].astype(BF16)
            wub[...] = wuf[s].astype(BF16)
            wdb[...] = wdf[s].astype(BF16)

        xa, xb = _unpack_halves(xs_ref[...], BF16)
        half = D_MODEL // 2
        g = _dot(xa, wgb[:half, :]) + _dot(xb, wgb[half:, :])
        u = _dot(xa, wub[:half, :]) + _dot(xb, wub[half:, :])
        a = (g / (1.0 + jnp.exp(-g))) * u
        ys_ref[...] = _pack_halves(_dot(a.astype(BF16), wdb[...]))


def _experts(tile_expert, n_used, weight_slot, next_expert, xs, wg, wu, wd):
    n_tiles = xs.shape[0] // TM_MOE
    tile = lambda j, te, nu, *_: (jnp.minimum(j, nu[0] - 1), 0)
    hbm = pl.BlockSpec(memory_space=pl.ANY)
    return pl.pallas_call(
        _expert_kernel,
        grid_spec=pltpu.PrefetchScalarGridSpec(
            num_scalar_prefetch=4, grid=(n_tiles,),
            in_specs=[pl.BlockSpec((TM_MOE, D_MODEL // 2), tile), hbm, hbm, hbm],
            out_specs=pl.BlockSpec((TM_MOE, D_MODEL // 2), tile),
            scratch_shapes=[pltpu.VMEM((2, D_MODEL, D_EXPERT), F32), pltpu.VMEM((2, D_MODEL, D_EXPERT), F32),
                            pltpu.VMEM((2, D_EXPERT, D_MODEL), F32),
                            pltpu.VMEM((D_MODEL, D_EXPERT), BF16), pltpu.VMEM((D_MODEL, D_EXPERT), BF16),
                            pltpu.VMEM((D_EXPERT, D_MODEL), BF16), pltpu.SemaphoreType.DMA((2,))]),
        out_shape=jax.ShapeDtypeStruct(xs.shape, jnp.uint32),
        compiler_params=_cparams("arbitrary"), name="experts",
    )(tile_expert, n_used, weight_slot, next_expert, xs, wg, wu, wd)


def _combine_kernel(pos_ref, x1_ref, route_ref, mod_ref, ys_ref, o_ref, ybuf, sem, *, tc, n_tokens, tok0):
    _row_copies(pos_ref, n_tokens, tok0 + pl.program_id(0) * tc, tc,
                lambda k, r, p: pltpu.make_async_copy(ys_ref.at[pl.ds(p, 1)], ybuf.at[k, pl.ds(r, 1)], sem))
    for k in range(2):
        pltpu.make_async_copy(ys_ref.at[pl.ds(0, tc)], ybuf.at[k], sem).wait()
    route = route_ref[...]
    w1, w2 = route[:, 2:3], route[:, 3:4]
    half = D_MODEL // 2
    for h, (y1, y2) in enumerate(zip(_unpack_halves(ybuf[0], F32), _unpack_halves(ybuf[1], F32))):
        cols = slice(h * half, (h + 1) * half)
        gate2 = mod_ref[0, :, 5 * D_MODEL + h * half:5 * D_MODEL + (h + 1) * half]
        o_ref[:, cols] = x1_ref[:, cols] + gate2 * (w1 * y1 + w2 * y2)


def _combine(pos, x1, route, mod3, row_of_tile, ys, tok0, n_tokens, tc=256):
    t = x1.shape[0]
    row = lambda i, *_: (i, 0)
    return pl.pallas_call(
        functools.partial(_combine_kernel, tc=tc, n_tokens=n_tokens, tok0=tok0),
        grid_spec=pltpu.PrefetchScalarGridSpec(
            num_scalar_prefetch=1, grid=(t // tc,),
            in_specs=[pl.BlockSpec((tc, D_MODEL), row), pl.BlockSpec((tc, LANES), row),
                      pl.BlockSpec((1, 1, N_MOD * D_MODEL), lambda i, *_: (row_of_tile(i), 0, 0)),
                      pl.BlockSpec(memory_space=pl.ANY)],
            out_specs=pl.BlockSpec((tc, D_MODEL), row),
            scratch_shapes=[pltpu.VMEM((2, tc, ys.shape[1]), ys.dtype), pltpu.SemaphoreType.DMA]),
        out_shape=jax.ShapeDtypeStruct((t, D_MODEL), F32),
        compiler_params=_cparams("arbitrary"), name="combine",
    )(pos, x1, route, mod3, ys)


def _rope_tables(seq):
    pos = np.arange(seq)
    n_freq = ATT_HEAD_DIM // 4
    inv = ROPE_THETA ** (-np.arange(n_freq, dtype=np.float32) / n_freq)
    ang = np.concatenate([(pos // GRID_W).astype(np.float32)[:, None] * inv,
                          (pos % GRID_W).astype(np.float32)[:, None] * inv], axis=-1).astype(np.float32)
    ang = jnp.asarray(ang)
    cos, sin = jnp.cos(ang), jnp.sin(ang)
    cos_t = jnp.repeat(cos, 2, axis=1)
    sin_t = jnp.stack([-sin, sin], axis=-1).reshape(seq, ATT_HEAD_DIM)
    return cos_t, sin_t


def _pad_lanes(a):
    return jnp.pad(a, ((0, 0), (0, LANES - a.shape[1])))


def _layer(x2d, n_batch, seq, mod3, row_of_tile, lw, *, rope_tabs, cache, state, emit, h2_all, tile0, t_total):
    (n1, n2, w_main, wg_cat, bg, qg, kg, mg, w_out, rw_cat, rb) = lw
    cos_t, sin_t = rope_tabs
    outs = _project(x2d, mod3, row_of_tile(256), n1, w_main, wg_cat, bg, qg, kg, cos_t, sin_t,
                    rope=cache is not None, emit_kv=emit)
    q, k, v, qm, kmt, vm, om, g = outs[:8]
    rowq, colq = _gate_scans(g)
    attn = _attention(q, k, v, n_batch, seq, cache=cache)
    ml = _mlstm(qm, kmt, vm, rowq, colq, n_batch, seq, state=state, emit_state=emit)
    mixed = _mix_out(attn, ml[0], om, x2d, mod3, row_of_tile(MIX_TILE), mg, w_out, n2, rw_cat, rb,
                     h2_all, tile0, t_total, tm=MIX_TILE)
    return mixed, outs[8:], ml[1:]


def kernel(x_prompt, x_sample, cache_k, cache_v, state_C, state_n, state_m, c, c_ctx, mod_w, mod_b, norm1_g, norm2_g,
           w_in, b_gates, q_norm_g, k_norm_g, mlstm_norm_g, w_out, router_group_w, router_group_b, router_expert_w,
           router_expert_b, expert_w_gate, expert_w_up, expert_w_down):
    assert mod_w.shape[0] == 1, "single-layer stack"
    n_ctx, s_ctx, _ = x_prompt.shape
    n_lat, s_lat, _ = x_sample.shape
    t_ctx, t_lat = n_ctx * s_ctx, n_lat * s_lat
    t_all = t_ctx + t_lat
    ctx_row = n_lat

    cond = jnp.concatenate([c, c_ctx[None], jnp.zeros((8 - n_lat - 1, D_MODEL), F32)], axis=0)
    mod3 = _modulation(cond, mod_w[0], mod_b[0][None]).reshape(8, 1, N_MOD * D_MODEL)

    rw =jnp.concatenate([router_group_w[0], jnp.moveaxis(router_expert_w[0], 0, 1).reshape(D_MODEL, N_EXPERTS)], axis=1)
    rb = _pad_lanes(jnp.concatenate([router_group_b[0], router_expert_b[0].reshape(-1)])[None])
    w_in_t = w_in[0].T
    lw = (norm1_g, norm2_g, _cast_bf16_t(w_in_t, MAIN_WIDTH), _gate_cols(w_in_t), _pad_lanes(b_gates),
          q_norm_g, k_norm_g, mlstm_norm_g, _cast_bf16(w_out, D_MODEL), _hi_lo_cat(_pad_lanes(rw)), rb)
    rope_tabs = _rope_tables(s_lat)

    ctx_rows = lambda tm: (lambda i: ctx_row)
    lat_rows = lambda tm: (lambda i: i // (s_lat // tm))
    (x1p, h2_all, routep), (ka, va), (s_c, s_n, s_m) = _layer(
        x_prompt.reshape(t_ctx, D_MODEL), n_ctx, s_ctx, mod3, ctx_rows, lw,
        rope_tabs=rope_tabs, cache=None, state=None, emit=True, h2_all=None, tile0=0, t_total=t_all)

    caug0 = jnp.concatenate([state_C[:, 0], state_n[:, 0][..., None],
                             jnp.zeros(state_n[:, 0].shape + (LANES - 1,), F32)], axis=-1)
    caug0 = caug0.reshape(n_lat, N_CHAINS, MLSTM_HEAD_DIM, AUG)
    past = cache_k.shape[2]
    cache = (cache_k[:, 0].reshape(n_lat * past, KV_WIDTH), cache_v[:, 0].reshape(n_lat * past, KV_WIDTH))
    (x1s, h2_all, routes), _, _ = _layer(
        x_sample.reshape(t_lat, D_MODEL), n_lat, s_lat, mod3, lat_rows, lw,
        rope_tabs=rope_tabs, cache=cache, state=(caug0, state_m[:, 0].reshape(-1)), emit=False,
        h2_all=h2_all, tile0=t_ctx // MIX_TILE, t_total=t_all)

    route = jnp.concatenate([routep, routes], axis=0)
    ranks, counts = _ranks(route)
    pos, tile_plan, tails, n_rows = _routing_plan(route, ranks, counts)
    xs = _dispatch(pos, tails, h2_all, n_rows)
    y_sorted = _experts(*tile_plan, xs, expert_w_gate[0], expert_w_up[0], expert_w_down[0])
    yp = _combine(pos, x1p, routep, mod3, ctx_rows(256), y_sorted, 0, t_all)
    ys = _combine(pos, x1s, routes, mod3, lat_rows(256), y_sorted, t_ctx, t_all)

    kv_shape = (n_ctx, 1, s_ctx, ATT_KV_HEADS, ATT_HEAD_DIM)
    return (yp.reshape(x_prompt.shape), ys.reshape(x_sample.shape), ka.reshape(kv_shape), va.reshape(kv_shape),
            s_c.reshape(n_ctx, 1, 2, MLSTM_HEADS, MLSTM_HEAD_DIM, MLSTM_HEAD_DIM),
            s_n.reshape(n_ctx, 1, 2, MLSTM_HEADS, MLSTM_HEAD_DIM), s_m[..., 0, 0].reshape(n_ctx, 1, 2, MLSTM_HEADS))
```

```python
import functools

import numpy as np
import jax
import jax.numpy as jnp
from jax import lax
from jax.experimental import pallas as pl
from jax.experimental.pallas import tpu as pltpu

F32 = jnp.float32
BF16 = jnp.bfloat16

D_MODEL = 2048
GRID_W = 64
ATT_HEADS = 8
ATT_KV_HEADS = 2
ATT_HEAD_DIM = 128
ATT_GROUP = ATT_HEADS // ATT_KV_HEADS
ATT_WIDTH = ATT_HEADS * ATT_HEAD_DIM
KV_WIDTH = ATT_KV_HEADS * ATT_HEAD_DIM
ROPE_THETA = 10000.0
MLSTM_HEADS = 4
MLSTM_HEAD_DIM = 256
MLSTM_WIDTH = MLSTM_HEADS * MLSTM_HEAD_DIM
CHUNK = 256
N_GATES = 4 * MLSTM_HEADS
N_CHAINS = 2 * MLSTM_HEADS
MAIN_WIDTH = ATT_WIDTH + 2 * KV_WIDTH + 4 * MLSTM_WIDTH
N_GROUPS = 4
EXPERTS_PER_GROUP = 8
N_EXPERTS = N_GROUPS * EXPERTS_PER_GROUP
D_EXPERT = 512
N_MOD = 6
TM_MOE = 256
MIX_TILE = 512
ATT_KEY_CHUNK = 256
Q_SCALE = ATT_HEAD_DIM ** -0.5 * float(np.log2(np.e))
EPS = 1e-6

LANES = 128
AUG = MLSTM_HEAD_DIM + LANES
N_SCAN_ROWS = 5 * N_CHAINS
ROUTER_LANE0 = N_GROUPS
VMEM_LIMIT = 56 * 1024 * 1024


def _cparams(*sem):
    return pltpu.CompilerParams(dimension_semantics=sem, vmem_limit_bytes=VMEM_LIMIT)


def _const_spec(shape):
    nd = len(shape)
    return pl.BlockSpec(shape, lambda *_: (0,) * nd, pipeline_mode=pl.Buffered(1))


def _split_hi_lo(x):
    hi = x.astype(BF16)
    lo = (x - hi.astype(F32)).astype(BF16)
    return hi, lo


def _dot(a, b):
    return jnp.dot(a, b, preferred_element_type=F32)


def _hi_lo_cat(w):
    return jnp.concatenate(_split_hi_lo(w), axis=1)


def _dot_hi_lo(a_hi, a_lo, w_ref):
    r = _dot(a_hi, w_ref[...])
    return r[:, :LANES] + r[:, LANES:] + _dot(a_lo, w_ref[:, :LANES])


def _rms(x, g):
    return x * lax.rsqrt(jnp.mean(x * x, axis=-1, keepdims=True) + EPS) * g


def _mod_kernel(c_ref, w_ref, b_ref, o_ref):
    c = c_ref[...]
    s = c / (1.0 + jnp.exp(-c))
    s_hi = s.astype(BF16).astype(F32)
    lhs = jnp.concatenate([s_hi, s - s_hi], axis=0).astype(BF16)
    w_hi, w_lo = _split_hi_lo(w_ref[...])
    r = _dot(lhs, w_hi)
    r2 = _dot(lhs, w_lo)
    o_ref[...] = r[:8] + r[8:] + r2[:8] + b_ref[...]


def _modulation(cond, mod_w, mod_b):
    n = mod_w.shape[1]
    tn = 1024
    return pl.pallas_call(
        _mod_kernel,
        grid=(n // tn,),
        in_specs=[pl.BlockSpec((8, D_MODEL), lambda j: (0, 0)),
                  pl.BlockSpec((D_MODEL, tn), lambda j: (0, j)),
                  pl.BlockSpec((1, tn), lambda j: (0, j))],
        out_specs=pl.BlockSpec((8, tn), lambda j: (0, j)),
        out_shape=jax.ShapeDtypeStruct((8, n), F32),
        compiler_params=_cparams("parallel"),
        name="modulation",
    )(cond, mod_w, mod_b)


def _cast_kernel(w_ref, o_ref):
    o_ref[...] = w_ref[0].astype(BF16)


def _cast_bf16(w, n_cols, tn=512):
    rows = w.shape[1]
    return pl.pallas_call(
        _cast_kernel, grid=(n_cols // tn,),
        in_specs=[pl.BlockSpec((1, rows, tn), lambda j: (0, 0, j))],
        out_specs=pl.BlockSpec((rows, tn), lambda j: (0, j)),
        out_shape=jax.ShapeDtypeStruct((rows, n_cols), BF16),
        compiler_params=_cparams("parallel"), name="cast_bf16",
    )(w)


def _cast_t_kernel(wt_ref, o_ref):
    o_ref[...] = wt_ref[...].T.astype(BF16)


def _cast_bf16_t(wt, n_cols, tn=512):
    rows = wt.shape[1]
    return pl.pallas_call(
        _cast_t_kernel, grid=(n_cols // tn,),
        in_specs=[pl.BlockSpec((tn, rows), lambda j: (j, 0))],
        out_specs=pl.BlockSpec((rows, tn), lambda j: (0, j)),
        out_shape=jax.ShapeDtypeStruct((rows, n_cols), BF16),
        compiler_params=_cparams("parallel"), name="cast_bf16_t",
    )(wt)


def _gate_cols_kernel(wt_ref, o_ref):
    sub = lax.broadcasted_iota(jnp.int32, wt_ref.shape, 0)
    hi, lo = _split_hi_lo(jnp.where(sub < N_GATES, wt_ref[...], 0.0).T)
    o_ref[:, :LANES] = hi
    o_ref[:, LANES:] = lo


def _gate_cols(wt):
    rows = wt.shape[1]
    return pl.pallas_call(
        _gate_cols_kernel, grid=(1,),
        in_specs=[pl.BlockSpec((LANES, rows), lambda i: (MAIN_WIDTH // LANES, 0))],
        out_specs=pl.BlockSpec((rows, 2 * LANES), lambda i: (0, 0)),
        out_shape=jax.ShapeDtypeStruct((rows, 2 * LANES), BF16),
        compiler_params=_cparams("arbitrary"), name="gate_cols",
    )(wt)


def _pair_swap(x):
    lane = lax.broadcasted_iota(jnp.int32, x.shape, 1)
    return jnp.where((lane & 1) == 0, pltpu.roll(x, LANES - 1, 1), pltpu.roll(x, 1, 1))


def _proj_kernel(x_ref, mod_ref, n1_ref, w_ref, wg_ref, bg_ref, qg_ref, kg_ref, cos_ref, sin_ref,
                 q_ref, k_ref, v_ref, qm_ref, kmt_ref, vm_ref, om_ref, g_ref, *kv_refs, rope, tm):
    mod = mod_ref[0]
    shift, scale = mod[:, :D_MODEL], mod[:, D_MODEL:2 * D_MODEL]
    h = _rms(x_ref[...], n1_ref[...]) * (1.0 + scale) + shift
    h_hi, h_lo = _split_hi_lo(h)

    def rot(seg):
        return seg * cos_ref[...] + _pair_swap(seg) * sin_ref[...] if rope else seg

    qa = _dot(h_hi, w_ref[:, :ATT_WIDTH])
    for hh in range(ATT_HEADS):
        sl = slice(hh * ATT_HEAD_DIM, (hh + 1) * ATT_HEAD_DIM)
        seg = rot(_rms(qa[:, sl], qg_ref[...]))
        q_ref[:, sl] = (seg * Q_SCALE).astype(BF16)

    kv = _dot(h_hi, w_ref[:, ATT_WIDTH:ATT_WIDTH + 2 * KV_WIDTH])
    for hh in range(ATT_KV_HEADS):
        sl = slice(hh * ATT_HEAD_DIM, (hh + 1) * ATT_HEAD_DIM)
        seg = _rms(kv[:, sl], kg_ref[...])
        if kv_refs:
            kv_refs[0][:, sl] = seg
        k_ref[:, sl] = rot(seg).astype(BF16)
    va = kv[:, KV_WIDTH:]
    if kv_refs:
        kv_refs[1][...] = va
    v_ref[...] = va.astype(BF16)

    c0 = ATT_WIDTH + 2 * KV_WIDTH
    qm_ref[...] = (_dot(h_hi, w_ref[:, c0:c0 + MLSTM_WIDTH]) * MLSTM_HEAD_DIM ** -0.5).astype(BF16)
    km = _dot(h_hi, w_ref[:, c0 + MLSTM_WIDTH:c0 + 2 * MLSTM_WIDTH])
    for cc in range(tm // CHUNK):
        kmt_ref[cc] = km[cc * CHUNK:(cc + 1) * CHUNK, :].T.astype(BF16)
    vm_ref[...] = _dot(h_hi, w_ref[:, c0 + 2 * MLSTM_WIDTH:c0 + 3 * MLSTM_WIDTH]).astype(BF16)
    om_ref[...] = _dot(h_hi, w_ref[:, c0 + 3 * MLSTM_WIDTH:c0 + 4 * MLSTM_WIDTH])
    g_ref[...] = _dot_hi_lo(h_hi, h_lo, wg_ref) + bg_ref[...]


def _project(x2d, mod3, row_of_tile, n1, w_main, wg_cat, bg, qg, kg, cos_t, sin_t, *, rope, emit_kv, tm=256):
    t = x2d.shape[0]
    n_pos = cos_t.shape[0] // tm
    row = lambda i: (i, 0)
    in_specs = [pl.BlockSpec((tm, D_MODEL), row),
                pl.BlockSpec((1, 1, N_MOD * D_MODEL), lambda i: (row_of_tile(i), 0, 0)),
                _const_spec((1, D_MODEL)),
                _const_spec((D_MODEL, MAIN_WIDTH)),
                _const_spec((D_MODEL, 2 * LANES)), _const_spec((1, LANES)),
                _const_spec((1, ATT_HEAD_DIM)), _const_spec((1, ATT_HEAD_DIM)),
                pl.BlockSpec((tm, ATT_HEAD_DIM), lambda i: (i % n_pos, 0)),
                pl.BlockSpec((tm, ATT_HEAD_DIM), lambda i: (i % n_pos, 0))]
    out_shape = [jax.ShapeDtypeStruct((t, ATT_WIDTH), BF16), jax.ShapeDtypeStruct((t, KV_WIDTH), BF16),
                 jax.ShapeDtypeStruct((t, KV_WIDTH), BF16), jax.ShapeDtypeStruct((t, MLSTM_WIDTH), BF16),
                 jax.ShapeDtypeStruct((t // CHUNK, MLSTM_WIDTH, CHUNK), BF16),
                 jax.ShapeDtypeStruct((t, MLSTM_WIDTH), BF16), jax.ShapeDtypeStruct((t, MLSTM_WIDTH), F32),
                 jax.ShapeDtypeStruct((t, LANES), F32)]
    out_specs = [pl.BlockSpec((tm, ATT_WIDTH), row), pl.BlockSpec((tm, KV_WIDTH), row),
                 pl.BlockSpec((tm, KV_WIDTH), row), pl.BlockSpec((tm, MLSTM_WIDTH), row),
                 pl.BlockSpec((tm // CHUNK, MLSTM_WIDTH, CHUNK), lambda i: (i, 0, 0)),
                 pl.BlockSpec((tm, MLSTM_WIDTH), row), pl.BlockSpec((tm, MLSTM_WIDTH), row),
                 pl.BlockSpec((tm, LANES), row)]
    if emit_kv:
        out_shape += [jax.ShapeDtypeStruct((t, KV_WIDTH), F32)] * 2
        out_specs += [pl.BlockSpec((tm, KV_WIDTH), row)] * 2
    return pl.pallas_call(
        functools.partial(_proj_kernel, rope=rope, tm=tm),
        grid=(t // tm,), in_specs=in_specs, out_specs=out_specs, out_shape=out_shape,
        compiler_params=_cparams("parallel"), name="in_proj",
    )(x2d, mod3, n1, w_main, wg_cat, bg, qg, kg, cos_t, sin_t)


def _lane_scan(x, op, fill, is_fwd, lane):
    s = 1
    while s < CHUNK:
        from_left = jnp.where(lane >= s, pltpu.roll(x, s, 1), fill)
        from_right = jnp.where(lane < CHUNK - s, pltpu.roll(x, CHUNK - s, 1), fill)
        x = op(x, jnp.where(is_fwd, from_left, from_right))
        s *= 2
    return x


def _scan_kernel(g_ref, row_ref, col_ref, *, n_chunks):
    lane = lax.broadcasted_iota(jnp.int32, (N_CHAINS, CHUNK), 1)
    is_fwd = lax.broadcasted_iota(jnp.int32, (N_CHAINS, CHUNK), 0) < MLSTM_HEADS
    for cc in range(n_chunks):
        gt = g_ref[cc * CHUNK:(cc + 1) * CHUNK, :].T
        fwd, bwd = gt[:N_CHAINS], gt[N_CHAINS:2 * N_CHAINS]
        li = jnp.where(is_fwd, fwd, pltpu.roll(bwd, MLSTM_HEADS, 0))
        f = jnp.where(is_fwd, pltpu.roll(fwd, MLSTM_HEADS, 0), bwd)
        lf = jnp.minimum(f, 0.0) - jnp.log(1.0 + jnp.exp(-jnp.abs(f)))
        b = _lane_scan(lf, jnp.add, 0.0, is_fwd, lane)
        a = li - b
        run_max = _lane_scan(a, jnp.maximum, -jnp.inf, is_fwd, lane)
        all_max = jnp.broadcast_to(jnp.max(a, axis=1, keepdims=True), a.shape)
        last = jnp.where(is_fwd, CHUNK - 1, 0)
        total = jnp.broadcast_to(jnp.sum(jnp.where(lane == last, b, 0.0), axis=1, keepdims=True), a.shape)
        rows = jnp.concatenate([a, b, run_max, all_max, total], axis=0)
        row_ref[cc] = rows
        padded = jnp.concatenate([rows, jnp.zeros((LANES - N_SCAN_ROWS, CHUNK), F32)], axis=0)
        col_ref[cc * CHUNK:(cc + 1) * CHUNK, :] = padded.T


def _gate_scans(g, tb=1024):
    t = g.shape[0]
    tb = min(tb, t)
    return pl.pallas_call(
        functools.partial(_scan_kernel, n_chunks=tb // CHUNK),
        grid=(t // tb,),
        in_specs=[pl.BlockSpec((tb, LANES), lambda i: (i, 0))],
        out_specs=[pl.BlockSpec((tb // CHUNK, N_SCAN_ROWS, CHUNK), lambda i: (i, 0, 0)),
                   pl.BlockSpec((tb, LANES), lambda i: (i, 0))],
        out_shape=[jax.ShapeDtypeStruct((t // CHUNK, N_SCAN_ROWS, CHUNK), F32),
                   jax.ShapeDtypeStruct((t, LANES), F32)],
        compiler_params=_cparams("parallel"), name="gate_scans",
    )(g)


def _mlstm_kernel(*refs, n_chunks, has_state, emit_state):
    it = iter(refs)
    q_ref, kt_ref, v_ref, row_ref, col_ref = [next(it) for _ in range(5)]
    c0_ref, m0_ref = (next(it), next(it)) if has_state else (None, None)
    o_ref = next(it)
    c_out, n_out, m_out = (next(it), next(it), next(it)) if emit_state else (None, None, None)
    caug, hb = next(it), next(it)

    b = pl.program_id(0)
    if has_state:
        caug[...] = c0_ref[0]
        m_init = tuple(jnp.full((1, 1), m0_ref[b * N_CHAINS + r], F32) for r in range(N_CHAINS))
    else:
        caug[...] = jnp.zeros(caug.shape, F32)
        m_init = tuple(jnp.zeros((1, 1), F32) for _ in range(N_CHAINS))

    sub = lax.broadcasted_iota(jnp.int32, (CHUNK, CHUNK), 0)
    lane = lax.broadcasted_iota(jnp.int32, (CHUNK, CHUNK), 1)
    ones_col = (lax.broadcasted_iota(jnp.int32, (CHUNK, LANES), 1) == 0).astype(BF16)

    def chain(r, cc, m_prev):
        d, hd = divmod(r, MLSTM_HEADS)
        t0 = pl.multiple_of(cc * CHUNK, CHUNK)
        hs = slice(hd * MLSTM_HEAD_DIM, (hd + 1) * MLSTM_HEAD_DIM)
        rows = row_ref[cc]
        cols = col_ref[pl.ds(t0, CHUNK), :]
        row = lambda k: rows[k * N_CHAINS + r:k * N_CHAINS + r + 1, :]
        col = lambda k: cols[:, k * N_CHAINS + r:k * N_CHAINS + r + 1]
        q = q_ref[pl.ds(t0, CHUNK), hs]
        kt = kt_ref[cc, hs, :]
        vaug = jnp.concatenate([v_ref[pl.ds(t0, CHUNK), hs], ones_col], axis=1)

        m_col = jnp.maximum(m_prev, col(2))
        keep = (lane <= sub) if d == 0 else (lane >= sub)
        w = jnp.where(keep, jnp.exp(row(0) - m_col), 0.0)
        w_inter = jnp.exp(m_prev - m_col)
        p = (_dot(q, kt) * w).astype(BF16)
        intra = _dot(p, vaug)
        inter = _dot(q, caug[r].astype(BF16))
        num = intra[:, :MLSTM_HEAD_DIM] + w_inter * inter[:, :MLSTM_HEAD_DIM]
        den = intra[:, MLSTM_HEAD_DIM:MLSTM_HEAD_DIM + 1] + w_inter * inter[:, MLSTM_HEAD_DIM:MLSTM_HEAD_DIM + 1]
        h_out = num / jnp.maximum(jnp.abs(den), jnp.exp(-(col(1) + m_col)))

        m_last = jnp.maximum(m_prev, row(3))
        kw = (kt.astype(F32) * jnp.exp(row(0) - m_last)).astype(BF16)
        caug[r] = jnp.exp(m_prev - m_last[:, :1]) * caug[r] + _dot(kw, vaug)
        return h_out, (row(4) + m_last)[:, :1]

    def body(c, ms):
        new = []
        for r in range(N_CHAINS):
            cc = c if r < MLSTM_HEADS else n_chunks - 1 - c
            h_out, m_new = chain(r, cc, ms[r])
            hd = r % MLSTM_HEADS
            dst = o_ref if r < MLSTM_HEADS else hb
            dst[pl.ds(pl.multiple_of(cc * CHUNK, CHUNK), CHUNK), hd * MLSTM_HEAD_DIM:(hd + 1) * MLSTM_HEAD_DIM] = h_out
            new.append(m_new)
        return tuple(new)

    ms = lax.fori_loop(0, n_chunks, body, m_init)
    o_ref[...] += hb[...]
    if emit_state:
        for r in range(N_CHAINS):
            c_out[0, r] = caug[r, :, :MLSTM_HEAD_DIM]
            n_out[0, r] = caug[r, :, MLSTM_HEAD_DIM:].T[:1, :]
            m_out[0, r] = jnp.broadcast_to(ms[r], (1, LANES))


def _mlstm(qm, kmt, vm, rowq, colq, n_batch, seq, state=None, emit_state=False):
    nc = seq // CHUNK
    mode = dict(pipeline_mode=pl.Buffered(1)) if seq * MLSTM_WIDTH * 4 > (2 << 20) else {}
    in_specs = [pl.BlockSpec((seq, MLSTM_WIDTH), lambda b: (b, 0), **mode),
                pl.BlockSpec((nc, MLSTM_WIDTH, CHUNK), lambda b: (b, 0, 0), **mode),
                pl.BlockSpec((seq, MLSTM_WIDTH), lambda b: (b, 0), **mode),
                pl.BlockSpec((nc, N_SCAN_ROWS, CHUNK), lambda b: (b, 0, 0), **mode),
                pl.BlockSpec((seq, LANES), lambda b: (b, 0), **mode)]
    args = [qm, kmt, vm, rowq, colq]
    if state is not None:
        in_specs += [pl.BlockSpec((1, N_CHAINS, MLSTM_HEAD_DIM, AUG), lambda b: (b, 0, 0, 0), **mode),
                     pl.BlockSpec(memory_space=pltpu.SMEM)]
        args += list(state)
    out_shape = [jax.ShapeDtypeStruct((n_batch * seq, MLSTM_WIDTH), F32)]
    out_specs = [pl.BlockSpec((seq, MLSTM_WIDTH), lambda b: (b, 0))]
    if emit_state:
        out_shape += [jax.ShapeDtypeStruct((n_batch, N_CHAINS, MLSTM_HEAD_DIM, MLSTM_HEAD_DIM), F32),
                      jax.ShapeDtypeStruct((n_batch, N_CHAINS, 1, MLSTM_HEAD_DIM), F32),
                      jax.ShapeDtypeStruct((n_batch, N_CHAINS, 1, LANES), F32)]
        out_specs += [pl.BlockSpec((1, N_CHAINS, MLSTM_HEAD_DIM, MLSTM_HEAD_DIM), lambda b: (b, 0, 0, 0)),
                      pl.BlockSpec((1, N_CHAINS, 1, MLSTM_HEAD_DIM), lambda b: (b, 0, 0, 0)),
                      pl.BlockSpec((1, N_CHAINS, 1, LANES), lambda b: (b, 0, 0, 0))]
    return pl.pallas_call(
        functools.partial(_mlstm_kernel, n_chunks=nc, has_state=state is not None, emit_state=emit_state),
        grid=(n_batch,), in_specs=in_specs, out_specs=out_specs, out_shape=out_shape,
        scratch_shapes=[pltpu.VMEM((N_CHAINS, MLSTM_HEAD_DIM, AUG), F32), pltpu.VMEM((seq, MLSTM_WIDTH), F32)],
        compiler_params=_cparams("parallel"), name="mlstm",
    )(*args)


def _attn_kernel(*refs, tq, has_cache):
    if has_cache:
        q_ref, k_ref, v_ref, ck_ref, cv_ref, o_ref = refs
    else:
        q_ref, k_ref, v_ref, o_ref = refs
    q = q_ref[...]
    qs = jnp.concatenate([q[:, g * ATT_HEAD_DIM:(g + 1) * ATT_HEAD_DIM] for g in range(ATT_GROUP)], axis=0)
    nt = (((1,), (1,)), ((), ()))
    seq = k_ref.shape[0]
    kc = min(seq, ATT_KEY_CHUNK)
    chunks = [(k_ref, v_ref, c * kc) for c in range(seq // kc)]
    if has_cache:
        chunks.insert(0, (ck_ref, cv_ref, None))
    m = l = o = None
    for kr, vr, start in chunks:
        kk, vv = (kr[...], vr[...]) if start is None else (kr[start:start + kc, :], vr[start:start + kc, :])
        s = lax.dot_general(qs, kk.astype(BF16), nt, preferred_element_type=F32)
        mc = jnp.max(s, axis=-1, keepdims=True)
        if m is None:
            m = mc
            p = jnp.exp2(s - m)
            l = jnp.sum(p, axis=-1, keepdims=True)
            o = _dot(p.astype(BF16), vv.astype(BF16))
        else:
            m_new = jnp.maximum(m, mc)
            alpha = jnp.exp2(m - m_new)
            p = jnp.exp2(s - m_new)
            l = alpha * l + jnp.sum(p, axis=-1, keepdims=True)
            o = alpha * o + _dot(p.astype(BF16), vv.astype(BF16))
            m = m_new
    o = o / l
    for g in range(ATT_GROUP):
        o_ref[:, g * ATT_HEAD_DIM:(g + 1) * ATT_HEAD_DIM] = o[g * tq:(g + 1) * tq].astype(BF16)


def _attention(q, k, v, n_batch, seq, cache=None, tq=256):
    nqb = seq // tq
    gw = ATT_GROUP * ATT_HEAD_DIM
    in_specs = [pl.BlockSpec((tq, gw), lambda b, h, i: (b * nqb + i, h)),
                pl.BlockSpec((seq, ATT_HEAD_DIM), lambda b, h, i: (b, h)),
                pl.BlockSpec((seq, ATT_HEAD_DIM), lambda b, h, i: (b, h))]
    args = [q, k, v]
    if cache is not None:
        past = cache[0].shape[0] // n_batch
        in_specs += [pl.BlockSpec((past, ATT_HEAD_DIM), lambda b, h, i: (b, h))] * 2
        args += list(cache)
    return pl.pallas_call(
        functools.partial(_attn_kernel, tq=tq, has_cache=cache is not None),
        grid=(n_batch, ATT_KV_HEADS, nqb), in_specs=in_specs,
        out_specs=pl.BlockSpec((tq, gw), lambda b, h, i: (b * nqb + i, h)),
        out_shape=jax.ShapeDtypeStruct((n_batch * seq, ATT_WIDTH), BF16),
        compiler_params=_cparams("parallel", "parallel", "parallel"), name="attention",
    )(*args)


def _mix_kernel(attn_ref, hm_ref, om_ref, x_ref, mod_ref, mg_ref, wo_ref, n2_ref, rw_ref, rb_ref, *rest, sub):
    for r0 in range(0, x_ref.shape[0], sub):
        _mix_rows(slice(r0, r0 + sub), attn_ref, hm_ref, om_ref, x_ref, mod_ref, mg_ref, wo_ref, n2_ref, rw_ref,
                  rb_ref, *rest[-3:])


def _mix_rows(rows, attn_ref, hm_ref, om_ref, x_ref, mod_ref, mg_ref, wo_ref, n2_ref, rw_ref, rb_ref,
              x1_ref, h2_ref, route_ref):
    mod = mod_ref[0]
    gate1 = mod[:, 2 * D_MODEL:3 * D_MODEL]
    shift2, scale2 = mod[:, 3 * D_MODEL:4 * D_MODEL], mod[:, 4 * D_MODEL:5 * D_MODEL]
    hm = hm_ref[rows, :]
    mg = mg_ref[...]
    parts = []
    for hd in range(MLSTM_HEADS):
        sl = slice(hd * MLSTM_HEAD_DIM, (hd + 1) * MLSTM_HEAD_DIM)
        parts.append(_rms(hm[:, sl], mg[:, sl]))
    om = om_ref[rows, :]
    hmg = jnp.concatenate(parts, axis=1) * (1.0 / (1.0 + jnp.exp(-om)))
    y = _dot(attn_ref[rows, :], wo_ref[:ATT_WIDTH, :]) + _dot(hmg.astype(BF16), wo_ref[ATT_WIDTH:, :])
    x1 = x_ref[rows, :] + gate1 * y
    x1_ref[rows, :] = x1
    h2 = _rms(x1, n2_ref[...]) * (1.0 + scale2) + shift2
    h2_hi, h2_lo = _split_hi_lo(h2)
    h2_ref[rows, :] = _pack_halves(h2)

    lg = _dot_hi_lo(h2_hi, h2_lo, rw_ref) + rb_ref[...]
    lane = lax.broadcasted_iota(jnp.int32, lg.shape, 1).astype(F32)
    neg = -jnp.inf
    first = lambda hit: jnp.min(jnp.where(hit, lane, float(LANES)), axis=-1, keepdims=True)
    gl = jnp.where(lane < N_GROUPS, lg, neg)
    gmax = jnp.max(gl, axis=-1, keepdims=True)
    grp = first(gl == gmax)
    p_grp = 1.0 / jnp.sum(jnp.exp(gl - gmax), axis=-1, keepdims=True)
    lo = ROUTER_LANE0 + grp * EXPERTS_PER_GROUP
    el = jnp.where((lane >= lo) & (lane < lo + EXPERTS_PER_GROUP), lg, neg)
    m1 = jnp.max(el, axis=-1, keepdims=True)
    i1 = first(el == m1)
    el2 = jnp.where(lane == i1, neg, el)
    m2 = jnp.max(el2, axis=-1, keepdims=True)
    i2 = first(el2 == m2)
    r = jnp.exp(m2 - m1)
    w1 = p_grp / (1.0 + r)
    w2 = w1 * r
    route_ref[rows, :] = jnp.where(lane == 0.0, i1 - ROUTER_LANE0, jnp.where(lane == 1.0, i2 - ROUTER_LANE0,
                                   jnp.where(lane == 2.0, w1, jnp.where(lane == 3.0, w2, 0.0))))


def _mix_out(attn, hm, om, x2d, mod3, row_of_tile, mg, w_out, n2, rw_cat, rb, h2_all, tile0, t_total, tm=512):
    t = x2d.shape[0]
    row = lambda i: (i, 0)
    in_specs = [pl.BlockSpec((tm, ATT_WIDTH), row), pl.BlockSpec((tm, MLSTM_WIDTH), row),
                pl.BlockSpec((tm, MLSTM_WIDTH), row), pl.BlockSpec((tm, D_MODEL), row),
                pl.BlockSpec((1, 1, N_MOD * D_MODEL), lambda i: (row_of_tile(i), 0, 0)),
                _const_spec((1, MLSTM_WIDTH)), _const_spec((D_MODEL, D_MODEL)), _const_spec((1, D_MODEL)),
                _const_spec((D_MODEL, 2 * LANES)), _const_spec((1, LANES))]
    args = [attn, hm, om, x2d, mod3, mg, w_out, n2, rw_cat, rb]
    aliases = {}
    if h2_all is not None:
        aliases = {len(args): 1}
        in_specs.append(pl.BlockSpec(memory_space=pl.ANY))
        args.append(h2_all)
    return pl.pallas_call(
        functools.partial(_mix_kernel, sub=256), grid=(t // tm,), in_specs=in_specs,
        out_specs=[pl.BlockSpec((tm, D_MODEL), row), pl.BlockSpec((tm, D_MODEL // 2), lambda i: (tile0 + i, 0)),
                   pl.BlockSpec((tm, LANES), row)],
        out_shape=[jax.ShapeDtypeStruct((t, D_MODEL), F32), jax.ShapeDtypeStruct((t_total, D_MODEL // 2), jnp.uint32),
                   jax.ShapeDtypeStruct((t, LANES), F32)],
        input_output_aliases=aliases,
        compiler_params=_cparams("parallel"), name="mix_out",
    )(*args)


def _rank_kernel(route_ref, rank_ref, cnt_ref, run_ref, tri_ref):
    tr = route_ref.shape[0]

    @pl.when(pl.program_id(0) == 0)
    def _():
        run_ref[...] = jnp.zeros(run_ref.shape, F32)
        tri_ref[...] = (lax.broadcasted_iota(jnp.int32, (tr, tr), 1)
                        < lax.broadcasted_iota(jnp.int32, (tr, tr), 0)).astype(BF16)

    route = route_ref[...]
    lane = lax.broadcasted_iota(jnp.int32, route.shape, 1).astype(F32)
    hit1, hit2 = lane == route[:, 0:1], lane == route[:, 1:2]
    onehot = jnp.where(hit1, 1.0, jnp.where(hit2, 1.0, 0.0))
    before = _dot(tri_ref[...], onehot.astype(BF16)) + run_ref[0:1, :]
    r1 = jnp.sum(jnp.where(hit1, before, 0.0), axis=-1, keepdims=True)
    r2 = jnp.sum(jnp.where(hit2, before, 0.0), axis=-1, keepdims=True)
    rank_ref[...] = jnp.where(lane == 0.0, r1, jnp.where(lane == 1.0, r2, 0.0))
    run_ref[...] = run_ref[...] + jnp.sum(onehot, axis=0, keepdims=True)
    cnt_ref[...] = run_ref[...]


def _ranks(route, tr=512):
    t = route.shape[0]
    return pl.pallas_call(
        _rank_kernel, grid=(t // tr,),
        in_specs=[pl.BlockSpec((tr, LANES), lambda i: (i, 0))],
        out_specs=[pl.BlockSpec((tr, LANES), lambda i: (i, 0)), pl.BlockSpec((8, LANES), lambda i: (0, 0))],
        out_shape=[jax.ShapeDtypeStruct((t, LANES), F32), jax.ShapeDtypeStruct((8, LANES), F32)],
        scratch_shapes=[pltpu.VMEM((8, LANES), F32), pltpu.VMEM((tr, tr), BF16)],
        compiler_params=_cparams("arbitrary"), name="expert_ranks",
    )(route)


def _pos_kernel(route_ref, rank_ref, start_ref, o_ref):
    route, rank = route_ref[...], rank_ref[...]
    lane = lax.broadcasted_iota(jnp.int32, route.shape, 1).astype(F32)
    start = start_ref[...]
    first = lambda col: jnp.sum(jnp.where(lane == route[:, col:col + 1], start, 0.0), axis=-1, keepdims=True)
    p1 = first(0) + rank[:, 0:1]
    p2 = first(1) + rank[:, 1:2]
    tile = jnp.where(lane == 0.0, p1, jnp.where(lane == 1.0, p2, 0.0))
    o_ref[...] = tile.T[:8, :].astype(jnp.int32)


def _positions(route, ranks, starts, tr=512):
    t = route.shape[0]
    pos = pl.pallas_call(
        _pos_kernel, grid=(t // tr,),
        in_specs=[pl.BlockSpec((tr, LANES), lambda i: (i, 0)), pl.BlockSpec((tr, LANES), lambda i: (i, 0)),
                  pl.BlockSpec((1, LANES), lambda i: (0, 0))],
        out_specs=pl.BlockSpec((8, tr), lambda i: (0, i)),
        out_shape=jax.ShapeDtypeStruct((8, t), jnp.int32),
        compiler_params=_cparams("parallel"), name="pair_rows",
    )(route, ranks, starts)
    return pos[:2].reshape(-1)


def _routing_plan(route, ranks, counts):
    t = route.shape[0]
    n_tiles = 2 * t // TM_MOE + N_EXPERTS
    cnt = counts[0, :N_EXPERTS].astype(jnp.int32)
    padded = (cnt + TM_MOE - 1) // TM_MOE * TM_MOE
    ends = jnp.cumsum(padded)
    pos = _positions(route, ranks, _pad_lanes((ends - padded).astype(F32)[None]))
    n_used = (ends[-1:] // TM_MOE).astype(jnp.int32)
    tile_start = jnp.arange(n_tiles, dtype=jnp.int32) * TM_MOE
    tile_expert = jnp.sum((ends[None, :] <= tile_start[:, None]).astype(jnp.int32), axis=1)
    tile_expert = jnp.minimum(tile_expert, tile_expert[n_used[0] - 1]).astype(jnp.int32)
    tails = jnp.where(padded > 0, ends - TM_MOE, -1).astype(jnp.int32)
    change = jnp.concatenate([jnp.ones((1,), jnp.int32), (tile_expert[1:] != tile_expert[:-1]).astype(jnp.int32)])
    weight_slot = (jnp.cumsum(change) - 1) % 2
    ids = jnp.arange(N_EXPERTS, dtype=jnp.int32)
    later_used = (padded > 0)[None, :] & (ids[None, :] > ids[:, None])
    next_used = jnp.min(jnp.where(later_used, ids[None, :], N_EXPERTS), axis=1)
    next_used = jnp.where(next_used < N_EXPERTS, next_used, -1)
    next_expert = jnp.sum(jnp.where(tile_expert[:, None] == ids[None, :], next_used[None, :], 0), axis=1)
    return pos, (tile_expert, n_used, weight_slot.astype(jnp.int32), next_expert.astype(jnp.int32)), tails, \
        n_tiles * TM_MOE


def _row_copies(pos_ref, n_pairs, tok0, n_rows, make_copy):
    def body(r, carry):
        for k in range(2):
            make_copy(k, r, pos_ref[k * n_pairs + tok0 + r]).start()
        return carry
    lax.fori_loop(0, n_rows, body, 0, unroll=8)


def _dispatch_kernel(pos_ref, tail_ref, h_ref, h_hbm, xs_ref, zero_ref, sem, sem2, *, td, n_tokens):
    i = pl.program_id(0)

    @pl.when(i == 0)
    def _():
        zero_ref[...] = jnp.zeros(zero_ref.shape, zero_ref.dtype)
        tail_copy = lambda e: pltpu.make_async_copy(
            zero_ref, xs_ref.at[pl.ds(pl.multiple_of(tail_ref[e], TM_MOE), TM_MOE)], sem)
        for e in range(N_EXPERTS):
            pl.when(tail_ref[e] >= 0)(lambda e=e: tail_copy(e).start())
        for e in range(N_EXPERTS):
            pl.when(tail_ref[e] >= 0)(lambda e=e: tail_copy(e).wait())

    def make_copy(k, r, p):
        if k == 0:
            return pltpu.make_async_copy(h_ref.at[pl.ds(r, 1)], xs_ref.at[pl.ds(p, 1)], sem)
        return pltpu.make_async_copy(h_hbm.at[pl.ds(i * td + r, 1)], xs_ref.at[pl.ds(p, 1)], sem2)

    _row_copies(pos_ref, n_tokens, i * td, td, make_copy)
    pltpu.make_async_copy(h_ref, xs_ref.at[pl.ds(0, td)], sem).wait()
    pltpu.make_async_copy(h_hbm.at[pl.ds(0, td)], xs_ref.at[pl.ds(0, td)], sem2).wait()


def _dispatch(pos, tails, h2, n_rows, td=256):
    t = h2.shape[0]
    return pl.pallas_call(
        functools.partial(_dispatch_kernel, td=td, n_tokens=t),
        grid_spec=pltpu.PrefetchScalarGridSpec(
            num_scalar_prefetch=2, grid=(t // td,),
            in_specs=[pl.BlockSpec((td, h2.shape[1]), lambda i, *_: (i, 0)), pl.BlockSpec(memory_space=pl.ANY)],
            out_specs=pl.BlockSpec(memory_space=pl.ANY),
            scratch_shapes=[pltpu.VMEM((TM_MOE, h2.shape[1]), h2.dtype), pltpu.SemaphoreType.DMA,
                            pltpu.SemaphoreType.DMA]),
        out_shape=jax.ShapeDtypeStruct((n_rows, h2.shape[1]), h2.dtype),
        compiler_params=_cparams("arbitrary"), name="dispatch",
    )(pos, tails, h2, h2)


def _pack_halves(x):
    half = x.shape[1] // 2
    return pltpu.pack_elementwise([x[:, :half], x[:, half:]], packed_dtype=BF16)


def _unpack_halves(p, dtype):
    return tuple(pltpu.unpack_elementwise(p, index=i, packed_dtype=BF16, unpacked_dtype=F32).astype(dtype)
                 for i in range(2))


def _expert_kernel(te_ref, nu_ref, slot_ref, next_ref, xs_ref, wg_ref, wu_ref, wd_ref, ys_ref,
                   wgf, wuf, wdf, wgb, wub, wdb, wsem):
    j = pl.program_id(0)

    def weight_copies(e, s):
        return [pltpu.make_async_copy(w_ref.at[e], buf.at[s], wsem.at[s])
                for w_ref, buf in ((wg_ref, wgf), (wu_ref, wuf), (wd_ref, wdf))]

    @pl.when(j < nu_ref[0])
    def _():
        e, s = te_ref[j], slot_ref[j]

        @pl.when(j == 0)
        def _():
            for cp in weight_copies(e, s):
                cp.start()

        @pl.when((j == 0) | (e != te_ref[jnp.maximum(j - 1, 0)]))
        def _():
            for cp in weight_copies(e, s):
                cp.wait()
            nxt = next_ref[j]

            @pl.when(nxt >= 0)
            def _():
                for cp in weight_copies(nxt, 1 - s):
                    cp.start()

            wgb[...] = wgf[s].astype(BF16)
            wub[...] = wuf[s].astype(BF16)
            wdb[...] = wdf[s].astype(BF16)

        xa, xb = _unpack_halves(xs_ref[...], BF16)
        half = D_MODEL // 2
        g = _dot(xa, wgb[:half, :]) + _dot(xb, wgb[half:, :])
        u = _dot(xa, wub[:half, :]) + _dot(xb, wub[half:, :])
        a = (g / (1.0 + jnp.exp(-g))) * u
        ys_ref[...] = _pack_halves(_dot(a.astype(BF16), wdb[...]))


def _experts(tile_expert, n_used, weight_slot, next_expert, xs, wg, wu, wd):
    n_tiles = xs.shape[0] // TM_MOE
    tile = lambda j, te, nu, *_: (jnp.minimum(j, nu[0] - 1), 0)
    hbm = pl.BlockSpec(memory_space=pl.ANY)
    return pl.pallas_call(
        _expert_kernel,
        grid_spec=pltpu.PrefetchScalarGridSpec(
            num_scalar_prefetch=4, grid=(n_tiles,),
            in_specs=[pl.BlockSpec((TM_MOE, D_MODEL // 2), tile), hbm, hbm, hbm],
            out_specs=pl.BlockSpec((TM_MOE, D_MODEL // 2), tile),
            scratch_shapes=[pltpu.VMEM((2, D_MODEL, D_EXPERT), F32), pltpu.VMEM((2, D_MODEL, D_EXPERT), F32),
                            pltpu.VMEM((2, D_EXPERT, D_MODEL), F32),
                            pltpu.VMEM((D_MODEL, D_EXPERT), BF16), pltpu.VMEM((D_MODEL, D_EXPERT), BF16),
                            pltpu.VMEM((D_EXPERT, D_MODEL), BF16), pltpu.SemaphoreType.DMA((2,))]),
        out_shape=jax.ShapeDtypeStruct(xs.shape, jnp.uint32),
        compiler_params=_cparams("arbitrary"), name="experts",
    )(tile_expert, n_used, weight_slot, next_expert, xs, wg, wu, wd)


def _combine_kernel(pos_ref, x1_ref, route_ref, mod_ref, ys_ref, o_ref, ybuf, sem, *, tc, n_tokens, tok0):
    _row_copies(pos_ref, n_tokens, tok0 + pl.program_id(0) * tc, tc,
                lambda k, r, p: pltpu.make_async_copy(ys_ref.at[pl.ds(p, 1)], ybuf.at[k, pl.ds(r, 1)], sem))
    for k in range(2):
        pltpu.make_async_copy(ys_ref.at[pl.ds(0, tc)], ybuf.at[k], sem).wait()
    route = route_ref[...]
    w1, w2 = route[:, 2:3], route[:, 3:4]
    half = D_MODEL // 2
    for h, (y1, y2) in enumerate(zip(_unpack_halves(ybuf[0], F32), _unpack_halves(ybuf[1], F32))):
        cols = slice(h * half, (h + 1) * half)
        gate2 = mod_ref[0, :, 5 * D_MODEL + h * half:5 * D_MODEL + (h + 1) * half]
        o_ref[:, cols] = x1_ref[:, cols] + gate2 * (w1 * y1 + w2 * y2)


def _combine(pos, x1, route, mod3, row_of_tile, ys, tok0, n_tokens, tc=256):
    t = x1.shape[0]
    row = lambda i, *_: (i, 0)
    return pl.pallas_call(
        functools.partial(_combine_kernel, tc=tc, n_tokens=n_tokens, tok0=tok0),
        grid_spec=pltpu.PrefetchScalarGridSpec(
            num_scalar_prefetch=1, grid=(t // tc,),
            in_specs=[pl.BlockSpec((tc, D_MODEL), row), pl.BlockSpec((tc, LANES), row),
                      pl.BlockSpec((1, 1, N_MOD * D_MODEL), lambda i, *_: (row_of_tile(i), 0, 0)),
                      pl.BlockSpec(memory_space=pl.ANY)],
            out_specs=pl.BlockSpec((tc, D_MODEL), row),
            scratch_shapes=[pltpu.VMEM((2, tc, ys.shape[1]), ys.dtype), pltpu.SemaphoreType.DMA]),
        out_shape=jax.ShapeDtypeStruct((t, D_MODEL), F32),
        compiler_params=_cparams("arbitrary"), name="combine",
    )(pos, x1, route, mod3, ys)


def _rope_tables(seq):
    pos = np.arange(seq)
    n_freq = ATT_HEAD_DIM // 4
    inv = ROPE_THETA ** (-np.arange(n_freq, dtype=np.float32) / n_freq)
    ang = np.concatenate([(pos // GRID_W).astype(np.float32)[:, None] * inv,
                          (pos % GRID_W).astype(np.float32)[:, None] * inv], axis=-1).astype(np.float32)
    ang = jnp.asarray(ang)
    cos, sin = jnp.cos(ang), jnp.sin(ang)
    cos_t = jnp.repeat(cos, 2, axis=1)
    sin_t = jnp.stack([-sin, sin], axis=-1).reshape(seq, ATT_HEAD_DIM)
    return cos_t, sin_t


def _pad_lanes(a):
    return jnp.pad(a, ((0, 0), (0, LANES - a.shape[1])))


def _layer(x2d, n_batch, seq, mod3, row_of_tile, lw, *, rope_tabs, cache, state, emit, h2_all, tile0, t_total):
    (n1, n2, w_main, wg_cat, bg, qg, kg, mg, w_out, rw_cat, rb) = lw
    cos_t, sin_t = rope_tabs
    outs = _project(x2d, mod3, row_of_tile(256), n1, w_main, wg_cat, bg, qg, kg, cos_t, sin_t,
                    rope=cache is not None, emit_kv=emit)
    q, k, v, qm, kmt, vm, om, g = outs[:8]
    rowq, colq = _gate_scans(g)
    attn = _attention(q, k, v, n_batch, seq, cache=cache)
    ml = _mlstm(qm, kmt, vm, rowq, colq, n_batch, seq, state=state, emit_state=emit)
    mixed = _mix_out(attn, ml[0], om, x2d, mod3, row_of_tile(MIX_TILE), mg, w_out, n2, rw_cat, rb,
                     h2_all, tile0, t_total, tm=MIX_TILE)
    return mixed, outs[8:], ml[1:]


def kernel(x_prompt, x_sample, cache_k, cache_v, state_C, state_n, state_m, c, c_ctx, mod_w, mod_b, norm1_g, norm2_g,
           w_in, b_gates, q_norm_g, k_norm_g, mlstm_norm_g, w_out, router_group_w, router_group_b, router_expert_w,
           router_expert_b, expert_w_gate, expert_w_up, expert_w_down):
    assert mod_w.shape[0] == 1, "single-layer stack"
    n_ctx, s_ctx, _ = x_prompt.shape
    n_lat, s_lat, _ = x_sample.shape
    t_ctx, t_lat = n_ctx * s_ctx, n_lat * s_lat
    t_all = t_ctx + t_lat
    ctx_row = n_lat

    cond = jnp.concatenate([c, c_ctx[None], jnp.zeros((8 - n_lat - 1, D_MODEL), F32)], axis=0)
    mod3 = _modulation(cond, mod_w[0], mod_b[0][None]).reshape(8, 1, N_MOD * D_MODEL)

    rw =jnp.concatenate([router_group_w[0], jnp.moveaxis(router_expert_w[0], 0, 1).reshape(D_MODEL, N_EXPERTS)], axis=1)
    rb = _pad_lanes(jnp.concatenate([router_group_b[0], router_expert_b[0].reshape(-1)])[None])
    w_in_t = w_in[0].T
    lw = (norm1_g, norm2_g, _cast_bf16_t(w_in_t, MAIN_WIDTH), _gate_cols(w_in_t), _pad_lanes(b_gates),
          q_norm_g, k_norm_g, mlstm_norm_g, _cast_bf16(w_out, D_MODEL), _hi_lo_cat(_pad_lanes(rw)), rb)
    rope_tabs = _rope_tables(s_lat)

    ctx_rows = lambda tm: (lambda i: ctx_row)
    lat_rows = lambda tm: (lambda i: i // (s_lat // tm))
    (x1p, h2_all, routep), (ka, va), (s_c, s_n, s_m) = _layer(
        x_prompt.reshape(t_ctx, D_MODEL), n_ctx, s_ctx, mod3, ctx_rows, lw,
        rope_tabs=rope_tabs, cache=None, state=None, emit=True, h2_all=None, tile0=0, t_total=t_all)

    caug0 = jnp.concatenate([state_C[:, 0], state_n[:, 0][..., None],
                             jnp.zeros(state_n[:, 0].shape + (LANES - 1,), F32)], axis=-1)
    caug0 = caug0.reshape(n_lat, N_CHAINS, MLSTM_HEAD_DIM, AUG)
    past = cache_k.shape[2]
    cache = (cache_k[:, 0].reshape(n_lat * past, KV_WIDTH), cache_v[:, 0].reshape(n_lat * past, KV_WIDTH))
    (x1s, h2_all, routes), _, _ = _layer(
        x_sample.reshape(t_lat, D_MODEL), n_lat, s_lat, mod3, lat_rows, lw,
        rope_tabs=rope_tabs, cache=cache, state=(caug0, state_m[:, 0].reshape(-1)), emit=False,
        h2_all=h2_all, tile0=t_ctx // MIX_TILE, t_total=t_all)

    route = jnp.concatenate([routep, routes], axis=0)
    ranks, counts = _ranks(route)
    pos, tile_plan, tails, n_rows = _routing_plan(route, ranks, counts)
    xs = _dispatch(pos, tails, h2_all, n_rows)
    y_sorted = _experts(*tile_plan, xs, expert_w_gate[0], expert_w_up[0], expert_w_down[0])
    yp = _combine(pos, x1p, routep, mod3, ctx_rows(256), y_sorted, 0, t_all)
    ys = _combine(pos, x1s, routes, mod3, lat_rows(256), y_sorted, t_ctx, t_all)

    kv_shape = (n_ctx, 1, s_ctx, ATT_KV_HEADS, ATT_HEAD_DIM)
    return (yp.reshape(x_prompt.shape), ys.reshape(x_sample.shape), ka.reshape(kv_shape), va.reshape(kv_shape),
            s_c.reshape(n_ctx, 1, 2, MLSTM_HEADS, MLSTM_HEAD_DIM, MLSTM_HEAD_DIM),
            s_n.reshape(n_ctx, 1, 2, MLSTM_HEADS, MLSTM_HEAD_DIM), s_m[..., 0, 0].reshape(n_ctx, 1, 2, MLSTM_HEADS))
```

```python
import functools

import numpy as np
import jax
import jax.numpy as jnp
from jax import lax
from jax.experimental import pallas as pl
from jax.experimental.pallas import tpu as pltpu

F32 = jnp.float32
BF16 = jnp.bfloat16

D_MODEL = 2048
GRID_W = 64
ATT_HEADS = 8
ATT_KV_HEADS = 2
ATT_HEAD_DIM = 128
ATT_GROUP = ATT_HEADS // ATT_KV_HEADS
ATT_WIDTH = ATT_HEADS * ATT_HEAD_DIM
KV_WIDTH = ATT_KV_HEADS * ATT_HEAD_DIM
ROPE_THETA = 10000.0
MLSTM_HEADS = 4
MLSTM_HEAD_DIM = 256
MLSTM_WIDTH = MLSTM_HEADS * MLSTM_HEAD_DIM
CHUNK = 256
N_GATES = 4 * MLSTM_HEADS
N_CHAINS = 2 * MLSTM_HEADS
MAIN_WIDTH = ATT_WIDTH + 2 * KV_WIDTH + 4 * MLSTM_WIDTH
N_GROUPS = 4
EXPERTS_PER_GROUP = 8
N_EXPERTS = N_GROUPS * EXPERTS_PER_GROUP
D_EXPERT = 512
N_MOD = 6
TM_MOE = 256
MIX_TILE = 512
ATT_KEY_CHUNK = 256
Q_SCALE = ATT_HEAD_DIM ** -0.5 * float(np.log2(np.e))
EPS = 1e-6

LANES = 128
AUG = MLSTM_HEAD_DIM + LANES
N_SCAN_ROWS = 5 * N_CHAINS
ROUTER_LANE0 = N_GROUPS
VMEM_LIMIT = 56 * 1024 * 1024


def _cparams(*sem):
    return pltpu.CompilerParams(dimension_semantics=sem, vmem_limit_bytes=VMEM_LIMIT)


def _const_spec(shape):
    nd = len(shape)
    return pl.BlockSpec(shape, lambda *_: (0,) * nd, pipeline_mode=pl.Buffered(1))


def _split_hi_lo(x):
    hi = x.astype(BF16)
    lo = (x - hi.astype(F32)).astype(BF16)
    return hi, lo


def _dot(a, b):
    return jnp.dot(a, b, preferred_element_type=F32)


def _hi_lo_cat(w):
    return jnp.concatenate(_split_hi_lo(w), axis=1)


def _dot_hi_lo(a_hi, a_lo, w_ref):
    r = _dot(a_hi, w_ref[...])
    return r[:, :LANES] + r[:, LANES:] + _dot(a_lo, w_ref[:, :LANES])


def _rms(x, g):
    return x * lax.rsqrt(jnp.mean(x * x, axis=-1, keepdims=True) + EPS) * g


def _mod_kernel(c_ref, w_ref, b_ref, o_ref):
    c = c_ref[...]
    s = c / (1.0 + jnp.exp(-c))
    s_hi = s.astype(BF16).astype(F32)
    lhs = jnp.concatenate([s_hi, s - s_hi], axis=0).astype(BF16)
    w_hi, w_lo = _split_hi_lo(w_ref[...])
    r = _dot(lhs, w_hi)
    r2 = _dot(lhs, w_lo)
    o_ref[...] = r[:8] + r[8:] + r2[:8] + b_ref[...]


def _modulation(cond, mod_w, mod_b):
    n = mod_w.shape[1]
    tn = 1024
    return pl.pallas_call(
        _mod_kernel,
        grid=(n // tn,),
        in_specs=[pl.BlockSpec((8, D_MODEL), lambda j: (0, 0)),
                  pl.BlockSpec((D_MODEL, tn), lambda j: (0, j)),
                  pl.BlockSpec((1, tn), lambda j: (0, j))],
        out_specs=pl.BlockSpec((8, tn), lambda j: (0, j)),
        out_shape=jax.ShapeDtypeStruct((8, n), F32),
        compiler_params=_cparams("parallel"),
        name="modulation",
    )(cond, mod_w, mod_b)


def _cast_kernel(w_ref, o_ref):
    o_ref[...] = w_ref[0].astype(BF16)


def _cast_bf16(w, n_cols, tn=512):
    rows = w.shape[1]
    return pl.pallas_call(
        _cast_kernel, grid=(n_cols // tn,),
        in_specs=[pl.BlockSpec((1, rows, tn), lambda j: (0, 0, j))],
        out_specs=pl.BlockSpec((rows, tn), lambda j: (0, j)),
        out_shape=jax.ShapeDtypeStruct((rows, n_cols), BF16),
        compiler_params=_cparams("parallel"), name="cast_bf16",
    )(w)


def _cast_t_kernel(wt_ref, o_ref):
    o_ref[...] = wt_ref[...].T.astype(BF16)


def _cast_bf16_t(wt, n_cols, tn=512):
    rows = wt.shape[1]
    return pl.pallas_call(
        _cast_t_kernel, grid=(n_cols // tn,),
        in_specs=[pl.BlockSpec((tn, rows), lambda j: (j, 0))],
        out_specs=pl.BlockSpec((rows, tn), lambda j: (0, j)),
        out_shape=jax.ShapeDtypeStruct((rows, n_cols), BF16),
        compiler_params=_cparams("parallel"), name="cast_bf16_t",
    )(wt)


def _gate_cols_kernel(wt_ref, o_ref):
    sub = lax.broadcasted_iota(jnp.int32, wt_ref.shape, 0)
    hi, lo = _split_hi_lo(jnp.where(sub < N_GATES, wt_ref[...], 0.0).T)
    o_ref[:, :LANES] = hi
    o_ref[:, LANES:] = lo


def _gate_cols(wt):
    rows = wt.shape[1]
    return pl.pallas_call(
        _gate_cols_kernel, grid=(1,),
        in_specs=[pl.BlockSpec((LANES, rows), lambda i: (MAIN_WIDTH // LANES, 0))],
        out_specs=pl.BlockSpec((rows, 2 * LANES), lambda i: (0, 0)),
        out_shape=jax.ShapeDtypeStruct((rows, 2 * LANES), BF16),
        compiler_params=_cparams("arbitrary"), name="gate_cols",
    )(wt)


def _pair_swap(x):
    lane = lax.broadcasted_iota(jnp.int32, x.shape, 1)
    return jnp.where((lane & 1) == 0, pltpu.roll(x, LANES - 1, 1), pltpu.roll(x, 1, 1))


def _proj_kernel(x_ref, mod_ref, n1_ref, w_ref, wg_ref, bg_ref, qg_ref, kg_ref, cos_ref, sin_ref,
                 q_ref, k_ref, v_ref, qm_ref, kmt_ref, vm_ref, om_ref, g_ref, *kv_refs, rope, tm):
    mod = mod_ref[0]
    shift, scale = mod[:, :D_MODEL], mod[:, D_MODEL:2 * D_MODEL]
    h = _rms(x_ref[...], n1_ref[...]) * (1.0 + scale) + shift
    h_hi, h_lo = _split_hi_lo(h)

    def rot(seg):
        return seg * cos_ref[...] + _pair_swap(seg) * sin_ref[...] if rope else seg

    qa = _dot(h_hi, w_ref[:, :ATT_WIDTH])
    for hh in range(ATT_HEADS):
        sl = slice(hh * ATT_HEAD_DIM, (hh + 1) * ATT_HEAD_DIM)
        seg = rot(_rms(qa[:, sl], qg_ref[...]))
        q_ref[:, sl] = (seg * Q_SCALE).astype(BF16)

    kv = _dot(h_hi, w_ref[:, ATT_WIDTH:ATT_WIDTH + 2 * KV_WIDTH])
    for hh in range(ATT_KV_HEADS):
        sl = slice(hh * ATT_HEAD_DIM, (hh + 1) * ATT_HEAD_DIM)
        seg = _rms(kv[:, sl], kg_ref[...])
        if kv_refs:
            kv_refs[0][:, sl] = seg
        k_ref[:, sl] = rot(seg).astype(BF16)
    va = kv[:, KV_WIDTH:]
    if kv_refs:
        kv_refs[1][...] = va
    v_ref[...] = va.astype(BF16)

    c0 = ATT_WIDTH + 2 * KV_WIDTH
    qm_ref[...] = (_dot(h_hi, w_ref[:, c0:c0 + MLSTM_WIDTH]) * MLSTM_HEAD_DIM ** -0.5).astype(BF16)
    km = _dot(h_hi, w_ref[:, c0 + MLSTM_WIDTH:c0 + 2 * MLSTM_WIDTH])
    for cc in range(tm // CHUNK):
        kmt_ref[cc] = km[cc * CHUNK:(cc + 1) * CHUNK, :].T.astype(BF16)
    vm_ref[...] = _dot(h_hi, w_ref[:, c0 + 2 * MLSTM_WIDTH:c0 + 3 * MLSTM_WIDTH]).astype(BF16)
    om_ref[...] = _dot(h_hi, w_ref[:, c0 + 3 * MLSTM_WIDTH:c0 + 4 * MLSTM_WIDTH])
    g_ref[...] = _dot_hi_lo(h_hi, h_lo, wg_ref) + bg_ref[...]


def _project(x2d, mod3, row_of_tile, n1, w_main, wg_cat, bg, qg, kg, cos_t, sin_t, *, rope, emit_kv, tm=256):
    t = x2d.shape[0]
    n_pos = cos_t.shape[0] // tm
    row = lambda i: (i, 0)
    in_specs = [pl.BlockSpec((tm, D_MODEL), row),
                pl.BlockSpec((1, 1, N_MOD * D_MODEL), lambda i: (row_of_tile(i), 0, 0)),
                _const_spec((1, D_MODEL)),
                _const_spec((D_MODEL, MAIN_WIDTH)),
                _const_spec((D_MODEL, 2 * LANES)), _const_spec((1, LANES)),
                _const_spec((1, ATT_HEAD_DIM)), _const_spec((1, ATT_HEAD_DIM)),
                pl.BlockSpec((tm, ATT_HEAD_DIM), lambda i: (i % n_pos, 0)),
                pl.BlockSpec((tm, ATT_HEAD_DIM), lambda i: (i % n_pos, 0))]
    out_shape = [jax.ShapeDtypeStruct((t, ATT_WIDTH), BF16), jax.ShapeDtypeStruct((t, KV_WIDTH), BF16),
                 jax.ShapeDtypeStruct((t, KV_WIDTH), BF16), jax.ShapeDtypeStruct((t, MLSTM_WIDTH), BF16),
                 jax.ShapeDtypeStruct((t // CHUNK, MLSTM_WIDTH, CHUNK), BF16),
                 jax.ShapeDtypeStruct((t, MLSTM_WIDTH), BF16), jax.ShapeDtypeStruct((t, MLSTM_WIDTH), F32),
                 jax.ShapeDtypeStruct((t, LANES), F32)]
    out_specs = [pl.BlockSpec((tm, ATT_WIDTH), row), pl.BlockSpec((tm, KV_WIDTH), row),
                 pl.BlockSpec((tm, KV_WIDTH), row), pl.BlockSpec((tm, MLSTM_WIDTH), row),
                 pl.BlockSpec((tm // CHUNK, MLSTM_WIDTH, CHUNK), lambda i: (i, 0, 0)),
                 pl.BlockSpec((tm, MLSTM_WIDTH), row), pl.BlockSpec((tm, MLSTM_WIDTH), row),
                 pl.BlockSpec((tm, LANES), row)]
    if emit_kv:
        out_shape += [jax.ShapeDtypeStruct((t, KV_WIDTH), F32)] * 2
        out_specs += [pl.BlockSpec((tm, KV_WIDTH), row)] * 2
    return pl.pallas_call(
        functools.partial(_proj_kernel, rope=rope, tm=tm),
        grid=(t // tm,), in_specs=in_specs, out_specs=out_specs, out_shape=out_shape,
        compiler_params=_cparams("parallel"), name="in_proj",
    )(x2d, mod3, n1, w_main, wg_cat, bg, qg, kg, cos_t, sin_t)


def _lane_scan(x, op, fill, is_fwd, lane):
    s = 1
    while s < CHUNK:
        from_left = jnp.where(lane >= s, pltpu.roll(x, s, 1), fill)
        from_right = jnp.where(lane < CHUNK - s, pltpu.roll(x, CHUNK - s, 1), fill)
        x = op(x, jnp.where(is_fwd, from_left, from_right))
        s *= 2
    return x


def _scan_kernel(g_ref, row_ref, col_ref, *, n_chunks):
    lane = lax.broadcasted_iota(jnp.int32, (N_CHAINS, CHUNK), 1)
    is_fwd = lax.broadcasted_iota(jnp.int32, (N_CHAINS, CHUNK), 0) < MLSTM_HEADS
    for cc in range(n_chunks):
        gt = g_ref[cc * CHUNK:(cc + 1) * CHUNK, :].T
        fwd, bwd = gt[:N_CHAINS], gt[N_CHAINS:2 * N_CHAINS]
        li = jnp.where(is_fwd, fwd, pltpu.roll(bwd, MLSTM_HEADS, 0))
        f = jnp.where(is_fwd, pltpu.roll(fwd, MLSTM_HEADS, 0), bwd)
        lf = jnp.minimum(f, 0.0) - jnp.log(1.0 + jnp.exp(-jnp.abs(f)))
        b = _lane_scan(lf, jnp.add, 0.0, is_fwd, lane)
        a = li - b
        run_max = _lane_scan(a, jnp.maximum, -jnp.inf, is_fwd, lane)
        all_max = jnp.broadcast_to(jnp.max(a, axis=1, keepdims=True), a.shape)
        last = jnp.where(is_fwd, CHUNK - 1, 0)
        total = jnp.broadcast_to(jnp.sum(jnp.where(lane == last, b, 0.0), axis=1, keepdims=True), a.shape)
        rows = jnp.concatenate([a, b, run_max, all_max, total], axis=0)
        row_ref[cc] = rows
        padded = jnp.concatenate([rows, jnp.zeros((LANES - N_SCAN_ROWS, CHUNK), F32)], axis=0)
        col_ref[cc * CHUNK:(cc + 1) * CHUNK, :] = padded.T


def _gate_scans(g, tb=1024):
    t = g.shape[0]
    tb = min(tb, t)
    return pl.pallas_call(
        functools.partial(_scan_kernel, n_chunks=tb // CHUNK),
        grid=(t // tb,),
        in_specs=[pl.BlockSpec((tb, LANES), lambda i: (i, 0))],
        out_specs=[pl.BlockSpec((tb // CHUNK, N_SCAN_ROWS, CHUNK), lambda i: (i, 0, 0)),
                   pl.BlockSpec((tb, LANES), lambda i: (i, 0))],
        out_shape=[jax.ShapeDtypeStruct((t // CHUNK, N_SCAN_ROWS, CHUNK), F32),
                   jax.ShapeDtypeStruct((t, LANES), F32)],
        compiler_params=_cparams("parallel"), name="gate_scans",
    )(g)


def _mlstm_kernel(*refs, n_chunks, has_state, emit_state):
    it = iter(refs)
    q_ref, kt_ref, v_ref, row_ref, col_ref = [next(it) for _ in range(5)]
    c0_ref, m0_ref = (next(it), next(it)) if has_state else (None, None)
    o_ref = next(it)
    c_out, n_out, m_out = (next(it), next(it), next(it)) if emit_state else (None, None, None)
    caug, hb = next(it), next(it)

    b = pl.program_id(0)
    if has_state:
        caug[...] = c0_ref[0]
        m_init = tuple(jnp.full((1, 1), m0_ref[b * N_CHAINS + r], F32) for r in range(N_CHAINS))
    else:
        caug[...] = jnp.zeros(caug.shape, F32)
        m_init = tuple(jnp.zeros((1, 1), F32) for _ in range(N_CHAINS))

    sub = lax.broadcasted_iota(jnp.int32, (CHUNK, CHUNK), 0)
    lane = lax.broadcasted_iota(jnp.int32, (CHUNK, CHUNK), 1)
    ones_col = (lax.broadcasted_iota(jnp.int32, (CHUNK, LANES), 1) == 0).astype(BF16)

    def chain(r, cc, m_prev):
        d, hd = divmod(r, MLSTM_HEADS)
        t0 = pl.multiple_of(cc * CHUNK, CHUNK)
        hs = slice(hd * MLSTM_HEAD_DIM, (hd + 1) * MLSTM_HEAD_DIM)
        rows = row_ref[cc]
        cols = col_ref[pl.ds(t0, CHUNK), :]
        row = lambda k: rows[k * N_CHAINS + r:k * N_CHAINS + r + 1, :]
        col = lambda k: cols[:, k * N_CHAINS + r:k * N_CHAINS + r + 1]
        q = q_ref[pl.ds(t0, CHUNK), hs]
        kt = kt_ref[cc, hs, :]
        vaug = jnp.concatenate([v_ref[pl.ds(t0, CHUNK), hs], ones_col], axis=1)

        m_col = jnp.maximum(m_prev, col(2))
        keep = (lane <= sub) if d == 0 else (lane >= sub)
        w = jnp.where(keep, jnp.exp(row(0) - m_col), 0.0)
        w_inter = jnp.exp(m_prev - m_col)
        p = (_dot(q, kt) * w).astype(BF16)
        intra = _dot(p, vaug)
        inter = _dot(q, caug[r].astype(BF16))
        num = intra[:, :MLSTM_HEAD_DIM] + w_inter * inter[:, :MLSTM_HEAD_DIM]
        den = intra[:, MLSTM_HEAD_DIM:MLSTM_HEAD_DIM + 1] + w_inter * inter[:, MLSTM_HEAD_DIM:MLSTM_HEAD_DIM + 1]
        h_out = num / jnp.maximum(jnp.abs(den), jnp.exp(-(col(1) + m_col)))

        m_last = jnp.maximum(m_prev, row(3))
        kw = (kt.astype(F32) * jnp.exp(row(0) - m_last)).astype(BF16)
        caug[r] = jnp.exp(m_prev - m_last[:, :1]) * caug[r] + _dot(kw, vaug)
        return h_out, (row(4) + m_last)[:, :1]

    def body(c, ms):
        new = []
        for r in range(N_CHAINS):
            cc = c if r < MLSTM_HEADS else n_chunks - 1 - c
            h_out, m_new = chain(r, cc, ms[r])
            hd = r % MLSTM_HEADS
            dst = o_ref if r < MLSTM_HEADS else hb
            dst[pl.ds(pl.multiple_of(cc * CHUNK, CHUNK), CHUNK), hd * MLSTM_HEAD_DIM:(hd + 1) * MLSTM_HEAD_DIM] = h_out
            new.append(m_new)
        return tuple(new)

    ms = lax.fori_loop(0, n_chunks, body, m_init)
    o_ref[...] += hb[...]
    if emit_state:
        for r in range(N_CHAINS):
            c_out[0, r] = caug[r, :, :MLSTM_HEAD_DIM]
            n_out[0, r] = caug[r, :, MLSTM_HEAD_DIM:].T[:1, :]
            m_out[0, r] = jnp.broadcast_to(ms[r], (1, LANES))


def _mlstm(qm, kmt, vm, rowq, colq, n_batch, seq, state=None, emit_state=False):
    nc = seq // CHUNK
    mode = dict(pipeline_mode=pl.Buffered(1)) if seq * MLSTM_WIDTH * 4 > (2 << 20) else {}
    in_specs = [pl.BlockSpec((seq, MLSTM_WIDTH), lambda b: (b, 0), **mode),
                pl.BlockSpec((nc, MLSTM_WIDTH, CHUNK), lambda b: (b, 0, 0), **mode),
                pl.BlockSpec((seq, MLSTM_WIDTH), lambda b: (b, 0), **mode),
                pl.BlockSpec((nc, N_SCAN_ROWS, CHUNK), lambda b: (b, 0, 0), **mode),
                pl.BlockSpec((seq, LANES), lambda b: (b, 0), **mode)]
    args = [qm, kmt, vm, rowq, colq]
    if state is not None:
        in_specs += [pl.BlockSpec((1, N_CHAINS, MLSTM_HEAD_DIM, AUG), lambda b: (b, 0, 0, 0), **mode),
                     pl.BlockSpec(memory_space=pltpu.SMEM)]
        args += list(state)
    out_shape = [jax.ShapeDtypeStruct((n_batch * seq, MLSTM_WIDTH), F32)]
    out_specs = [pl.BlockSpec((seq, MLSTM_WIDTH), lambda b: (b, 0))]
    if emit_state:
        out_shape += [jax.ShapeDtypeStruct((n_batch, N_CHAINS, MLSTM_HEAD_DIM, MLSTM_HEAD_DIM), F32),
                      jax.ShapeDtypeStruct((n_batch, N_CHAINS, 1, MLSTM_HEAD_DIM), F32),
                      jax.ShapeDtypeStruct((n_batch, N_CHAINS, 1, LANES), F32)]
        out_specs += [pl.BlockSpec((1, N_CHAINS, MLSTM_HEAD_DIM, MLSTM_HEAD_DIM), lambda b: (b, 0, 0, 0)),
                      pl.BlockSpec((1, N_CHAINS, 1, MLSTM_HEAD_DIM), lambda b: (b, 0, 0, 0)),
                      pl.BlockSpec((1, N_CHAINS, 1, LANES), lambda b: (b, 0, 0, 0))]
    return pl.pallas_call(
        functools.partial(_mlstm_kernel, n_chunks=nc, has_state=state is not None, emit_state=emit_state),
        grid=(n_batch,), in_specs=in_specs, out_specs=out_specs, out_shape=out_shape,
        scratch_shapes=[pltpu.VMEM((N_CHAINS, MLSTM_HEAD_DIM, AUG), F32), pltpu.VMEM((seq, MLSTM_WIDTH), F32)],
        compiler_params=_cparams("parallel"), name="mlstm",
    )(*args)


def _attn_kernel(*refs, tq, has_cache):
    if has_cache:
        q_ref, k_ref, v_ref, ck_ref, cv_ref, o_ref = refs
    else:
        q_ref, k_ref, v_ref, o_ref = refs
    q = q_ref[...]
    qs = jnp.concatenate([q[:, g * ATT_HEAD_DIM:(g + 1) * ATT_HEAD_DIM] for g in range(ATT_GROUP)], axis=0)
    nt = (((1,), (1,)), ((), ()))
    seq = k_ref.shape[0]
    kc = min(seq, ATT_KEY_CHUNK)
    chunks = [(k_ref, v_ref, c * kc) for c in range(seq // kc)]
    if has_cache:
        chunks.insert(0, (ck_ref, cv_ref, None))
    m = l = o = None
    for kr, vr, start in chunks:
        kk, vv = (kr[...], vr[...]) if start is None else (kr[start:start + kc, :], vr[start:start + kc, :])
        s = lax.dot_general(qs, kk.astype(BF16), nt, preferred_element_type=F32)
        mc = jnp.max(s, axis=-1, keepdims=True)
        if m is None:
            m = mc
            p = jnp.exp2(s - m)
            l = jnp.sum(p, axis=-1, keepdims=True)
            o = _dot(p.astype(BF16), vv.astype(BF16))
        else:
            m_new = jnp.maximum(m, mc)
            alpha = jnp.exp2(m - m_new)
            p = jnp.exp2(s - m_new)
            l = alpha * l + jnp.sum(p, axis=-1, keepdims=True)
            o = alpha * o + _dot(p.astype(BF16), vv.astype(BF16))
            m = m_new
    o = o / l
    for g in range(ATT_GROUP):
        o_ref[:, g * ATT_HEAD_DIM:(g + 1) * ATT_HEAD_DIM] = o[g * tq:(g + 1) * tq].astype(BF16)


def _attention(q, k, v, n_batch, seq, cache=None, tq=256):
    nqb = seq // tq
    gw = ATT_GROUP * ATT_HEAD_DIM
    in_specs = [pl.BlockSpec((tq, gw), lambda b, h, i: (b * nqb + i, h)),
                pl.BlockSpec((seq, ATT_HEAD_DIM), lambda b, h, i: (b, h)),
                pl.BlockSpec((seq, ATT_HEAD_DIM), lambda b, h, i: (b, h))]
    args = [q, k, v]
    if cache is not None:
        past = cache[0].shape[0] // n_batch
        in_specs += [pl.BlockSpec((past, ATT_HEAD_DIM), lambda b, h, i: (b, h))] * 2
        args += list(cache)
    return pl.pallas_call(
        functools.partial(_attn_kernel, tq=tq, has_cache=cache is not None),
        grid=(n_batch, ATT_KV_HEADS, nqb), in_specs=in_specs,
        out_specs=pl.BlockSpec((tq, gw), lambda b, h, i: (b * nqb + i, h)),
        out_shape=jax.ShapeDtypeStruct((n_batch * seq, ATT_WIDTH), BF16),
        compiler_params=_cparams("parallel", "parallel", "parallel"), name="attention",
    )(*args)


def _mix_kernel(attn_ref, hm_ref, om_ref, x_ref, mod_ref, mg_ref, wo_ref, n2_ref, rw_ref, rb_ref, *rest, sub):
    for r0 in range(0, x_ref.shape[0], sub):
        _mix_rows(slice(r0, r0 + sub), attn_ref, hm_ref, om_ref, x_ref, mod_ref, mg_ref, wo_ref, n2_ref, rw_ref,
                  rb_ref, *rest[-3:])


def _mix_rows(rows, attn_ref, hm_ref, om_ref, x_ref, mod_ref, mg_ref, wo_ref, n2_ref, rw_ref, rb_ref,
              x1_ref, h2_ref, route_ref):
    mod = mod_ref[0]
    gate1 = mod[:, 2 * D_MODEL:3 * D_MODEL]
    shift2, scale2 = mod[:, 3 * D_MODEL:4 * D_MODEL], mod[:, 4 * D_MODEL:5 * D_MODEL]
    hm = hm_ref[rows, :]
    mg = mg_ref[...]
    parts = []
    for hd in range(MLSTM_HEADS):
        sl = slice(hd * MLSTM_HEAD_DIM, (hd + 1) * MLSTM_HEAD_DIM)
        parts.append(_rms(hm[:, sl], mg[:, sl]))
    om = om_ref[rows, :]
    hmg = jnp.concatenate(parts, axis=1) * (1.0 / (1.0 + jnp.exp(-om)))
    y = _dot(attn_ref[rows, :], wo_ref[:ATT_WIDTH, :]) + _dot(hmg.astype(BF16), wo_ref[ATT_WIDTH:, :])
    x1 = x_ref[rows, :] + gate1 * y
    x1_ref[rows, :] = x1
    h2 = _rms(x1, n2_ref[...]) * (1.0 + scale2) + shift2
    h2_hi, h2_lo = _split_hi_lo(h2)
    h2_ref[rows, :] = _pack_halves(h2)

    lg = _dot_hi_lo(h2_hi, h2_lo, rw_ref) + rb_ref[...]
    lane = lax.broadcasted_iota(jnp.int32, lg.shape, 1).astype(F32)
    neg = -jnp.inf
    first = lambda hit: jnp.min(jnp.where(hit, lane, float(LANES)), axis=-1, keepdims=True)
    gl = jnp.where(lane < N_GROUPS, lg, neg)
    gmax = jnp.max(gl, axis=-1, keepdims=True)
    grp = first(gl == gmax)
    p_grp = 1.0 / jnp.sum(jnp.exp(gl - gmax), axis=-1, keepdims=True)
    lo = ROUTER_LANE0 + grp * EXPERTS_PER_GROUP
    el = jnp.where((lane >= lo) & (lane < lo + EXPERTS_PER_GROUP), lg, neg)
    m1 = jnp.max(el, axis=-1, keepdims=True)
    i1 = first(el == m1)
    el2 = jnp.where(lane == i1, neg, el)
    m2 = jnp.max(el2, axis=-1, keepdims=True)
    i2 = first(el2 == m2)
    r = jnp.exp(m2 - m1)
    w1 = p_grp / (1.0 + r)
    w2 = w1 * r
    route_ref[rows, :] = jnp.where(lane == 0.0, i1 - ROUTER_LANE0, jnp.where(lane == 1.0, i2 - ROUTER_LANE0,
                                   jnp.where(lane == 2.0, w1, jnp.where(lane == 3.0, w2, 0.0))))


def _mix_out(attn, hm, om, x2d, mod3, row_of_tile, mg, w_out, n2, rw_cat, rb, h2_all, tile0, t_total, tm=512):
    t = x2d.shape[0]
    row = lambda i: (i, 0)
    in_specs = [pl.BlockSpec((tm, ATT_WIDTH), row), pl.BlockSpec((tm, MLSTM_WIDTH), row),
                pl.BlockSpec((tm, MLSTM_WIDTH), row), pl.BlockSpec((tm, D_MODEL), row),
                pl.BlockSpec((1, 1, N_MOD * D_MODEL), lambda i: (row_of_tile(i), 0, 0)),
                _const_spec((1, MLSTM_WIDTH)), _const_spec((D_MODEL, D_MODEL)), _const_spec((1, D_MODEL)),
                _const_spec((D_MODEL, 2 * LANES)), _const_spec((1, LANES))]
    args = [attn, hm, om, x2d, mod3, mg, w_out, n2, rw_cat, rb]
    aliases = {}
    if h2_all is not None:
        aliases = {len(args): 1}
        in_specs.append(pl.BlockSpec(memory_space=pl.ANY))
        args.append(h2_all)
    return pl.pallas_call(
        functools.partial(_mix_kernel, sub=256), grid=(t // tm,), in_specs=in_specs,
        out_specs=[pl.BlockSpec((tm, D_MODEL), row), pl.BlockSpec((tm, D_MODEL // 2), lambda i: (tile0 + i, 0)),
                   pl.BlockSpec((tm, LANES), row)],
        out_shape=[jax.ShapeDtypeStruct((t, D_MODEL), F32), jax.ShapeDtypeStruct((t_total, D_MODEL // 2), jnp.uint32),
                   jax.ShapeDtypeStruct((t, LANES), F32)],
        input_output_aliases=aliases,
        compiler_params=_cparams("parallel"), name="mix_out",
    )(*args)


def _rank_kernel(route_ref, rank_ref, cnt_ref, run_ref, tri_ref):
    tr = route_ref.shape[0]

    @pl.when(pl.program_id(0) == 0)
    def _():
        run_ref[...] = jnp.zeros(run_ref.shape, F32)
        tri_ref[...] = (lax.broadcasted_iota(jnp.int32, (tr, tr), 1)
                        < lax.broadcasted_iota(jnp.int32, (tr, tr), 0)).astype(BF16)

    route = route_ref[...]
    lane = lax.broadcasted_iota(jnp.int32, route.shape, 1).astype(F32)
    hit1, hit2 = lane == route[:, 0:1], lane == route[:, 1:2]
    onehot = jnp.where(hit1, 1.0, jnp.where(hit2, 1.0, 0.0))
    before = _dot(tri_ref[...], onehot.astype(BF16)) + run_ref[0:1, :]
    r1 = jnp.sum(jnp.where(hit1, before, 0.0), axis=-1, keepdims=True)
    r2 = jnp.sum(jnp.where(hit2, before, 0.0), axis=-1, keepdims=True)
    rank_ref[...] = jnp.where(lane == 0.0, r1, jnp.where(lane == 1.0, r2, 0.0))
    run_ref[...] = run_ref[...] + jnp.sum(onehot, axis=0, keepdims=True)
    cnt_ref[...] = run_ref[...]


def _ranks(route, tr=512):
    t = route.shape[0]
    return pl.pallas_call(
        _rank_kernel, grid=(t // tr,),
        in_specs=[pl.BlockSpec((tr, LANES), lambda i: (i, 0))],
        out_specs=[pl.BlockSpec((tr, LANES), lambda i: (i, 0)), pl.BlockSpec((8, LANES), lambda i: (0, 0))],
        out_shape=[jax.ShapeDtypeStruct((t, LANES), F32), jax.ShapeDtypeStruct((8, LANES), F32)],
        scratch_shapes=[pltpu.VMEM((8, LANES), F32), pltpu.VMEM((tr, tr), BF16)],
        compiler_params=_cparams("arbitrary"), name="expert_ranks",
    )(route)


def _pos_kernel(route_ref, rank_ref, start_ref, o_ref):
    route, rank = route_ref[...], rank_ref[...]
    lane = lax.broadcasted_iota(jnp.int32, route.shape, 1).astype(F32)
    start = start_ref[...]
    first = lambda col: jnp.sum(jnp.where(lane == route[:, col:col + 1], start, 0.0), axis=-1, keepdims=True)
    p1 = first(0) + rank[:, 0:1]
    p2 = first(1) + rank[:, 1:2]
    tile = jnp.where(lane == 0.0, p1, jnp.where(lane == 1.0, p2, 0.0))
    o_ref[...] = tile.T[:8, :].astype(jnp.int32)


def _positions(route, ranks, starts, tr=512):
    t = route.shape[0]
    pos = pl.pallas_call(
        _pos_kernel, grid=(t // tr,),
        in_specs=[pl.BlockSpec((tr, LANES), lambda i: (i, 0)), pl.BlockSpec((tr, LANES), lambda i: (i, 0)),
                  pl.BlockSpec((1, LANES), lambda i: (0, 0))],
        out_specs=pl.BlockSpec((8, tr), lambda i: (0, i)),
        out_shape=jax.ShapeDtypeStruct((8, t), jnp.int32),
        compiler_params=_cparams("parallel"), name="pair_rows",
    )(route, ranks, starts)
    return pos[:2].reshape(-1)


def _routing_plan(route, ranks, counts):
    t = route.shape[0]
    n_tiles = 2 * t // TM_MOE + N_EXPERTS
    cnt = counts[0, :N_EXPERTS].astype(jnp.int32)
    padded = (cnt + TM_MOE - 1) // TM_MOE * TM_MOE
    ends = jnp.cumsum(padded)
    pos = _positions(route, ranks, _pad_lanes((ends - padded).astype(F32)[None]))
    n_used = (ends[-1:] // TM_MOE).astype(jnp.int32)
    tile_start = jnp.arange(n_tiles, dtype=jnp.int32) * TM_MOE
    tile_expert = jnp.sum((ends[None, :] <= tile_start[:, None]).astype(jnp.int32), axis=1)
    tile_expert = jnp.minimum(tile_expert, tile_expert[n_used[0] - 1]).astype(jnp.int32)
    tails = jnp.where(padded > 0, ends - TM_MOE, -1).astype(jnp.int32)
    change = jnp.concatenate([jnp.ones((1,), jnp.int32), (tile_expert[1:] != tile_expert[:-1]).astype(jnp.int32)])
    weight_slot = (jnp.cumsum(change) - 1) % 2
    ids = jnp.arange(N_EXPERTS, dtype=jnp.int32)
    later_used = (padded > 0)[None, :] & (ids[None, :] > ids[:, None])
    next_used = jnp.min(jnp.where(later_used, ids[None, :], N_EXPERTS), axis=1)
    next_used = jnp.where(next_used < N_EXPERTS, next_used, -1)
    next_expert = jnp.sum(jnp.where(tile_expert[:, None] == ids[None, :], next_used[None, :], 0), axis=1)
    return pos, (tile_expert, n_used, weight_slot.astype(jnp.int32), next_expert.astype(jnp.int32)), tails, \
        n_tiles * TM_MOE


def _row_copies(pos_ref, n_pairs, tok0, n_rows, make_copy):
    def body(r, carry):
        for k in range(2):
            make_copy(k, r, pos_ref[k * n_pairs + tok0 + r]).start()
        return carry
    lax.fori_loop(0, n_rows, body, 0, unroll=8)


def _dispatch_kernel(pos_ref, tail_ref, h_ref, xs_ref, hbuf, zero_ref, tsem, rsem, *, td, n_tokens):
    i = pl.program_id(0)
    last = pl.num_programs(0) - 1
    tile_copy = lambda t: pltpu.make_async_copy(
        h_ref.at[pl.ds(pl.multiple_of(t * td, td), td)], hbuf.at[t % 3], tsem.at[t % 3])

    def wait_rows(t):
        for _ in range(2):
            pltpu.make_async_copy(hbuf.at[0], xs_ref.at[pl.ds(0, td)], rsem.at[t % 2]).wait()

    @pl.when(i == 0)
    def _():
        tile_copy(0).start()
        zero_ref[...] = jnp.zeros(zero_ref.shape, zero_ref.dtype)
        tail_copy = lambda e: pltpu.make_async_copy(
            zero_ref, xs_ref.at[pl.ds(pl.multiple_of(tail_ref[e], TM_MOE), TM_MOE)], rsem.at[1])
        for e in range(N_EXPERTS):
            pl.when(tail_ref[e] >= 0)(lambda e=e: tail_copy(e).start())
        for e in range(N_EXPERTS):
            pl.when(tail_ref[e] >= 0)(lambda e=e: tail_copy(e).wait())

    @pl.when(i < last)
    def _():
        tile_copy(i + 1).start()

    tile_copy(i).wait()
    src = hbuf.at[i % 3]
    _row_copies(pos_ref, n_tokens, i * td, td, lambda k, r, p: pltpu.make_async_copy(
        src.at[pl.ds(r, 1)], xs_ref.at[pl.ds(p, 1)], rsem.at[i % 2]))

    @pl.when(i >= 1)
    def _():
        wait_rows(i - 1)

    @pl.when(i == last)
    def _():
        wait_rows(i)


def _dispatch(pos, tails, h2, n_rows, td=256):
    t, width = h2.shape
    return pl.pallas_call(
        functools.partial(_dispatch_kernel, td=td, n_tokens=t),
        grid_spec=pltpu.PrefetchScalarGridSpec(
            num_scalar_prefetch=2, grid=(t // td,),
            in_specs=[pl.BlockSpec(memory_space=pl.ANY)],
            out_specs=pl.BlockSpec(memory_space=pl.ANY),
            scratch_shapes=[pltpu.VMEM((3, td, width), h2.dtype), pltpu.VMEM((TM_MOE, width), h2.dtype),
                            pltpu.SemaphoreType.DMA((3,)), pltpu.SemaphoreType.DMA((2,))]),
        out_shape=jax.ShapeDtypeStruct((n_rows, width), h2.dtype),
        compiler_params=_cparams("arbitrary"), name="dispatch",
    )(pos, tails, h2)


def _pack_halves(x):
    half = x.shape[1] // 2
    return pltpu.pack_elementwise([x[:, :half], x[:, half:]], packed_dtype=BF16)


def _unpack_halves(p, dtype):
    return tuple(pltpu.unpack_elementwise(p, index=i, packed_dtype=BF16, unpacked_dtype=F32).astype(dtype)
                 for i in range(2))


def _expert_kernel(te_ref, nu_ref, slot_ref, next_ref, xs_ref, wg_ref, wu_ref, wd_ref, ys_ref,
                   wgf, wuf, wdf, wgb, wub, wdb, wsem):
    j = pl.program_id(0)

    def weight_copies(e, s):
        return [pltpu.make_async_copy(w_ref.at[e], buf.at[s], wsem.at[s])
                for w_ref, buf in ((wg_ref, wgf), (wu_ref, wuf), (wd_ref, wdf))]

    @pl.when(j < nu_ref[0])
    def _():
        e, s = te_ref[j], slot_ref[j]

        @pl.when(j == 0)
        def _():
            for cp in weight_copies(e, s):
                cp.start()

        @pl.when((j == 0) | (e != te_ref[jnp.maximum(j - 1, 0)]))
        def _():
            for cp in weight_copies(e, s):
                cp.wait()
            nxt = next_ref[j]

            @pl.when(nxt >= 0)
            def _():
                for cp in weight_copies(nxt, 1 - s):
                    cp.start()

            wgb[...] = wgf[s].astype(BF16)
            wub[...] = wuf[s].astype(BF16)
            wdb[...] = wdf[s].astype(BF16)

        xa, xb = _unpack_halves(xs_ref[...], BF16)
        half = D_MODEL // 2
        g = _dot(xa, wgb[:half, :]) + _dot(xb, wgb[half:, :])
        u = _dot(xa, wub[:half, :]) + _dot(xb, wub[half:, :])
        a = (g / (1.0 + jnp.exp(-g))) * u
        ys_ref[...] = _pack_halves(_dot(a.astype(BF16), wdb[...]))


def _experts(tile_expert, n_used, weight_slot, next_expert, xs, wg, wu, wd):
    n_tiles = xs.shape[0] // TM_MOE
    tile = lambda j, te, nu, *_: (jnp.minimum(j, nu[0] - 1), 0)
    hbm = pl.BlockSpec(memory_space=pl.ANY)
    return pl.pallas_call(
        _expert_kernel,
        grid_spec=pltpu.PrefetchScalarGridSpec(
            num_scalar_prefetch=4, grid=(n_tiles,),
            in_specs=[pl.BlockSpec((TM_MOE, D_MODEL // 2), tile), hbm, hbm, hbm],
            out_specs=pl.BlockSpec((TM_MOE, D_MODEL // 2), tile),
            scratch_shapes=[pltpu.VMEM((2, D_MODEL, D_EXPERT), F32), pltpu.VMEM((2, D_MODEL, D_EXPERT), F32),
                            pltpu.VMEM((2, D_EXPERT, D_MODEL), F32),
                            pltpu.VMEM((D_MODEL, D_EXPERT), BF16), pltpu.VMEM((D_MODEL, D_EXPERT), BF16),
                            pltpu.VMEM((D_EXPERT, D_MODEL), BF16), pltpu.SemaphoreType.DMA((2,))]),
        out_shape=jax.ShapeDtypeStruct(xs.shape, jnp.uint32),
        compiler_params=_cparams("arbitrary"), name="experts",
    )(tile_expert, n_used, weight_slot, next_expert, xs, wg, wu, wd)


def _combine_kernel(pos_ref, x1_ref, route_ref, mod_ref, ys_ref, o_ref, ybuf, sem, *, tc, n_tokens, tok0):
    i = pl.program_id(0)
    slot = i % 2

    def gather(tile, s):
        _row_copies(pos_ref, n_tokens, tok0 + tile * tc, tc, lambda k, r, p: pltpu.make_async_copy(
            ys_ref.at[pl.ds(p, 1)], ybuf.at[s, k, pl.ds(r, 1)], sem.at[s]))

    @pl.when(i == 0)
    def _():
        gather(0, 0)

    @pl.when(i + 1 < pl.num_programs(0))
    def _():
        gather(i + 1, 1 - slot)

    for k in range(2):
        pltpu.make_async_copy(ys_ref.at[pl.ds(0, tc)], ybuf.at[slot, k], sem.at[slot]).wait()
    route = route_ref[...]
    w1, w2 = route[:, 2:3], route[:, 3:4]
    half = D_MODEL // 2
    for h, (y1, y2) in enumerate(zip(_unpack_halves(ybuf[slot, 0], F32), _unpack_halves(ybuf[slot, 1], F32))):
        cols = slice(h * half, (h + 1) * half)
        gate2 = mod_ref[0, :, 5 * D_MODEL + h * half:5 * D_MODEL + (h + 1) * half]
        o_ref[:, cols] = x1_ref[:, cols] + gate2 * (w1 * y1 + w2 * y2)


def _combine(pos, x1, route, mod3, row_of_tile, ys, tok0, n_tokens, tc=256):
    t = x1.shape[0]
    row = lambda i, *_: (i, 0)
    return pl.pallas_call(
        functools.partial(_combine_kernel, tc=tc, n_tokens=n_tokens, tok0=tok0),
        grid_spec=pltpu.PrefetchScalarGridSpec(
            num_scalar_prefetch=1, grid=(t // tc,),
            in_specs=[pl.BlockSpec((tc, D_MODEL), row), pl.BlockSpec((tc, LANES), row),
                      pl.BlockSpec((1, 1, N_MOD * D_MODEL), lambda i, *_: (row_of_tile(i), 0, 0)),
                      pl.BlockSpec(memory_space=pl.ANY)],
            out_specs=pl.BlockSpec((tc, D_MODEL), row),
            scratch_shapes=[pltpu.VMEM((2, 2, tc, ys.shape[1]), ys.dtype), pltpu.SemaphoreType.DMA((2,))]),
        out_shape=jax.ShapeDtypeStruct((t, D_MODEL), F32),
        compiler_params=_cparams("arbitrary"), name="combine",
    )(pos, x1, route, mod3, ys)


def _rope_tables(seq):
    pos = np.arange(seq)
    n_freq = ATT_HEAD_DIM // 4
    inv = ROPE_THETA ** (-np.arange(n_freq, dtype=np.float32) / n_freq)
    ang = np.concatenate([(pos // GRID_W).astype(np.float32)[:, None] * inv,
                          (pos % GRID_W).astype(np.float32)[:, None] * inv], axis=-1).astype(np.float32)
    ang = jnp.asarray(ang)
    cos, sin = jnp.cos(ang), jnp.sin(ang)
    cos_t = jnp.repeat(cos, 2, axis=1)
    sin_t = jnp.stack([-sin, sin], axis=-1).reshape(seq, ATT_HEAD_DIM)
    return cos_t, sin_t


def _pad_lanes(a):
    return jnp.pad(a, ((0, 0), (0, LANES - a.shape[1])))


def _layer(x2d, n_batch, seq, mod3, row_of_tile, lw, *, rope_tabs, cache, state, emit, h2_all, tile0, t_total):
    (n1, n2, w_main, wg_cat, bg, qg, kg, mg, w_out, rw_cat, rb) = lw
    cos_t, sin_t = rope_tabs
    outs = _project(x2d, mod3, row_of_tile(256), n1, w_main, wg_cat, bg, qg, kg, cos_t, sin_t,
                    rope=cache is not None, emit_kv=emit)
    q, k, v, qm, kmt, vm, om, g = outs[:8]
    rowq, colq = _gate_scans(g)
    attn = _attention(q, k, v, n_batch, seq, cache=cache)
    ml = _mlstm(qm, kmt, vm, rowq, colq, n_batch, seq, state=state, emit_state=emit)
    mixed = _mix_out(attn, ml[0], om, x2d, mod3, row_of_tile(MIX_TILE), mg, w_out, n2, rw_cat, rb,
                     h2_all, tile0, t_total, tm=MIX_TILE)
    return mixed, outs[8:], ml[1:]


def kernel(x_prompt, x_sample, cache_k, cache_v, state_C, state_n, state_m, c, c_ctx, mod_w, mod_b, norm1_g, norm2_g,
           w_in, b_gates, q_norm_g, k_norm_g, mlstm_norm_g, w_out, router_group_w, router_group_b, router_expert_w,
           router_expert_b, expert_w_gate, expert_w_up, expert_w_down):
    assert mod_w.shape[0] == 1, "single-layer stack"
    n_ctx, s_ctx, _ = x_prompt.shape
    n_lat, s_lat, _ = x_sample.shape
    t_ctx, t_lat = n_ctx * s_ctx, n_lat * s_lat
    t_all = t_ctx + t_lat
    ctx_row = n_lat

    cond = jnp.concatenate([c, c_ctx[None], jnp.zeros((8 - n_lat - 1, D_MODEL), F32)], axis=0)
    mod3 = _modulation(cond, mod_w[0], mod_b[0][None]).reshape(8, 1, N_MOD * D_MODEL)

    rw =jnp.concatenate([router_group_w[0], jnp.moveaxis(router_expert_w[0], 0, 1).reshape(D_MODEL, N_EXPERTS)], axis=1)
    rb = _pad_lanes(jnp.concatenate([router_group_b[0], router_expert_b[0].reshape(-1)])[None])
    w_in_t = w_in[0].T
    lw = (norm1_g, norm2_g, _cast_bf16_t(w_in_t, MAIN_WIDTH), _gate_cols(w_in_t), _pad_lanes(b_gates),
          q_norm_g, k_norm_g, mlstm_norm_g, _cast_bf16(w_out, D_MODEL), _hi_lo_cat(_pad_lanes(rw)), rb)
    rope_tabs = _rope_tables(s_lat)

    ctx_rows = lambda tm: (lambda i: ctx_row)
    lat_rows = lambda tm: (lambda i: i // (s_lat // tm))
    (x1p, h2_all, routep), (ka, va), (s_c, s_n, s_m) = _layer(
        x_prompt.reshape(t_ctx, D_MODEL), n_ctx, s_ctx, mod3, ctx_rows, lw,
        rope_tabs=rope_tabs, cache=None, state=None, emit=True, h2_all=None, tile0=0, t_total=t_all)

    caug0 = jnp.concatenate([state_C[:, 0], state_n[:, 0][..., None],
                             jnp.zeros(state_n[:, 0].shape + (LANES - 1,), F32)], axis=-1)
    caug0 = caug0.reshape(n_lat, N_CHAINS, MLSTM_HEAD_DIM, AUG)
    past = cache_k.shape[2]
    cache = (cache_k[:, 0].reshape(n_lat * past, KV_WIDTH), cache_v[:, 0].reshape(n_lat * past, KV_WIDTH))
    (x1s, h2_all, routes), _, _ = _layer(
        x_sample.reshape(t_lat, D_MODEL), n_lat, s_lat, mod3, lat_rows, lw,
        rope_tabs=rope_tabs, cache=cache, state=(caug0, state_m[:, 0].reshape(-1)), emit=False,
        h2_all=h2_all, tile0=t_ctx // MIX_TILE, t_total=t_all)

    route = jnp.concatenate([routep, routes], axis=0)
    ranks, counts = _ranks(route)
    pos, tile_plan, tails, n_rows = _routing_plan(route, ranks, counts)
    xs = _dispatch(pos, tails, h2_all, n_rows)
    y_sorted = _experts(*tile_plan, xs, expert_w_gate[0], expert_w_up[0], expert_w_down[0])
    yp = _combine(pos, x1p, routep, mod3, ctx_rows(256), y_sorted, 0, t_all)
    ys = _combine(pos, x1s, routes, mod3, lat_rows(256), y_sorted, t_ctx, t_all)

    kv_shape = (n_ctx, 1, s_ctx, ATT_KV_HEADS, ATT_HEAD_DIM)
    return (yp.reshape(x_prompt.shape), ys.reshape(x_sample.shape), ka.reshape(kv_shape), va.reshape(kv_shape),
            s_c.reshape(n_ctx, 1, 2, MLSTM_HEADS, MLSTM_HEAD_DIM, MLSTM_HEAD_DIM),
            s_n.reshape(n_ctx, 1, 2, MLSTM_HEADS, MLSTM_HEAD_DIM), s_m[..., 0, 0].reshape(n_ctx, 1, 2, MLSTM_HEADS))
```

```python
import functools

import numpy as np
import jax
import jax.numpy as jnp
from jax import lax
from jax.experimental import pallas as pl
from jax.experimental.pallas import tpu as pltpu

F32 = jnp.float32
BF16 = jnp.bfloat16

D_MODEL = 2048
GRID_W = 64
ATT_HEADS = 8
ATT_KV_HEADS = 2
ATT_HEAD_DIM = 128
ATT_GROUP = ATT_HEADS // ATT_KV_HEADS
ATT_WIDTH = ATT_HEADS * ATT_HEAD_DIM
KV_WIDTH = ATT_KV_HEADS * ATT_HEAD_DIM
ROPE_THETA = 10000.0
MLSTM_HEADS = 4
MLSTM_HEAD_DIM = 256
MLSTM_WIDTH = MLSTM_HEADS * MLSTM_HEAD_DIM
CHUNK = 256
N_GATES = 4 * MLSTM_HEADS
N_CHAINS = 2 * MLSTM_HEADS
MAIN_WIDTH = ATT_WIDTH + 2 * KV_WIDTH + 4 * MLSTM_WIDTH
N_GROUPS = 4
EXPERTS_PER_GROUP = 8
N_EXPERTS = N_GROUPS * EXPERTS_PER_GROUP
D_EXPERT = 512
N_MOD = 6
TM_MOE = 256
MIX_TILE = 512
ATT_KEY_CHUNK = 256
Q_SCALE = ATT_HEAD_DIM ** -0.5 * float(np.log2(np.e))
EPS = 1e-6

LANES = 128
AUG = MLSTM_HEAD_DIM + LANES
N_SCAN_ROWS = 5 * N_CHAINS
ROUTER_LANE0 = N_GROUPS
VMEM_LIMIT = 56 * 1024 * 1024


def _cparams(*sem):
    return pltpu.CompilerParams(dimension_semantics=sem, vmem_limit_bytes=VMEM_LIMIT)


def _const_spec(shape):
    nd = len(shape)
    return pl.BlockSpec(shape, lambda *_: (0,) * nd, pipeline_mode=pl.Buffered(1))


def _split_hi_lo(x):
    hi = x.astype(BF16)
    lo = (x - hi.astype(F32)).astype(BF16)
    return hi, lo


def _dot(a, b):
    return jnp.dot(a, b, preferred_element_type=F32)


def _hi_lo_cat(w):
    return jnp.concatenate(_split_hi_lo(w), axis=1)


def _dot_hi_lo(a_hi, a_lo, w_ref):
    r = _dot(a_hi, w_ref[...])
    return r[:, :LANES] + r[:, LANES:] + _dot(a_lo, w_ref[:, :LANES])


def _rms(x, g):
    return x * lax.rsqrt(jnp.mean(x * x, axis=-1, keepdims=True) + EPS) * g


def _mod_kernel(c_ref, w_ref, b_ref, o_ref):
    c = c_ref[...]
    s = c / (1.0 + jnp.exp(-c))
    s_hi = s.astype(BF16).astype(F32)
    lhs = jnp.concatenate([s_hi, s - s_hi], axis=0).astype(BF16)
    w_hi, w_lo = _split_hi_lo(w_ref[...])
    r = _dot(lhs, w_hi)
    r2 = _dot(lhs, w_lo)
    o_ref[...] = r[:8] + r[8:] + r2[:8] + b_ref[...]


def _modulation(cond, mod_w, mod_b):
    n = mod_w.shape[1]
    tn = 1024
    return pl.pallas_call(
        _mod_kernel,
        grid=(n // tn,),
        in_specs=[pl.BlockSpec((8, D_MODEL), lambda j: (0, 0)),
                  pl.BlockSpec((D_MODEL, tn), lambda j: (0, j)),
                  pl.BlockSpec((1, tn), lambda j: (0, j))],
        out_specs=pl.BlockSpec((8, tn), lambda j: (0, j)),
        out_shape=jax.ShapeDtypeStruct((8, n), F32),
        compiler_params=_cparams("parallel"),
        name="modulation",
    )(cond, mod_w, mod_b)


def _cast_kernel(w_ref, o_ref):
    o_ref[...] = w_ref[0].astype(BF16)


def _cast_bf16(w, n_cols, tn=512):
    rows = w.shape[1]
    return pl.pallas_call(
        _cast_kernel, grid=(n_cols // tn,),
        in_specs=[pl.BlockSpec((1, rows, tn), lambda j: (0, 0, j))],
        out_specs=pl.BlockSpec((rows, tn), lambda j: (0, j)),
        out_shape=jax.ShapeDtypeStruct((rows, n_cols), BF16),
        compiler_params=_cparams("parallel"), name="cast_bf16",
    )(w)


def _cast_t_kernel(wt_ref, o_ref):
    o_ref[...] = wt_ref[...].T.astype(BF16)


def _cast_bf16_t(wt, n_cols, tn=512):
    rows = wt.shape[1]
    return pl.pallas_call(
        _cast_t_kernel, grid=(n_cols // tn,),
        in_specs=[pl.BlockSpec((tn, rows), lambda j: (j, 0))],
        out_specs=pl.BlockSpec((rows, tn), lambda j: (0, j)),
        out_shape=jax.ShapeDtypeStruct((rows, n_cols), BF16),
        compiler_params=_cparams("parallel"), name="cast_bf16_t",
    )(wt)


def _gate_cols_kernel(wt_ref, o_ref):
    sub = lax.broadcasted_iota(jnp.int32, wt_ref.shape, 0)
    hi, lo = _split_hi_lo(jnp.where(sub < N_GATES, wt_ref[...], 0.0).T)
    o_ref[:, :LANES] = hi
    o_ref[:, LANES:] = lo


def _gate_cols(wt):
    rows = wt.shape[1]
    return pl.pallas_call(
        _gate_cols_kernel, grid=(1,),
        in_specs=[pl.BlockSpec((LANES, rows), lambda i: (MAIN_WIDTH // LANES, 0))],
        out_specs=pl.BlockSpec((rows, 2 * LANES), lambda i: (0, 0)),
        out_shape=jax.ShapeDtypeStruct((rows, 2 * LANES), BF16),
        compiler_params=_cparams("arbitrary"), name="gate_cols",
    )(wt)


def _pair_swap(x):
    lane = lax.broadcasted_iota(jnp.int32, x.shape, 1)
    return jnp.where((lane & 1) == 0, pltpu.roll(x, LANES - 1, 1), pltpu.roll(x, 1, 1))


def _proj_kernel(x_ref, mod_ref, n1_ref, w_ref, wg_ref, bg_ref, qg_ref, kg_ref, cos_ref, sin_ref,
                 q_ref, k_ref, v_ref, qm_ref, kmt_ref, vm_ref, om_ref, g_ref, *kv_refs, rope, tm):
    mod = mod_ref[0]
    shift, scale = mod[:, :D_MODEL], mod[:, D_MODEL:2 * D_MODEL]
    h = _rms(x_ref[...], n1_ref[...]) * (1.0 + scale) + shift
    h_hi, h_lo = _split_hi_lo(h)

    def rot(seg):
        return seg * cos_ref[...] + _pair_swap(seg) * sin_ref[...] if rope else seg

    qa = _dot(h_hi, w_ref[:, :ATT_WIDTH])
    for hh in range(ATT_HEADS):
        sl = slice(hh * ATT_HEAD_DIM, (hh + 1) * ATT_HEAD_DIM)
        seg = rot(_rms(qa[:, sl], qg_ref[...]))
        q_ref[:, sl] = (seg * Q_SCALE).astype(BF16)

    kv = _dot(h_hi, w_ref[:, ATT_WIDTH:ATT_WIDTH + 2 * KV_WIDTH])
    for hh in range(ATT_KV_HEADS):
        sl = slice(hh * ATT_HEAD_DIM, (hh + 1) * ATT_HEAD_DIM)
        seg = _rms(kv[:, sl], kg_ref[...])
        if kv_refs:
            kv_refs[0][:, sl] = seg
        k_ref[:, sl] = rot(seg).astype(BF16)
    va = kv[:, KV_WIDTH:]
    if kv_refs:
        kv_refs[1][...] = va
    v_ref[...] = va.astype(BF16)

    c0 = ATT_WIDTH + 2 * KV_WIDTH
    qm_ref[...] = (_dot(h_hi, w_ref[:, c0:c0 + MLSTM_WIDTH]) * MLSTM_HEAD_DIM ** -0.5).astype(BF16)
    km = _dot(h_hi, w_ref[:, c0 + MLSTM_WIDTH:c0 + 2 * MLSTM_WIDTH])
    for cc in range(tm // CHUNK):
        kmt_ref[cc] = km[cc * CHUNK:(cc + 1) * CHUNK, :].T.astype(BF16)
    vm_ref[...] = _dot(h_hi, w_ref[:, c0 + 2 * MLSTM_WIDTH:c0 + 3 * MLSTM_WIDTH]).astype(BF16)
    om_ref[...] = _dot(h_hi, w_ref[:, c0 + 3 * MLSTM_WIDTH:c0 + 4 * MLSTM_WIDTH])
    g_ref[...] = _dot_hi_lo(h_hi, h_lo, wg_ref) + bg_ref[...]


def _project(x2d, mod3, row_of_tile, n1, w_main, wg_cat, bg, qg, kg, cos_t, sin_t, *, rope, emit_kv, tm=256):
    t = x2d.shape[0]
    n_pos = cos_t.shape[0] // tm
    row = lambda i: (i, 0)
    in_specs = [pl.BlockSpec((tm, D_MODEL), row),
                pl.BlockSpec((1, 1, N_MOD * D_MODEL), lambda i: (row_of_tile(i), 0, 0)),
                _const_spec((1, D_MODEL)),
                _const_spec((D_MODEL, MAIN_WIDTH)),
                _const_spec((D_MODEL, 2 * LANES)), _const_spec((1, LANES)),
                _const_spec((1, ATT_HEAD_DIM)), _const_spec((1, ATT_HEAD_DIM)),
                pl.BlockSpec((tm, ATT_HEAD_DIM), lambda i: (i % n_pos, 0)),
                pl.BlockSpec((tm, ATT_HEAD_DIM), lambda i: (i % n_pos, 0))]
    out_shape = [jax.ShapeDtypeStruct((t, ATT_WIDTH), BF16), jax.ShapeDtypeStruct((t, KV_WIDTH), BF16),
                 jax.ShapeDtypeStruct((t, KV_WIDTH), BF16), jax.ShapeDtypeStruct((t, MLSTM_WIDTH), BF16),
                 jax.ShapeDtypeStruct((t // CHUNK, MLSTM_WIDTH, CHUNK), BF16),
                 jax.ShapeDtypeStruct((t, MLSTM_WIDTH), BF16), jax.ShapeDtypeStruct((t, MLSTM_WIDTH), F32),
                 jax.ShapeDtypeStruct((t, LANES), F32)]
    out_specs = [pl.BlockSpec((tm, ATT_WIDTH), row), pl.BlockSpec((tm, KV_WIDTH), row),
                 pl.BlockSpec((tm, KV_WIDTH), row), pl.BlockSpec((tm, MLSTM_WIDTH), row),
                 pl.BlockSpec((tm // CHUNK, MLSTM_WIDTH, CHUNK), lambda i: (i, 0, 0)),
                 pl.BlockSpec((tm, MLSTM_WIDTH), row), pl.BlockSpec((tm, MLSTM_WIDTH), row),
                 pl.BlockSpec((tm, LANES), row)]
    if emit_kv:
        out_shape += [jax.ShapeDtypeStruct((t, KV_WIDTH), F32)] * 2
        out_specs += [pl.BlockSpec((tm, KV_WIDTH), row)] * 2
    return pl.pallas_call(
        functools.partial(_proj_kernel, rope=rope, tm=tm),
        grid=(t // tm,), in_specs=in_specs, out_specs=out_specs, out_shape=out_shape,
        compiler_params=_cparams("parallel"), name="in_proj",
    )(x2d, mod3, n1, w_main, wg_cat, bg, qg, kg, cos_t, sin_t)


def _lane_scan(x, op, fill, is_fwd, lane):
    s = 1
    while s < CHUNK:
        from_left = jnp.where(lane >= s, pltpu.roll(x, s, 1), fill)
        from_right = jnp.where(lane < CHUNK - s, pltpu.roll(x, CHUNK - s, 1), fill)
        x = op(x, jnp.where(is_fwd, from_left, from_right))
        s *= 2
    return x


def _scan_kernel(g_ref, row_ref, col_ref, *, n_chunks):
    lane = lax.broadcasted_iota(jnp.int32, (N_CHAINS, CHUNK), 1)
    is_fwd = lax.broadcasted_iota(jnp.int32, (N_CHAINS, CHUNK), 0) < MLSTM_HEADS
    for cc in range(n_chunks):
        gt = g_ref[cc * CHUNK:(cc + 1) * CHUNK, :].T
        fwd, bwd = gt[:N_CHAINS], gt[N_CHAINS:2 * N_CHAINS]
        li = jnp.where(is_fwd, fwd, pltpu.roll(bwd, MLSTM_HEADS, 0))
        f = jnp.where(is_fwd, pltpu.roll(fwd, MLSTM_HEADS, 0), bwd)
        lf = jnp.minimum(f, 0.0) - jnp.log(1.0 + jnp.exp(-jnp.abs(f)))
        b = _lane_scan(lf, jnp.add, 0.0, is_fwd, lane)
        a = li - b
        run_max = _lane_scan(a, jnp.maximum, -jnp.inf, is_fwd, lane)
        all_max = jnp.broadcast_to(jnp.max(a, axis=1, keepdims=True), a.shape)
        last = jnp.where(is_fwd, CHUNK - 1, 0)
        total = jnp.broadcast_to(jnp.sum(jnp.where(lane == last, b, 0.0), axis=1, keepdims=True), a.shape)
        rows = jnp.concatenate([a, b, run_max, all_max, total], axis=0)
        row_ref[cc] = rows
        padded = jnp.concatenate([rows, jnp.zeros((LANES - N_SCAN_ROWS, CHUNK), F32)], axis=0)
        col_ref[cc * CHUNK:(cc + 1) * CHUNK, :] = padded.T


def _gate_scans(g, tb=1024):
    t = g.shape[0]
    tb = min(tb, t)
    return pl.pallas_call(
        functools.partial(_scan_kernel, n_chunks=tb // CHUNK),
        grid=(t // tb,),
        in_specs=[pl.BlockSpec((tb, LANES), lambda i: (i, 0))],
        out_specs=[pl.BlockSpec((tb // CHUNK, N_SCAN_ROWS, CHUNK), lambda i: (i, 0, 0)),
                   pl.BlockSpec((tb, LANES), lambda i: (i, 0))],
        out_shape=[jax.ShapeDtypeStruct((t // CHUNK, N_SCAN_ROWS, CHUNK), F32),
                   jax.ShapeDtypeStruct((t, LANES), F32)],
        compiler_params=_cparams("parallel"), name="gate_scans",
    )(g)


def _mlstm_kernel(*refs, n_chunks, has_state, emit_state):
    it = iter(refs)
    q_ref, kt_ref, v_ref, row_ref, col_ref = [next(it) for _ in range(5)]
    c0_ref, m0_ref = (next(it), next(it)) if has_state else (None, None)
    o_ref = next(it)
    c_out, n_out, m_out = (next(it), next(it), next(it)) if emit_state else (None, None, None)
    caug, hb = next(it), next(it)

    b = pl.program_id(0)
    if has_state:
        caug[...] = c0_ref[0]
        m_init = tuple(jnp.full((1, 1), m0_ref[b * N_CHAINS + r], F32) for r in range(N_CHAINS))
    else:
        caug[...] = jnp.zeros(caug.shape, F32)
        m_init = tuple(jnp.zeros((1, 1), F32) for _ in range(N_CHAINS))

    sub = lax.broadcasted_iota(jnp.int32, (CHUNK, CHUNK), 0)
    lane = lax.broadcasted_iota(jnp.int32, (CHUNK, CHUNK), 1)
    ones_col = (lax.broadcasted_iota(jnp.int32, (CHUNK, LANES), 1) == 0).astype(BF16)

    def chain(r, cc, m_prev):
        d, hd = divmod(r, MLSTM_HEADS)
        t0 = pl.multiple_of(cc * CHUNK, CHUNK)
        hs = slice(hd * MLSTM_HEAD_DIM, (hd + 1) * MLSTM_HEAD_DIM)
        rows = row_ref[cc]
        cols = col_ref[pl.ds(t0, CHUNK), :]
        row = lambda k: rows[k * N_CHAINS + r:k * N_CHAINS + r + 1, :]
        col = lambda k: cols[:, k * N_CHAINS + r:k * N_CHAINS + r + 1]
        q = q_ref[pl.ds(t0, CHUNK), hs]
        kt = kt_ref[cc, hs, :]
        vaug = jnp.concatenate([v_ref[pl.ds(t0, CHUNK), hs], ones_col], axis=1)

        m_col = jnp.maximum(m_prev, col(2))
        keep = (lane <= sub) if d == 0 else (lane >= sub)
        w = jnp.where(keep, jnp.exp(row(0) - m_col), 0.0)
        w_inter = jnp.exp(m_prev - m_col)
        p = (_dot(q, kt) * w).astype(BF16)
        intra = _dot(p, vaug)
        inter = _dot(q, caug[r].astype(BF16))
        num = intra[:, :MLSTM_HEAD_DIM] + w_inter * inter[:, :MLSTM_HEAD_DIM]
        den = intra[:, MLSTM_HEAD_DIM:MLSTM_HEAD_DIM + 1] + w_inter * inter[:, MLSTM_HEAD_DIM:MLSTM_HEAD_DIM + 1]
        h_out = num / jnp.maximum(jnp.abs(den), jnp.exp(-(col(1) + m_col)))

        m_last = jnp.maximum(m_prev, row(3))
        kw = (kt.astype(F32) * jnp.exp(row(0) - m_last)).astype(BF16)
        caug[r] = jnp.exp(m_prev - m_last[:, :1]) * caug[r] + _dot(kw, vaug)
        return h_out, (row(4) + m_last)[:, :1]

    def body(c, ms):
        new = []
        for r in range(N_CHAINS):
            cc = c if r < MLSTM_HEADS else n_chunks - 1 - c
            h_out, m_new = chain(r, cc, ms[r])
            hd = r % MLSTM_HEADS
            dst = o_ref if r < MLSTM_HEADS else hb
            dst[pl.ds(pl.multiple_of(cc * CHUNK, CHUNK), CHUNK), hd * MLSTM_HEAD_DIM:(hd + 1) * MLSTM_HEAD_DIM] = h_out
            new.append(m_new)
        return tuple(new)

    ms = lax.fori_loop(0, n_chunks, body, m_init)
    o_ref[...] += hb[...]
    if emit_state:
        for r in range(N_CHAINS):
            c_out[0, r] = caug[r, :, :MLSTM_HEAD_DIM]
            n_out[0, r] = caug[r, :, MLSTM_HEAD_DIM:].T[:1, :]
            m_out[0, r] = jnp.broadcast_to(ms[r], (1, LANES))


def _mlstm(qm, kmt, vm, rowq, colq, n_batch, seq, state=None, emit_state=False):
    nc = seq // CHUNK
    mode = dict(pipeline_mode=pl.Buffered(1)) if seq * MLSTM_WIDTH * 4 > (2 << 20) else {}
    in_specs = [pl.BlockSpec((seq, MLSTM_WIDTH), lambda b: (b, 0), **mode),
                pl.BlockSpec((nc, MLSTM_WIDTH, CHUNK), lambda b: (b, 0, 0), **mode),
                pl.BlockSpec((seq, MLSTM_WIDTH), lambda b: (b, 0), **mode),
                pl.BlockSpec((nc, N_SCAN_ROWS, CHUNK), lambda b: (b, 0, 0), **mode),
                pl.BlockSpec((seq, LANES), lambda b: (b, 0), **mode)]
    args = [qm, kmt, vm, rowq, colq]
    if state is not None:
        in_specs += [pl.BlockSpec((1, N_CHAINS, MLSTM_HEAD_DIM, AUG), lambda b: (b, 0, 0, 0), **mode),
                     pl.BlockSpec(memory_space=pltpu.SMEM)]
        args += list(state)
    out_shape = [jax.ShapeDtypeStruct((n_batch * seq, MLSTM_WIDTH), F32)]
    out_specs = [pl.BlockSpec((seq, MLSTM_WIDTH), lambda b: (b, 0))]
    if emit_state:
        out_shape += [jax.ShapeDtypeStruct((n_batch, N_CHAINS, MLSTM_HEAD_DIM, MLSTM_HEAD_DIM), F32),
                      jax.ShapeDtypeStruct((n_batch, N_CHAINS, 1, MLSTM_HEAD_DIM), F32),
                      jax.ShapeDtypeStruct((n_batch, N_CHAINS, 1, LANES), F32)]
        out_specs += [pl.BlockSpec((1, N_CHAINS, MLSTM_HEAD_DIM, MLSTM_HEAD_DIM), lambda b: (b, 0, 0, 0)),
                      pl.BlockSpec((1, N_CHAINS, 1, MLSTM_HEAD_DIM), lambda b: (b, 0, 0, 0)),
                      pl.BlockSpec((1, N_CHAINS, 1, LANES), lambda b: (b, 0, 0, 0))]
    return pl.pallas_call(
        functools.partial(_mlstm_kernel, n_chunks=nc, has_state=state is not None, emit_state=emit_state),
        grid=(n_batch,), in_specs=in_specs, out_specs=out_specs, out_shape=out_shape,
        scratch_shapes=[pltpu.VMEM((N_CHAINS, MLSTM_HEAD_DIM, AUG), F32), pltpu.VMEM((seq, MLSTM_WIDTH), F32)],
        compiler_params=_cparams("parallel"), name="mlstm",
    )(*args)


def _attn_kernel(*refs, tq, has_cache):
    if has_cache:
        q_ref, k_ref, v_ref, ck_ref, cv_ref, o_ref = refs
    else:
        q_ref, k_ref, v_ref, o_ref = refs
    q = q_ref[...]
    qs = jnp.concatenate([q[:, g * ATT_HEAD_DIM:(g + 1) * ATT_HEAD_DIM] for g in range(ATT_GROUP)], axis=0)
    nt = (((1,), (1,)), ((), ()))
    seq = k_ref.shape[0]
    kc = min(seq, ATT_KEY_CHUNK)
    chunks = [(k_ref, v_ref, c * kc) for c in range(seq // kc)]
    if has_cache:
        chunks.insert(0, (ck_ref, cv_ref, None))
    m = l = o = None
    for kr, vr, start in chunks:
        kk, vv = (kr[...], vr[...]) if start is None else (kr[start:start + kc, :], vr[start:start + kc, :])
        s = lax.dot_general(qs, kk.astype(BF16), nt, preferred_element_type=F32)
        mc = jnp.max(s, axis=-1, keepdims=True)
        if m is None:
            m = mc
            p = jnp.exp2(s - m)
            l = jnp.sum(p, axis=-1, keepdims=True)
            o = _dot(p.astype(BF16), vv.astype(BF16))
        else:
            m_new = jnp.maximum(m, mc)
            alpha = jnp.exp2(m - m_new)
            p = jnp.exp2(s - m_new)
            l = alpha * l + jnp.sum(p, axis=-1, keepdims=True)
            o = alpha * o + _dot(p.astype(BF16), vv.astype(BF16))
            m = m_new
    o = o / l
    for g in range(ATT_GROUP):
        o_ref[:, g * ATT_HEAD_DIM:(g + 1) * ATT_HEAD_DIM] = o[g * tq:(g + 1) * tq].astype(BF16)


def _attention(q, k, v, n_batch, seq, cache=None, tq=256):
    nqb = seq // tq
    gw = ATT_GROUP * ATT_HEAD_DIM
    in_specs = [pl.BlockSpec((tq, gw), lambda b, h, i: (b * nqb + i, h)),
                pl.BlockSpec((seq, ATT_HEAD_DIM), lambda b, h, i: (b, h)),
                pl.BlockSpec((seq, ATT_HEAD_DIM), lambda b, h, i: (b, h))]
    args = [q, k, v]
    if cache is not None:
        past = cache[0].shape[0] // n_batch
        in_specs += [pl.BlockSpec((past, ATT_HEAD_DIM), lambda b, h, i: (b, h))] * 2
        args += list(cache)
    return pl.pallas_call(
        functools.partial(_attn_kernel, tq=tq, has_cache=cache is not None),
        grid=(n_batch, ATT_KV_HEADS, nqb), in_specs=in_specs,
        out_specs=pl.BlockSpec((tq, gw), lambda b, h, i: (b * nqb + i, h)),
        out_shape=jax.ShapeDtypeStruct((n_batch * seq, ATT_WIDTH), BF16),
        compiler_params=_cparams("parallel", "parallel", "parallel"), name="attention",
    )(*args)


def _mix_kernel(attn_ref, hm_ref, om_ref, x_ref, mod_ref, mg_ref, wo_ref, n2_ref, rw_ref, rb_ref, *rest, sub):
    for r0 in range(0, x_ref.shape[0], sub):
        _mix_rows(slice(r0, r0 + sub), attn_ref, hm_ref, om_ref, x_ref, mod_ref, mg_ref, wo_ref, n2_ref, rw_ref,
                  rb_ref, *rest[-3:])


def _mix_rows(rows, attn_ref, hm_ref, om_ref, x_ref, mod_ref, mg_ref, wo_ref, n2_ref, rw_ref, rb_ref,
              x1_ref, h2_ref, route_ref):
    mod = mod_ref[0]
    gate1 = mod[:, 2 * D_MODEL:3 * D_MODEL]
    shift2, scale2 = mod[:, 3 * D_MODEL:4 * D_MODEL], mod[:, 4 * D_MODEL:5 * D_MODEL]
    hm = hm_ref[rows, :]
    mg = mg_ref[...]
    parts = []
    for hd in range(MLSTM_HEADS):
        sl = slice(hd * MLSTM_HEAD_DIM, (hd + 1) * MLSTM_HEAD_DIM)
        parts.append(_rms(hm[:, sl], mg[:, sl]))
    om = om_ref[rows, :]
    hmg = jnp.concatenate(parts, axis=1) * (1.0 / (1.0 + jnp.exp(-om)))
    y = _dot(attn_ref[rows, :], wo_ref[:ATT_WIDTH, :]) + _dot(hmg.astype(BF16), wo_ref[ATT_WIDTH:, :])
    x1 = x_ref[rows, :] + gate1 * y
    x1_ref[rows, :] = x1
    h2 = _rms(x1, n2_ref[...]) * (1.0 + scale2) + shift2
    h2_hi, h2_lo = _split_hi_lo(h2)
    h2_ref[rows, :] = _pack_halves(h2)

    lg = _dot_hi_lo(h2_hi, h2_lo, rw_ref) + rb_ref[...]
    lane = lax.broadcasted_iota(jnp.int32, lg.shape, 1).astype(F32)
    neg = -jnp.inf
    first = lambda hit: jnp.min(jnp.where(hit, lane, float(LANES)), axis=-1, keepdims=True)
    gl = jnp.where(lane < N_GROUPS, lg, neg)
    gmax = jnp.max(gl, axis=-1, keepdims=True)
    grp = first(gl == gmax)
    p_grp = 1.0 / jnp.sum(jnp.exp(gl - gmax), axis=-1, keepdims=True)
    lo = ROUTER_LANE0 + grp * EXPERTS_PER_GROUP
    el = jnp.where((lane >= lo) & (lane < lo + EXPERTS_PER_GROUP), lg, neg)
    m1 = jnp.max(el, axis=-1, keepdims=True)
    i1 = first(el == m1)
    el2 = jnp.where(lane == i1, neg, el)
    m2 = jnp.max(el2, axis=-1, keepdims=True)
    i2 = first(el2 == m2)
    r = jnp.exp(m2 - m1)
    w1 = p_grp / (1.0 + r)
    w2 = w1 * r
    route_ref[rows, :] = jnp.where(lane == 0.0, i1 - ROUTER_LANE0, jnp.where(lane == 1.0, i2 - ROUTER_LANE0,
                                   jnp.where(lane == 2.0, w1, jnp.where(lane == 3.0, w2, 0.0))))


def _mix_out(attn, hm, om, x2d, mod3, row_of_tile, mg, w_out, n2, rw_cat, rb, h2_all, tile0, t_total, tm=512):
    t = x2d.shape[0]
    row = lambda i: (i, 0)
    in_specs = [pl.BlockSpec((tm, ATT_WIDTH), row), pl.BlockSpec((tm, MLSTM_WIDTH), row),
                pl.BlockSpec((tm, MLSTM_WIDTH), row), pl.BlockSpec((tm, D_MODEL), row),
                pl.BlockSpec((1, 1, N_MOD * D_MODEL), lambda i: (row_of_tile(i), 0, 0)),
                _const_spec((1, MLSTM_WIDTH)), _const_spec((D_MODEL, D_MODEL)), _const_spec((1, D_MODEL)),
                _const_spec((D_MODEL, 2 * LANES)), _const_spec((1, LANES))]
    args = [attn, hm, om, x2d, mod3, mg, w_out, n2, rw_cat, rb]
    aliases = {}
    if h2_all is not None:
        aliases = {len(args): 1}
        in_specs.append(pl.BlockSpec(memory_space=pl.ANY))
        args.append(h2_all)
    return pl.pallas_call(
        functools.partial(_mix_kernel, sub=256), grid=(t // tm,), in_specs=in_specs,
        out_specs=[pl.BlockSpec((tm, D_MODEL), row), pl.BlockSpec((tm, D_MODEL // 2), lambda i: (tile0 + i, 0)),
                   pl.BlockSpec((tm, LANES), row)],
        out_shape=[jax.ShapeDtypeStruct((t, D_MODEL), F32), jax.ShapeDtypeStruct((t_total, D_MODEL // 2), jnp.uint32),
                   jax.ShapeDtypeStruct((t, LANES), F32)],
        input_output_aliases=aliases,
        compiler_params=_cparams("parallel"), name="mix_out",
    )(*args)


def _rank_kernel(route_ref, rank_ref, cnt_ref, run_ref, tri_ref):
    tr = route_ref.shape[0]

    @pl.when(pl.program_id(0) == 0)
    def _():
        run_ref[...] = jnp.zeros(run_ref.shape, F32)
        tri_ref[...] = (lax.broadcasted_iota(jnp.int32, (tr, tr), 1)
                        < lax.broadcasted_iota(jnp.int32, (tr, tr), 0)).astype(BF16)

    route = route_ref[...]
    lane = lax.broadcasted_iota(jnp.int32, route.shape, 1).astype(F32)
    hit1, hit2 = lane == route[:, 0:1], lane == route[:, 1:2]
    onehot = jnp.where(hit1, 1.0, jnp.where(hit2, 1.0, 0.0))
    before = _dot(tri_ref[...], onehot.astype(BF16)) + run_ref[0:1, :]
    r1 = jnp.sum(jnp.where(hit1, before, 0.0), axis=-1, keepdims=True)
    r2 = jnp.sum(jnp.where(hit2, before, 0.0), axis=-1, keepdims=True)
    rank_ref[...] = jnp.where(lane == 0.0, r1, jnp.where(lane == 1.0, r2, 0.0))
    run_ref[...] = run_ref[...] + jnp.sum(onehot, axis=0, keepdims=True)
    cnt_ref[...] = run_ref[...]


def _ranks(route, tr=512):
    t = route.shape[0]
    return pl.pallas_call(
        _rank_kernel, grid=(t // tr,),
        in_specs=[pl.BlockSpec((tr, LANES), lambda i: (i, 0))],
        out_specs=[pl.BlockSpec((tr, LANES), lambda i: (i, 0)), pl.BlockSpec((8, LANES), lambda i: (0, 0))],
        out_shape=[jax.ShapeDtypeStruct((t, LANES), F32), jax.ShapeDtypeStruct((8, LANES), F32)],
        scratch_shapes=[pltpu.VMEM((8, LANES), F32), pltpu.VMEM((tr, tr), BF16)],
        compiler_params=_cparams("arbitrary"), name="expert_ranks",
    )(route)


def _pos_kernel(route_ref, rank_ref, start_ref, o_ref):
    route, rank = route_ref[...], rank_ref[...]
    lane = lax.broadcasted_iota(jnp.int32, route.shape, 1).astype(F32)
    start = start_ref[...]
    first = lambda col: jnp.sum(jnp.where(lane == route[:, col:col + 1], start, 0.0), axis=-1, keepdims=True)
    p1 = first(0) + rank[:, 0:1]
    p2 = first(1) + rank[:, 1:2]
    tile = jnp.where(lane == 0.0, p1, jnp.where(lane == 1.0, p2, 0.0))
    o_ref[...] = tile.T[:8, :].astype(jnp.int32)


def _positions(route, ranks, starts, tr=512):
    t = route.shape[0]
    pos = pl.pallas_call(
        _pos_kernel, grid=(t // tr,),
        in_specs=[pl.BlockSpec((tr, LANES), lambda i: (i, 0)), pl.BlockSpec((tr, LANES), lambda i: (i, 0)),
                  pl.BlockSpec((1, LANES), lambda i: (0, 0))],
        out_specs=pl.BlockSpec((8, tr), lambda i: (0, i)),
        out_shape=jax.ShapeDtypeStruct((8, t), jnp.int32),
        compiler_params=_cparams("parallel"), name="pair_rows",
    )(route, ranks, starts)
    return pos[:2].reshape(-1)


def _routing_plan(route, ranks, counts):
    t = route.shape[0]
    n_tiles = 2 * t // TM_MOE + N_EXPERTS
    cnt = counts[0, :N_EXPERTS].astype(jnp.int32)
    padded = (cnt + TM_MOE - 1) // TM_MOE * TM_MOE
    ends = jnp.cumsum(padded)
    pos = _positions(route, ranks, _pad_lanes((ends - padded).astype(F32)[None]))
    n_used = (ends[-1:] // TM_MOE).astype(jnp.int32)
    tile_start = jnp.arange(n_tiles, dtype=jnp.int32) * TM_MOE
    tile_expert = jnp.sum((ends[None, :] <= tile_start[:, None]).astype(jnp.int32), axis=1)
    tile_expert = jnp.minimum(tile_expert, tile_expert[n_used[0] - 1]).astype(jnp.int32)
    tails = jnp.where(padded > 0, ends - TM_MOE, -1).astype(jnp.int32)
    change = jnp.concatenate([jnp.ones((1,), jnp.int32), (tile_expert[1:] != tile_expert[:-1]).astype(jnp.int32)])
    weight_slot = (jnp.cumsum(change) - 1) % 2
    ids = jnp.arange(N_EXPERTS, dtype=jnp.int32)
    later_used = (padded > 0)[None, :] & (ids[None, :] > ids[:, None])
    next_used = jnp.min(jnp.where(later_used, ids[None, :], N_EXPERTS), axis=1)
    next_used = jnp.where(next_used < N_EXPERTS, next_used, -1)
    next_expert = jnp.sum(jnp.where(tile_expert[:, None] == ids[None, :], next_used[None, :], 0), axis=1)
    return pos, (tile_expert, n_used, weight_slot.astype(jnp.int32), next_expert.astype(jnp.int32)), tails, \
        n_tiles * TM_MOE


def _row_copies(pos_ref, n_pairs, tok0, n_rows, make_copy, r0=0):
    def body(r, carry):
        for k in range(2):
            make_copy(k, r, pos_ref[k * n_pairs + tok0 + r]).start()
        return carry
    lax.fori_loop(r0, r0 + n_rows, body, 0, unroll=8)


def _dispatch_kernel(pos_ref, tail_ref, h_ref, xs_ref, hbuf, zero_ref, tsem, rsem, *, td, n_tokens):
    i = pl.program_id(0)
    last = pl.num_programs(0) - 1
    tile_copy = lambda t: pltpu.make_async_copy(
        h_ref.at[pl.ds(pl.multiple_of(t * td, td), td)], hbuf.at[t % 3], tsem.at[t % 3])

    def wait_rows(t):
        for _ in range(2):
            pltpu.make_async_copy(hbuf.at[0], xs_ref.at[pl.ds(0, td)], rsem.at[t % 2]).wait()

    @pl.when(i == 0)
    def _():
        tile_copy(0).start()
        zero_ref[...] = jnp.zeros(zero_ref.shape, zero_ref.dtype)
        tail_copy = lambda e: pltpu.make_async_copy(
            zero_ref, xs_ref.at[pl.ds(pl.multiple_of(tail_ref[e], TM_MOE), TM_MOE)], rsem.at[1])
        for e in range(N_EXPERTS):
            pl.when(tail_ref[e] >= 0)(lambda e=e: tail_copy(e).start())
        for e in range(N_EXPERTS):
            pl.when(tail_ref[e] >= 0)(lambda e=e: tail_copy(e).wait())

    @pl.when(i < last)
    def _():
        tile_copy(i + 1).start()

    tile_copy(i).wait()
    src = hbuf.at[i % 3]
    _row_copies(pos_ref, n_tokens, i * td, td, lambda k, r, p: pltpu.make_async_copy(
        src.at[pl.ds(r, 1)], xs_ref.at[pl.ds(p, 1)], rsem.at[i % 2]))

    @pl.when(i >= 1)
    def _():
        wait_rows(i - 1)

    @pl.when(i == last)
    def _():
        wait_rows(i)


def _dispatch(pos, tails, h2, n_rows, td=256):
    t, width = h2.shape
    return pl.pallas_call(
        functools.partial(_dispatch_kernel, td=td, n_tokens=t),
        grid_spec=pltpu.PrefetchScalarGridSpec(
            num_scalar_prefetch=2, grid=(t // td,),
            in_specs=[pl.BlockSpec(memory_space=pl.ANY)],
            out_specs=pl.BlockSpec(memory_space=pl.ANY),
            scratch_shapes=[pltpu.VMEM((3, td, width), h2.dtype), pltpu.VMEM((TM_MOE, width), h2.dtype),
                            pltpu.SemaphoreType.DMA((3,)), pltpu.SemaphoreType.DMA((2,))]),
        out_shape=jax.ShapeDtypeStruct((n_rows, width), h2.dtype),
        compiler_params=_cparams("arbitrary"), name="dispatch",
    )(pos, tails, h2)


def _pack_halves(x):
    half = x.shape[1] // 2
    return pltpu.pack_elementwise([x[:, :half], x[:, half:]], packed_dtype=BF16)


def _unpack_halves(p, dtype):
    return tuple(pltpu.unpack_elementwise(p, index=i, packed_dtype=BF16, unpacked_dtype=F32).astype(dtype)
                 for i in range(2))


def _expert_kernel(te_ref, nu_ref, slot_ref, next_ref, xs_ref, wg_ref, wu_ref, wd_ref, ys_ref,
                   wgf, wuf, wdf, wgb, wub, wdb, wsem):
    j = pl.program_id(0)

    def weight_copies(e, s):
        return [pltpu.make_async_copy(w_ref.at[e], buf.at[s], wsem.at[s])
                for w_ref, buf in ((wg_ref, wgf), (wu_ref, wuf), (wd_ref, wdf))]

    @pl.when(j < nu_ref[0])
    def _():
        e, s = te_ref[j], slot_ref[j]

        @pl.when(j == 0)
        def _():
            for cp in weight_copies(e, s):
                cp.start()

        @pl.when((j == 0) | (e != te_ref[jnp.maximum(j - 1, 0)]))
        def _():
            for cp in weight_copies(e, s):
                cp.wait()
            nxt = next_ref[j]

            @pl.when(nxt >= 0)
            def _():
                for cp in weight_copies(nxt, 1 - s):
                    cp.start()

            wgb[...] = wgf[s].astype(BF16)
            wub[...] = wuf[s].astype(BF16)
            wdb[...] = wdf[s].astype(BF16)

        xa, xb = _unpack_halves(xs_ref[...], BF16)
        half = D_MODEL // 2
        g = _dot(xa, wgb[:half, :]) + _dot(xb, wgb[half:, :])
        u = _dot(xa, wub[:half, :]) + _dot(xb, wub[half:, :])
        a = (g / (1.0 + jnp.exp(-g))) * u
        ys_ref[...] = _pack_halves(_dot(a.astype(BF16), wdb[...]))


def _experts(tile_expert, n_used, weight_slot, next_expert, xs, wg, wu, wd):
    n_tiles = xs.shape[0] // TM_MOE
    tile = lambda j, te, nu, *_: (jnp.minimum(j, nu[0] - 1), 0)
    hbm = pl.BlockSpec(memory_space=pl.ANY)
    return pl.pallas_call(
        _expert_kernel,
        grid_spec=pltpu.PrefetchScalarGridSpec(
            num_scalar_prefetch=4, grid=(n_tiles,),
            in_specs=[pl.BlockSpec((TM_MOE, D_MODEL // 2), tile), hbm, hbm, hbm],
            out_specs=pl.BlockSpec((TM_MOE, D_MODEL // 2), tile),
            scratch_shapes=[pltpu.VMEM((2, D_MODEL, D_EXPERT), F32), pltpu.VMEM((2, D_MODEL, D_EXPERT), F32),
                            pltpu.VMEM((2, D_EXPERT, D_MODEL), F32),
                            pltpu.VMEM((D_MODEL, D_EXPERT), BF16), pltpu.VMEM((D_MODEL, D_EXPERT), BF16),
                            pltpu.VMEM((D_EXPERT, D_MODEL), BF16), pltpu.SemaphoreType.DMA((2,))]),
        out_shape=jax.ShapeDtypeStruct(xs.shape, jnp.uint32),
        compiler_params=_cparams("arbitrary"), name="experts",
    )(tile_expert, n_used, weight_slot, next_expert, xs, wg, wu, wd)


def _combine_kernel(pos_ref, x1_ref, route_ref, mod_ref, ys_ref, o_ref, ybuf, sem, *, tc, n_tokens, tok0):
    i = pl.program_id(0)
    slot = i % 2

    def gather(tile, s, r0, n_rows):
        _row_copies(pos_ref, n_tokens, tok0 + tile * tc, n_rows, lambda k, r, p: pltpu.make_async_copy(
            ys_ref.at[pl.ds(p, 1)], ybuf.at[s, k, pl.ds(r, 1)], sem.at[s]), r0=r0)

    @pl.when(i == 0)
    def _():
        gather(0, 0, 0, tc)

    for k in range(2):
        pltpu.make_async_copy(ys_ref.at[pl.ds(0, tc)], ybuf.at[slot, k], sem.at[slot]).wait()
    route = route_ref[...]
    w1 = jnp.broadcast_to(route[:, 2:3], (tc, LANES))
    w2 = jnp.broadcast_to(route[:, 3:4], (tc, LANES))
    half = D_MODEL // 2
    n_parts = half // LANES
    for part in range(n_parts):
        @pl.when(i + 1 < pl.num_programs(0))
        def _():
            gather(i + 1, 1 - slot, part * (tc // n_parts), tc // n_parts)

        lanes = slice(part * LANES, (part + 1) * LANES)
        for h, (y1, y2) in enumerate(zip(_unpack_halves(ybuf[slot, 0, :, lanes], F32),
                                         _unpack_halves(ybuf[slot, 1, :, lanes], F32))):
            cols = slice(h * half + part * LANES, h * half + (part + 1) * LANES)
            gate2 = mod_ref[0, :, 5 * D_MODEL + cols.start:5 * D_MODEL + cols.stop]
            o_ref[:, cols] = x1_ref[:, cols] + gate2 * (w1 * y1 + w2 * y2)


def _combine(pos, x1, route, mod3, row_of_tile, ys, tok0, n_tokens, tc=256):
    t = x1.shape[0]
    row = lambda i, *_: (i, 0)
    return pl.pallas_call(
        functools.partial(_combine_kernel, tc=tc, n_tokens=n_tokens, tok0=tok0),
        grid_spec=pltpu.PrefetchScalarGridSpec(
            num_scalar_prefetch=1, grid=(t // tc,),
            in_specs=[pl.BlockSpec((tc, D_MODEL), row), pl.BlockSpec((tc, LANES), row),
                      pl.BlockSpec((1, 1, N_MOD * D_MODEL), lambda i, *_: (row_of_tile(i), 0, 0)),
                      pl.BlockSpec(memory_space=pl.ANY)],
            out_specs=pl.BlockSpec((tc, D_MODEL), row),
            scratch_shapes=[pltpu.VMEM((2, 2, tc, ys.shape[1]), ys.dtype), pltpu.SemaphoreType.DMA((2,))]),
        out_shape=jax.ShapeDtypeStruct((t, D_MODEL), F32),
        compiler_params=_cparams("arbitrary"), name="combine",
    )(pos, x1, route, mod3, ys)


def _rope_tables(seq):
    pos = np.arange(seq)
    n_freq = ATT_HEAD_DIM // 4
    inv = ROPE_THETA ** (-np.arange(n_freq, dtype=np.float32) / n_freq)
    ang = np.concatenate([(pos // GRID_W).astype(np.float32)[:, None] * inv,
                          (pos % GRID_W).astype(np.float32)[:, None] * inv], axis=-1).astype(np.float32)
    ang = jnp.asarray(ang)
    cos, sin = jnp.cos(ang), jnp.sin(ang)
    cos_t = jnp.repeat(cos, 2, axis=1)
    sin_t = jnp.stack([-sin, sin], axis=-1).reshape(seq, ATT_HEAD_DIM)
    return cos_t, sin_t


def _pad_lanes(a):
    return jnp.pad(a, ((0, 0), (0, LANES - a.shape[1])))


def _layer(x2d, n_batch, seq, mod3, row_of_tile, lw, *, rope_tabs, cache, state, emit, h2_all, tile0, t_total):
    (n1, n2, w_main, wg_cat, bg, qg, kg, mg, w_out, rw_cat, rb) = lw
    cos_t, sin_t = rope_tabs
    outs = _project(x2d, mod3, row_of_tile(256), n1, w_main, wg_cat, bg, qg, kg, cos_t, sin_t,
                    rope=cache is not None, emit_kv=emit)
    q, k, v, qm, kmt, vm, om, g = outs[:8]
    rowq, colq = _gate_scans(g)
    attn = _attention(q, k, v, n_batch, seq, cache=cache, tq=min(seq, 512))
    ml = _mlstm(qm, kmt, vm, rowq, colq, n_batch, seq, state=state, emit_state=emit)
    mixed = _mix_out(attn, ml[0], om, x2d, mod3, row_of_tile(MIX_TILE), mg, w_out, n2, rw_cat, rb,
                     h2_all, tile0, t_total, tm=MIX_TILE)
    return mixed, outs[8:], ml[1:]


def kernel(x_prompt, x_sample, cache_k, cache_v, state_C, state_n, state_m, c, c_ctx, mod_w, mod_b, norm1_g, norm2_g,
           w_in, b_gates, q_norm_g, k_norm_g, mlstm_norm_g, w_out, router_group_w, router_group_b, router_expert_w,
           router_expert_b, expert_w_gate, expert_w_up, expert_w_down):
    assert mod_w.shape[0] == 1, "single-layer stack"
    n_ctx, s_ctx, _ = x_prompt.shape
    n_lat, s_lat, _ = x_sample.shape
    t_ctx, t_lat = n_ctx * s_ctx, n_lat * s_lat
    t_all = t_ctx + t_lat
    ctx_row = n_lat

    cond = jnp.concatenate([c, c_ctx[None], jnp.zeros((8 - n_lat - 1, D_MODEL), F32)], axis=0)
    mod3 = _modulation(cond, mod_w[0], mod_b[0][None]).reshape(8, 1, N_MOD * D_MODEL)

    rw =jnp.concatenate([router_group_w[0], jnp.moveaxis(router_expert_w[0], 0, 1).reshape(D_MODEL, N_EXPERTS)], axis=1)
    rb = _pad_lanes(jnp.concatenate([router_group_b[0], router_expert_b[0].reshape(-1)])[None])
    w_in_t = w_in[0].T
    lw = (norm1_g, norm2_g, _cast_bf16_t(w_in_t, MAIN_WIDTH), _gate_cols(w_in_t), _pad_lanes(b_gates),
          q_norm_g, k_norm_g, mlstm_norm_g, _cast_bf16(w_out, D_MODEL), _hi_lo_cat(_pad_lanes(rw)), rb)
    rope_tabs = _rope_tables(s_lat)

    ctx_rows = lambda tm: (lambda i: ctx_row)
    lat_rows = lambda tm: (lambda i: i // (s_lat // tm))
    (x1p, h2_all, routep), (ka, va), (s_c, s_n, s_m) = _layer(
        x_prompt.reshape(t_ctx, D_MODEL), n_ctx, s_ctx, mod3, ctx_rows, lw,
        rope_tabs=rope_tabs, cache=None, state=None, emit=True, h2_all=None, tile0=0, t_total=t_all)

    caug0 = jnp.concatenate([state_C[:, 0], state_n[:, 0][..., None],
                             jnp.zeros(state_n[:, 0].shape + (LANES - 1,), F32)], axis=-1)
    caug0 = caug0.reshape(n_lat, N_CHAINS, MLSTM_HEAD_DIM, AUG)
    past = cache_k.shape[2]
    cache = (cache_k[:, 0].reshape(n_lat * past, KV_WIDTH), cache_v[:, 0].reshape(n_lat * past, KV_WIDTH))
    (x1s, h2_all, routes), _, _ = _layer(
        x_sample.reshape(t_lat, D_MODEL), n_lat, s_lat, mod3, lat_rows, lw,
        rope_tabs=rope_tabs, cache=cache, state=(caug0, state_m[:, 0].reshape(-1)), emit=False,
        h2_all=h2_all, tile0=t_ctx // MIX_TILE, t_total=t_all)

    route = jnp.concatenate([routep, routes], axis=0)
    ranks, counts = _ranks(route)
    pos, tile_plan, tails, n_rows = _routing_plan(route, ranks, counts)
    xs = _dispatch(pos, tails, h2_all, n_rows)
    y_sorted = _experts(*tile_plan, xs, expert_w_gate[0], expert_w_up[0], expert_w_down[0])
    yp = _combine(pos, x1p, routep, mod3, ctx_rows(256), y_sorted, 0, t_all)
    ys = _combine(pos, x1s, routes, mod3, lat_rows(256), y_sorted, t_ctx, t_all)

    kv_shape = (n_ctx, 1, s_ctx, ATT_KV_HEADS, ATT_HEAD_DIM)
    return (yp.reshape(x_prompt.shape), ys.reshape(x_sample.shape), ka.reshape(kv_shape), va.reshape(kv_shape),
            s_c.reshape(n_ctx, 1, 2, MLSTM_HEADS, MLSTM_HEAD_DIM, MLSTM_HEAD_DIM),
            s_n.reshape(n_ctx, 1, 2, MLSTM_HEADS, MLSTM_HEAD_DIM), s_m[..., 0, 0].reshape(n_ctx, 1, 2, MLSTM_HEADS))
```

```python
import functools

import numpy as np
import jax
import jax.numpy as jnp
from jax import lax
from jax.experimental import pallas as pl
from jax.experimental.pallas import tpu as pltpu

F32 = jnp.float32
BF16 = jnp.bfloat16

D_MODEL = 2048
GRID_W = 64
ATT_HEADS = 8
ATT_KV_HEADS = 2
ATT_HEAD_DIM = 128
ATT_GROUP = ATT_HEADS // ATT_KV_HEADS
ATT_WIDTH = ATT_HEADS * ATT_HEAD_DIM
KV_WIDTH = ATT_KV_HEADS * ATT_HEAD_DIM
ROPE_THETA = 10000.0
MLSTM_HEADS = 4
MLSTM_HEAD_DIM = 256
MLSTM_WIDTH = MLSTM_HEADS * MLSTM_HEAD_DIM
CHUNK = 256
N_GATES = 4 * MLSTM_HEADS
N_CHAINS = 2 * MLSTM_HEADS
MAIN_WIDTH = ATT_WIDTH + 2 * KV_WIDTH + 4 * MLSTM_WIDTH
N_GROUPS = 4
EXPERTS_PER_GROUP = 8
N_EXPERTS = N_GROUPS * EXPERTS_PER_GROUP
D_EXPERT = 512
N_MOD = 6
TM_MOE = 256
MIX_TILE = 512
ATT_KEY_CHUNK = 256
Q_SCALE = ATT_HEAD_DIM ** -0.5 * float(np.log2(np.e))
EPS = 1e-6

LANES = 128
AUG = MLSTM_HEAD_DIM + LANES
N_SCAN_ROWS = 5 * N_CHAINS
ROUTER_LANE0 = N_GROUPS
VMEM_LIMIT = 56 * 1024 * 1024


def _cparams(*sem):
    return pltpu.CompilerParams(dimension_semantics=sem, vmem_limit_bytes=VMEM_LIMIT)


def _const_spec(shape):
    nd = len(shape)
    return pl.BlockSpec(shape, lambda *_: (0,) * nd, pipeline_mode=pl.Buffered(1))


def _split_hi_lo(x):
    hi = x.astype(BF16)
    lo = (x - hi.astype(F32)).astype(BF16)
    return hi, lo


def _dot(a, b):
    return jnp.dot(a, b, preferred_element_type=F32)


def _hi_lo_cat(w):
    return jnp.concatenate(_split_hi_lo(w), axis=1)


def _dot_hi_lo(a_hi, a_lo, w_ref):
    r = _dot(a_hi, w_ref[...])
    return r[:, :LANES] + r[:, LANES:] + _dot(a_lo, w_ref[:, :LANES])


def _rms(x, g):
    return x * lax.rsqrt(jnp.mean(x * x, axis=-1, keepdims=True) + EPS) * g


def _mod_kernel(c_ref, w_ref, b_ref, o_ref):
    c = c_ref[...]
    s = c / (1.0 + jnp.exp(-c))
    s_hi = s.astype(BF16).astype(F32)
    lhs = jnp.concatenate([s_hi, s - s_hi], axis=0).astype(BF16)
    w_hi, w_lo = _split_hi_lo(w_ref[...])
    r = _dot(lhs, w_hi)
    r2 = _dot(lhs, w_lo)
    o_ref[...] = r[:8] + r[8:] + r2[:8] + b_ref[...]


def _modulation(cond, mod_w, mod_b):
    n = mod_w.shape[1]
    tn = 1024
    return pl.pallas_call(
        _mod_kernel,
        grid=(n // tn,),
        in_specs=[pl.BlockSpec((8, D_MODEL), lambda j: (0, 0)),
                  pl.BlockSpec((D_MODEL, tn), lambda j: (0, j)),
                  pl.BlockSpec((1, tn), lambda j: (0, j))],
        out_specs=pl.BlockSpec((8, tn), lambda j: (0, j)),
        out_shape=jax.ShapeDtypeStruct((8, n), F32),
        compiler_params=_cparams("parallel"),
        name="modulation",
    )(cond, mod_w, mod_b)


def _cast_kernel(w_ref, o_ref):
    o_ref[...] = w_ref[0].astype(BF16)


def _cast_bf16(w, n_cols, tn=512):
    rows = w.shape[1]
    return pl.pallas_call(
        _cast_kernel, grid=(n_cols // tn,),
        in_specs=[pl.BlockSpec((1, rows, tn), lambda j: (0, 0, j))],
        out_specs=pl.BlockSpec((rows, tn), lambda j: (0, j)),
        out_shape=jax.ShapeDtypeStruct((rows, n_cols), BF16),
        compiler_params=_cparams("parallel"), name="cast_bf16",
    )(w)


def _cast_t_kernel(wt_ref, o_ref):
    o_ref[...] = wt_ref[...].T.astype(BF16)


def _cast_bf16_t(wt, n_cols, tn=512):
    rows = wt.shape[1]
    return pl.pallas_call(
        _cast_t_kernel, grid=(n_cols // tn,),
        in_specs=[pl.BlockSpec((tn, rows), lambda j: (j, 0))],
        out_specs=pl.BlockSpec((rows, tn), lambda j: (0, j)),
        out_shape=jax.ShapeDtypeStruct((rows, n_cols), BF16),
        compiler_params=_cparams("parallel"), name="cast_bf16_t",
    )(wt)


def _gate_cols_kernel(wt_ref, o_ref):
    sub = lax.broadcasted_iota(jnp.int32, wt_ref.shape, 0)
    hi, lo = _split_hi_lo(jnp.where(sub < N_GATES, wt_ref[...], 0.0).T)
    o_ref[:, :LANES] = hi
    o_ref[:, LANES:] = lo


def _gate_cols(wt):
    rows = wt.shape[1]
    return pl.pallas_call(
        _gate_cols_kernel, grid=(1,),
        in_specs=[pl.BlockSpec((LANES, rows), lambda i: (MAIN_WIDTH // LANES, 0))],
        out_specs=pl.BlockSpec((rows, 2 * LANES), lambda i: (0, 0)),
        out_shape=jax.ShapeDtypeStruct((rows, 2 * LANES), BF16),
        compiler_params=_cparams("arbitrary"), name="gate_cols",
    )(wt)


def _pair_swap(x):
    lane = lax.broadcasted_iota(jnp.int32, x.shape, 1)
    return jnp.where((lane & 1) == 0, pltpu.roll(x, LANES - 1, 1), pltpu.roll(x, 1, 1))


def _proj_kernel(x_ref, mod_ref, n1_ref, w_ref, wg_ref, bg_ref, qg_ref, kg_ref, cos_ref, sin_ref,
                 q_ref, k_ref, v_ref, qm_ref, kmt_ref, vm_ref, om_ref, row_ref, col_ref, *kv_refs, rope, tm):
    mod = mod_ref[0]
    shift, scale = mod[:, :D_MODEL], mod[:, D_MODEL:2 * D_MODEL]
    h = _rms(x_ref[...], n1_ref[...]) * (1.0 + scale) + shift
    h_hi, h_lo = _split_hi_lo(h)

    g = _dot_hi_lo(h_hi, h_lo, wg_ref) + bg_ref[...]
    for cc in range(tm // CHUNK):
        rows, cols = _gate_scan_chunk(g[cc * CHUNK:(cc + 1) * CHUNK, :])
        row_ref[cc] = rows
        col_ref[cc * CHUNK:(cc + 1) * CHUNK, :] = cols

    def rot(seg):
        return seg * cos_ref[...] + _pair_swap(seg) * sin_ref[...] if rope else seg

    qa = _dot(h_hi, w_ref[:, :ATT_WIDTH])
    for hh in range(ATT_HEADS):
        sl = slice(hh * ATT_HEAD_DIM, (hh + 1) * ATT_HEAD_DIM)
        seg = rot(_rms(qa[:, sl], qg_ref[...]))
        q_ref[:, sl] = (seg * Q_SCALE).astype(BF16)

    kv = _dot(h_hi, w_ref[:, ATT_WIDTH:ATT_WIDTH + 2 * KV_WIDTH])
    for hh in range(ATT_KV_HEADS):
        sl = slice(hh * ATT_HEAD_DIM, (hh + 1) * ATT_HEAD_DIM)
        seg = _rms(kv[:, sl], kg_ref[...])
        if kv_refs:
            kv_refs[0][:, sl] = seg
        k_ref[:, sl] = rot(seg).astype(BF16)
    va = kv[:, KV_WIDTH:]
    if kv_refs:
        kv_refs[1][...] = va
    v_ref[...] = va.astype(BF16)

    c0 = ATT_WIDTH + 2 * KV_WIDTH
    qm_ref[...] = (_dot(h_hi, w_ref[:, c0:c0 + MLSTM_WIDTH]) * MLSTM_HEAD_DIM ** -0.5).astype(BF16)
    km = _dot(h_hi, w_ref[:, c0 + MLSTM_WIDTH:c0 + 2 * MLSTM_WIDTH])
    for cc in range(tm // CHUNK):
        kmt_ref[cc] = km[cc * CHUNK:(cc + 1) * CHUNK, :].T.astype(BF16)
    vm_ref[...] = _dot(h_hi, w_ref[:, c0 + 2 * MLSTM_WIDTH:c0 + 3 * MLSTM_WIDTH]).astype(BF16)
    om_ref[...] = _dot(h_hi, w_ref[:, c0 + 3 * MLSTM_WIDTH:c0 + 4 * MLSTM_WIDTH]).astype(BF16)


def _project(x2d, mod3, row_of_tile, n1, w_main, wg_cat, bg, qg, kg, cos_t, sin_t, *, rope, emit_kv, tm=256):
    t = x2d.shape[0]
    n_pos = cos_t.shape[0] // tm
    row = lambda i: (i, 0)
    in_specs = [pl.BlockSpec((tm, D_MODEL), row),
                pl.BlockSpec((1, 1, N_MOD * D_MODEL), lambda i: (row_of_tile(i), 0, 0)),
                _const_spec((1, D_MODEL)),
                _const_spec((D_MODEL, MAIN_WIDTH)),
                _const_spec((D_MODEL, 2 * LANES)), _const_spec((1, LANES)),
                _const_spec((1, ATT_HEAD_DIM)), _const_spec((1, ATT_HEAD_DIM)),
                pl.BlockSpec((tm, ATT_HEAD_DIM), lambda i: (i % n_pos, 0)),
                pl.BlockSpec((tm, ATT_HEAD_DIM), lambda i: (i % n_pos, 0))]
    out_shape = [jax.ShapeDtypeStruct((t, ATT_WIDTH), BF16), jax.ShapeDtypeStruct((t, KV_WIDTH), BF16),
                 jax.ShapeDtypeStruct((t, KV_WIDTH), BF16), jax.ShapeDtypeStruct((t, MLSTM_WIDTH), BF16),
                 jax.ShapeDtypeStruct((t // CHUNK, MLSTM_WIDTH, CHUNK), BF16),
                 jax.ShapeDtypeStruct((t, MLSTM_WIDTH), BF16), jax.ShapeDtypeStruct((t, MLSTM_WIDTH), BF16),
                 jax.ShapeDtypeStruct((t // CHUNK, N_SCAN_ROWS, CHUNK), F32), jax.ShapeDtypeStruct((t, LANES), F32)]
    out_specs = [pl.BlockSpec((tm, ATT_WIDTH), row), pl.BlockSpec((tm, KV_WIDTH), row),
                 pl.BlockSpec((tm, KV_WIDTH), row), pl.BlockSpec((tm, MLSTM_WIDTH), row),
                 pl.BlockSpec((tm // CHUNK, MLSTM_WIDTH, CHUNK), lambda i: (i, 0, 0)),
                 pl.BlockSpec((tm, MLSTM_WIDTH), row), pl.BlockSpec((tm, MLSTM_WIDTH), row),
                 pl.BlockSpec((tm // CHUNK, N_SCAN_ROWS, CHUNK), lambda i: (i, 0, 0)), pl.BlockSpec((tm, LANES), row)]
    if emit_kv:
        out_shape += [jax.ShapeDtypeStruct((t, KV_WIDTH), F32)] * 2
        out_specs += [pl.BlockSpec((tm, KV_WIDTH), row)] * 2
    return pl.pallas_call(
        functools.partial(_proj_kernel, rope=rope, tm=tm),
        grid=(t // tm,), in_specs=in_specs, out_specs=out_specs, out_shape=out_shape,
        compiler_params=_cparams("parallel"), name="in_proj",
    )(x2d, mod3, n1, w_main, wg_cat, bg, qg, kg, cos_t, sin_t)


def _lane_scan(x, op, fill, is_fwd, lane):
    s = 1
    while s < CHUNK:
        from_left = jnp.where(lane >= s, pltpu.roll(x, s, 1), fill)
        from_right = jnp.where(lane < CHUNK - s, pltpu.roll(x, CHUNK - s, 1), fill)
        x = op(x, jnp.where(is_fwd, from_left, from_right))
        s *= 2
    return x


def _gate_scan_chunk(g):
    lane = lax.broadcasted_iota(jnp.int32, (N_CHAINS, CHUNK), 1)
    is_fwd = lax.broadcasted_iota(jnp.int32, (N_CHAINS, CHUNK), 0) < MLSTM_HEADS
    gt = g.T
    fwd, bwd = gt[:N_CHAINS], gt[N_CHAINS:2 * N_CHAINS]
    li = jnp.where(is_fwd, fwd, pltpu.roll(bwd, MLSTM_HEADS, 0))
    f = jnp.where(is_fwd, pltpu.roll(fwd, MLSTM_HEADS, 0), bwd)
    lf = jnp.minimum(f, 0.0) - jnp.log(1.0 + jnp.exp(-jnp.abs(f)))
    b = _lane_scan(lf, jnp.add, 0.0, is_fwd, lane)
    a = li - b
    run_max = _lane_scan(a, jnp.maximum, -jnp.inf, is_fwd, lane)
    all_max = jnp.broadcast_to(jnp.max(a, axis=1, keepdims=True), a.shape)
    last = jnp.where(is_fwd, CHUNK - 1, 0)
    total = jnp.broadcast_to(jnp.sum(jnp.where(lane == last, b, 0.0), axis=1, keepdims=True), a.shape)
    rows = jnp.concatenate([a, b, run_max, all_max, total], axis=0)
    padded = jnp.concatenate([rows, jnp.zeros((LANES - N_SCAN_ROWS, CHUNK), F32)], axis=0)
    return rows, padded.T


def _mlstm_kernel(*refs, n_chunks, has_state, emit_state):
    it = iter(refs)
    q_ref, kt_ref, v_ref, row_ref, col_ref = [next(it) for _ in range(5)]
    c0_ref, m0_ref = (next(it), next(it)) if has_state else (None, None)
    o_ref = next(it)
    c_out, n_out, m_out = (next(it), next(it), next(it)) if emit_state else (None, None, None)
    caug, hb = next(it), next(it)

    b = pl.program_id(0)
    if has_state:
        caug[...] = c0_ref[0]
        m_init = tuple(jnp.full((1, 1), m0_ref[b * N_CHAINS + r], F32) for r in range(N_CHAINS))
    else:
        caug[...] = jnp.zeros(caug.shape, F32)
        m_init = tuple(jnp.zeros((1, 1), F32) for _ in range(N_CHAINS))

    sub = lax.broadcasted_iota(jnp.int32, (CHUNK, CHUNK), 0)
    lane = lax.broadcasted_iota(jnp.int32, (CHUNK, CHUNK), 1)
    ones_col = (lax.broadcasted_iota(jnp.int32, (CHUNK, LANES), 1) == 0).astype(BF16)

    def chain(r, cc, m_prev):
        d, hd = divmod(r, MLSTM_HEADS)
        t0 = pl.multiple_of(cc * CHUNK, CHUNK)
        hs = slice(hd * MLSTM_HEAD_DIM, (hd + 1) * MLSTM_HEAD_DIM)
        rows = row_ref[cc]
        cols = col_ref[pl.ds(t0, CHUNK), :]
        row = lambda k: rows[k * N_CHAINS + r:k * N_CHAINS + r + 1, :]
        col = lambda k: cols[:, k * N_CHAINS + r:k * N_CHAINS + r + 1]
        q = q_ref[pl.ds(t0, CHUNK), hs]
        kt = kt_ref[cc, hs, :]
        vaug = jnp.concatenate([v_ref[pl.ds(t0, CHUNK), hs], ones_col], axis=1)

        m_col = jnp.maximum(m_prev, col(2))
        keep = (lane <= sub) if d == 0 else (lane >= sub)
        w = jnp.where(keep, jnp.exp(row(0) - m_col), 0.0)
        w_inter = jnp.exp(m_prev - m_col)
        p = (_dot(q, kt) * w).astype(BF16)
        intra = _dot(p, vaug)
        inter = _dot(q, caug[r].astype(BF16))
        num = intra[:, :MLSTM_HEAD_DIM] + w_inter * inter[:, :MLSTM_HEAD_DIM]
        den = intra[:, MLSTM_HEAD_DIM:MLSTM_HEAD_DIM + 1] + w_inter * inter[:, MLSTM_HEAD_DIM:MLSTM_HEAD_DIM + 1]
        h_out = num / jnp.maximum(jnp.abs(den), jnp.exp(-(col(1) + m_col)))

        m_last = jnp.maximum(m_prev, row(3))
        kw = (kt.astype(F32) * jnp.exp(row(0) - m_last)).astype(BF16)
        caug[r] = jnp.exp(m_prev - m_last[:, :1]) * caug[r] + _dot(kw, vaug)
        return h_out, (row(4) + m_last)[:, :1]

    def body(c, ms):
        new = []
        for r in range(N_CHAINS):
            cc = c if r < MLSTM_HEADS else n_chunks - 1 - c
            h_out, m_new = chain(r, cc, ms[r])
            hd = r % MLSTM_HEADS
            dst = o_ref if r < MLSTM_HEADS else hb
            dst[pl.ds(pl.multiple_of(cc * CHUNK, CHUNK), CHUNK), hd * MLSTM_HEAD_DIM:(hd + 1) * MLSTM_HEAD_DIM] = h_out
            new.append(m_new)
        return tuple(new)

    ms = lax.fori_loop(0, n_chunks, body, m_init)
    o_ref[...] += hb[...]
    if emit_state:
        for r in range(N_CHAINS):
            c_out[0, r] = caug[r, :, :MLSTM_HEAD_DIM]
            n_out[0, r] = caug[r, :, MLSTM_HEAD_DIM:].T[:1, :]
            m_out[0, r] = jnp.broadcast_to(ms[r], (1, LANES))


def _mlstm(qm, kmt, vm, rowq, colq, n_batch, seq, state=None, emit_state=False):
    nc = seq // CHUNK
    mode = dict(pipeline_mode=pl.Buffered(1)) if seq * MLSTM_WIDTH * 4 > (2 << 20) else {}
    in_specs = [pl.BlockSpec((seq, MLSTM_WIDTH), lambda b: (b, 0), **mode),
                pl.BlockSpec((nc, MLSTM_WIDTH, CHUNK), lambda b: (b, 0, 0), **mode),
                pl.BlockSpec((seq, MLSTM_WIDTH), lambda b: (b, 0), **mode),
                pl.BlockSpec((nc, N_SCAN_ROWS, CHUNK), lambda b: (b, 0, 0), **mode),
                pl.BlockSpec((seq, LANES), lambda b: (b, 0), **mode)]
    args = [qm, kmt, vm, rowq, colq]
    if state is not None:
        in_specs += [pl.BlockSpec((1, N_CHAINS, MLSTM_HEAD_DIM, AUG), lambda b: (b, 0, 0, 0), **mode),
                     pl.BlockSpec(memory_space=pltpu.SMEM)]
        args += list(state)
    out_shape = [jax.ShapeDtypeStruct((n_batch * seq, MLSTM_WIDTH), F32)]
    out_specs = [pl.BlockSpec((seq, MLSTM_WIDTH), lambda b: (b, 0))]
    if emit_state:
        out_shape += [jax.ShapeDtypeStruct((n_batch, N_CHAINS, MLSTM_HEAD_DIM, MLSTM_HEAD_DIM), F32),
                      jax.ShapeDtypeStruct((n_batch, N_CHAINS, 1, MLSTM_HEAD_DIM), F32),
                      jax.ShapeDtypeStruct((n_batch, N_CHAINS, 1, LANES), F32)]
        out_specs += [pl.BlockSpec((1, N_CHAINS, MLSTM_HEAD_DIM, MLSTM_HEAD_DIM), lambda b: (b, 0, 0, 0)),
                      pl.BlockSpec((1, N_CHAINS, 1, MLSTM_HEAD_DIM), lambda b: (b, 0, 0, 0)),
                      pl.BlockSpec((1, N_CHAINS, 1, LANES), lambda b: (b, 0, 0, 0))]
    return pl.pallas_call(
        functools.partial(_mlstm_kernel, n_chunks=nc, has_state=state is not None, emit_state=emit_state),
        grid=(n_batch,), in_specs=in_specs, out_specs=out_specs, out_shape=out_shape,
        scratch_shapes=[pltpu.VMEM((N_CHAINS, MLSTM_HEAD_DIM, AUG), F32), pltpu.VMEM((seq, MLSTM_WIDTH), F32)],
        compiler_params=_cparams("parallel"), name="mlstm",
    )(*args)


def _attn_kernel(*refs, tq, has_cache):
    if has_cache:
        q_ref, k_ref, v_ref, ck_ref, cv_ref, o_ref = refs
    else:
        q_ref, k_ref, v_ref, o_ref = refs
    q = q_ref[...]
    qs = jnp.concatenate([q[:, g * ATT_HEAD_DIM:(g + 1) * ATT_HEAD_DIM] for g in range(ATT_GROUP)], axis=0)
    nt = (((1,), (1,)), ((), ()))
    seq = k_ref.shape[0]
    kc = min(seq, ATT_KEY_CHUNK)
    chunks = [(k_ref, v_ref, c * kc) for c in range(seq // kc)]
    if has_cache:
        chunks.insert(0, (ck_ref, cv_ref, None))
    m = l = o = None
    for kr, vr, start in chunks:
        kk, vv = (kr[...], vr[...]) if start is None else (kr[start:start + kc, :], vr[start:start + kc, :])
        s = lax.dot_general(qs, kk.astype(BF16), nt, preferred_element_type=F32)
        mc = jnp.max(s, axis=-1, keepdims=True)
        if m is None:
            m = mc
            p = jnp.exp2(s - m)
            l = jnp.sum(p, axis=-1, keepdims=True)
            o = _dot(p.astype(BF16), vv.astype(BF16))
        else:
            m_new = jnp.maximum(m, mc)
            alpha = jnp.exp2(m - m_new)
            p = jnp.exp2(s - m_new)
            l = alpha * l + jnp.sum(p, axis=-1, keepdims=True)
            o = alpha * o + _dot(p.astype(BF16), vv.astype(BF16))
            m = m_new
    o = o / l
    for g in range(ATT_GROUP):
        o_ref[:, g * ATT_HEAD_DIM:(g + 1) * ATT_HEAD_DIM] = o[g * tq:(g + 1) * tq].astype(BF16)


def _attention(q, k, v, n_batch, seq, cache=None, tq=256):
    nqb = seq // tq
    gw = ATT_GROUP * ATT_HEAD_DIM
    in_specs = [pl.BlockSpec((tq, gw), lambda b, h, i: (b * nqb + i, h)),
                pl.BlockSpec((seq, ATT_HEAD_DIM), lambda b, h, i: (b, h)),
                pl.BlockSpec((seq, ATT_HEAD_DIM), lambda b, h, i: (b, h))]
    args = [q, k, v]
    if cache is not None:
        past = cache[0].shape[0] // n_batch
        in_specs += [pl.BlockSpec((past, ATT_HEAD_DIM), lambda b, h, i: (b, h))] * 2
        args += list(cache)
    return pl.pallas_call(
        functools.partial(_attn_kernel, tq=tq, has_cache=cache is not None),
        grid=(n_batch, ATT_KV_HEADS, nqb), in_specs=in_specs,
        out_specs=pl.BlockSpec((tq, gw), lambda b, h, i: (b * nqb + i, h)),
        out_shape=jax.ShapeDtypeStruct((n_batch * seq, ATT_WIDTH), BF16),
        compiler_params=_cparams("parallel", "parallel", "parallel"), name="attention",
    )(*args)


def _mix_kernel(attn_ref, hm_ref, om_ref, x_ref, mod_ref, mg_ref, wo_ref, n2_ref, rw_ref, rb_ref, *rest, sub):
    for r0 in range(0, x_ref.shape[0], sub):
        _mix_rows(slice(r0, r0 + sub), attn_ref, hm_ref, om_ref, x_ref, mod_ref, mg_ref, wo_ref, n2_ref, rw_ref,
                  rb_ref, *rest[-3:])


def _mix_rows(rows, attn_ref, hm_ref, om_ref, x_ref, mod_ref, mg_ref, wo_ref, n2_ref, rw_ref, rb_ref,
              x1_ref, h2_ref, route_ref):
    mod = mod_ref[0]
    gate1 = mod[:, 2 * D_MODEL:3 * D_MODEL]
    shift2, scale2 = mod[:, 3 * D_MODEL:4 * D_MODEL], mod[:, 4 * D_MODEL:5 * D_MODEL]
    hm = hm_ref[rows, :]
    mg = mg_ref[...]
    parts = []
    for hd in range(MLSTM_HEADS):
        sl = slice(hd * MLSTM_HEAD_DIM, (hd + 1) * MLSTM_HEAD_DIM)
        parts.append(_rms(hm[:, sl], mg[:, sl]))
    om = om_ref[rows, :].astype(F32)
    hmg = jnp.concatenate(parts, axis=1) * (1.0 / (1.0 + jnp.exp(-om)))
    y = _dot(attn_ref[rows, :], wo_ref[:ATT_WIDTH, :]) + _dot(hmg.astype(BF16), wo_ref[ATT_WIDTH:, :])
    x1 = x_ref[rows, :] + gate1 * y
    x1_ref[rows, :] = x1
    h2 = _rms(x1, n2_ref[...]) * (1.0 + scale2) + shift2
    h2_hi, h2_lo = _split_hi_lo(h2)
    h2_ref[rows, :] = _pack_halves(h2)

    lg = _dot_hi_lo(h2_hi, h2_lo, rw_ref) + rb_ref[...]
    lane = lax.broadcasted_iota(jnp.int32, lg.shape, 1).astype(F32)
    neg = -jnp.inf
    first = lambda hit: jnp.min(jnp.where(hit, lane, float(LANES)), axis=-1, keepdims=True)
    gl = jnp.where(lane < N_GROUPS, lg, neg)
    gmax = jnp.max(gl, axis=-1, keepdims=True)
    grp = first(gl == gmax)
    p_grp = 1.0 / jnp.sum(jnp.exp(gl - gmax), axis=-1, keepdims=True)
    lo = ROUTER_LANE0 + grp * EXPERTS_PER_GROUP
    el = jnp.where((lane >= lo) & (lane < lo + EXPERTS_PER_GROUP), lg, neg)
    m1 = jnp.max(el, axis=-1, keepdims=True)
    i1 = first(el == m1)
    el2 = jnp.where(lane == i1, neg, el)
    m2 = jnp.max(el2, axis=-1, keepdims=True)
    i2 = first(el2 == m2)
    r = jnp.exp(m2 - m1)
    w1 = p_grp / (1.0 + r)
    w2 = w1 * r
    route_ref[rows, :] = jnp.where(lane == 0.0, i1 - ROUTER_LANE0, jnp.where(lane == 1.0, i2 - ROUTER_LANE0,
                                   jnp.where(lane == 2.0, w1, jnp.where(lane == 3.0, w2, 0.0))))


def _mix_out(attn, hm, om, x2d, mod3, row_of_tile, mg, w_out, n2, rw_cat, rb, h2_all, tile0, t_total, tm=512):
    t = x2d.shape[0]
    row = lambda i: (i, 0)
    in_specs = [pl.BlockSpec((tm, ATT_WIDTH), row), pl.BlockSpec((tm, MLSTM_WIDTH), row),
                pl.BlockSpec((tm, MLSTM_WIDTH), row), pl.BlockSpec((tm, D_MODEL), row),
                pl.BlockSpec((1, 1, N_MOD * D_MODEL), lambda i: (row_of_tile(i), 0, 0)),
                _const_spec((1, MLSTM_WIDTH)), _const_spec((D_MODEL, D_MODEL)), _const_spec((1, D_MODEL)),
                _const_spec((D_MODEL, 2 * LANES)), _const_spec((1, LANES))]
    args = [attn, hm, om, x2d, mod3, mg, w_out, n2, rw_cat, rb]
    aliases = {}
    if h2_all is not None:
        aliases = {len(args): 1}
        in_specs.append(pl.BlockSpec(memory_space=pl.ANY))
        args.append(h2_all)
    return pl.pallas_call(
        functools.partial(_mix_kernel, sub=256), grid=(t // tm,), in_specs=in_specs,
        out_specs=[pl.BlockSpec((tm, D_MODEL), row), pl.BlockSpec((tm, D_MODEL // 2), lambda i: (tile0 + i, 0)),
                   pl.BlockSpec((tm, LANES), row)],
        out_shape=[jax.ShapeDtypeStruct((t, D_MODEL), F32), jax.ShapeDtypeStruct((t_total, D_MODEL // 2), jnp.uint32),
                   jax.ShapeDtypeStruct((t, LANES), F32)],
        input_output_aliases=aliases,
        compiler_params=_cparams("parallel"), name="mix_out",
    )(*args)


def _rank_kernel(route_ref, rank_ref, cnt_ref, run_ref, tri_ref):
    tr = route_ref.shape[0]

    @pl.when(pl.program_id(0) == 0)
    def _():
        run_ref[...] = jnp.zeros(run_ref.shape, F32)
        tri_ref[...] = (lax.broadcasted_iota(jnp.int32, (tr, tr), 1)
                        < lax.broadcasted_iota(jnp.int32, (tr, tr), 0)).astype(BF16)

    route = route_ref[...]
    lane = lax.broadcasted_iota(jnp.int32, route.shape, 1).astype(F32)
    hit1, hit2 = lane == route[:, 0:1], lane == route[:, 1:2]
    onehot = jnp.where(hit1, 1.0, jnp.where(hit2, 1.0, 0.0))
    before = _dot(tri_ref[...], onehot.astype(BF16)) + run_ref[0:1, :]
    r1 = jnp.sum(jnp.where(hit1, before, 0.0), axis=-1, keepdims=True)
    r2 = jnp.sum(jnp.where(hit2, before, 0.0), axis=-1, keepdims=True)
    rank_ref[...] = jnp.where(lane == 0.0, r1, jnp.where(lane == 1.0, r2, 0.0))
    run_ref[...] = run_ref[...] + jnp.sum(onehot, axis=0, keepdims=True)
    cnt_ref[...] = run_ref[...]


def _ranks(route, tr=512):
    t = route.shape[0]
    return pl.pallas_call(
        _rank_kernel, grid=(t // tr,),
        in_specs=[pl.BlockSpec((tr, LANES), lambda i: (i, 0))],
        out_specs=[pl.BlockSpec((tr, LANES), lambda i: (i, 0)), pl.BlockSpec((8, LANES), lambda i: (0, 0))],
        out_shape=[jax.ShapeDtypeStruct((t, LANES), F32), jax.ShapeDtypeStruct((8, LANES), F32)],
        scratch_shapes=[pltpu.VMEM((8, LANES), F32), pltpu.VMEM((tr, tr), BF16)],
        compiler_params=_cparams("arbitrary"), name="expert_ranks",
    )(route)


def _pos_kernel(route_ref, rank_ref, start_ref, o_ref):
    route, rank = route_ref[...], rank_ref[...]
    lane = lax.broadcasted_iota(jnp.int32, route.shape, 1).astype(F32)
    start = start_ref[...]
    first = lambda col: jnp.sum(jnp.where(lane == route[:, col:col + 1], start, 0.0), axis=-1, keepdims=True)
    p1 = first(0) + rank[:, 0:1]
    p2 = first(1) + rank[:, 1:2]
    tile = jnp.where(lane == 0.0, p1, jnp.where(lane == 1.0, p2, 0.0))
    o_ref[...] = tile.T[:8, :].astype(jnp.int32)


def _positions(route, ranks, starts, tr=512):
    t = route.shape[0]
    pos = pl.pallas_call(
        _pos_kernel, grid=(t // tr,),
        in_specs=[pl.BlockSpec((tr, LANES), lambda i: (i, 0)), pl.BlockSpec((tr, LANES), lambda i: (i, 0)),
                  pl.BlockSpec((1, LANES), lambda i: (0, 0))],
        out_specs=pl.BlockSpec((8, tr), lambda i: (0, i)),
        out_shape=jax.ShapeDtypeStruct((8, t), jnp.int32),
        compiler_params=_cparams("parallel"), name="pair_rows",
    )(route, ranks, starts)
    return pos[:2].reshape(-1)


def _routing_plan(route, ranks, counts):
    t = route.shape[0]
    n_tiles = 2 * t // TM_MOE + N_EXPERTS
    cnt = counts[0, :N_EXPERTS].astype(jnp.int32)
    padded = (cnt + TM_MOE - 1) // TM_MOE * TM_MOE
    ends = jnp.cumsum(padded)
    pos = _positions(route, ranks, _pad_lanes((ends - padded).astype(F32)[None]))
    n_used = (ends[-1:] // TM_MOE).astype(jnp.int32)
    tile_start = jnp.arange(n_tiles, dtype=jnp.int32) * TM_MOE
    tile_expert = jnp.sum((ends[None, :] <= tile_start[:, None]).astype(jnp.int32), axis=1)
    tile_expert = jnp.minimum(tile_expert, tile_expert[n_used[0] - 1]).astype(jnp.int32)
    tails = jnp.where(padded > 0, ends - TM_MOE, -1).astype(jnp.int32)
    change = jnp.concatenate([jnp.ones((1,), jnp.int32), (tile_expert[1:] != tile_expert[:-1]).astype(jnp.int32)])
    weight_slot = (jnp.cumsum(change) - 1) % 2
    ids = jnp.arange(N_EXPERTS, dtype=jnp.int32)
    later_used = (padded > 0)[None, :] & (ids[None, :] > ids[:, None])
    next_used = jnp.min(jnp.where(later_used, ids[None, :], N_EXPERTS), axis=1)
    next_used = jnp.where(next_used < N_EXPERTS, next_used, -1)
    next_expert = jnp.sum(jnp.where(tile_expert[:, None] == ids[None, :], next_used[None, :], 0), axis=1)
    return pos, (tile_expert, n_used, weight_slot.astype(jnp.int32), next_expert.astype(jnp.int32)), tails, \
        n_tiles * TM_MOE


def _row_copies(pos_ref, n_pairs, tok0, n_rows, make_copy):
    def body(r, carry):
        for k in range(2):
            make_copy(k, r, pos_ref[k * n_pairs + tok0 + r]).start()
        return carry
    lax.fori_loop(0, n_rows, body, 0, unroll=8)


def _dispatch_kernel(pos_ref, tail_ref, h_ref, xs_ref, hbuf, zero_ref, tsem, rsem, *, td, n_tokens):
    i = pl.program_id(0)
    last = pl.num_programs(0) - 1
    tile_copy = lambda t: pltpu.make_async_copy(
        h_ref.at[pl.ds(pl.multiple_of(t * td, td), td)], hbuf.at[t % 3], tsem.at[t % 3])

    def wait_rows(t):
        for _ in range(2):
            pltpu.make_async_copy(hbuf.at[0], xs_ref.at[pl.ds(0, td)], rsem.at[t % 2]).wait()

    @pl.when(i == 0)
    def _():
        tile_copy(0).start()
        zero_ref[...] = jnp.zeros(zero_ref.shape, zero_ref.dtype)
        tail_copy = lambda e: pltpu.make_async_copy(
            zero_ref, xs_ref.at[pl.ds(pl.multiple_of(tail_ref[e], TM_MOE), TM_MOE)], rsem.at[1])
        for e in range(N_EXPERTS):
            pl.when(tail_ref[e] >= 0)(lambda e=e: tail_copy(e).start())
        for e in range(N_EXPERTS):
            pl.when(tail_ref[e] >= 0)(lambda e=e: tail_copy(e).wait())

    @pl.when(i < last)
    def _():
        tile_copy(i + 1).start()

    tile_copy(i).wait()
    src = hbuf.at[i % 3]
    _row_copies(pos_ref, n_tokens, i * td, td, lambda k, r, p: pltpu.make_async_copy(
        src.at[pl.ds(r, 1)], xs_ref.at[pl.ds(p, 1)], rsem.at[i % 2]))

    @pl.when(i >= 1)
    def _():
        wait_rows(i - 1)

    @pl.when(i == last)
    def _():
        wait_rows(i)


def _dispatch(pos, tails, h2, n_rows, td=256):
    t, width = h2.shape
    return pl.pallas_call(
        functools.partial(_dispatch_kernel, td=td, n_tokens=t),
        grid_spec=pltpu.PrefetchScalarGridSpec(
            num_scalar_prefetch=2, grid=(t // td,),
            in_specs=[pl.BlockSpec(memory_space=pl.ANY)],
            out_specs=pl.BlockSpec(memory_space=pl.ANY),
            scratch_shapes=[pltpu.VMEM((3, td, width), h2.dtype), pltpu.VMEM((TM_MOE, width), h2.dtype),
                            pltpu.SemaphoreType.DMA((3,)), pltpu.SemaphoreType.DMA((2,))]),
        out_shape=jax.ShapeDtypeStruct((n_rows, width), h2.dtype),
        compiler_params=_cparams("arbitrary"), name="dispatch",
    )(pos, tails, h2)


def _pack_halves(x):
    half = x.shape[1] // 2
    return pltpu.pack_elementwise([x[:, :half], x[:, half:]], packed_dtype=BF16)


def _unpack_halves(p, dtype):
    return tuple(pltpu.unpack_elementwise(p, index=i, packed_dtype=BF16, unpacked_dtype=F32).astype(dtype)
                 for i in range(2))


def _expert_kernel(te_ref, nu_ref, slot_ref, next_ref, xs_ref, wg_ref, wu_ref, wd_ref, ys_ref,
                   wgf, wuf, wdf, wgb, wub, wdb, wsem):
    j = pl.program_id(0)

    def weight_copies(e, s):
        return [pltpu.make_async_copy(w_ref.at[e], buf.at[s], wsem.at[s])
                for w_ref, buf in ((wg_ref, wgf), (wu_ref, wuf), (wd_ref, wdf))]

    @pl.when(j < nu_ref[0])
    def _():
        e, s = te_ref[j], slot_ref[j]

        @pl.when(j == 0)
        def _():
            for cp in weight_copies(e, s):
                cp.start()

        @pl.when((j == 0) | (e != te_ref[jnp.maximum(j - 1, 0)]))
        def _():
            for cp in weight_copies(e, s):
                cp.wait()
            nxt = next_ref[j]

            @pl.when(nxt >= 0)
            def _():
                for cp in weight_copies(nxt, 1 - s):
                    cp.start()

            wgb[...] = wgf[s].astype(BF16)
            wub[...] = wuf[s].astype(BF16)
            wdb[...] = wdf[s].astype(BF16)

        xa, xb = _unpack_halves(xs_ref[...], BF16)
        half = D_MODEL // 2
        g = _dot(xa, wgb[:half, :]) + _dot(xb, wgb[half:, :])
        u = _dot(xa, wub[:half, :]) + _dot(xb, wub[half:, :])
        a = (g / (1.0 + jnp.exp(-g))) * u
        ys_ref[...] = _pack_halves(_dot(a.astype(BF16), wdb[...]))


def _experts(tile_expert, n_used, weight_slot, next_expert, xs, wg, wu, wd):
    n_tiles = xs.shape[0] // TM_MOE
    tile = lambda j, te, nu, *_: (jnp.minimum(j, nu[0] - 1), 0)
    hbm = pl.BlockSpec(memory_space=pl.ANY)
    return pl.pallas_call(
        _expert_kernel,
        grid_spec=pltpu.PrefetchScalarGridSpec(
            num_scalar_prefetch=4, grid=(n_tiles,),
            in_specs=[pl.BlockSpec((TM_MOE, D_MODEL // 2), tile), hbm, hbm, hbm],
            out_specs=pl.BlockSpec((TM_MOE, D_MODEL // 2), tile),
            scratch_shapes=[pltpu.VMEM((2, D_MODEL, D_EXPERT), F32), pltpu.VMEM((2, D_MODEL, D_EXPERT), F32),
                            pltpu.VMEM((2, D_EXPERT, D_MODEL), F32),
                            pltpu.VMEM((D_MODEL, D_EXPERT), BF16), pltpu.VMEM((D_MODEL, D_EXPERT), BF16),
                            pltpu.VMEM((D_EXPERT, D_MODEL), BF16), pltpu.SemaphoreType.DMA((2,))]),
        out_shape=jax.ShapeDtypeStruct(xs.shape, jnp.uint32),
        compiler_params=_cparams("arbitrary"), name="experts",
    )(tile_expert, n_used, weight_slot, next_expert, xs, wg, wu, wd)


def _combine_kernel(pos_ref, x1_ref, route_ref, mod_ref, ys_ref, o_ref, ybuf, sem, *, tc, n_tokens, tok0):
    i = pl.program_id(0)
    slot = i % 2

    def gather(tile, s):
        _row_copies(pos_ref, n_tokens, tok0 + tile * tc, tc, lambda k, r, p: pltpu.make_async_copy(
            ys_ref.at[pl.ds(p, 1)], ybuf.at[s, k, pl.ds(r, 1)], sem.at[s]))

    @pl.when(i == 0)
    def _():
        gather(0, 0)

    @pl.when(i + 1 < pl.num_programs(0))
    def _():
        gather(i + 1, 1 - slot)

    for k in range(2):
        pltpu.make_async_copy(ys_ref.at[pl.ds(0, tc)], ybuf.at[slot, k], sem.at[slot]).wait()
    route = route_ref[...]
    w1, w2 = route[:, 2:3], route[:, 3:4]
    half = D_MODEL // 2
    for h, (y1, y2) in enumerate(zip(_unpack_halves(ybuf[slot, 0], F32), _unpack_halves(ybuf[slot, 1], F32))):
        cols = slice(h * half, (h + 1) * half)
        gate2 = mod_ref[0, :, 5 * D_MODEL + h * half:5 * D_MODEL + (h + 1) * half]
        o_ref[:, cols] = x1_ref[:, cols] + gate2 * (w1 * y1 + w2 * y2)


def _combine(pos, x1, route, mod3, row_of_tile, ys, tok0, n_tokens, tc=256):
    t = x1.shape[0]
    row = lambda i, *_: (i, 0)
    return pl.pallas_call(
        functools.partial(_combine_kernel, tc=tc, n_tokens=n_tokens, tok0=tok0),
        grid_spec=pltpu.PrefetchScalarGridSpec(
            num_scalar_prefetch=1, grid=(t // tc,),
            in_specs=[pl.BlockSpec((tc, D_MODEL), row), pl.BlockSpec((tc, LANES), row),
                      pl.BlockSpec((1, 1, N_MOD * D_MODEL), lambda i, *_: (row_of_tile(i), 0, 0)),
                      pl.BlockSpec(memory_space=pl.ANY)],
            out_specs=pl.BlockSpec((tc, D_MODEL), row),
            scratch_shapes=[pltpu.VMEM((2, 2, tc, ys.shape[1]), ys.dtype), pltpu.SemaphoreType.DMA((2,))]),
        out_shape=jax.ShapeDtypeStruct((t, D_MODEL), F32),
        compiler_params=_cparams("arbitrary"), name="combine",
    )(pos, x1, route, mod3, ys)


def _rope_tables(seq):
    pos = np.arange(seq)
    n_freq = ATT_HEAD_DIM // 4
    inv = ROPE_THETA ** (-np.arange(n_freq, dtype=np.float32) / n_freq)
    ang = np.concatenate([(pos // GRID_W).astype(np.float32)[:, None] * inv,
                          (pos % GRID_W).astype(np.float32)[:, None] * inv], axis=-1).astype(np.float32)
    ang = jnp.asarray(ang)
    cos, sin = jnp.cos(ang), jnp.sin(ang)
    cos_t = jnp.repeat(cos, 2, axis=1)
    sin_t = jnp.stack([-sin, sin], axis=-1).reshape(seq, ATT_HEAD_DIM)
    return cos_t, sin_t


def _pad_lanes(a):
    return jnp.pad(a, ((0, 0), (0, LANES - a.shape[1])))


def _layer(x2d, n_batch, seq, mod3, row_of_tile, lw, *, rope_tabs, cache, state, emit, h2_all, tile0, t_total):
    (n1, n2, w_main, wg_cat, bg, qg, kg, mg, w_out, rw_cat, rb) = lw
    cos_t, sin_t = rope_tabs
    outs = _project(x2d, mod3, row_of_tile(256), n1, w_main, wg_cat, bg, qg, kg, cos_t, sin_t,
                    rope=cache is not None, emit_kv=emit)
    q, k, v, qm, kmt, vm, om, rowq, colq = outs[:9]
    attn = _attention(q, k, v, n_batch, seq, cache=cache, tq=min(seq, 512))
    ml = _mlstm(qm, kmt, vm, rowq, colq, n_batch, seq, state=state, emit_state=emit)
    mixed = _mix_out(attn, ml[0], om, x2d, mod3, row_of_tile(MIX_TILE), mg, w_out, n2, rw_cat, rb,
                     h2_all, tile0, t_total, tm=MIX_TILE)
    return mixed, outs[9:], ml[1:]


def kernel(x_prompt, x_sample, cache_k, cache_v, state_C, state_n, state_m, c, c_ctx, mod_w, mod_b, norm1_g, norm2_g,
           w_in, b_gates, q_norm_g, k_norm_g, mlstm_norm_g, w_out, router_group_w, router_group_b, router_expert_w,
           router_expert_b, expert_w_gate, expert_w_up, expert_w_down):
    assert mod_w.shape[0] == 1, "single-layer stack"
    n_ctx, s_ctx, _ = x_prompt.shape
    n_lat, s_lat, _ = x_sample.shape
    t_ctx, t_lat = n_ctx * s_ctx, n_lat * s_lat
    t_all = t_ctx + t_lat
    ctx_row = n_lat

    cond = jnp.concatenate([c, c_ctx[None], jnp.zeros((8 - n_lat - 1, D_MODEL), F32)], axis=0)
    mod3 = _modulation(cond, mod_w[0], mod_b[0][None]).reshape(8, 1, N_MOD * D_MODEL)

    rw =jnp.concatenate([router_group_w[0], jnp.moveaxis(router_expert_w[0], 0, 1).reshape(D_MODEL, N_EXPERTS)], axis=1)
    rb = _pad_lanes(jnp.concatenate([router_group_b[0], router_expert_b[0].reshape(-1)])[None])
    w_in_t = w_in[0].T
    lw = (norm1_g, norm2_g, _cast_bf16_t(w_in_t, MAIN_WIDTH), _gate_cols(w_in_t), _pad_lanes(b_gates),
          q_norm_g, k_norm_g, mlstm_norm_g, _cast_bf16(w_out, D_MODEL), _hi_lo_cat(_pad_lanes(rw)), rb)
    rope_tabs = _rope_tables(s_lat)

    ctx_rows = lambda tm: (lambda i: ctx_row)
    lat_rows = lambda tm: (lambda i: i // (s_lat // tm))
    (x1p, h2_all, routep), (ka, va), (s_c, s_n, s_m) = _layer(
        x_prompt.reshape(t_ctx, D_MODEL), n_ctx, s_ctx, mod3, ctx_rows, lw,
        rope_tabs=rope_tabs, cache=None, state=None, emit=True, h2_all=None, tile0=0, t_total=t_all)

    caug0 = jnp.concatenate([state_C[:, 0], state_n[:, 0][..., None],
                             jnp.zeros(state_n[:, 0].shape + (LANES - 1,), F32)], axis=-1)
    caug0 = caug0.reshape(n_lat, N_CHAINS, MLSTM_HEAD_DIM, AUG)
    past = cache_k.shape[2]
    cache = (cache_k[:, 0].reshape(n_lat * past, KV_WIDTH), cache_v[:, 0].reshape(n_lat * past, KV_WIDTH))
    (x1s, h2_all, routes), _, _ = _layer(
        x_sample.reshape(t_lat, D_MODEL), n_lat, s_lat, mod3, lat_rows, lw,
        rope_tabs=rope_tabs, cache=cache, state=(caug0, state_m[:, 0].reshape(-1)), emit=False,
        h2_all=h2_all, tile0=t_ctx // MIX_TILE, t_total=t_all)

    route = jnp.concatenate([routep, routes], axis=0)
    ranks, counts = _ranks(route)
    pos, tile_plan, tails, n_rows = _routing_plan(route, ranks, counts)
    xs = _dispatch(pos, tails, h2_all, n_rows)
    y_sorted = _experts(*tile_plan, xs, expert_w_gate[0], expert_w_up[0], expert_w_down[0])
    yp = _combine(pos, x1p, routep, mod3, ctx_rows(256), y_sorted, 0, t_all)
    ys = _combine(pos, x1s, routes, mod3, lat_rows(256), y_sorted, t_ctx, t_all)

    kv_shape = (n_ctx, 1, s_ctx, ATT_KV_HEADS, ATT_HEAD_DIM)
    return (yp.reshape(x_prompt.shape), ys.reshape(x_sample.shape), ka.reshape(kv_shape), va.reshape(kv_shape),
            s_c.reshape(n_ctx, 1, 2, MLSTM_HEADS, MLSTM_HEAD_DIM, MLSTM_HEAD_DIM),
            s_n.reshape(n_ctx, 1, 2, MLSTM_HEADS, MLSTM_HEAD_DIM), s_m[..., 0, 0].reshape(n_ctx, 1, 2, MLSTM_HEADS))
```

```python
import functools

import numpy as np
import jax
import jax.numpy as jnp
from jax import lax
from jax.experimental import pallas as pl
from jax.experimental.pallas import tpu as pltpu

F32 = jnp.float32
BF16 = jnp.bfloat16

D_MODEL = 2048
GRID_W = 64
ATT_HEADS = 8
ATT_KV_HEADS = 2
ATT_HEAD_DIM = 128
ATT_GROUP = ATT_HEADS // ATT_KV_HEADS
ATT_WIDTH = ATT_HEADS * ATT_HEAD_DIM
KV_WIDTH = ATT_KV_HEADS * ATT_HEAD_DIM
ROPE_THETA = 10000.0
MLSTM_HEADS = 4
MLSTM_HEAD_DIM = 256
MLSTM_WIDTH = MLSTM_HEADS * MLSTM_HEAD_DIM
CHUNK = 256
N_GATES = 4 * MLSTM_HEADS
N_CHAINS = 2 * MLSTM_HEADS
MAIN_WIDTH = ATT_WIDTH + 2 * KV_WIDTH + 4 * MLSTM_WIDTH
N_GROUPS = 4
EXPERTS_PER_GROUP = 8
N_EXPERTS = N_GROUPS * EXPERTS_PER_GROUP
D_EXPERT = 512
N_MOD = 6
TM_MOE = 256
MIX_TILE = 512
ATT_KEY_CHUNK = 256
Q_SCALE = ATT_HEAD_DIM ** -0.5 * float(np.log2(np.e))
EPS = 1e-6

LANES = 128
AUG = MLSTM_HEAD_DIM + LANES
N_SCAN_ROWS = 5 * N_CHAINS
ROUTER_LANE0 = N_GROUPS
VMEM_LIMIT = 56 * 1024 * 1024


def _cparams(*sem):
    return pltpu.CompilerParams(dimension_semantics=sem, vmem_limit_bytes=VMEM_LIMIT)


def _const_spec(shape):
    nd = len(shape)
    return pl.BlockSpec(shape, lambda *_: (0,) * nd, pipeline_mode=pl.Buffered(1))


def _split_hi_lo(x):
    hi = x.astype(BF16)
    lo = (x - hi.astype(F32)).astype(BF16)
    return hi, lo


def _dot(a, b):
    return jnp.dot(a, b, preferred_element_type=F32)


def _hi_lo_cat(w):
    return jnp.concatenate(_split_hi_lo(w), axis=1)


def _dot_hi_lo(a_hi, a_lo, w_ref):
    r = _dot(a_hi, w_ref[...])
    return r[:, :LANES] + r[:, LANES:] + _dot(a_lo, w_ref[:, :LANES])


def _rms(x, g):
    return x * lax.rsqrt(jnp.mean(x * x, axis=-1, keepdims=True) + EPS) * g


def _mod_kernel(c_ref, w_ref, b_ref, o_ref):
    c = c_ref[...]
    s = c / (1.0 + jnp.exp(-c))
    s_hi = s.astype(BF16).astype(F32)
    lhs = jnp.concatenate([s_hi, s - s_hi], axis=0).astype(BF16)
    w_hi, w_lo = _split_hi_lo(w_ref[...])
    r = _dot(lhs, w_hi)
    r2 = _dot(lhs, w_lo)
    o_ref[...] = r[:8] + r[8:] + r2[:8] + b_ref[...]


def _modulation(cond, mod_w, mod_b):
    n = mod_w.shape[1]
    tn = 1024
    return pl.pallas_call(
        _mod_kernel,
        grid=(n // tn,),
        in_specs=[pl.BlockSpec((8, D_MODEL), lambda j: (0, 0)),
                  pl.BlockSpec((D_MODEL, tn), lambda j: (0, j)),
                  pl.BlockSpec((1, tn), lambda j: (0, j))],
        out_specs=pl.BlockSpec((8, tn), lambda j: (0, j)),
        out_shape=jax.ShapeDtypeStruct((8, n), F32),
        compiler_params=_cparams("parallel"),
        name="modulation",
    )(cond, mod_w, mod_b)


def _cast_kernel(w_ref, o_ref):
    o_ref[...] = w_ref[0].astype(BF16)


def _cast_bf16(w, n_cols, tn=512):
    rows = w.shape[1]
    return pl.pallas_call(
        _cast_kernel, grid=(n_cols // tn,),
        in_specs=[pl.BlockSpec((1, rows, tn), lambda j: (0, 0, j))],
        out_specs=pl.BlockSpec((rows, tn), lambda j: (0, j)),
        out_shape=jax.ShapeDtypeStruct((rows, n_cols), BF16),
        compiler_params=_cparams("parallel"), name="cast_bf16",
    )(w)


def _cast_t_kernel(wt_ref, o_ref):
    o_ref[...] = wt_ref[...].T.astype(BF16)


def _cast_bf16_t(wt, n_cols, tn=512):
    rows = wt.shape[1]
    return pl.pallas_call(
        _cast_t_kernel, grid=(n_cols // tn,),
        in_specs=[pl.BlockSpec((tn, rows), lambda j: (j, 0))],
        out_specs=pl.BlockSpec((rows, tn), lambda j: (0, j)),
        out_shape=jax.ShapeDtypeStruct((rows, n_cols), BF16),
        compiler_params=_cparams("parallel"), name="cast_bf16_t",
    )(wt)


def _gate_cols_kernel(wt_ref, o_ref):
    sub = lax.broadcasted_iota(jnp.int32, wt_ref.shape, 0)
    hi, lo = _split_hi_lo(jnp.where(sub < N_GATES, wt_ref[...], 0.0).T)
    o_ref[:, :LANES] = hi
    o_ref[:, LANES:] = lo


def _gate_cols(wt):
    rows = wt.shape[1]
    return pl.pallas_call(
        _gate_cols_kernel, grid=(1,),
        in_specs=[pl.BlockSpec((LANES, rows), lambda i: (MAIN_WIDTH // LANES, 0))],
        out_specs=pl.BlockSpec((rows, 2 * LANES), lambda i: (0, 0)),
        out_shape=jax.ShapeDtypeStruct((rows, 2 * LANES), BF16),
        compiler_params=_cparams("arbitrary"), name="gate_cols",
    )(wt)


def _pair_swap(x):
    lane = lax.broadcasted_iota(jnp.int32, x.shape, 1)
    return jnp.where((lane & 1) == 0, pltpu.roll(x, LANES - 1, 1), pltpu.roll(x, 1, 1))


def _proj_kernel(x_ref, mod_ref, n1_ref, w_ref, wg_ref, bg_ref, qg_ref, kg_ref, cos_ref, sin_ref,
                 q_ref, k_ref, v_ref, qm_ref, kmt_ref, vm_ref, om_ref, row_ref, col_ref, *kv_refs, rope, tm):
    mod = mod_ref[0]
    shift, scale = mod[:, :D_MODEL], mod[:, D_MODEL:2 * D_MODEL]
    h = _rms(x_ref[...], n1_ref[...]) * (1.0 + scale) + shift
    h_hi, h_lo = _split_hi_lo(h)

    g = _dot_hi_lo(h_hi, h_lo, wg_ref) + bg_ref[...]
    for cc in range(tm // CHUNK):
        rows, cols = _gate_scan_chunk(g[cc * CHUNK:(cc + 1) * CHUNK, :])
        row_ref[cc] = rows
        col_ref[cc * CHUNK:(cc + 1) * CHUNK, :] = cols

    def rot(seg):
        return seg * cos_ref[...] + _pair_swap(seg) * sin_ref[...] if rope else seg

    qa = _dot(h_hi, w_ref[:, :ATT_WIDTH])
    for hh in range(ATT_HEADS):
        sl = slice(hh * ATT_HEAD_DIM, (hh + 1) * ATT_HEAD_DIM)
        seg = rot(_rms(qa[:, sl], qg_ref[...]))
        q_ref[:, sl] = (seg * Q_SCALE).astype(BF16)

    kv = _dot(h_hi, w_ref[:, ATT_WIDTH:ATT_WIDTH + 2 * KV_WIDTH])
    for hh in range(ATT_KV_HEADS):
        sl = slice(hh * ATT_HEAD_DIM, (hh + 1) * ATT_HEAD_DIM)
        seg = _rms(kv[:, sl], kg_ref[...])
        if kv_refs:
            kv_refs[0][:, sl] = seg
        k_ref[:, sl] = rot(seg).astype(BF16)
    va = kv[:, KV_WIDTH:]
    if kv_refs:
        kv_refs[1][...] = va
    v_ref[...] = va.astype(BF16)

    c0 = ATT_WIDTH + 2 * KV_WIDTH
    qm_ref[...] = (_dot(h_hi, w_ref[:, c0:c0 + MLSTM_WIDTH]) * MLSTM_HEAD_DIM ** -0.5).astype(BF16)
    km = _dot(h_hi, w_ref[:, c0 + MLSTM_WIDTH:c0 + 2 * MLSTM_WIDTH])
    for cc in range(tm // CHUNK):
        kmt_ref[cc] = km[cc * CHUNK:(cc + 1) * CHUNK, :].T.astype(BF16)
    vm_ref[...] = _dot(h_hi, w_ref[:, c0 + 2 * MLSTM_WIDTH:c0 + 3 * MLSTM_WIDTH]).astype(BF16)
    om_ref[...] = _dot(h_hi, w_ref[:, c0 + 3 * MLSTM_WIDTH:c0 + 4 * MLSTM_WIDTH]).astype(BF16)


def _project(x2d, mod3, row_of_tile, n1, w_main, wg_cat, bg, qg, kg, cos_t, sin_t, *, rope, emit_kv, tm=256):
    t = x2d.shape[0]
    n_pos = cos_t.shape[0] // tm
    row = lambda i: (i, 0)
    in_specs = [pl.BlockSpec((tm, D_MODEL), row),
                pl.BlockSpec((1, 1, N_MOD * D_MODEL), lambda i: (row_of_tile(i), 0, 0)),
                _const_spec((1, D_MODEL)),
                _const_spec((D_MODEL, MAIN_WIDTH)),
                _const_spec((D_MODEL, 2 * LANES)), _const_spec((1, LANES)),
                _const_spec((1, ATT_HEAD_DIM)), _const_spec((1, ATT_HEAD_DIM)),
                pl.BlockSpec((tm, ATT_HEAD_DIM), lambda i: (i % n_pos, 0)),
                pl.BlockSpec((tm, ATT_HEAD_DIM), lambda i: (i % n_pos, 0))]
    out_shape = [jax.ShapeDtypeStruct((t, ATT_WIDTH), BF16), jax.ShapeDtypeStruct((t, KV_WIDTH), BF16),
                 jax.ShapeDtypeStruct((t, KV_WIDTH), BF16), jax.ShapeDtypeStruct((t, MLSTM_WIDTH), BF16),
                 jax.ShapeDtypeStruct((t // CHUNK, MLSTM_WIDTH, CHUNK), BF16),
                 jax.ShapeDtypeStruct((t, MLSTM_WIDTH), BF16), jax.ShapeDtypeStruct((t, MLSTM_WIDTH), BF16),
                 jax.ShapeDtypeStruct((t // CHUNK, N_SCAN_ROWS, CHUNK), F32), jax.ShapeDtypeStruct((t, LANES), F32)]
    out_specs = [pl.BlockSpec((tm, ATT_WIDTH), row), pl.BlockSpec((tm, KV_WIDTH), row),
                 pl.BlockSpec((tm, KV_WIDTH), row), pl.BlockSpec((tm, MLSTM_WIDTH), row),
                 pl.BlockSpec((tm // CHUNK, MLSTM_WIDTH, CHUNK), lambda i: (i, 0, 0)),
                 pl.BlockSpec((tm, MLSTM_WIDTH), row), pl.BlockSpec((tm, MLSTM_WIDTH), row),
                 pl.BlockSpec((tm // CHUNK, N_SCAN_ROWS, CHUNK), lambda i: (i, 0, 0)), pl.BlockSpec((tm, LANES), row)]
    if emit_kv:
        out_shape += [jax.ShapeDtypeStruct((t, KV_WIDTH), F32)] * 2
        out_specs += [pl.BlockSpec((tm, KV_WIDTH), row)] * 2
    return pl.pallas_call(
        functools.partial(_proj_kernel, rope=rope, tm=tm),
        grid=(t // tm,), in_specs=in_specs, out_specs=out_specs, out_shape=out_shape,
        compiler_params=_cparams("parallel"), name="in_proj",
    )(x2d, mod3, n1, w_main, wg_cat, bg, qg, kg, cos_t, sin_t)


def _lane_scan(x, op, fill, is_fwd, lane):
    s = 1
    while s < CHUNK:
        from_left = jnp.where(lane >= s, pltpu.roll(x, s, 1), fill)
        from_right = jnp.where(lane < CHUNK - s, pltpu.roll(x, CHUNK - s, 1), fill)
        x = op(x, jnp.where(is_fwd, from_left, from_right))
        s *= 2
    return x


def _gate_scan_chunk(g):
    lane = lax.broadcasted_iota(jnp.int32, (N_CHAINS, CHUNK), 1)
    is_fwd = lax.broadcasted_iota(jnp.int32, (N_CHAINS, CHUNK), 0) < MLSTM_HEADS
    gt = g.T
    fwd, bwd = gt[:N_CHAINS], gt[N_CHAINS:2 * N_CHAINS]
    li = jnp.where(is_fwd, fwd, pltpu.roll(bwd, MLSTM_HEADS, 0))
    f = jnp.where(is_fwd, pltpu.roll(fwd, MLSTM_HEADS, 0), bwd)
    lf = jnp.minimum(f, 0.0) - jnp.log(1.0 + jnp.exp(-jnp.abs(f)))
    b = _lane_scan(lf, jnp.add, 0.0, is_fwd, lane)
    a = li - b
    run_max = _lane_scan(a, jnp.maximum, -jnp.inf, is_fwd, lane)
    all_max = jnp.broadcast_to(jnp.max(a, axis=1, keepdims=True), a.shape)
    last = jnp.where(is_fwd, CHUNK - 1, 0)
    total = jnp.broadcast_to(jnp.sum(jnp.where(lane == last, b, 0.0), axis=1, keepdims=True), a.shape)
    rows = jnp.concatenate([a, b, run_max, all_max, total], axis=0)
    padded = jnp.concatenate([rows, jnp.zeros((LANES - N_SCAN_ROWS, CHUNK), F32)], axis=0)
    return rows, padded.T


def _mlstm_kernel(*refs, n_chunks, has_state, emit_state):
    it = iter(refs)
    q_ref, kt_ref, v_ref, row_ref, col_ref = [next(it) for _ in range(5)]
    c0_ref, m0_ref = (next(it), next(it)) if has_state else (None, None)
    o_ref = next(it)
    c_out, n_out, m_out = (next(it), next(it), next(it)) if emit_state else (None, None, None)
    caug, hb = next(it), next(it)

    fresh = not has_state and n_chunks == 1
    start = lambda cc: cc * CHUNK if isinstance(cc, int) else pl.multiple_of(cc * CHUNK, CHUNK)
    b = pl.program_id(0)
    if has_state:
        caug[...] = c0_ref[0]
        m_init = tuple(jnp.full((1, 1), m0_ref[b * N_CHAINS + r], F32) for r in range(N_CHAINS))
    else:
        if not fresh:
            caug[...] = jnp.zeros(caug.shape, F32)
        m_init = tuple(jnp.zeros((1, 1), F32) for _ in range(N_CHAINS))

    sub = lax.broadcasted_iota(jnp.int32, (CHUNK, CHUNK), 0)
    lane = lax.broadcasted_iota(jnp.int32, (CHUNK, CHUNK), 1)
    ones_col = (lax.broadcasted_iota(jnp.int32, (CHUNK, LANES), 1) == 0).astype(BF16)

    def chain(r, cc, m_prev):
        d, hd = divmod(r, MLSTM_HEADS)
        t0 = start(cc)
        hs = slice(hd * MLSTM_HEAD_DIM, (hd + 1) * MLSTM_HEAD_DIM)
        rows = row_ref[cc]
        cols = col_ref[pl.ds(t0, CHUNK), :]
        row = lambda k: rows[k * N_CHAINS + r:k * N_CHAINS + r + 1, :]
        col = lambda k: cols[:, k * N_CHAINS + r:k * N_CHAINS + r + 1]
        q = q_ref[pl.ds(t0, CHUNK), hs]
        kt = kt_ref[cc, hs, :]
        vaug = jnp.concatenate([v_ref[pl.ds(t0, CHUNK), hs], ones_col], axis=1)

        m_col = jnp.maximum(m_prev, col(2))
        keep = (lane <= sub) if d == 0 else (lane >= sub)
        w = jnp.where(keep, jnp.exp(row(0) - m_col), 0.0)
        w_inter = jnp.exp(m_prev - m_col)
        p = (_dot(q, kt) * w).astype(BF16)
        intra = _dot(p, vaug)
        num, den = intra[:, :MLSTM_HEAD_DIM], intra[:, MLSTM_HEAD_DIM:MLSTM_HEAD_DIM + 1]
        if not fresh:
            inter = _dot(q, caug[r].astype(BF16))
            num = num + w_inter * inter[:, :MLSTM_HEAD_DIM]
            den = den + w_inter * inter[:, MLSTM_HEAD_DIM:MLSTM_HEAD_DIM + 1]
        h_out = num / jnp.maximum(jnp.abs(den), jnp.exp(-(col(1) + m_col)))

        m_last = jnp.maximum(m_prev, row(3))
        kw = (kt.astype(F32) * jnp.exp(row(0) - m_last)).astype(BF16)
        update = _dot(kw, vaug)
        caug[r] = update if fresh else jnp.exp(m_prev - m_last[:, :1]) * caug[r] + update
        return h_out, (row(4) + m_last)[:, :1]

    def body(c, ms):
        new = []
        for r in range(N_CHAINS):
            cc = c if r < MLSTM_HEADS else n_chunks - 1 - c
            h_out, m_new = chain(r, cc, ms[r])
            hd = r % MLSTM_HEADS
            dst = o_ref if r < MLSTM_HEADS else hb
            dst[pl.ds(start(cc), CHUNK), hd * MLSTM_HEAD_DIM:(hd + 1) * MLSTM_HEAD_DIM] = h_out
            new.append(m_new)
        return tuple(new)

    ms = body(0, m_init) if n_chunks == 1 else lax.fori_loop(0, n_chunks, body, m_init)
    o_ref[...] += hb[...]
    if emit_state:
        for r in range(N_CHAINS):
            c_out[0, r] = caug[r, :, :MLSTM_HEAD_DIM]
            n_out[0, r] = caug[r, :, MLSTM_HEAD_DIM:].T[:1, :]
            m_out[0, r] = jnp.broadcast_to(ms[r], (1, LANES))


def _mlstm(qm, kmt, vm, rowq, colq, n_batch, seq, state=None, emit_state=False):
    nc = seq // CHUNK
    mode = dict(pipeline_mode=pl.Buffered(1)) if seq * MLSTM_WIDTH * 4 > (2 << 20) else {}
    in_specs = [pl.BlockSpec((seq, MLSTM_WIDTH), lambda b: (b, 0), **mode),
                pl.BlockSpec((nc, MLSTM_WIDTH, CHUNK), lambda b: (b, 0, 0), **mode),
                pl.BlockSpec((seq, MLSTM_WIDTH), lambda b: (b, 0), **mode),
                pl.BlockSpec((nc, N_SCAN_ROWS, CHUNK), lambda b: (b, 0, 0), **mode),
                pl.BlockSpec((seq, LANES), lambda b: (b, 0), **mode)]
    args = [qm, kmt, vm, rowq, colq]
    if state is not None:
        in_specs += [pl.BlockSpec((1, N_CHAINS, MLSTM_HEAD_DIM, AUG), lambda b: (b, 0, 0, 0), **mode),
                     pl.BlockSpec(memory_space=pltpu.SMEM)]
        args += list(state)
    out_shape = [jax.ShapeDtypeStruct((n_batch * seq, MLSTM_WIDTH), F32)]
    out_specs = [pl.BlockSpec((seq, MLSTM_WIDTH), lambda b: (b, 0))]
    if emit_state:
        out_shape += [jax.ShapeDtypeStruct((n_batch, N_CHAINS, MLSTM_HEAD_DIM, MLSTM_HEAD_DIM), F32),
                      jax.ShapeDtypeStruct((n_batch, N_CHAINS, 1, MLSTM_HEAD_DIM), F32),
                      jax.ShapeDtypeStruct((n_batch, N_CHAINS, 1, LANES), F32)]
        out_specs += [pl.BlockSpec((1, N_CHAINS, MLSTM_HEAD_DIM, MLSTM_HEAD_DIM), lambda b: (b, 0, 0, 0)),
                      pl.BlockSpec((1, N_CHAINS, 1, MLSTM_HEAD_DIM), lambda b: (b, 0, 0, 0)),
                      pl.BlockSpec((1, N_CHAINS, 1, LANES), lambda b: (b, 0, 0, 0))]
    return pl.pallas_call(
        functools.partial(_mlstm_kernel, n_chunks=nc, has_state=state is not None, emit_state=emit_state),
        grid=(n_batch,), in_specs=in_specs, out_specs=out_specs, out_shape=out_shape,
        scratch_shapes=[pltpu.VMEM((N_CHAINS, MLSTM_HEAD_DIM, AUG), F32), pltpu.VMEM((seq, MLSTM_WIDTH), F32)],
        compiler_params=_cparams("parallel"), name="mlstm",
    )(*args)


def _attn_kernel(*refs, tq, has_cache):
    if has_cache:
        q_ref, k_ref, v_ref, ck_ref, cv_ref, o_ref = refs
    else:
        q_ref, k_ref, v_ref, o_ref = refs
    q = q_ref[...]
    qs = jnp.concatenate([q[:, g * ATT_HEAD_DIM:(g + 1) * ATT_HEAD_DIM] for g in range(ATT_GROUP)], axis=0)
    nt = (((1,), (1,)), ((), ()))
    seq = k_ref.shape[0]
    kc = min(seq, ATT_KEY_CHUNK)
    chunks = [(k_ref, v_ref, c * kc) for c in range(seq // kc)]
    if has_cache:
        chunks.insert(0, (ck_ref, cv_ref, None))
    m = l = o = None
    for kr, vr, start in chunks:
        kk, vv = (kr[...], vr[...]) if start is None else (kr[start:start + kc, :], vr[start:start + kc, :])
        s = lax.dot_general(qs, kk.astype(BF16), nt, preferred_element_type=F32)
        mc = jnp.max(s, axis=-1, keepdims=True)
        if m is None:
            m = mc
            p = jnp.exp2(s - m)
            l = jnp.sum(p, axis=-1, keepdims=True)
            o = _dot(p.astype(BF16), vv.astype(BF16))
        else:
            m_new = jnp.maximum(m, mc)
            alpha = jnp.exp2(m - m_new)
            p = jnp.exp2(s - m_new)
            l = alpha * l + jnp.sum(p, axis=-1, keepdims=True)
            o = alpha * o + _dot(p.astype(BF16), vv.astype(BF16))
            m = m_new
    o = o / l
    for g in range(ATT_GROUP):
        o_ref[:, g * ATT_HEAD_DIM:(g + 1) * ATT_HEAD_DIM] = o[g * tq:(g + 1) * tq].astype(BF16)


def _attention(q, k, v, n_batch, seq, cache=None, tq=256):
    nqb = seq // tq
    gw = ATT_GROUP * ATT_HEAD_DIM
    in_specs = [pl.BlockSpec((tq, gw), lambda b, h, i: (b * nqb + i, h)),
                pl.BlockSpec((seq, ATT_HEAD_DIM), lambda b, h, i: (b, h)),
                pl.BlockSpec((seq, ATT_HEAD_DIM), lambda b, h, i: (b, h))]
    args = [q, k, v]
    if cache is not None:
        past = cache[0].shape[0] // n_batch
        in_specs += [pl.BlockSpec((past, ATT_HEAD_DIM), lambda b, h, i: (b, h))] * 2
        args += list(cache)
    return pl.pallas_call(
        functools.partial(_attn_kernel, tq=tq, has_cache=cache is not None),
        grid=(n_batch, ATT_KV_HEADS, nqb), in_specs=in_specs,
        out_specs=pl.BlockSpec((tq, gw), lambda b, h, i: (b * nqb + i, h)),
        out_shape=jax.ShapeDtypeStruct((n_batch * seq, ATT_WIDTH), BF16),
        compiler_params=_cparams("parallel", "parallel", "parallel"), name="attention",
    )(*args)


def _mix_kernel(attn_ref, hm_ref, om_ref, x_ref, mod_ref, mg_ref, wo_ref, n2_ref, rw_ref, rb_ref, *rest, sub):
    for r0 in range(0, x_ref.shape[0], sub):
        _mix_rows(slice(r0, r0 + sub), attn_ref, hm_ref, om_ref, x_ref, mod_ref, mg_ref, wo_ref, n2_ref, rw_ref,
                  rb_ref, *rest[-3:])


def _mix_rows(rows, attn_ref, hm_ref, om_ref, x_ref, mod_ref, mg_ref, wo_ref, n2_ref, rw_ref, rb_ref,
              x1_ref, h2_ref, route_ref):
    mod = mod_ref[0]
    gate1 = mod[:, 2 * D_MODEL:3 * D_MODEL]
    shift2, scale2 = mod[:, 3 * D_MODEL:4 * D_MODEL], mod[:, 4 * D_MODEL:5 * D_MODEL]
    hm = hm_ref[rows, :]
    mg = mg_ref[...]
    parts = []
    for hd in range(MLSTM_HEADS):
        sl = slice(hd * MLSTM_HEAD_DIM, (hd + 1) * MLSTM_HEAD_DIM)
        parts.append(_rms(hm[:, sl], mg[:, sl]))
    om = om_ref[rows, :].astype(F32)
    hmg = jnp.concatenate(parts, axis=1) * (1.0 / (1.0 + jnp.exp(-om)))
    y = _dot(attn_ref[rows, :], wo_ref[:ATT_WIDTH, :]) + _dot(hmg.astype(BF16), wo_ref[ATT_WIDTH:, :])
    x1 = x_ref[rows, :] + gate1 * y
    x1_ref[rows, :] = x1
    h2 = _rms(x1, n2_ref[...]) * (1.0 + scale2) + shift2
    h2_hi, h2_lo = _split_hi_lo(h2)
    h2_ref[rows, :] = _pack_halves(h2)

    lg = _dot_hi_lo(h2_hi, h2_lo, rw_ref) + rb_ref[...]
    lane = lax.broadcasted_iota(jnp.int32, lg.shape, 1).astype(F32)
    neg = -jnp.inf
    first = lambda hit: jnp.min(jnp.where(hit, lane, float(LANES)), axis=-1, keepdims=True)
    gl = jnp.where(lane < N_GROUPS, lg, neg)
    gmax = jnp.max(gl, axis=-1, keepdims=True)
    grp = first(gl == gmax)
    p_grp = 1.0 / jnp.sum(jnp.exp(gl - gmax), axis=-1, keepdims=True)
    lo = ROUTER_LANE0 + grp * EXPERTS_PER_GROUP
    el = jnp.where((lane >= lo) & (lane < lo + EXPERTS_PER_GROUP), lg, neg)
    m1 = jnp.max(el, axis=-1, keepdims=True)
    i1 = first(el == m1)
    el2 = jnp.where(lane == i1, neg, el)
    m2 = jnp.max(el2, axis=-1, keepdims=True)
    i2 = first(el2 == m2)
    r = jnp.exp(m2 - m1)
    w1 = p_grp / (1.0 + r)
    w2 = w1 * r
    route_ref[rows, :] = jnp.where(lane == 0.0, i1 - ROUTER_LANE0, jnp.where(lane == 1.0, i2 - ROUTER_LANE0,
                                   jnp.where(lane == 2.0, w1, jnp.where(lane == 3.0, w2, 0.0))))


def _mix_out(attn, hm, om, x2d, mod3, row_of_tile, mg, w_out, n2, rw_cat, rb, h2_all, tile0, t_total, tm=512):
    t = x2d.shape[0]
    row = lambda i: (i, 0)
    in_specs = [pl.BlockSpec((tm, ATT_WIDTH), row), pl.BlockSpec((tm, MLSTM_WIDTH), row),
                pl.BlockSpec((tm, MLSTM_WIDTH), row), pl.BlockSpec((tm, D_MODEL), row),
                pl.BlockSpec((1, 1, N_MOD * D_MODEL), lambda i: (row_of_tile(i), 0, 0)),
                _const_spec((1, MLSTM_WIDTH)), _const_spec((D_MODEL, D_MODEL)), _const_spec((1, D_MODEL)),
                _const_spec((D_MODEL, 2 * LANES)), _const_spec((1, LANES))]
    args = [attn, hm, om, x2d, mod3, mg, w_out, n2, rw_cat, rb]
    aliases = {}
    if h2_all is not None:
        aliases = {len(args): 1}
        in_specs.append(pl.BlockSpec(memory_space=pl.ANY))
        args.append(h2_all)
    return pl.pallas_call(
        functools.partial(_mix_kernel, sub=256), grid=(t // tm,), in_specs=in_specs,
        out_specs=[pl.BlockSpec((tm, D_MODEL), row), pl.BlockSpec((tm, D_MODEL // 2), lambda i: (tile0 + i, 0)),
                   pl.BlockSpec((tm, LANES), row)],
        out_shape=[jax.ShapeDtypeStruct((t, D_MODEL), F32), jax.ShapeDtypeStruct((t_total, D_MODEL // 2), jnp.uint32),
                   jax.ShapeDtypeStruct((t, LANES), F32)],
        input_output_aliases=aliases,
        compiler_params=_cparams("parallel"), name="mix_out",
    )(*args)


def _rank_kernel(route_ref, rank_ref, cnt_ref, run_ref, tri_ref):
    tr = route_ref.shape[0]

    @pl.when(pl.program_id(0) == 0)
    def _():
        run_ref[...] = jnp.zeros(run_ref.shape, F32)
        tri_ref[...] = (lax.broadcasted_iota(jnp.int32, (tr, tr), 1)
                        < lax.broadcasted_iota(jnp.int32, (tr, tr), 0)).astype(BF16)

    route = route_ref[...]
    lane = lax.broadcasted_iota(jnp.int32, route.shape, 1).astype(F32)
    hit1, hit2 = lane == route[:, 0:1], lane == route[:, 1:2]
    onehot = jnp.where(hit1, 1.0, jnp.where(hit2, 1.0, 0.0))
    before = _dot(tri_ref[...], onehot.astype(BF16)) + run_ref[0:1, :]
    r1 = jnp.sum(jnp.where(hit1, before, 0.0), axis=-1, keepdims=True)
    r2 = jnp.sum(jnp.where(hit2, before, 0.0), axis=-1, keepdims=True)
    rank_ref[...] = jnp.where(lane == 0.0, r1, jnp.where(lane == 1.0, r2, 0.0))
    run_ref[...] = run_ref[...] + jnp.sum(onehot, axis=0, keepdims=True)
    cnt_ref[...] = run_ref[...]


def _ranks(route, tr=512):
    t = route.shape[0]
    return pl.pallas_call(
        _rank_kernel, grid=(t // tr,),
        in_specs=[pl.BlockSpec((tr, LANES), lambda i: (i, 0))],
        out_specs=[pl.BlockSpec((tr, LANES), lambda i: (i, 0)), pl.BlockSpec((8, LANES), lambda i: (0, 0))],
        out_shape=[jax.ShapeDtypeStruct((t, LANES), F32), jax.ShapeDtypeStruct((8, LANES), F32)],
        scratch_shapes=[pltpu.VMEM((8, LANES), F32), pltpu.VMEM((tr, tr), BF16)],
        compiler_params=_cparams("arbitrary"), name="expert_ranks",
    )(route)


def _pos_kernel(route_ref, rank_ref, start_ref, o_ref):
    route, rank = route_ref[...], rank_ref[...]
    lane = lax.broadcasted_iota(jnp.int32, route.shape, 1).astype(F32)
    start = start_ref[...]
    first = lambda col: jnp.sum(jnp.where(lane == route[:, col:col + 1], start, 0.0), axis=-1, keepdims=True)
    p1 = first(0) + rank[:, 0:1]
    p2 = first(1) + rank[:, 1:2]
    tile = jnp.where(lane == 0.0, p1, jnp.where(lane == 1.0, p2, 0.0))
    o_ref[...] = tile.T[:8, :].astype(jnp.int32)


def _positions(route, ranks, starts, tr=512):
    t = route.shape[0]
    pos = pl.pallas_call(
        _pos_kernel, grid=(t // tr,),
        in_specs=[pl.BlockSpec((tr, LANES), lambda i: (i, 0)), pl.BlockSpec((tr, LANES), lambda i: (i, 0)),
                  pl.BlockSpec((1, LANES), lambda i: (0, 0))],
        out_specs=pl.BlockSpec((8, tr), lambda i: (0, i)),
        out_shape=jax.ShapeDtypeStruct((8, t), jnp.int32),
        compiler_params=_cparams("parallel"), name="pair_rows",
    )(route, ranks, starts)
    return pos[:2].reshape(-1)


def _routing_plan(route, ranks, counts):
    t = route.shape[0]
    n_tiles = 2 * t // TM_MOE + N_EXPERTS
    cnt = counts[0, :N_EXPERTS].astype(jnp.int32)
    padded = (cnt + TM_MOE - 1) // TM_MOE * TM_MOE
    ends = jnp.cumsum(padded)
    pos = _positions(route, ranks, _pad_lanes((ends - padded).astype(F32)[None]))
    n_used = (ends[-1:] // TM_MOE).astype(jnp.int32)
    tile_start = jnp.arange(n_tiles, dtype=jnp.int32) * TM_MOE
    tile_expert = jnp.sum((ends[None, :] <= tile_start[:, None]).astype(jnp.int32), axis=1)
    tile_expert = jnp.minimum(tile_expert, tile_expert[n_used[0] - 1]).astype(jnp.int32)
    tails = jnp.where(padded > 0, ends - TM_MOE, -1).astype(jnp.int32)
    change = jnp.concatenate([jnp.ones((1,), jnp.int32), (tile_expert[1:] != tile_expert[:-1]).astype(jnp.int32)])
    weight_slot = (jnp.cumsum(change) - 1) % 2
    ids = jnp.arange(N_EXPERTS, dtype=jnp.int32)
    later_used = (padded > 0)[None, :] & (ids[None, :] > ids[:, None])
    next_used = jnp.min(jnp.where(later_used, ids[None, :], N_EXPERTS), axis=1)
    next_used = jnp.where(next_used < N_EXPERTS, next_used, -1)
    next_expert = jnp.sum(jnp.where(tile_expert[:, None] == ids[None, :], next_used[None, :], 0), axis=1)
    return pos, (tile_expert, n_used, weight_slot.astype(jnp.int32), next_expert.astype(jnp.int32)), tails, \
        n_tiles * TM_MOE


def _row_copies(pos_ref, n_pairs, tok0, n_rows, make_copy):
    def body(r, carry):
        for k in range(2):
            make_copy(k, r, pos_ref[k * n_pairs + tok0 + r]).start()
        return carry
    lax.fori_loop(0, n_rows, body, 0, unroll=8)


def _dispatch_kernel(pos_ref, tail_ref, h_ref, xs_ref, hbuf, zero_ref, tsem, rsem, *, td, n_tokens):
    i = pl.program_id(0)
    last = pl.num_programs(0) - 1
    tile_copy = lambda t: pltpu.make_async_copy(
        h_ref.at[pl.ds(pl.multiple_of(t * td, td), td)], hbuf.at[t % 3], tsem.at[t % 3])

    def wait_rows(t):
        for _ in range(2):
            pltpu.make_async_copy(hbuf.at[0], xs_ref.at[pl.ds(0, td)], rsem.at[t % 2]).wait()

    @pl.when(i == 0)
    def _():
        tile_copy(0).start()
        zero_ref[...] = jnp.zeros(zero_ref.shape, zero_ref.dtype)
        tail_copy = lambda e: pltpu.make_async_copy(
            zero_ref, xs_ref.at[pl.ds(pl.multiple_of(tail_ref[e], TM_MOE), TM_MOE)], rsem.at[1])
        for e in range(N_EXPERTS):
            pl.when(tail_ref[e] >= 0)(lambda e=e: tail_copy(e).start())
        for e in range(N_EXPERTS):
            pl.when(tail_ref[e] >= 0)(lambda e=e: tail_copy(e).wait())

    @pl.when(i < last)
    def _():
        tile_copy(i + 1).start()

    tile_copy(i).wait()
    src = hbuf.at[i % 3]
    _row_copies(pos_ref, n_tokens, i * td, td, lambda k, r, p: pltpu.make_async_copy(
        src.at[pl.ds(r, 1)], xs_ref.at[pl.ds(p, 1)], rsem.at[i % 2]))

    @pl.when(i >= 1)
    def _():
        wait_rows(i - 1)

    @pl.when(i == last)
    def _():
        wait_rows(i)


def _dispatch(pos, tails, h2, n_rows, td=256):
    t, width = h2.shape
    return pl.pallas_call(
        functools.partial(_dispatch_kernel, td=td, n_tokens=t),
        grid_spec=pltpu.PrefetchScalarGridSpec(
            num_scalar_prefetch=2, grid=(t // td,),
            in_specs=[pl.BlockSpec(memory_space=pl.ANY)],
            out_specs=pl.BlockSpec(memory_space=pl.ANY),
            scratch_shapes=[pltpu.VMEM((3, td, width), h2.dtype), pltpu.VMEM((TM_MOE, width), h2.dtype),
                            pltpu.SemaphoreType.DMA((3,)), pltpu.SemaphoreType.DMA((2,))]),
        out_shape=jax.ShapeDtypeStruct((n_rows, width), h2.dtype),
        compiler_params=_cparams("arbitrary"), name="dispatch",
    )(pos, tails, h2)


def _pack_halves(x):
    half = x.shape[1] // 2
    return pltpu.pack_elementwise([x[:, :half], x[:, half:]], packed_dtype=BF16)


def _unpack_halves(p, dtype):
    return tuple(pltpu.unpack_elementwise(p, index=i, packed_dtype=BF16, unpacked_dtype=F32).astype(dtype)
                 for i in range(2))


def _expert_kernel(te_ref, nu_ref, slot_ref, next_ref, xs_ref, wg_ref, wu_ref, wd_ref, ys_ref,
                   wgf, wuf, wdf, wgb, wub, wdb, wsem):
    j = pl.program_id(0)

    def weight_copies(e, s):
        return [pltpu.make_async_copy(w_ref.at[e], buf.at[s], wsem.at[s])
                for w_ref, buf in ((wg_ref, wgf), (wu_ref, wuf), (wd_ref, wdf))]

    @pl.when(j < nu_ref[0])
    def _():
        e, s = te_ref[j], slot_ref[j]

        @pl.when(j == 0)
        def _():
            for cp in weight_copies(e, s):
                cp.start()

        @pl.when((j == 0) | (e != te_ref[jnp.maximum(j - 1, 0)]))
        def _():
            for cp in weight_copies(e, s):
                cp.wait()
            nxt = next_ref[j]

            @pl.when(nxt >= 0)
            def _():
                for cp in weight_copies(nxt, 1 - s):
                    cp.start()

            wgb[...] = wgf[s].astype(BF16)
            wub[...] = wuf[s].astype(BF16)
            wdb[...] = wdf[s].astype(BF16)

        xa, xb = _unpack_halves(xs_ref[...], BF16)
        half = D_MODEL // 2
        g = _dot(xa, wgb[:half, :]) + _dot(xb, wgb[half:, :])
        u = _dot(xa, wub[:half, :]) + _dot(xb, wub[half:, :])
        a = (g / (1.0 + jnp.exp(-g))) * u
        ys_ref[...] = _pack_halves(_dot(a.astype(BF16), wdb[...]))


def _experts(tile_expert, n_used, weight_slot, next_expert, xs, wg, wu, wd):
    n_tiles = xs.shape[0] // TM_MOE
    tile = lambda j, te, nu, *_: (jnp.minimum(j, nu[0] - 1), 0)
    hbm = pl.BlockSpec(memory_space=pl.ANY)
    return pl.pallas_call(
        _expert_kernel,
        grid_spec=pltpu.PrefetchScalarGridSpec(
            num_scalar_prefetch=4, grid=(n_tiles,),
            in_specs=[pl.BlockSpec((TM_MOE, D_MODEL // 2), tile), hbm, hbm, hbm],
            out_specs=pl.BlockSpec((TM_MOE, D_MODEL // 2), tile),
            scratch_shapes=[pltpu.VMEM((2, D_MODEL, D_EXPERT), F32), pltpu.VMEM((2, D_MODEL, D_EXPERT), F32),
                            pltpu.VMEM((2, D_EXPERT, D_MODEL), F32),
                            pltpu.VMEM((D_MODEL, D_EXPERT), BF16), pltpu.VMEM((D_MODEL, D_EXPERT), BF16),
                            pltpu.VMEM((D_EXPERT, D_MODEL), BF16), pltpu.SemaphoreType.DMA((2,))]),
        out_shape=jax.ShapeDtypeStruct(xs.shape, jnp.uint32),
        compiler_params=_cparams("arbitrary"), name="experts",
    )(tile_expert, n_used, weight_slot, next_expert, xs, wg, wu, wd)


def _combine_kernel(pos_ref, x1_ref, route_ref, mod_ref, ys_ref, o_ref, ybuf, sem, *, tc, n_tokens, tok0):
    i = pl.program_id(0)
    slot = i % 2

    def gather(tile, s):
        _row_copies(pos_ref, n_tokens, tok0 + tile * tc, tc, lambda k, r, p: pltpu.make_async_copy(
            ys_ref.at[pl.ds(p, 1)], ybuf.at[s, k, pl.ds(r, 1)], sem.at[s]))

    @pl.when(i == 0)
    def _():
        gather(0, 0)

    @pl.when(i + 1 < pl.num_programs(0))
    def _():
        gather(i + 1, 1 - slot)

    for k in range(2):
        pltpu.make_async_copy(ys_ref.at[pl.ds(0, tc)], ybuf.at[slot, k], sem.at[slot]).wait()
    route = route_ref[...]
    w1, w2 = route[:, 2:3], route[:, 3:4]
    half = D_MODEL // 2
    for h, (y1, y2) in enumerate(zip(_unpack_halves(ybuf[slot, 0], F32), _unpack_halves(ybuf[slot, 1], F32))):
        cols = slice(h * half, (h + 1) * half)
        gate2 = mod_ref[0, :, 5 * D_MODEL + h * half:5 * D_MODEL + (h + 1) * half]
        o_ref[:, cols] = x1_ref[:, cols] + gate2 * (w1 * y1 + w2 * y2)


def _combine(pos, x1, route, mod3, row_of_tile, ys, tok0, n_tokens, tc=256):
    t = x1.shape[0]
    row = lambda i, *_: (i, 0)
    return pl.pallas_call(
        functools.partial(_combine_kernel, tc=tc, n_tokens=n_tokens, tok0=tok0),
        grid_spec=pltpu.PrefetchScalarGridSpec(
            num_scalar_prefetch=1, grid=(t // tc,),
            in_specs=[pl.BlockSpec((tc, D_MODEL), row), pl.BlockSpec((tc, LANES), row),
                      pl.BlockSpec((1, 1, N_MOD * D_MODEL), lambda i, *_: (row_of_tile(i), 0, 0)),
                      pl.BlockSpec(memory_space=pl.ANY)],
            out_specs=pl.BlockSpec((tc, D_MODEL), row),
            scratch_shapes=[pltpu.VMEM((2, 2, tc, ys.shape[1]), ys.dtype), pltpu.SemaphoreType.DMA((2,))]),
        out_shape=jax.ShapeDtypeStruct((t, D_MODEL), F32),
        compiler_params=_cparams("arbitrary"), name="combine",
    )(pos, x1, route, mod3, ys)


def _rope_tables(seq):
    pos = np.arange(seq)
    n_freq = ATT_HEAD_DIM // 4
    inv = ROPE_THETA ** (-np.arange(n_freq, dtype=np.float32) / n_freq)
    ang = np.concatenate([(pos // GRID_W).astype(np.float32)[:, None] * inv,
                          (pos % GRID_W).astype(np.float32)[:, None] * inv], axis=-1).astype(np.float32)
    ang = jnp.asarray(ang)
    cos, sin = jnp.cos(ang), jnp.sin(ang)
    cos_t = jnp.repeat(cos, 2, axis=1)
    sin_t = jnp.stack([-sin, sin], axis=-1).reshape(seq, ATT_HEAD_DIM)
    return cos_t, sin_t


def _pad_lanes(a):
    return jnp.pad(a, ((0, 0), (0, LANES - a.shape[1])))


def _layer(x2d, n_batch, seq, mod3, row_of_tile, lw, *, rope_tabs, cache, state, emit, h2_all, tile0, t_total):
    (n1, n2, w_main, wg_cat, bg, qg, kg, mg, w_out, rw_cat, rb) = lw
    cos_t, sin_t = rope_tabs
    outs = _project(x2d, mod3, row_of_tile(256), n1, w_main, wg_cat, bg, qg, kg, cos_t, sin_t,
                    rope=cache is not None, emit_kv=emit)
    q, k, v, qm, kmt, vm, om, rowq, colq = outs[:9]
    attn = _attention(q, k, v, n_batch, seq, cache=cache, tq=min(seq, 512))
    ml = _mlstm(qm, kmt, vm, rowq, colq, n_batch, seq, state=state, emit_state=emit)
    mixed = _mix_out(attn, ml[0], om, x2d, mod3, row_of_tile(MIX_TILE), mg, w_out, n2, rw_cat, rb,
                     h2_all, tile0, t_total, tm=MIX_TILE)
    return mixed, outs[9:], ml[1:]


def kernel(x_prompt, x_sample, cache_k, cache_v, state_C, state_n, state_m, c, c_ctx, mod_w, mod_b, norm1_g, norm2_g,
           w_in, b_gates, q_norm_g, k_norm_g, mlstm_norm_g, w_out, router_group_w, router_group_b, router_expert_w,
           router_expert_b, expert_w_gate, expert_w_up, expert_w_down):
    assert mod_w.shape[0] == 1, "single-layer stack"
    n_ctx, s_ctx, _ = x_prompt.shape
    n_lat, s_lat, _ = x_sample.shape
    t_ctx, t_lat = n_ctx * s_ctx, n_lat * s_lat
    t_all = t_ctx + t_lat
    ctx_row = n_lat

    cond = jnp.concatenate([c, c_ctx[None], jnp.zeros((8 - n_lat - 1, D_MODEL), F32)], axis=0)
    mod3 = _modulation(cond, mod_w[0], mod_b[0][None]).reshape(8, 1, N_MOD * D_MODEL)

    rw =jnp.concatenate([router_group_w[0], jnp.moveaxis(router_expert_w[0], 0, 1).reshape(D_MODEL, N_EXPERTS)], axis=1)
    rb = _pad_lanes(jnp.concatenate([router_group_b[0], router_expert_b[0].reshape(-1)])[None])
    w_in_t = w_in[0].T
    lw = (norm1_g, norm2_g, _cast_bf16_t(w_in_t, MAIN_WIDTH), _gate_cols(w_in_t), _pad_lanes(b_gates),
          q_norm_g, k_norm_g, mlstm_norm_g, _cast_bf16(w_out, D_MODEL), _hi_lo_cat(_pad_lanes(rw)), rb)
    rope_tabs = _rope_tables(s_lat)

    ctx_rows = lambda tm: (lambda i: ctx_row)
    lat_rows = lambda tm: (lambda i: i // (s_lat // tm))
    (x1p, h2_all, routep), (ka, va), (s_c, s_n, s_m) = _layer(
        x_prompt.reshape(t_ctx, D_MODEL), n_ctx, s_ctx, mod3, ctx_rows, lw,
        rope_tabs=rope_tabs, cache=None, state=None, emit=True, h2_all=None, tile0=0, t_total=t_all)

    caug0 = jnp.concatenate([state_C[:, 0], state_n[:, 0][..., None],
                             jnp.zeros(state_n[:, 0].shape + (LANES - 1,), F32)], axis=-1)
    caug0 = caug0.reshape(n_lat, N_CHAINS, MLSTM_HEAD_DIM, AUG)
    past = cache_k.shape[2]
    cache = (cache_k[:, 0].reshape(n_lat * past, KV_WIDTH), cache_v[:, 0].reshape(n_lat * past, KV_WIDTH))
    (x1s, h2_all, routes), _, _ = _layer(
        x_sample.reshape(t_lat, D_MODEL), n_lat, s_lat, mod3, lat_rows, lw,
        rope_tabs=rope_tabs, cache=cache, state=(caug0, state_m[:, 0].reshape(-1)), emit=False,
        h2_all=h2_all, tile0=t_ctx // MIX_TILE, t_total=t_all)

    route = jnp.concatenate([routep, routes], axis=0)
    ranks, counts = _ranks(route)
    pos, tile_plan, tails, n_rows = _routing_plan(route, ranks, counts)
    xs = _dispatch(pos, tails, h2_all, n_rows)
    y_sorted = _experts(*tile_plan, xs, expert_w_gate[0], expert_w_up[0], expert_w_down[0])
    yp = _combine(pos, x1p, routep, mod3, ctx_rows(256), y_sorted, 0, t_all)
    ys = _combine(pos, x1s, routes, mod3, lat_rows(256), y_sorted, t_ctx, t_all)

    kv_shape = (n_ctx, 1, s_ctx, ATT_KV_HEADS, ATT_HEAD_DIM)
    return (yp.reshape(x_prompt.shape), ys.reshape(x_sample.shape), ka.reshape(kv_shape), va.reshape(kv_shape),
            s_c.reshape(n_ctx, 1, 2, MLSTM_HEADS, MLSTM_HEAD_DIM, MLSTM_HEAD_DIM),
            s_n.reshape(n_ctx, 1, 2, MLSTM_HEADS, MLSTM_HEAD_DIM), s_m[..., 0, 0].reshape(n_ctx, 1, 2, MLSTM_HEADS))
```

```python
import functools

import numpy as np
import jax
import jax.numpy as jnp
from jax import lax
from jax.experimental import pallas as pl
from jax.experimental.pallas import tpu as pltpu

F32 = jnp.float32
BF16 = jnp.bfloat16

D_MODEL = 2048
GRID_W = 64
ATT_HEADS = 8
ATT_KV_HEADS = 2
ATT_HEAD_DIM = 128
ATT_GROUP = ATT_HEADS // ATT_KV_HEADS
ATT_WIDTH = ATT_HEADS * ATT_HEAD_DIM
KV_WIDTH = ATT_KV_HEADS * ATT_HEAD_DIM
ROPE_THETA = 10000.0
MLSTM_HEADS = 4
MLSTM_HEAD_DIM = 256
MLSTM_WIDTH = MLSTM_HEADS * MLSTM_HEAD_DIM
CHUNK = 256
N_GATES = 4 * MLSTM_HEADS
N_CHAINS = 2 * MLSTM_HEADS
MAIN_WIDTH = ATT_WIDTH + 2 * KV_WIDTH + 4 * MLSTM_WIDTH
N_GROUPS = 4
EXPERTS_PER_GROUP = 8
N_EXPERTS = N_GROUPS * EXPERTS_PER_GROUP
D_EXPERT = 512
N_MOD = 6
TM_MOE = 256
MIX_TILE = 512
ATT_KEY_CHUNK = 256
Q_SCALE = ATT_HEAD_DIM ** -0.5 * float(np.log2(np.e))
EPS = 1e-6

LANES = 128
AUG = MLSTM_HEAD_DIM + LANES
N_SCAN_ROWS = 5 * N_CHAINS
ROUTER_LANE0 = N_GROUPS
VMEM_LIMIT = 56 * 1024 * 1024


def _cparams(*sem):
    return pltpu.CompilerParams(dimension_semantics=sem, vmem_limit_bytes=VMEM_LIMIT)


def _const_spec(shape):
    nd = len(shape)
    return pl.BlockSpec(shape, lambda *_: (0,) * nd, pipeline_mode=pl.Buffered(1))


def _split_hi_lo(x):
    hi = x.astype(BF16)
    lo = (x - hi.astype(F32)).astype(BF16)
    return hi, lo


def _dot(a, b):
    return jnp.dot(a, b, preferred_element_type=F32)


def _hi_lo_cat(w):
    return jnp.concatenate(_split_hi_lo(w), axis=1)


def _dot_hi_lo(a_hi, a_lo, w_ref):
    r = _dot(a_hi, w_ref[...])
    return r[:, :LANES] + r[:, LANES:] + _dot(a_lo, w_ref[:, :LANES])


def _rms(x, g):
    return x * lax.rsqrt(jnp.mean(x * x, axis=-1, keepdims=True) + EPS) * g


def _mod_kernel(c_ref, w_ref, b_ref, o_ref):
    c = c_ref[...]
    s = c / (1.0 + jnp.exp(-c))
    s_hi = s.astype(BF16).astype(F32)
    lhs = jnp.concatenate([s_hi, s - s_hi], axis=0).astype(BF16)
    w_hi, w_lo = _split_hi_lo(w_ref[...])
    r = _dot(lhs, w_hi)
    r2 = _dot(lhs, w_lo)
    o_ref[...] = r[:8] + r[8:] + r2[:8] + b_ref[...]


def _modulation(cond, mod_w, mod_b):
    n = mod_w.shape[1]
    tn = 1024
    return pl.pallas_call(
        _mod_kernel,
        grid=(n // tn,),
        in_specs=[pl.BlockSpec((8, D_MODEL), lambda j: (0, 0)),
                  pl.BlockSpec((D_MODEL, tn), lambda j: (0, j)),
                  pl.BlockSpec((1, tn), lambda j: (0, j))],
        out_specs=pl.BlockSpec((8, tn), lambda j: (0, j)),
        out_shape=jax.ShapeDtypeStruct((8, n), F32),
        compiler_params=_cparams("parallel"),
        name="modulation",
    )(cond, mod_w, mod_b)


def _cast_kernel(w_ref, o_ref):
    o_ref[...] = w_ref[0].astype(BF16)


def _cast_bf16(w, n_cols, tn=512):
    rows = w.shape[1]
    return pl.pallas_call(
        _cast_kernel, grid=(n_cols // tn,),
        in_specs=[pl.BlockSpec((1, rows, tn), lambda j: (0, 0, j))],
        out_specs=pl.BlockSpec((rows, tn), lambda j: (0, j)),
        out_shape=jax.ShapeDtypeStruct((rows, n_cols), BF16),
        compiler_params=_cparams("parallel"), name="cast_bf16",
    )(w)


def _cast_t_kernel(wt_ref, o_ref):
    o_ref[...] = wt_ref[...].T.astype(BF16)


def _cast_bf16_t(wt, n_cols, tn=512):
    rows = wt.shape[1]
    return pl.pallas_call(
        _cast_t_kernel, grid=(n_cols // tn,),
        in_specs=[pl.BlockSpec((tn, rows), lambda j: (j, 0))],
        out_specs=pl.BlockSpec((rows, tn), lambda j: (0, j)),
        out_shape=jax.ShapeDtypeStruct((rows, n_cols), BF16),
        compiler_params=_cparams("parallel"), name="cast_bf16_t",
    )(wt)


def _gate_cols_kernel(wt_ref, o_ref):
    sub = lax.broadcasted_iota(jnp.int32, wt_ref.shape, 0)
    hi, lo = _split_hi_lo(jnp.where(sub < N_GATES, wt_ref[...], 0.0).T)
    o_ref[:, :LANES] = hi
    o_ref[:, LANES:] = lo


def _gate_cols(wt):
    rows = wt.shape[1]
    return pl.pallas_call(
        _gate_cols_kernel, grid=(1,),
        in_specs=[pl.BlockSpec((LANES, rows), lambda i: (MAIN_WIDTH // LANES, 0))],
        out_specs=pl.BlockSpec((rows, 2 * LANES), lambda i: (0, 0)),
        out_shape=jax.ShapeDtypeStruct((rows, 2 * LANES), BF16),
        compiler_params=_cparams("arbitrary"), name="gate_cols",
    )(wt)


def _pair_swap(x):
    lane = lax.broadcasted_iota(jnp.int32, x.shape, 1)
    return jnp.where((lane & 1) == 0, pltpu.roll(x, LANES - 1, 1), pltpu.roll(x, 1, 1))


def _proj_kernel(x_ref, mod_ref, n1_ref, w_ref, wg_ref, bg_ref, qg_ref, kg_ref, cos_ref, sin_ref,
                 q_ref, k_ref, v_ref, qm_ref, kmt_ref, vm_ref, om_ref, row_ref, col_ref, *kv_refs, rope, tm):
    mod = mod_ref[0]
    shift, scale = mod[:, :D_MODEL], mod[:, D_MODEL:2 * D_MODEL]
    h = _rms(x_ref[...], n1_ref[...]) * (1.0 + scale) + shift
    h_hi, h_lo = _split_hi_lo(h)

    g = _dot_hi_lo(h_hi, h_lo, wg_ref) + bg_ref[...]
    for cc in range(tm // CHUNK):
        rows, cols = _gate_scan_chunk(g[cc * CHUNK:(cc + 1) * CHUNK, :])
        row_ref[cc] = rows
        col_ref[cc * CHUNK:(cc + 1) * CHUNK, :] = cols

    def rot(seg):
        return seg * cos_ref[...] + _pair_swap(seg) * sin_ref[...] if rope else seg

    qa = _dot(h_hi, w_ref[:, :ATT_WIDTH])
    for hh in range(ATT_HEADS):
        sl = slice(hh * ATT_HEAD_DIM, (hh + 1) * ATT_HEAD_DIM)
        seg = rot(_rms(qa[:, sl], qg_ref[...]))
        q_ref[:, sl] = (seg * Q_SCALE).astype(BF16)

    kv = _dot(h_hi, w_ref[:, ATT_WIDTH:ATT_WIDTH + 2 * KV_WIDTH])
    for hh in range(ATT_KV_HEADS):
        sl = slice(hh * ATT_HEAD_DIM, (hh + 1) * ATT_HEAD_DIM)
        seg = _rms(kv[:, sl], kg_ref[...])
        if kv_refs:
            kv_refs[0][:, sl] = seg
        k_ref[:, sl] = rot(seg).astype(BF16)
    va = kv[:, KV_WIDTH:]
    if kv_refs:
        kv_refs[1][...] = va
    v_ref[...] = va.astype(BF16)

    c0 = ATT_WIDTH + 2 * KV_WIDTH
    qm_ref[...] = (_dot(h_hi, w_ref[:, c0:c0 + MLSTM_WIDTH]) * MLSTM_HEAD_DIM ** -0.5).astype(BF16)
    km = _dot(h_hi, w_ref[:, c0 + MLSTM_WIDTH:c0 + 2 * MLSTM_WIDTH])
    for cc in range(tm // CHUNK):
        kmt_ref[cc] = km[cc * CHUNK:(cc + 1) * CHUNK, :].T.astype(BF16)
    vm_ref[...] = _dot(h_hi, w_ref[:, c0 + 2 * MLSTM_WIDTH:c0 + 3 * MLSTM_WIDTH]).astype(BF16)
    om_ref[...] = _dot(h_hi, w_ref[:, c0 + 3 * MLSTM_WIDTH:c0 + 4 * MLSTM_WIDTH]).astype(BF16)


def _project(x2d, mod3, row_of_tile, n1, w_main, wg_cat, bg, qg, kg, cos_t, sin_t, *, rope, emit_kv, tm=256):
    t = x2d.shape[0]
    n_pos = cos_t.shape[0] // tm
    row = lambda i: (i, 0)
    in_specs = [pl.BlockSpec((tm, D_MODEL), row),
                pl.BlockSpec((1, 1, N_MOD * D_MODEL), lambda i: (row_of_tile(i), 0, 0)),
                _const_spec((1, D_MODEL)),
                _const_spec((D_MODEL, MAIN_WIDTH)),
                _const_spec((D_MODEL, 2 * LANES)), _const_spec((1, LANES)),
                _const_spec((1, ATT_HEAD_DIM)), _const_spec((1, ATT_HEAD_DIM)),
                pl.BlockSpec((tm, ATT_HEAD_DIM), lambda i: (i % n_pos, 0)),
                pl.BlockSpec((tm, ATT_HEAD_DIM), lambda i: (i % n_pos, 0))]
    out_shape = [jax.ShapeDtypeStruct((t, ATT_WIDTH), BF16), jax.ShapeDtypeStruct((t, KV_WIDTH), BF16),
                 jax.ShapeDtypeStruct((t, KV_WIDTH), BF16), jax.ShapeDtypeStruct((t, MLSTM_WIDTH), BF16),
                 jax.ShapeDtypeStruct((t // CHUNK, MLSTM_WIDTH, CHUNK), BF16),
                 jax.ShapeDtypeStruct((t, MLSTM_WIDTH), BF16), jax.ShapeDtypeStruct((t, MLSTM_WIDTH), BF16),
                 jax.ShapeDtypeStruct((t // CHUNK, N_SCAN_ROWS, CHUNK), F32), jax.ShapeDtypeStruct((t, LANES), F32)]
    out_specs = [pl.BlockSpec((tm, ATT_WIDTH), row), pl.BlockSpec((tm, KV_WIDTH), row),
                 pl.BlockSpec((tm, KV_WIDTH), row), pl.BlockSpec((tm, MLSTM_WIDTH), row),
                 pl.BlockSpec((tm // CHUNK, MLSTM_WIDTH, CHUNK), lambda i: (i, 0, 0)),
                 pl.BlockSpec((tm, MLSTM_WIDTH), row), pl.BlockSpec((tm, MLSTM_WIDTH), row),
                 pl.BlockSpec((tm // CHUNK, N_SCAN_ROWS, CHUNK), lambda i: (i, 0, 0)), pl.BlockSpec((tm, LANES), row)]
    if emit_kv:
        out_shape += [jax.ShapeDtypeStruct((t, KV_WIDTH), F32)] * 2
        out_specs += [pl.BlockSpec((tm, KV_WIDTH), row)] * 2
    return pl.pallas_call(
        functools.partial(_proj_kernel, rope=rope, tm=tm),
        grid=(t // tm,), in_specs=in_specs, out_specs=out_specs, out_shape=out_shape,
        compiler_params=_cparams("parallel"), name="in_proj",
    )(x2d, mod3, n1, w_main, wg_cat, bg, qg, kg, cos_t, sin_t)


def _lane_scan(x, op, fill, is_fwd, lane):
    s = 1
    while s < CHUNK:
        from_left = jnp.where(lane >= s, pltpu.roll(x, s, 1), fill)
        from_right = jnp.where(lane < CHUNK - s, pltpu.roll(x, CHUNK - s, 1), fill)
        x = op(x, jnp.where(is_fwd, from_left, from_right))
        s *= 2
    return x


def _gate_scan_chunk(g):
    lane = lax.broadcasted_iota(jnp.int32, (N_CHAINS, CHUNK), 1)
    is_fwd = lax.broadcasted_iota(jnp.int32, (N_CHAINS, CHUNK), 0) < MLSTM_HEADS
    gt = g.T
    fwd, bwd = gt[:N_CHAINS], gt[N_CHAINS:2 * N_CHAINS]
    li = jnp.where(is_fwd, fwd, pltpu.roll(bwd, MLSTM_HEADS, 0))
    f = jnp.where(is_fwd, pltpu.roll(fwd, MLSTM_HEADS, 0), bwd)
    lf = jnp.minimum(f, 0.0) - jnp.log(1.0 + jnp.exp(-jnp.abs(f)))
    b = _lane_scan(lf, jnp.add, 0.0, is_fwd, lane)
    a = li - b
    run_max = _lane_scan(a, jnp.maximum, -jnp.inf, is_fwd, lane)
    all_max = jnp.broadcast_to(jnp.max(a, axis=1, keepdims=True), a.shape)
    last = jnp.where(is_fwd, CHUNK - 1, 0)
    total = jnp.broadcast_to(jnp.sum(jnp.where(lane == last, b, 0.0), axis=1, keepdims=True), a.shape)
    rows = jnp.concatenate([a, b, run_max, all_max, total], axis=0)
    padded = jnp.concatenate([rows, jnp.zeros((LANES - N_SCAN_ROWS, CHUNK), F32)], axis=0)
    return rows, padded.T


def _mlstm_kernel(*refs, n_chunks, has_state, emit_state):
    it = iter(refs)
    q_ref, kt_ref, v_ref, row_ref, col_ref = [next(it) for _ in range(5)]
    c0_ref, m0_ref = (next(it), next(it)) if has_state else (None, None)
    o_ref = next(it)
    c_out, n_out, m_out = (next(it), next(it), next(it)) if emit_state else (None, None, None)
    caug = next(it)

    fresh = not has_state and n_chunks == 1
    start = lambda cc: cc * CHUNK if isinstance(cc, int) else pl.multiple_of(cc * CHUNK, CHUNK)
    b = pl.program_id(0)
    if has_state:
        caug[...] = c0_ref[0]
        m_init = tuple(jnp.full((1, 1), m0_ref[b * N_CHAINS + r], F32) for r in range(N_CHAINS))
    else:
        if not fresh:
            caug[...] = jnp.zeros(caug.shape, F32)
        m_init = tuple(jnp.zeros((1, 1), F32) for _ in range(N_CHAINS))

    sub = lax.broadcasted_iota(jnp.int32, (CHUNK, CHUNK), 0)
    lane = lax.broadcasted_iota(jnp.int32, (CHUNK, CHUNK), 1)
    ones_col = (lax.broadcasted_iota(jnp.int32, (CHUNK, LANES), 1) == 0).astype(BF16)

    def chain(r, cc, m_prev):
        d, hd = divmod(r, MLSTM_HEADS)
        t0 = start(cc)
        hs = slice(hd * MLSTM_HEAD_DIM, (hd + 1) * MLSTM_HEAD_DIM)
        rows = row_ref[cc]
        cols = col_ref[pl.ds(t0, CHUNK), :]
        row = lambda k: rows[k * N_CHAINS + r:k * N_CHAINS + r + 1, :]
        col = lambda k: cols[:, k * N_CHAINS + r:k * N_CHAINS + r + 1]
        q = q_ref[pl.ds(t0, CHUNK), hs]
        kt = kt_ref[cc, hs, :]
        vaug = jnp.concatenate([v_ref[pl.ds(t0, CHUNK), hs], ones_col], axis=1)

        m_col = jnp.maximum(m_prev, col(2))
        keep = (lane <= sub) if d == 0 else (lane >= sub)
        w = jnp.where(keep, jnp.exp(row(0) - m_col), 0.0)
        w_inter = jnp.exp(m_prev - m_col)
        p = (_dot(q, kt) * w).astype(BF16)
        intra = _dot(p, vaug)
        num, den = intra[:, :MLSTM_HEAD_DIM], intra[:, MLSTM_HEAD_DIM:MLSTM_HEAD_DIM + 1]
        if not fresh:
            inter = _dot(q, caug[r].astype(BF16))
            num = num + w_inter * inter[:, :MLSTM_HEAD_DIM]
            den = den + w_inter * inter[:, MLSTM_HEAD_DIM:MLSTM_HEAD_DIM + 1]
        h_out = num / jnp.maximum(jnp.abs(den), jnp.exp(-(col(1) + m_col)))

        m_last = jnp.maximum(m_prev, row(3))
        kw = (kt.astype(F32) * jnp.exp(row(0) - m_last)).astype(BF16)
        update = _dot(kw, vaug)
        caug[r] = update if fresh else jnp.exp(m_prev - m_last[:, :1]) * caug[r] + update
        return h_out, (row(4) + m_last)[:, :1]

    def make_body(add_fwd, add_bwd):
        def body(c, ms):
            new = []
            for r in range(N_CHAINS):
                is_fwd = r < MLSTM_HEADS
                cc = c if is_fwd else n_chunks - 1 - c
                h_out, m_new = chain(r, cc, ms[r])
                hd = r % MLSTM_HEADS
                where = (pl.ds(start(cc), CHUNK), slice(hd * MLSTM_HEAD_DIM, (hd + 1) * MLSTM_HEAD_DIM))
                if add_fwd if is_fwd else add_bwd:
                    o_ref[where] += h_out
                else:
                    o_ref[where] = h_out
                new.append(m_new)
            return tuple(new)
        return body

    half = n_chunks // 2
    ms = m_init
    if half:
        ms = lax.fori_loop(0, half, make_body(False, False), ms)
    if n_chunks % 2:
        ms = make_body(False, True)(half, ms)
    if half:
        ms = lax.fori_loop(n_chunks - half, n_chunks, make_body(True, True), ms)
    if emit_state:
        for r in range(N_CHAINS):
            c_out[0, r] = caug[r, :, :MLSTM_HEAD_DIM]
            n_out[0, r] = caug[r, :, MLSTM_HEAD_DIM:].T[:1, :]
            m_out[0, r] = jnp.broadcast_to(ms[r], (1, LANES))


def _mlstm(qm, kmt, vm, rowq, colq, n_batch, seq, state=None, emit_state=False):
    nc = seq // CHUNK
    mode = {}
    in_specs = [pl.BlockSpec((seq, MLSTM_WIDTH), lambda b: (b, 0), **mode),
                pl.BlockSpec((nc, MLSTM_WIDTH, CHUNK), lambda b: (b, 0, 0), **mode),
                pl.BlockSpec((seq, MLSTM_WIDTH), lambda b: (b, 0), **mode),
                pl.BlockSpec((nc, N_SCAN_ROWS, CHUNK), lambda b: (b, 0, 0), **mode),
                pl.BlockSpec((seq, LANES), lambda b: (b, 0), **mode)]
    args = [qm, kmt, vm, rowq, colq]
    if state is not None:
        in_specs += [pl.BlockSpec((1, N_CHAINS, MLSTM_HEAD_DIM, AUG), lambda b: (b, 0, 0, 0), **mode),
                     pl.BlockSpec(memory_space=pltpu.SMEM)]
        args += list(state)
    out_shape = [jax.ShapeDtypeStruct((n_batch * seq, MLSTM_WIDTH), F32)]
    out_specs = [pl.BlockSpec((seq, MLSTM_WIDTH), lambda b: (b, 0))]
    if emit_state:
        out_shape += [jax.ShapeDtypeStruct((n_batch, N_CHAINS, MLSTM_HEAD_DIM, MLSTM_HEAD_DIM), F32),
                      jax.ShapeDtypeStruct((n_batch, N_CHAINS, 1, MLSTM_HEAD_DIM), F32),
                      jax.ShapeDtypeStruct((n_batch, N_CHAINS, 1, LANES), F32)]
        out_specs += [pl.BlockSpec((1, N_CHAINS, MLSTM_HEAD_DIM, MLSTM_HEAD_DIM), lambda b: (b, 0, 0, 0)),
                      pl.BlockSpec((1, N_CHAINS, 1, MLSTM_HEAD_DIM), lambda b: (b, 0, 0, 0)),
                      pl.BlockSpec((1, N_CHAINS, 1, LANES), lambda b: (b, 0, 0, 0))]
    return pl.pallas_call(
        functools.partial(_mlstm_kernel, n_chunks=nc, has_state=state is not None, emit_state=emit_state),
        grid=(n_batch,), in_specs=in_specs, out_specs=out_specs, out_shape=out_shape,
        scratch_shapes=[pltpu.VMEM((N_CHAINS, MLSTM_HEAD_DIM, AUG), F32)],
        compiler_params=_cparams("parallel"), name="mlstm",
    )(*args)


def _attn_kernel(*refs, tq, has_cache):
    if has_cache:
        q_ref, k_ref, v_ref, ck_ref, cv_ref, o_ref = refs
    else:
        q_ref, k_ref, v_ref, o_ref = refs
    q = q_ref[...]
    qs = jnp.concatenate([q[:, g * ATT_HEAD_DIM:(g + 1) * ATT_HEAD_DIM] for g in range(ATT_GROUP)], axis=0)
    nt = (((1,), (1,)), ((), ()))
    seq = k_ref.shape[0]
    kc = min(seq, ATT_KEY_CHUNK)
    chunks = [(k_ref, v_ref, c * kc) for c in range(seq // kc)]
    if has_cache:
        chunks.insert(0, (ck_ref, cv_ref, None))
    m = l = o = None
    for kr, vr, start in chunks:
        kk, vv = (kr[...], vr[...]) if start is None else (kr[start:start + kc, :], vr[start:start + kc, :])
        s = lax.dot_general(qs, kk.astype(BF16), nt, preferred_element_type=F32)
        mc = jnp.max(s, axis=-1, keepdims=True)
        if m is None:
            m = mc
            p = jnp.exp2(s - m)
            l = jnp.sum(p, axis=-1, keepdims=True)
            o = _dot(p.astype(BF16), vv.astype(BF16))
        else:
            m_new = jnp.maximum(m, mc)
            alpha = jnp.exp2(m - m_new)
            p = jnp.exp2(s - m_new)
            l = alpha * l + jnp.sum(p, axis=-1, keepdims=True)
            o = alpha * o + _dot(p.astype(BF16), vv.astype(BF16))
            m = m_new
    o = o / l
    for g in range(ATT_GROUP):
        o_ref[:, g * ATT_HEAD_DIM:(g + 1) * ATT_HEAD_DIM] = o[g * tq:(g + 1) * tq].astype(BF16)


def _attention(q, k, v, n_batch, seq, cache=None, tq=256):
    nqb = seq // tq
    gw = ATT_GROUP * ATT_HEAD_DIM
    in_specs = [pl.BlockSpec((tq, gw), lambda b, h, i: (b * nqb + i, h)),
                pl.BlockSpec((seq, ATT_HEAD_DIM), lambda b, h, i: (b, h)),
                pl.BlockSpec((seq, ATT_HEAD_DIM), lambda b, h, i: (b, h))]
    args = [q, k, v]
    if cache is not None:
        past = cache[0].shape[0] // n_batch
        in_specs += [pl.BlockSpec((past, ATT_HEAD_DIM), lambda b, h, i: (b, h))] * 2
        args += list(cache)
    return pl.pallas_call(
        functools.partial(_attn_kernel, tq=tq, has_cache=cache is not None),
        grid=(n_batch, ATT_KV_HEADS, nqb), in_specs=in_specs,
        out_specs=pl.BlockSpec((tq, gw), lambda b, h, i: (b * nqb + i, h)),
        out_shape=jax.ShapeDtypeStruct((n_batch * seq, ATT_WIDTH), BF16),
        compiler_params=_cparams("parallel", "parallel", "parallel"), name="attention",
    )(*args)


def _mix_kernel(attn_ref, hm_ref, om_ref, x_ref, mod_ref, mg_ref, wo_ref, n2_ref, rw_ref, rb_ref, *rest, sub):
    for r0 in range(0, x_ref.shape[0], sub):
        _mix_rows(slice(r0, r0 + sub), attn_ref, hm_ref, om_ref, x_ref, mod_ref, mg_ref, wo_ref, n2_ref, rw_ref,
                  rb_ref, *rest[-3:])


def _mix_rows(rows, attn_ref, hm_ref, om_ref, x_ref, mod_ref, mg_ref, wo_ref, n2_ref, rw_ref, rb_ref,
              x1_ref, h2_ref, route_ref):
    mod = mod_ref[0]
    gate1 = mod[:, 2 * D_MODEL:3 * D_MODEL]
    shift2, scale2 = mod[:, 3 * D_MODEL:4 * D_MODEL], mod[:, 4 * D_MODEL:5 * D_MODEL]
    hm = hm_ref[rows, :]
    mg = mg_ref[...]
    parts = []
    for hd in range(MLSTM_HEADS):
        sl = slice(hd * MLSTM_HEAD_DIM, (hd + 1) * MLSTM_HEAD_DIM)
        parts.append(_rms(hm[:, sl], mg[:, sl]))
    om = om_ref[rows, :].astype(F32)
    hmg = jnp.concatenate(parts, axis=1) * (1.0 / (1.0 + jnp.exp(-om)))
    y = _dot(attn_ref[rows, :], wo_ref[:ATT_WIDTH, :]) + _dot(hmg.astype(BF16), wo_ref[ATT_WIDTH:, :])
    x1 = x_ref[rows, :] + gate1 * y
    x1_ref[rows, :] = x1
    h2 = _rms(x1, n2_ref[...]) * (1.0 + scale2) + shift2
    h2_hi, h2_lo = _split_hi_lo(h2)
    h2_ref[rows, :] = _pack_halves(h2)

    lg = _dot_hi_lo(h2_hi, h2_lo, rw_ref) + rb_ref[...]
    lane = lax.broadcasted_iota(jnp.int32, lg.shape, 1).astype(F32)
    neg = -jnp.inf
    first = lambda hit: jnp.min(jnp.where(hit, lane, float(LANES)), axis=-1, keepdims=True)
    gl = jnp.where(lane < N_GROUPS, lg, neg)
    gmax = jnp.max(gl, axis=-1, keepdims=True)
    grp = first(gl == gmax)
    p_grp = 1.0 / jnp.sum(jnp.exp(gl - gmax), axis=-1, keepdims=True)
    lo = ROUTER_LANE0 + grp * EXPERTS_PER_GROUP
    el = jnp.where((lane >= lo) & (lane < lo + EXPERTS_PER_GROUP), lg, neg)
    m1 = jnp.max(el, axis=-1, keepdims=True)
    i1 = first(el == m1)
    el2 = jnp.where(lane == i1, neg, el)
    m2 = jnp.max(el2, axis=-1, keepdims=True)
    i2 = first(el2 == m2)
    r = jnp.exp(m2 - m1)
    w1 = p_grp / (1.0 + r)
    w2 = w1 * r
    route_ref[rows, :] = jnp.where(lane == 0.0, i1 - ROUTER_LANE0, jnp.where(lane == 1.0, i2 - ROUTER_LANE0,
                                   jnp.where(lane == 2.0, w1, jnp.where(lane == 3.0, w2, 0.0))))


def _mix_out(attn, hm, om, x2d, mod3, row_of_tile, mg, w_out, n2, rw_cat, rb, h2_all, tile0, t_total, tm=512):
    t = x2d.shape[0]
    row = lambda i: (i, 0)
    in_specs = [pl.BlockSpec((tm, ATT_WIDTH), row), pl.BlockSpec((tm, MLSTM_WIDTH), row),
                pl.BlockSpec((tm, MLSTM_WIDTH), row), pl.BlockSpec((tm, D_MODEL), row),
                pl.BlockSpec((1, 1, N_MOD * D_MODEL), lambda i: (row_of_tile(i), 0, 0)),
                _const_spec((1, MLSTM_WIDTH)), _const_spec((D_MODEL, D_MODEL)), _const_spec((1, D_MODEL)),
                _const_spec((D_MODEL, 2 * LANES)), _const_spec((1, LANES))]
    args = [attn, hm, om, x2d, mod3, mg, w_out, n2, rw_cat, rb]
    aliases = {}
    if h2_all is not None:
        aliases = {len(args): 1}
        in_specs.append(pl.BlockSpec(memory_space=pl.ANY))
        args.append(h2_all)
    return pl.pallas_call(
        functools.partial(_mix_kernel, sub=256), grid=(t // tm,), in_specs=in_specs,
        out_specs=[pl.BlockSpec((tm, D_MODEL), row), pl.BlockSpec((tm, D_MODEL // 2), lambda i: (tile0 + i, 0)),
                   pl.BlockSpec((tm, LANES), row)],
        out_shape=[jax.ShapeDtypeStruct((t, D_MODEL), F32), jax.ShapeDtypeStruct((t_total, D_MODEL // 2), jnp.uint32),
                   jax.ShapeDtypeStruct((t, LANES), F32)],
        input_output_aliases=aliases,
        compiler_params=_cparams("parallel"), name="mix_out",
    )(*args)


def _rank_kernel(route_ref, rank_ref, cnt_ref, run_ref, tri_ref):
    tr = route_ref.shape[0]

    @pl.when(pl.program_id(0) == 0)
    def _():
        run_ref[...] = jnp.zeros(run_ref.shape, F32)
        tri_ref[...] = (lax.broadcasted_iota(jnp.int32, (tr, tr), 1)
                        < lax.broadcasted_iota(jnp.int32, (tr, tr), 0)).astype(BF16)

    route = route_ref[...]
    lane = lax.broadcasted_iota(jnp.int32, route.shape, 1).astype(F32)
    hit1, hit2 = lane == route[:, 0:1], lane == route[:, 1:2]
    onehot = jnp.where(hit1, 1.0, jnp.where(hit2, 1.0, 0.0))
    before = _dot(tri_ref[...], onehot.astype(BF16)) + run_ref[0:1, :]
    r1 = jnp.sum(jnp.where(hit1, before, 0.0), axis=-1, keepdims=True)
    r2 = jnp.sum(jnp.where(hit2, before, 0.0), axis=-1, keepdims=True)
    rank_ref[...] = jnp.where(lane == 0.0, r1, jnp.where(lane == 1.0, r2, 0.0))
    run_ref[...] = run_ref[...] + jnp.sum(onehot, axis=0, keepdims=True)
    cnt_ref[...] = run_ref[...]


def _ranks(route, tr=512):
    t = route.shape[0]
    return pl.pallas_call(
        _rank_kernel, grid=(t // tr,),
        in_specs=[pl.BlockSpec((tr, LANES), lambda i: (i, 0))],
        out_specs=[pl.BlockSpec((tr, LANES), lambda i: (i, 0)), pl.BlockSpec((8, LANES), lambda i: (0, 0))],
        out_shape=[jax.ShapeDtypeStruct((t, LANES), F32), jax.ShapeDtypeStruct((8, LANES), F32)],
        scratch_shapes=[pltpu.VMEM((8, LANES), F32), pltpu.VMEM((tr, tr), BF16)],
        compiler_params=_cparams("arbitrary"), name="expert_ranks",
    )(route)


def _pos_kernel(route_ref, rank_ref, start_ref, o_ref):
    route, rank = route_ref[...], rank_ref[...]
    lane = lax.broadcasted_iota(jnp.int32, route.shape, 1).astype(F32)
    start = start_ref[...]
    first = lambda col: jnp.sum(jnp.where(lane == route[:, col:col + 1], start, 0.0), axis=-1, keepdims=True)
    p1 = first(0) + rank[:, 0:1]
    p2 = first(1) + rank[:, 1:2]
    tile = jnp.where(lane == 0.0, p1, jnp.where(lane == 1.0, p2, 0.0))
    o_ref[...] = tile.T[:8, :].astype(jnp.int32)


def _positions(route, ranks, starts, tr=512):
    t = route.shape[0]
    pos = pl.pallas_call(
        _pos_kernel, grid=(t // tr,),
        in_specs=[pl.BlockSpec((tr, LANES), lambda i: (i, 0)), pl.BlockSpec((tr, LANES), lambda i: (i, 0)),
                  pl.BlockSpec((1, LANES), lambda i: (0, 0))],
        out_specs=pl.BlockSpec((8, tr), lambda i: (0, i)),
        out_shape=jax.ShapeDtypeStruct((8, t), jnp.int32),
        compiler_params=_cparams("parallel"), name="pair_rows",
    )(route, ranks, starts)
    return pos[:2].reshape(-1)


def _routing_plan(route, ranks, counts):
    t = route.shape[0]
    n_tiles = 2 * t // TM_MOE + N_EXPERTS
    cnt = counts[0, :N_EXPERTS].astype(jnp.int32)
    padded = (cnt + TM_MOE - 1) // TM_MOE * TM_MOE
    ends = jnp.cumsum(padded)
    pos = _positions(route, ranks, _pad_lanes((ends - padded).astype(F32)[None]))
    n_used = (ends[-1:] // TM_MOE).astype(jnp.int32)
    tile_start = jnp.arange(n_tiles, dtype=jnp.int32) * TM_MOE
    tile_expert = jnp.sum((ends[None, :] <= tile_start[:, None]).astype(jnp.int32), axis=1)
    tile_expert = jnp.minimum(tile_expert, tile_expert[n_used[0] - 1]).astype(jnp.int32)
    tails = jnp.where(padded > 0, ends - TM_MOE, -1).astype(jnp.int32)
    change = jnp.concatenate([jnp.ones((1,), jnp.int32), (tile_expert[1:] != tile_expert[:-1]).astype(jnp.int32)])
    weight_slot = (jnp.cumsum(change) - 1) % 2
    ids = jnp.arange(N_EXPERTS, dtype=jnp.int32)
    later_used = (padded > 0)[None, :] & (ids[None, :] > ids[:, None])
    next_used = jnp.min(jnp.where(later_used, ids[None, :], N_EXPERTS), axis=1)
    next_used = jnp.where(next_used < N_EXPERTS, next_used, -1)
    next_expert = jnp.sum(jnp.where(tile_expert[:, None] == ids[None, :], next_used[None, :], 0), axis=1)
    return pos, (tile_expert, n_used, weight_slot.astype(jnp.int32), next_expert.astype(jnp.int32)), tails, \
        n_tiles * TM_MOE


def _row_copies(pos_ref, n_pairs, tok0, n_rows, make_copy):
    def body(r, carry):
        for k in range(2):
            make_copy(k, r, pos_ref[k * n_pairs + tok0 + r]).start()
        return carry
    lax.fori_loop(0, n_rows, body, 0, unroll=8)


def _dispatch_kernel(pos_ref, tail_ref, h_ref, xs_ref, hbuf, zero_ref, tsem, rsem, *, td, n_tokens):
    i = pl.program_id(0)
    last = pl.num_programs(0) - 1
    tile_copy = lambda t: pltpu.make_async_copy(
        h_ref.at[pl.ds(pl.multiple_of(t * td, td), td)], hbuf.at[t % 3], tsem.at[t % 3])

    def wait_rows(t):
        for _ in range(2):
            pltpu.make_async_copy(hbuf.at[0], xs_ref.at[pl.ds(0, td)], rsem.at[t % 2]).wait()

    @pl.when(i == 0)
    def _():
        tile_copy(0).start()
        zero_ref[...] = jnp.zeros(zero_ref.shape, zero_ref.dtype)
        tail_copy = lambda e: pltpu.make_async_copy(
            zero_ref, xs_ref.at[pl.ds(pl.multiple_of(tail_ref[e], TM_MOE), TM_MOE)], rsem.at[1])
        for e in range(N_EXPERTS):
            pl.when(tail_ref[e] >= 0)(lambda e=e: tail_copy(e).start())
        for e in range(N_EXPERTS):
            pl.when(tail_ref[e] >= 0)(lambda e=e: tail_copy(e).wait())

    @pl.when(i < last)
    def _():
        tile_copy(i + 1).start()

    tile_copy(i).wait()
    src = hbuf.at[i % 3]
    _row_copies(pos_ref, n_tokens, i * td, td, lambda k, r, p: pltpu.make_async_copy(
        src.at[pl.ds(r, 1)], xs_ref.at[pl.ds(p, 1)], rsem.at[i % 2]))

    @pl.when(i >= 1)
    def _():
        wait_rows(i - 1)

    @pl.when(i == last)
    def _():
        wait_rows(i)


def _dispatch(pos, tails, h2, n_rows, td=256):
    t, width = h2.shape
    return pl.pallas_call(
        functools.partial(_dispatch_kernel, td=td, n_tokens=t),
        grid_spec=pltpu.PrefetchScalarGridSpec(
            num_scalar_prefetch=2, grid=(t // td,),
            in_specs=[pl.BlockSpec(memory_space=pl.ANY)],
            out_specs=pl.BlockSpec(memory_space=pl.ANY),
            scratch_shapes=[pltpu.VMEM((3, td, width), h2.dtype), pltpu.VMEM((TM_MOE, width), h2.dtype),
                            pltpu.SemaphoreType.DMA((3,)), pltpu.SemaphoreType.DMA((2,))]),
        out_shape=jax.ShapeDtypeStruct((n_rows, width), h2.dtype),
        compiler_params=_cparams("arbitrary"), name="dispatch",
    )(pos, tails, h2)


def _pack_halves(x):
    half = x.shape[1] // 2
    return pltpu.pack_elementwise([x[:, :half], x[:, half:]], packed_dtype=BF16)


def _unpack_halves(p, dtype):
    return tuple(pltpu.unpack_elementwise(p, index=i, packed_dtype=BF16, unpacked_dtype=F32).astype(dtype)
                 for i in range(2))


def _expert_kernel(te_ref, nu_ref, slot_ref, next_ref, xs_ref, wg_ref, wu_ref, wd_ref, ys_ref,
                   wgf, wuf, wdf, wgb, wub, wdb, wsem):
    j = pl.program_id(0)

    def weight_copies(e, s):
        return [pltpu.make_async_copy(w_ref.at[e], buf.at[s], wsem.at[s])
                for w_ref, buf in ((wg_ref, wgf), (wu_ref, wuf), (wd_ref, wdf))]

    @pl.when(j < nu_ref[0])
    def _():
        e, s = te_ref[j], slot_ref[j]

        @pl.when(j == 0)
        def _():
            for cp in weight_copies(e, s):
                cp.start()

        @pl.when((j == 0) | (e != te_ref[jnp.maximum(j - 1, 0)]))
        def _():
            for cp in weight_copies(e, s):
                cp.wait()
            nxt = next_ref[j]

            @pl.when(nxt >= 0)
            def _():
                for cp in weight_copies(nxt, 1 - s):
                    cp.start()

            wgb[...] = wgf[s].astype(BF16)
            wub[...] = wuf[s].astype(BF16)
            wdb[...] = wdf[s].astype(BF16)

        xa, xb = _unpack_halves(xs_ref[...], BF16)
        half = D_MODEL // 2
        g = _dot(xa, wgb[:half, :]) + _dot(xb, wgb[half:, :])
        u = _dot(xa, wub[:half, :]) + _dot(xb, wub[half:, :])
        a = (g / (1.0 + jnp.exp(-g))) * u
        ys_ref[...] = _pack_halves(_dot(a.astype(BF16), wdb[...]))


def _experts(tile_expert, n_used, weight_slot, next_expert, xs, wg, wu, wd):
    n_tiles = xs.shape[0] // TM_MOE
    tile = lambda j, te, nu, *_: (jnp.minimum(j, nu[0] - 1), 0)
    hbm = pl.BlockSpec(memory_space=pl.ANY)
    return pl.pallas_call(
        _expert_kernel,
        grid_spec=pltpu.PrefetchScalarGridSpec(
            num_scalar_prefetch=4, grid=(n_tiles,),
            in_specs=[pl.BlockSpec((TM_MOE, D_MODEL // 2), tile), hbm, hbm, hbm],
            out_specs=pl.BlockSpec((TM_MOE, D_MODEL // 2), tile),
            scratch_shapes=[pltpu.VMEM((2, D_MODEL, D_EXPERT), F32), pltpu.VMEM((2, D_MODEL, D_EXPERT), F32),
                            pltpu.VMEM((2, D_EXPERT, D_MODEL), F32),
                            pltpu.VMEM((D_MODEL, D_EXPERT), BF16), pltpu.VMEM((D_MODEL, D_EXPERT), BF16),
                            pltpu.VMEM((D_EXPERT, D_MODEL), BF16), pltpu.SemaphoreType.DMA((2,))]),
        out_shape=jax.ShapeDtypeStruct(xs.shape, jnp.uint32),
        compiler_params=_cparams("arbitrary"), name="experts",
    )(tile_expert, n_used, weight_slot, next_expert, xs, wg, wu, wd)


def _combine_kernel(pos_ref, x1_ref, route_ref, mod_ref, ys_ref, o_ref, ybuf, sem, *, tc, n_tokens, tok0):
    i = pl.program_id(0)
    slot = i % 2

    def gather(tile, s):
        _row_copies(pos_ref, n_tokens, tok0 + tile * tc, tc, lambda k, r, p: pltpu.make_async_copy(
            ys_ref.at[pl.ds(p, 1)], ybuf.at[s, k, pl.ds(r, 1)], sem.at[s]))

    @pl.when(i == 0)
    def _():
        gather(0, 0)

    @pl.when(i + 1 < pl.num_programs(0))
    def _():
        gather(i + 1, 1 - slot)

    for k in range(2):
        pltpu.make_async_copy(ys_ref.at[pl.ds(0, tc)], ybuf.at[slot, k], sem.at[slot]).wait()
    route = route_ref[...]
    w1, w2 = route[:, 2:3], route[:, 3:4]
    half = D_MODEL // 2
    for h, (y1, y2) in enumerate(zip(_unpack_halves(ybuf[slot, 0], F32), _unpack_halves(ybuf[slot, 1], F32))):
        cols = slice(h * half, (h + 1) * half)
        gate2 = mod_ref[0, :, 5 * D_MODEL + h * half:5 * D_MODEL + (h + 1) * half]
        o_ref[:, cols] = x1_ref[:, cols] + gate2 * (w1 * y1 + w2 * y2)


def _combine(pos, x1, route, mod3, row_of_tile, ys, tok0, n_tokens, tc=256):
    t = x1.shape[0]
    row = lambda i, *_: (i, 0)
    return pl.pallas_call(
        functools.partial(_combine_kernel, tc=tc, n_tokens=n_tokens, tok0=tok0),
        grid_spec=pltpu.PrefetchScalarGridSpec(
            num_scalar_prefetch=1, grid=(t // tc,),
            in_specs=[pl.BlockSpec((tc, D_MODEL), row), pl.BlockSpec((tc, LANES), row),
                      pl.BlockSpec((1, 1, N_MOD * D_MODEL), lambda i, *_: (row_of_tile(i), 0, 0)),
                      pl.BlockSpec(memory_space=pl.ANY)],
            out_specs=pl.BlockSpec((tc, D_MODEL), row),
            scratch_shapes=[pltpu.VMEM((2, 2, tc, ys.shape[1]), ys.dtype), pltpu.SemaphoreType.DMA((2,))]),
        out_shape=jax.ShapeDtypeStruct((t, D_MODEL), F32),
        compiler_params=_cparams("arbitrary"), name="combine",
    )(pos, x1, route, mod3, ys)


def _rope_tables(seq):
    pos = np.arange(seq)
    n_freq = ATT_HEAD_DIM // 4
    inv = ROPE_THETA ** (-np.arange(n_freq, dtype=np.float32) / n_freq)
    ang = np.concatenate([(pos // GRID_W).astype(np.float32)[:, None] * inv,
                          (pos % GRID_W).astype(np.float32)[:, None] * inv], axis=-1).astype(np.float32)
    ang = jnp.asarray(ang)
    cos, sin = jnp.cos(ang), jnp.sin(ang)
    cos_t = jnp.repeat(cos, 2, axis=1)
    sin_t = jnp.stack([-sin, sin], axis=-1).reshape(seq, ATT_HEAD_DIM)
    return cos_t, sin_t


def _pad_lanes(a):
    return jnp.pad(a, ((0, 0), (0, LANES - a.shape[1])))


def _layer(x2d, n_batch, seq, mod3, row_of_tile, lw, *, rope_tabs, cache, state, emit, h2_all, tile0, t_total):
    (n1, n2, w_main, wg_cat, bg, qg, kg, mg, w_out, rw_cat, rb) = lw
    cos_t, sin_t = rope_tabs
    outs = _project(x2d, mod3, row_of_tile(256), n1, w_main, wg_cat, bg, qg, kg, cos_t, sin_t,
                    rope=cache is not None, emit_kv=emit)
    q, k, v, qm, kmt, vm, om, rowq, colq = outs[:9]
    attn = _attention(q, k, v, n_batch, seq, cache=cache, tq=min(seq, 512))
    ml = _mlstm(qm, kmt, vm, rowq, colq, n_batch, seq, state=state, emit_state=emit)
    mixed = _mix_out(attn, ml[0], om, x2d, mod3, row_of_tile(MIX_TILE), mg, w_out, n2, rw_cat, rb,
                     h2_all, tile0, t_total, tm=MIX_TILE)
    return mixed, outs[9:], ml[1:]


def kernel(x_prompt, x_sample, cache_k, cache_v, state_C, state_n, state_m, c, c_ctx, mod_w, mod_b, norm1_g, norm2_g,
           w_in, b_gates, q_norm_g, k_norm_g, mlstm_norm_g, w_out, router_group_w, router_group_b, router_expert_w,
           router_expert_b, expert_w_gate, expert_w_up, expert_w_down):
    assert mod_w.shape[0] == 1, "single-layer stack"
    n_ctx, s_ctx, _ = x_prompt.shape
    n_lat, s_lat, _ = x_sample.shape
    t_ctx, t_lat = n_ctx * s_ctx, n_lat * s_lat
    t_all = t_ctx + t_lat
    ctx_row = n_lat

    cond = jnp.concatenate([c, c_ctx[None], jnp.zeros((8 - n_lat - 1, D_MODEL), F32)], axis=0)
    mod3 = _modulation(cond, mod_w[0], mod_b[0][None]).reshape(8, 1, N_MOD * D_MODEL)

    rw =jnp.concatenate([router_group_w[0], jnp.moveaxis(router_expert_w[0], 0, 1).reshape(D_MODEL, N_EXPERTS)], axis=1)
    rb = _pad_lanes(jnp.concatenate([router_group_b[0], router_expert_b[0].reshape(-1)])[None])
    w_in_t = w_in[0].T
    lw = (norm1_g, norm2_g, _cast_bf16_t(w_in_t, MAIN_WIDTH), _gate_cols(w_in_t), _pad_lanes(b_gates),
          q_norm_g, k_norm_g, mlstm_norm_g, _cast_bf16(w_out, D_MODEL), _hi_lo_cat(_pad_lanes(rw)), rb)
    rope_tabs = _rope_tables(s_lat)

    ctx_rows = lambda tm: (lambda i: ctx_row)
    lat_rows = lambda tm: (lambda i: i // (s_lat // tm))
    (x1p, h2_all, routep), (ka, va), (s_c, s_n, s_m) = _layer(
        x_prompt.reshape(t_ctx, D_MODEL), n_ctx, s_ctx, mod3, ctx_rows, lw,
        rope_tabs=rope_tabs, cache=None, state=None, emit=True, h2_all=None, tile0=0, t_total=t_all)

    caug0 = jnp.concatenate([state_C[:, 0], state_n[:, 0][..., None],
                             jnp.zeros(state_n[:, 0].shape + (LANES - 1,), F32)], axis=-1)
    caug0 = caug0.reshape(n_lat, N_CHAINS, MLSTM_HEAD_DIM, AUG)
    past = cache_k.shape[2]
    cache = (cache_k[:, 0].reshape(n_lat * past, KV_WIDTH), cache_v[:, 0].reshape(n_lat * past, KV_WIDTH))
    (x1s, h2_all, routes), _, _ = _layer(
        x_sample.reshape(t_lat, D_MODEL), n_lat, s_lat, mod3, lat_rows, lw,
        rope_tabs=rope_tabs, cache=cache, state=(caug0, state_m[:, 0].reshape(-1)), emit=False,
        h2_all=h2_all, tile0=t_ctx // MIX_TILE, t_total=t_all)

    route = jnp.concatenate([routep, routes], axis=0)
    ranks, counts = _ranks(route)
    pos, tile_plan, tails, n_rows = _routing_plan(route, ranks, counts)
    xs = _dispatch(pos, tails, h2_all, n_rows)
    y_sorted = _experts(*tile_plan, xs, expert_w_gate[0], expert_w_up[0], expert_w_down[0])
    yp = _combine(pos, x1p, routep, mod3, ctx_rows(256), y_sorted, 0, t_all)
    ys = _combine(pos, x1s, routes, mod3, lat_rows(256), y_sorted, t_ctx, t_all)

    kv_shape = (n_ctx, 1, s_ctx, ATT_KV_HEADS, ATT_HEAD_DIM)
    return (yp.reshape(x_prompt.shape), ys.reshape(x_sample.shape), ka.reshape(kv_shape), va.reshape(kv_shape),
            s_c.reshape(n_ctx, 1, 2, MLSTM_HEADS, MLSTM_HEAD_DIM, MLSTM_HEAD_DIM),
            s_n.reshape(n_ctx, 1, 2, MLSTM_HEADS, MLSTM_HEAD_DIM), s_m[..., 0, 0].reshape(n_ctx, 1, 2, MLSTM_HEADS))
```

```python
import functools

import numpy as np
import jax
import jax.numpy as jnp
from jax import lax
from jax.experimental import pallas as pl
from jax.experimental.pallas import tpu as pltpu

F32 = jnp.float32
BF16 = jnp.bfloat16

D_MODEL = 2048
GRID_W = 64
ATT_HEADS = 8
ATT_KV_HEADS = 2
ATT_HEAD_DIM = 128
ATT_GROUP = ATT_HEADS // ATT_KV_HEADS
ATT_WIDTH = ATT_HEADS * ATT_HEAD_DIM
KV_WIDTH = ATT_KV_HEADS * ATT_HEAD_DIM
ROPE_THETA = 10000.0
MLSTM_HEADS = 4
MLSTM_HEAD_DIM = 256
MLSTM_WIDTH = MLSTM_HEADS * MLSTM_HEAD_DIM
CHUNK = 256
N_GATES = 4 * MLSTM_HEADS
N_CHAINS = 2 * MLSTM_HEADS
MAIN_WIDTH = ATT_WIDTH + 2 * KV_WIDTH + 4 * MLSTM_WIDTH
N_GROUPS = 4
EXPERTS_PER_GROUP = 8
N_EXPERTS = N_GROUPS * EXPERTS_PER_GROUP
D_EXPERT = 512
N_MOD = 6
TM_MOE = 256
TM_PAIR = 2 * TM_MOE
MIX_TILE = 512
ATT_KEY_CHUNK = 256
Q_SCALE = ATT_HEAD_DIM ** -0.5 * float(np.log2(np.e))
EPS = 1e-6

LANES = 128
AUG = MLSTM_HEAD_DIM + LANES
N_SCAN_ROWS = 5 * N_CHAINS
ROUTER_LANE0 = N_GROUPS
VMEM_LIMIT = 56 * 1024 * 1024


def _cparams(*sem):
    return pltpu.CompilerParams(dimension_semantics=sem, vmem_limit_bytes=VMEM_LIMIT)


def _const_spec(shape):
    nd = len(shape)
    return pl.BlockSpec(shape, lambda *_: (0,) * nd, pipeline_mode=pl.Buffered(1))


def _split_hi_lo(x):
    hi = x.astype(BF16)
    lo = (x - hi.astype(F32)).astype(BF16)
    return hi, lo


def _dot(a, b):
    return jnp.dot(a, b, preferred_element_type=F32)


def _hi_lo_cat(w):
    return jnp.concatenate(_split_hi_lo(w), axis=1)


def _dot_hi_lo(a_hi, a_lo, w_ref):
    r = _dot(a_hi, w_ref[...])
    return r[:, :LANES] + r[:, LANES:] + _dot(a_lo, w_ref[:, :LANES])


def _rms(x, g):
    return x * lax.rsqrt(jnp.mean(x * x, axis=-1, keepdims=True) + EPS) * g


def _mod_kernel(c_ref, w_ref, b_ref, o_ref):
    c = c_ref[...]
    s = c / (1.0 + jnp.exp(-c))
    s_hi = s.astype(BF16).astype(F32)
    lhs = jnp.concatenate([s_hi, s - s_hi], axis=0).astype(BF16)
    w_hi, w_lo = _split_hi_lo(w_ref[...])
    r = _dot(lhs, w_hi)
    r2 = _dot(lhs, w_lo)
    o_ref[...] = r[:8] + r[8:] + r2[:8] + b_ref[...]


def _modulation(cond, mod_w, mod_b):
    n = mod_w.shape[1]
    tn = 1024
    return pl.pallas_call(
        _mod_kernel,
        grid=(n // tn,),
        in_specs=[pl.BlockSpec((8, D_MODEL), lambda j: (0, 0)),
                  pl.BlockSpec((D_MODEL, tn), lambda j: (0, j)),
                  pl.BlockSpec((1, tn), lambda j: (0, j))],
        out_specs=pl.BlockSpec((8, tn), lambda j: (0, j)),
        out_shape=jax.ShapeDtypeStruct((8, n), F32),
        compiler_params=_cparams("parallel"),
        name="modulation",
    )(cond, mod_w, mod_b)


def _cast_kernel(w_ref, o_ref):
    o_ref[...] = w_ref[0].astype(BF16)


def _cast_bf16(w, n_cols, tn=512):
    rows = w.shape[1]
    return pl.pallas_call(
        _cast_kernel, grid=(n_cols // tn,),
        in_specs=[pl.BlockSpec((1, rows, tn), lambda j: (0, 0, j))],
        out_specs=pl.BlockSpec((rows, tn), lambda j: (0, j)),
        out_shape=jax.ShapeDtypeStruct((rows, n_cols), BF16),
        compiler_params=_cparams("parallel"), name="cast_bf16",
    )(w)


def _cast_t_kernel(wt_ref, o_ref):
    o_ref[...] = wt_ref[...].T.astype(BF16)


def _cast_bf16_t(wt, n_cols, tn=512):
    rows = wt.shape[1]
    return pl.pallas_call(
        _cast_t_kernel, grid=(n_cols // tn,),
        in_specs=[pl.BlockSpec((tn, rows), lambda j: (j, 0))],
        out_specs=pl.BlockSpec((rows, tn), lambda j: (0, j)),
        out_shape=jax.ShapeDtypeStruct((rows, n_cols), BF16),
        compiler_params=_cparams("parallel"), name="cast_bf16_t",
    )(wt)


def _gate_cols_kernel(wt_ref, o_ref):
    sub = lax.broadcasted_iota(jnp.int32, wt_ref.shape, 0)
    hi, lo = _split_hi_lo(jnp.where(sub < N_GATES, wt_ref[...], 0.0).T)
    o_ref[:, :LANES] = hi
    o_ref[:, LANES:] = lo


def _gate_cols(wt):
    rows = wt.shape[1]
    return pl.pallas_call(
        _gate_cols_kernel, grid=(1,),
        in_specs=[pl.BlockSpec((LANES, rows), lambda i: (MAIN_WIDTH // LANES, 0))],
        out_specs=pl.BlockSpec((rows, 2 * LANES), lambda i: (0, 0)),
        out_shape=jax.ShapeDtypeStruct((rows, 2 * LANES), BF16),
        compiler_params=_cparams("arbitrary"), name="gate_cols",
    )(wt)


def _pair_swap(x):
    lane = lax.broadcasted_iota(jnp.int32, x.shape, 1)
    return jnp.where((lane & 1) == 0, pltpu.roll(x, LANES - 1, 1), pltpu.roll(x, 1, 1))


def _proj_kernel(x_ref, mod_ref, n1_ref, w_ref, wg_ref, bg_ref, qg_ref, kg_ref, cos_ref, sin_ref,
                 q_ref, k_ref, v_ref, qm_ref, kmt_ref, vm_ref, om_ref, row_ref, col_ref, *kv_refs, rope, tm):
    mod = mod_ref[0]
    shift, scale = mod[:, :D_MODEL], mod[:, D_MODEL:2 * D_MODEL]
    h = _rms(x_ref[...], n1_ref[...]) * (1.0 + scale) + shift
    h_hi, h_lo = _split_hi_lo(h)

    g = _dot_hi_lo(h_hi, h_lo, wg_ref) + bg_ref[...]
    for cc in range(tm // CHUNK):
        rows, cols = _gate_scan_chunk(g[cc * CHUNK:(cc + 1) * CHUNK, :])
        row_ref[cc] = rows
        col_ref[cc * CHUNK:(cc + 1) * CHUNK, :] = cols

    def rot(seg):
        return seg * cos_ref[...] + _pair_swap(seg) * sin_ref[...] if rope else seg

    qa = _dot(h_hi, w_ref[:, :ATT_WIDTH])
    for hh in range(ATT_HEADS):
        sl = slice(hh * ATT_HEAD_DIM, (hh + 1) * ATT_HEAD_DIM)
        seg = rot(_rms(qa[:, sl], qg_ref[...]))
        q_ref[:, sl] = (seg * Q_SCALE).astype(BF16)

    kv = _dot(h_hi, w_ref[:, ATT_WIDTH:ATT_WIDTH + 2 * KV_WIDTH])
    for hh in range(ATT_KV_HEADS):
        sl = slice(hh * ATT_HEAD_DIM, (hh + 1) * ATT_HEAD_DIM)
        seg = _rms(kv[:, sl], kg_ref[...])
        if kv_refs:
            kv_refs[0][:, sl] = seg
        k_ref[:, sl] = rot(seg).astype(BF16)
    va = kv[:, KV_WIDTH:]
    if kv_refs:
        kv_refs[1][...] = va
    v_ref[...] = va.astype(BF16)

    c0 = ATT_WIDTH + 2 * KV_WIDTH
    qm_ref[...] = (_dot(h_hi, w_ref[:, c0:c0 + MLSTM_WIDTH]) * MLSTM_HEAD_DIM ** -0.5).astype(BF16)
    km = _dot(h_hi, w_ref[:, c0 + MLSTM_WIDTH:c0 + 2 * MLSTM_WIDTH])
    for cc in range(tm // CHUNK):
        kmt_ref[cc] = km[cc * CHUNK:(cc + 1) * CHUNK, :].T.astype(BF16)
    vm_ref[...] = _dot(h_hi, w_ref[:, c0 + 2 * MLSTM_WIDTH:c0 + 3 * MLSTM_WIDTH]).astype(BF16)
    om_ref[...] = _dot(h_hi, w_ref[:, c0 + 3 * MLSTM_WIDTH:c0 + 4 * MLSTM_WIDTH]).astype(BF16)


def _project(x2d, mod3, row_of_tile, n1, w_main, wg_cat, bg, qg, kg, cos_t, sin_t, *, rope, emit_kv, tm=256):
    t = x2d.shape[0]
    n_pos = cos_t.shape[0] // tm
    row = lambda i: (i, 0)
    in_specs = [pl.BlockSpec((tm, D_MODEL), row),
                pl.BlockSpec((1, 1, N_MOD * D_MODEL), lambda i: (row_of_tile(i), 0, 0)),
                _const_spec((1, D_MODEL)),
                _const_spec((D_MODEL, MAIN_WIDTH)),
                _const_spec((D_MODEL, 2 * LANES)), _const_spec((1, LANES)),
                _const_spec((1, ATT_HEAD_DIM)), _const_spec((1, ATT_HEAD_DIM)),
                pl.BlockSpec((tm, ATT_HEAD_DIM), lambda i: (i % n_pos, 0)),
                pl.BlockSpec((tm, ATT_HEAD_DIM), lambda i: (i % n_pos, 0))]
    out_shape = [jax.ShapeDtypeStruct((t, ATT_WIDTH), BF16), jax.ShapeDtypeStruct((t, KV_WIDTH), BF16),
                 jax.ShapeDtypeStruct((t, KV_WIDTH), BF16), jax.ShapeDtypeStruct((t, MLSTM_WIDTH), BF16),
                 jax.ShapeDtypeStruct((t // CHUNK, MLSTM_WIDTH, CHUNK), BF16),
                 jax.ShapeDtypeStruct((t, MLSTM_WIDTH), BF16), jax.ShapeDtypeStruct((t, MLSTM_WIDTH), BF16),
                 jax.ShapeDtypeStruct((t // CHUNK, N_SCAN_ROWS, CHUNK), F32), jax.ShapeDtypeStruct((t, LANES), F32)]
    out_specs = [pl.BlockSpec((tm, ATT_WIDTH), row), pl.BlockSpec((tm, KV_WIDTH), row),
                 pl.BlockSpec((tm, KV_WIDTH), row), pl.BlockSpec((tm, MLSTM_WIDTH), row),
                 pl.BlockSpec((tm // CHUNK, MLSTM_WIDTH, CHUNK), lambda i: (i, 0, 0)),
                 pl.BlockSpec((tm, MLSTM_WIDTH), row), pl.BlockSpec((tm, MLSTM_WIDTH), row),
                 pl.BlockSpec((tm // CHUNK, N_SCAN_ROWS, CHUNK), lambda i: (i, 0, 0)), pl.BlockSpec((tm, LANES), row)]
    if emit_kv:
        out_shape += [jax.ShapeDtypeStruct((t, KV_WIDTH), F32)] * 2
        out_specs += [pl.BlockSpec((tm, KV_WIDTH), row)] * 2
    return pl.pallas_call(
        functools.partial(_proj_kernel, rope=rope, tm=tm),
        grid=(t // tm,), in_specs=in_specs, out_specs=out_specs, out_shape=out_shape,
        compiler_params=_cparams("parallel"), name="in_proj",
    )(x2d, mod3, n1, w_main, wg_cat, bg, qg, kg, cos_t, sin_t)


def _lane_scan(x, op, fill, is_fwd, lane):
    s = 1
    while s < CHUNK:
        from_left = jnp.where(lane >= s, pltpu.roll(x, s, 1), fill)
        from_right = jnp.where(lane < CHUNK - s, pltpu.roll(x, CHUNK - s, 1), fill)
        x = op(x, jnp.where(is_fwd, from_left, from_right))
        s *= 2
    return x


def _gate_scan_chunk(g):
    lane = lax.broadcasted_iota(jnp.int32, (N_CHAINS, CHUNK), 1)
    is_fwd = lax.broadcasted_iota(jnp.int32, (N_CHAINS, CHUNK), 0) < MLSTM_HEADS
    gt = g.T
    fwd, bwd = gt[:N_CHAINS], gt[N_CHAINS:2 * N_CHAINS]
    li = jnp.where(is_fwd, fwd, pltpu.roll(bwd, MLSTM_HEADS, 0))
    f = jnp.where(is_fwd, pltpu.roll(fwd, MLSTM_HEADS, 0), bwd)
    lf = jnp.minimum(f, 0.0) - jnp.log(1.0 + jnp.exp(-jnp.abs(f)))
    b = _lane_scan(lf, jnp.add, 0.0, is_fwd, lane)
    a = li - b
    run_max = _lane_scan(a, jnp.maximum, -jnp.inf, is_fwd, lane)
    all_max = jnp.broadcast_to(jnp.max(a, axis=1, keepdims=True), a.shape)
    last = jnp.where(is_fwd, CHUNK - 1, 0)
    total = jnp.broadcast_to(jnp.sum(jnp.where(lane == last, b, 0.0), axis=1, keepdims=True), a.shape)
    rows = jnp.concatenate([a, b, run_max, all_max, total], axis=0)
    padded = jnp.concatenate([rows, jnp.zeros((LANES - N_SCAN_ROWS, CHUNK), F32)], axis=0)
    return rows, padded.T


def _mlstm_kernel(*refs, n_chunks, has_state, emit_state):
    it = iter(refs)
    q_ref, kt_ref, v_ref, row_ref, col_ref = [next(it) for _ in range(5)]
    c0_ref, m0_ref = (next(it), next(it)) if has_state else (None, None)
    o_ref = next(it)
    c_out, n_out, m_out = (next(it), next(it), next(it)) if emit_state else (None, None, None)
    caug = next(it)

    fresh = not has_state and n_chunks == 1
    start = lambda cc: cc * CHUNK if isinstance(cc, int) else pl.multiple_of(cc * CHUNK, CHUNK)
    b = pl.program_id(0)
    if has_state:
        caug[...] = c0_ref[0]
        m_init = tuple(jnp.full((1, 1), m0_ref[b * N_CHAINS + r], F32) for r in range(N_CHAINS))
    else:
        if not fresh:
            caug[...] = jnp.zeros(caug.shape, F32)
        m_init = tuple(jnp.zeros((1, 1), F32) for _ in range(N_CHAINS))

    sub = lax.broadcasted_iota(jnp.int32, (CHUNK, CHUNK), 0)
    lane = lax.broadcasted_iota(jnp.int32, (CHUNK, CHUNK), 1)
    ones_col = (lax.broadcasted_iota(jnp.int32, (CHUNK, LANES), 1) == 0).astype(BF16)

    def chain(r, cc, m_prev):
        d, hd = divmod(r, MLSTM_HEADS)
        t0 = start(cc)
        hs = slice(hd * MLSTM_HEAD_DIM, (hd + 1) * MLSTM_HEAD_DIM)
        rows = row_ref[cc]
        cols = col_ref[pl.ds(t0, CHUNK), :]
        row = lambda k: rows[k * N_CHAINS + r:k * N_CHAINS + r + 1, :]
        col = lambda k: cols[:, k * N_CHAINS + r:k * N_CHAINS + r + 1]
        q = q_ref[pl.ds(t0, CHUNK), hs]
        kt = kt_ref[cc, hs, :]
        vaug = jnp.concatenate([v_ref[pl.ds(t0, CHUNK), hs], ones_col], axis=1)

        m_col = jnp.maximum(m_prev, col(2))
        keep = (lane <= sub) if d == 0 else (lane >= sub)
        w = jnp.where(keep, jnp.exp(row(0) - m_col), 0.0)
        w_inter = jnp.exp(m_prev - m_col)
        p = (_dot(q, kt) * w).astype(BF16)
        intra = _dot(p, vaug)
        num, den = intra[:, :MLSTM_HEAD_DIM], intra[:, MLSTM_HEAD_DIM:MLSTM_HEAD_DIM + 1]
        if not fresh:
            inter = _dot(q, caug[r].astype(BF16))
            num = num + w_inter * inter[:, :MLSTM_HEAD_DIM]
            den = den + w_inter * inter[:, MLSTM_HEAD_DIM:MLSTM_HEAD_DIM + 1]
        h_out = num / jnp.maximum(jnp.abs(den), jnp.exp(-(col(1) + m_col)))

        m_last = jnp.maximum(m_prev, row(3))
        kw = (kt.astype(F32) * jnp.exp(row(0) - m_last)).astype(BF16)
        update = _dot(kw, vaug)
        caug[r] = update if fresh else jnp.exp(m_prev - m_last[:, :1]) * caug[r] + update
        return h_out, (row(4) + m_last)[:, :1]

    def make_body(add_fwd, add_bwd):
        def body(c, ms):
            new = []
            for r in range(N_CHAINS):
                is_fwd = r < MLSTM_HEADS
                cc = c if is_fwd else n_chunks - 1 - c
                h_out, m_new = chain(r, cc, ms[r])
                hd = r % MLSTM_HEADS
                where = (pl.ds(start(cc), CHUNK), slice(hd * MLSTM_HEAD_DIM, (hd + 1) * MLSTM_HEAD_DIM))
                if add_fwd if is_fwd else add_bwd:
                    o_ref[where] += h_out
                else:
                    o_ref[where] = h_out
                new.append(m_new)
            return tuple(new)
        return body

    half = n_chunks // 2
    ms = m_init
    if half:
        ms = lax.fori_loop(0, half, make_body(False, False), ms)
    if n_chunks % 2:
        ms = make_body(False, True)(half, ms)
    if half:
        ms = lax.fori_loop(n_chunks - half, n_chunks, make_body(True, True), ms)
    if emit_state:
        for r in range(N_CHAINS):
            c_out[0, r] = caug[r, :, :MLSTM_HEAD_DIM]
            n_out[0, r] = caug[r, :, MLSTM_HEAD_DIM:].T[:1, :]
            m_out[0, r] = jnp.broadcast_to(ms[r], (1, LANES))


def _mlstm(qm, kmt, vm, rowq, colq, n_batch, seq, state=None, emit_state=False):
    nc = seq // CHUNK
    mode = {}
    in_specs = [pl.BlockSpec((seq, MLSTM_WIDTH), lambda b: (b, 0), **mode),
                pl.BlockSpec((nc, MLSTM_WIDTH, CHUNK), lambda b: (b, 0, 0), **mode),
                pl.BlockSpec((seq, MLSTM_WIDTH), lambda b: (b, 0), **mode),
                pl.BlockSpec((nc, N_SCAN_ROWS, CHUNK), lambda b: (b, 0, 0), **mode),
                pl.BlockSpec((seq, LANES), lambda b: (b, 0), **mode)]
    args = [qm, kmt, vm, rowq, colq]
    if state is not None:
        in_specs += [pl.BlockSpec((1, N_CHAINS, MLSTM_HEAD_DIM, AUG), lambda b: (b, 0, 0, 0), **mode),
                     pl.BlockSpec(memory_space=pltpu.SMEM)]
        args += list(state)
    out_shape = [jax.ShapeDtypeStruct((n_batch * seq, MLSTM_WIDTH), F32)]
    out_specs = [pl.BlockSpec((seq, MLSTM_WIDTH), lambda b: (b, 0))]
    if emit_state:
        out_shape += [jax.ShapeDtypeStruct((n_batch, N_CHAINS, MLSTM_HEAD_DIM, MLSTM_HEAD_DIM), F32),
                      jax.ShapeDtypeStruct((n_batch, N_CHAINS, 1, MLSTM_HEAD_DIM), F32),
                      jax.ShapeDtypeStruct((n_batch, N_CHAINS, 1, LANES), F32)]
        out_specs += [pl.BlockSpec((1, N_CHAINS, MLSTM_HEAD_DIM, MLSTM_HEAD_DIM), lambda b: (b, 0, 0, 0)),
                      pl.BlockSpec((1, N_CHAINS, 1, MLSTM_HEAD_DIM), lambda b: (b, 0, 0, 0)),
                      pl.BlockSpec((1, N_CHAINS, 1, LANES), lambda b: (b, 0, 0, 0))]
    return pl.pallas_call(
        functools.partial(_mlstm_kernel, n_chunks=nc, has_state=state is not None, emit_state=emit_state),
        grid=(n_batch,), in_specs=in_specs, out_specs=out_specs, out_shape=out_shape,
        scratch_shapes=[pltpu.VMEM((N_CHAINS, MLSTM_HEAD_DIM, AUG), F32)],
        compiler_params=_cparams("parallel"), name="mlstm",
    )(*args)


def _attn_kernel(*refs, tq, has_cache):
    if has_cache:
        q_ref, k_ref, v_ref, ck_ref, cv_ref, o_ref = refs
    else:
        q_ref, k_ref, v_ref, o_ref = refs
    q = q_ref[...]
    qs = jnp.concatenate([q[:, g * ATT_HEAD_DIM:(g + 1) * ATT_HEAD_DIM] for g in range(ATT_GROUP)], axis=0)
    nt = (((1,), (1,)), ((), ()))
    seq = k_ref.shape[0]
    kc = min(seq, ATT_KEY_CHUNK)
    chunks = [(k_ref, v_ref, c * kc) for c in range(seq // kc)]
    if has_cache:
        chunks.insert(0, (ck_ref, cv_ref, None))
    m = l = o = None
    for kr, vr, start in chunks:
        kk, vv = (kr[...], vr[...]) if start is None else (kr[start:start + kc, :], vr[start:start + kc, :])
        s = lax.dot_general(qs, kk.astype(BF16), nt, preferred_element_type=F32)
        mc = jnp.max(s, axis=-1, keepdims=True)
        if m is None:
            m = mc
            p = jnp.exp2(s - m)
            l = jnp.sum(p, axis=-1, keepdims=True)
            o = _dot(p.astype(BF16), vv.astype(BF16))
        else:
            m_new = jnp.maximum(m, mc)
            alpha = jnp.exp2(m - m_new)
            p = jnp.exp2(s - m_new)
            l = alpha * l + jnp.sum(p, axis=-1, keepdims=True)
            o = alpha * o + _dot(p.astype(BF16), vv.astype(BF16))
            m = m_new
    o = o / l
    for g in range(ATT_GROUP):
        o_ref[:, g * ATT_HEAD_DIM:(g + 1) * ATT_HEAD_DIM] = o[g * tq:(g + 1) * tq].astype(BF16)


def _attention(q, k, v, n_batch, seq, cache=None, tq=256):
    nqb = seq // tq
    gw = ATT_GROUP * ATT_HEAD_DIM
    in_specs = [pl.BlockSpec((tq, gw), lambda b, h, i: (b * nqb + i, h)),
                pl.BlockSpec((seq, ATT_HEAD_DIM), lambda b, h, i: (b, h)),
                pl.BlockSpec((seq, ATT_HEAD_DIM), lambda b, h, i: (b, h))]
    args = [q, k, v]
    if cache is not None:
        past = cache[0].shape[0] // n_batch
        in_specs += [pl.BlockSpec((past, ATT_HEAD_DIM), lambda b, h, i: (b, h))] * 2
        args += list(cache)
    return pl.pallas_call(
        functools.partial(_attn_kernel, tq=tq, has_cache=cache is not None),
        grid=(n_batch, ATT_KV_HEADS, nqb), in_specs=in_specs,
        out_specs=pl.BlockSpec((tq, gw), lambda b, h, i: (b * nqb + i, h)),
        out_shape=jax.ShapeDtypeStruct((n_batch * seq, ATT_WIDTH), BF16),
        compiler_params=_cparams("parallel", "parallel", "parallel"), name="attention",
    )(*args)


def _mix_kernel(attn_ref, hm_ref, om_ref, x_ref, mod_ref, mg_ref, wo_ref, n2_ref, rw_ref, rb_ref, *rest, sub):
    for r0 in range(0, x_ref.shape[0], sub):
        _mix_rows(slice(r0, r0 + sub), attn_ref, hm_ref, om_ref, x_ref, mod_ref, mg_ref, wo_ref, n2_ref, rw_ref,
                  rb_ref, *rest[-3:])


def _mix_rows(rows, attn_ref, hm_ref, om_ref, x_ref, mod_ref, mg_ref, wo_ref, n2_ref, rw_ref, rb_ref,
              x1_ref, h2_ref, route_ref):
    mod = mod_ref[0]
    gate1 = mod[:, 2 * D_MODEL:3 * D_MODEL]
    shift2, scale2 = mod[:, 3 * D_MODEL:4 * D_MODEL], mod[:, 4 * D_MODEL:5 * D_MODEL]
    hm = hm_ref[rows, :]
    mg = mg_ref[...]
    parts = []
    for hd in range(MLSTM_HEADS):
        sl = slice(hd * MLSTM_HEAD_DIM, (hd + 1) * MLSTM_HEAD_DIM)
        parts.append(_rms(hm[:, sl], mg[:, sl]))
    om = om_ref[rows, :].astype(F32)
    hmg = jnp.concatenate(parts, axis=1) * (1.0 / (1.0 + jnp.exp(-om)))
    y = _dot(attn_ref[rows, :], wo_ref[:ATT_WIDTH, :]) + _dot(hmg.astype(BF16), wo_ref[ATT_WIDTH:, :])
    x1 = x_ref[rows, :] + gate1 * y
    x1_ref[rows, :] = x1
    h2 = _rms(x1, n2_ref[...]) * (1.0 + scale2) + shift2
    h2_hi, h2_lo = _split_hi_lo(h2)
    h2_ref[rows, :] = _pack_halves(h2)

    lg = _dot_hi_lo(h2_hi, h2_lo, rw_ref) + rb_ref[...]
    lane = lax.broadcasted_iota(jnp.int32, lg.shape, 1).astype(F32)
    neg = -jnp.inf
    first = lambda hit: jnp.min(jnp.where(hit, lane, float(LANES)), axis=-1, keepdims=True)
    gl = jnp.where(lane < N_GROUPS, lg, neg)
    gmax = jnp.max(gl, axis=-1, keepdims=True)
    grp = first(gl == gmax)
    p_grp = 1.0 / jnp.sum(jnp.exp(gl - gmax), axis=-1, keepdims=True)
    lo = ROUTER_LANE0 + grp * EXPERTS_PER_GROUP
    el = jnp.where((lane >= lo) & (lane < lo + EXPERTS_PER_GROUP), lg, neg)
    m1 = jnp.max(el, axis=-1, keepdims=True)
    i1 = first(el == m1)
    el2 = jnp.where(lane == i1, neg, el)
    m2 = jnp.max(el2, axis=-1, keepdims=True)
    i2 = first(el2 == m2)
    r = jnp.exp(m2 - m1)
    w1 = p_grp / (1.0 + r)
    w2 = w1 * r
    route_ref[rows, :] = jnp.where(lane == 0.0, i1 - ROUTER_LANE0, jnp.where(lane == 1.0, i2 - ROUTER_LANE0,
                                   jnp.where(lane == 2.0, w1, jnp.where(lane == 3.0, w2, 0.0))))


def _mix_out(attn, hm, om, x2d, mod3, row_of_tile, mg, w_out, n2, rw_cat, rb, h2_all, tile0, t_total, tm=512):
    t = x2d.shape[0]
    row = lambda i: (i, 0)
    in_specs = [pl.BlockSpec((tm, ATT_WIDTH), row), pl.BlockSpec((tm, MLSTM_WIDTH), row),
                pl.BlockSpec((tm, MLSTM_WIDTH), row), pl.BlockSpec((tm, D_MODEL), row),
                pl.BlockSpec((1, 1, N_MOD * D_MODEL), lambda i: (row_of_tile(i), 0, 0)),
                _const_spec((1, MLSTM_WIDTH)), _const_spec((D_MODEL, D_MODEL)), _const_spec((1, D_MODEL)),
                _const_spec((D_MODEL, 2 * LANES)), _const_spec((1, LANES))]
    args = [attn, hm, om, x2d, mod3, mg, w_out, n2, rw_cat, rb]
    aliases = {}
    if h2_all is not None:
        aliases = {len(args): 1}
        in_specs.append(pl.BlockSpec(memory_space=pl.ANY))
        args.append(h2_all)
    return pl.pallas_call(
        functools.partial(_mix_kernel, sub=256), grid=(t // tm,), in_specs=in_specs,
        out_specs=[pl.BlockSpec((tm, D_MODEL), row), pl.BlockSpec((tm, D_MODEL // 2), lambda i: (tile0 + i, 0)),
                   pl.BlockSpec((tm, LANES), row)],
        out_shape=[jax.ShapeDtypeStruct((t, D_MODEL), F32), jax.ShapeDtypeStruct((t_total, D_MODEL // 2), jnp.uint32),
                   jax.ShapeDtypeStruct((t, LANES), F32)],
        input_output_aliases=aliases,
        compiler_params=_cparams("parallel"), name="mix_out",
    )(*args)


def _rank_kernel(route_ref, rank_ref, cnt_ref, run_ref, tri_ref):
    tr = route_ref.shape[0]

    @pl.when(pl.program_id(0) == 0)
    def _():
        run_ref[...] = jnp.zeros(run_ref.shape, F32)
        tri_ref[...] = (lax.broadcasted_iota(jnp.int32, (tr, tr), 1)
                        < lax.broadcasted_iota(jnp.int32, (tr, tr), 0)).astype(BF16)

    route = route_ref[...]
    lane = lax.broadcasted_iota(jnp.int32, route.shape, 1).astype(F32)
    hit1, hit2 = lane == route[:, 0:1], lane == route[:, 1:2]
    onehot = jnp.where(hit1, 1.0, jnp.where(hit2, 1.0, 0.0))
    before = _dot(tri_ref[...], onehot.astype(BF16)) + run_ref[0:1, :]
    r1 = jnp.sum(jnp.where(hit1, before, 0.0), axis=-1, keepdims=True)
    r2 = jnp.sum(jnp.where(hit2, before, 0.0), axis=-1, keepdims=True)
    rank_ref[...] = jnp.where(lane == 0.0, r1, jnp.where(lane == 1.0, r2, 0.0))
    run_ref[...] = run_ref[...] + jnp.sum(onehot, axis=0, keepdims=True)
    cnt_ref[...] = run_ref[...]


def _ranks(route, tr=512):
    t = route.shape[0]
    return pl.pallas_call(
        _rank_kernel, grid=(t // tr,),
        in_specs=[pl.BlockSpec((tr, LANES), lambda i: (i, 0))],
        out_specs=[pl.BlockSpec((tr, LANES), lambda i: (i, 0)), pl.BlockSpec((8, LANES), lambda i: (0, 0))],
        out_shape=[jax.ShapeDtypeStruct((t, LANES), F32), jax.ShapeDtypeStruct((8, LANES), F32)],
        scratch_shapes=[pltpu.VMEM((8, LANES), F32), pltpu.VMEM((tr, tr), BF16)],
        compiler_params=_cparams("arbitrary"), name="expert_ranks",
    )(route)


def _pos_kernel(route_ref, rank_ref, start_ref, o_ref):
    route, rank = route_ref[...], rank_ref[...]
    lane = lax.broadcasted_iota(jnp.int32, route.shape, 1).astype(F32)
    start = start_ref[...]
    first = lambda col: jnp.sum(jnp.where(lane == route[:, col:col + 1], start, 0.0), axis=-1, keepdims=True)
    p1 = first(0) + rank[:, 0:1]
    p2 = first(1) + rank[:, 1:2]
    tile = jnp.where(lane == 0.0, p1, jnp.where(lane == 1.0, p2, 0.0))
    o_ref[...] = tile.T[:8, :].astype(jnp.int32)


def _positions(route, ranks, starts, tr=512):
    t = route.shape[0]
    pos = pl.pallas_call(
        _pos_kernel, grid=(t // tr,),
        in_specs=[pl.BlockSpec((tr, LANES), lambda i: (i, 0)), pl.BlockSpec((tr, LANES), lambda i: (i, 0)),
                  pl.BlockSpec((1, LANES), lambda i: (0, 0))],
        out_specs=pl.BlockSpec((8, tr), lambda i: (0, i)),
        out_shape=jax.ShapeDtypeStruct((8, t), jnp.int32),
        compiler_params=_cparams("parallel"), name="pair_rows",
    )(route, ranks, starts)
    return pos[:2].reshape(-1)


def _routing_plan(route, ranks, counts):
    t = route.shape[0]
    n_tiles = 2 * t // TM_PAIR + N_EXPERTS
    cnt = counts[0, :N_EXPERTS].astype(jnp.int32)
    padded = (cnt + TM_PAIR - 1) // TM_PAIR * TM_PAIR
    ends = jnp.cumsum(padded)
    starts = ends - padded
    pos = _positions(route, ranks, _pad_lanes(starts.astype(F32)[None]))
    n_used = (ends[-1:] // TM_PAIR).astype(jnp.int32)
    tile_start = jnp.arange(n_tiles, dtype=jnp.int32) * TM_PAIR
    tile_expert = jnp.sum((ends[None, :] <= tile_start[:, None]).astype(jnp.int32), axis=1)
    tile_expert = jnp.minimum(tile_expert, tile_expert[n_used[0] - 1]).astype(jnp.int32)
    mine = tile_expert[:, None] == jnp.arange(N_EXPERTS, dtype=jnp.int32)[None, :]
    rows_left = jnp.sum(jnp.where(mine, (starts + cnt)[None, :], 0), axis=1) - tile_start
    second_half = (rows_left > TM_MOE).astype(jnp.int32)
    tails = jnp.where(cnt > 0, starts + (cnt - 1) // TM_MOE * TM_MOE, -1).astype(jnp.int32)
    change = jnp.concatenate([jnp.ones((1,), jnp.int32), (tile_expert[1:] != tile_expert[:-1]).astype(jnp.int32)])
    weight_slot = (jnp.cumsum(change) - 1) % 2
    ids = jnp.arange(N_EXPERTS, dtype=jnp.int32)
    later_used = (padded > 0)[None, :] & (ids[None, :] > ids[:, None])
    next_used = jnp.min(jnp.where(later_used, ids[None, :], N_EXPERTS), axis=1)
    next_used = jnp.where(next_used < N_EXPERTS, next_used, -1)
    next_expert = jnp.sum(jnp.where(tile_expert[:, None] == ids[None, :], next_used[None, :], 0), axis=1)
    return pos, (tile_expert, n_used, weight_slot.astype(jnp.int32), next_expert.astype(jnp.int32), second_half), \
        tails, n_tiles * TM_PAIR


def _row_copies(pos_ref, n_pairs, tok0, n_rows, make_copy):
    def body(r, carry):
        for k in range(2):
            make_copy(k, r, pos_ref[k * n_pairs + tok0 + r]).start()
        return carry
    lax.fori_loop(0, n_rows, body, 0, unroll=8)


def _dispatch_kernel(pos_ref, tail_ref, h_ref, xs_ref, hbuf, zero_ref, tsem, rsem, *, td, n_tokens):
    i = pl.program_id(0)
    last = pl.num_programs(0) - 1
    tile_copy = lambda t: pltpu.make_async_copy(
        h_ref.at[pl.ds(pl.multiple_of(t * td, td), td)], hbuf.at[t % 3], tsem.at[t % 3])

    def wait_rows(t):
        for _ in range(2):
            pltpu.make_async_copy(hbuf.at[0], xs_ref.at[pl.ds(0, td)], rsem.at[t % 2]).wait()

    @pl.when(i == 0)
    def _():
        tile_copy(0).start()
        zero_ref[...] = jnp.zeros(zero_ref.shape, zero_ref.dtype)
        tail_copy = lambda e: pltpu.make_async_copy(
            zero_ref, xs_ref.at[pl.ds(pl.multiple_of(tail_ref[e], TM_MOE), TM_MOE)], rsem.at[1])
        for e in range(N_EXPERTS):
            pl.when(tail_ref[e] >= 0)(lambda e=e: tail_copy(e).start())
        for e in range(N_EXPERTS):
            pl.when(tail_ref[e] >= 0)(lambda e=e: tail_copy(e).wait())

    @pl.when(i < last)
    def _():
        tile_copy(i + 1).start()

    tile_copy(i).wait()
    src = hbuf.at[i % 3]
    _row_copies(pos_ref, n_tokens, i * td, td, lambda k, r, p: pltpu.make_async_copy(
        src.at[pl.ds(r, 1)], xs_ref.at[pl.ds(p, 1)], rsem.at[i % 2]))

    @pl.when(i >= 1)
    def _():
        wait_rows(i - 1)

    @pl.when(i == last)
    def _():
        wait_rows(i)


def _dispatch(pos, tails, h2, n_rows, td=256):
    t, width = h2.shape
    return pl.pallas_call(
        functools.partial(_dispatch_kernel, td=td, n_tokens=t),
        grid_spec=pltpu.PrefetchScalarGridSpec(
            num_scalar_prefetch=2, grid=(t // td,),
            in_specs=[pl.BlockSpec(memory_space=pl.ANY)],
            out_specs=pl.BlockSpec(memory_space=pl.ANY),
            scratch_shapes=[pltpu.VMEM((3, td, width), h2.dtype), pltpu.VMEM((TM_MOE, width), h2.dtype),
                            pltpu.SemaphoreType.DMA((3,)), pltpu.SemaphoreType.DMA((2,))]),
        out_shape=jax.ShapeDtypeStruct((n_rows, width), h2.dtype),
        compiler_params=_cparams("arbitrary"), name="dispatch",
    )(pos, tails, h2)


def _pack_halves(x):
    half = x.shape[1] // 2
    return pltpu.pack_elementwise([x[:, :half], x[:, half:]], packed_dtype=BF16)


def _unpack_halves(p, dtype):
    return tuple(pltpu.unpack_elementwise(p, index=i, packed_dtype=BF16, unpacked_dtype=F32).astype(dtype)
                 for i in range(2))


def _expert_kernel(te_ref, nu_ref, slot_ref, next_ref, full_ref, xs_ref, wg_ref, wu_ref, wd_ref, ys_ref,
                   wgf, wuf, wdf, wgb, wub, wdb, wsem):
    j = pl.program_id(0)

    def mlp(rows):
        xa, xb = _unpack_halves(xs_ref[rows, :], BF16)
        half = D_MODEL // 2
        g = _dot(xa, wgb[:half, :]) + _dot(xb, wgb[half:, :])
        u = _dot(xa, wub[:half, :]) + _dot(xb, wub[half:, :])
        a = (g / (1.0 + jnp.exp(-g))) * u
        ys_ref[rows, :] = _pack_halves(_dot(a.astype(BF16), wdb[...]))

    def weight_copies(e, s):
        return [pltpu.make_async_copy(w_ref.at[e], buf.at[s], wsem.at[s])
                for w_ref, buf in ((wg_ref, wgf), (wu_ref, wuf), (wd_ref, wdf))]

    @pl.when(j < nu_ref[0])
    def _():
        e, s = te_ref[j], slot_ref[j]

        @pl.when(j == 0)
        def _():
            for cp in weight_copies(e, s):
                cp.start()

        @pl.when((j == 0) | (e != te_ref[jnp.maximum(j - 1, 0)]))
        def _():
            for cp in weight_copies(e, s):
                cp.wait()
            nxt = next_ref[j]

            @pl.when(nxt >= 0)
            def _():
                for cp in weight_copies(nxt, 1 - s):
                    cp.start()

            wgb[...] = wgf[s].astype(BF16)
            wub[...] = wuf[s].astype(BF16)
            wdb[...] = wdf[s].astype(BF16)

        @pl.when(full_ref[j] == 1)
        def _():
            mlp(slice(None))

        @pl.when(full_ref[j] == 0)
        def _():
            mlp(slice(0, TM_MOE))
            ys_ref[TM_MOE:, :] = _pack_halves(jnp.zeros((TM_MOE, D_MODEL), F32))


def _experts(tile_expert, n_used, weight_slot, next_expert, second_half, xs, wg, wu, wd):
    n_tiles = xs.shape[0] // TM_PAIR
    tile = lambda j, te, nu, *_: (jnp.minimum(j, nu[0] - 1), 0)
    hbm = pl.BlockSpec(memory_space=pl.ANY)
    return pl.pallas_call(
        _expert_kernel,
        grid_spec=pltpu.PrefetchScalarGridSpec(
            num_scalar_prefetch=5, grid=(n_tiles,),
            in_specs=[pl.BlockSpec((TM_PAIR, D_MODEL // 2), tile), hbm, hbm, hbm],
            out_specs=pl.BlockSpec((TM_PAIR, D_MODEL // 2), tile),
            scratch_shapes=[pltpu.VMEM((2, D_MODEL, D_EXPERT), F32), pltpu.VMEM((2, D_MODEL, D_EXPERT), F32),
                            pltpu.VMEM((2, D_EXPERT, D_MODEL), F32),
                            pltpu.VMEM((D_MODEL, D_EXPERT), BF16), pltpu.VMEM((D_MODEL, D_EXPERT), BF16),
                            pltpu.VMEM((D_EXPERT, D_MODEL), BF16), pltpu.SemaphoreType.DMA((2,))]),
        out_shape=jax.ShapeDtypeStruct(xs.shape, jnp.uint32),
        compiler_params=_cparams("arbitrary"), name="experts",
    )(tile_expert, n_used, weight_slot, next_expert, second_half, xs, wg, wu, wd)


def _combine_kernel(pos_ref, x1_ref, route_ref, mod_ref, ys_ref, o_ref, ybuf, sem, *, tc, n_tokens, tok0):
    i = pl.program_id(0)
    slot = i % 2

    def gather(tile, s):
        _row_copies(pos_ref, n_tokens, tok0 + tile * tc, tc, lambda k, r, p: pltpu.make_async_copy(
            ys_ref.at[pl.ds(p, 1)], ybuf.at[s, k, pl.ds(r, 1)], sem.at[s]))

    @pl.when(i == 0)
    def _():
        gather(0, 0)

    @pl.when(i + 1 < pl.num_programs(0))
    def _():
        gather(i + 1, 1 - slot)

    for k in range(2):
        pltpu.make_async_copy(ys_ref.at[pl.ds(0, tc)], ybuf.at[slot, k], sem.at[slot]).wait()
    route = route_ref[...]
    w1, w2 = route[:, 2:3], route[:, 3:4]
    half = D_MODEL // 2
    for h, (y1, y2) in enumerate(zip(_unpack_halves(ybuf[slot, 0], F32), _unpack_halves(ybuf[slot, 1], F32))):
        cols = slice(h * half, (h + 1) * half)
        gate2 = mod_ref[0, :, 5 * D_MODEL + h * half:5 * D_MODEL + (h + 1) * half]
        o_ref[:, cols] = x1_ref[:, cols] + gate2 * (w1 * y1 + w2 * y2)


def _combine(pos, x1, route, mod3, row_of_tile, ys, tok0, n_tokens, tc=256):
    t = x1.shape[0]
    row = lambda i, *_: (i, 0)
    return pl.pallas_call(
        functools.partial(_combine_kernel, tc=tc, n_tokens=n_tokens, tok0=tok0),
        grid_spec=pltpu.PrefetchScalarGridSpec(
            num_scalar_prefetch=1, grid=(t // tc,),
            in_specs=[pl.BlockSpec((tc, D_MODEL), row), pl.BlockSpec((tc, LANES), row),
                      pl.BlockSpec((1, 1, N_MOD * D_MODEL), lambda i, *_: (row_of_tile(i), 0, 0)),
                      pl.BlockSpec(memory_space=pl.ANY)],
            out_specs=pl.BlockSpec((tc, D_MODEL), row),
            scratch_shapes=[pltpu.VMEM((2, 2, tc, ys.shape[1]), ys.dtype), pltpu.SemaphoreType.DMA((2,))]),
        out_shape=jax.ShapeDtypeStruct((t, D_MODEL), F32),
        compiler_params=_cparams("arbitrary"), name="combine",
    )(pos, x1, route, mod3, ys)


def _rope_tables(seq):
    pos = np.arange(seq)
    n_freq = ATT_HEAD_DIM // 4
    inv = ROPE_THETA ** (-np.arange(n_freq, dtype=np.float32) / n_freq)
    ang = np.concatenate([(pos // GRID_W).astype(np.float32)[:, None] * inv,
                          (pos % GRID_W).astype(np.float32)[:, None] * inv], axis=-1).astype(np.float32)
    ang = jnp.asarray(ang)
    cos, sin = jnp.cos(ang), jnp.sin(ang)
    cos_t = jnp.repeat(cos, 2, axis=1)
    sin_t = jnp.stack([-sin, sin], axis=-1).reshape(seq, ATT_HEAD_DIM)
    return cos_t, sin_t


def _pad_lanes(a):
    return jnp.pad(a, ((0, 0), (0, LANES - a.shape[1])))


def _layer(x2d, n_batch, seq, mod3, row_of_tile, lw, *, rope_tabs, cache, state, emit, h2_all, tile0, t_total):
    (n1, n2, w_main, wg_cat, bg, qg, kg, mg, w_out, rw_cat, rb) = lw
    cos_t, sin_t = rope_tabs
    outs = _project(x2d, mod3, row_of_tile(256), n1, w_main, wg_cat, bg, qg, kg, cos_t, sin_t,
                    rope=cache is not None, emit_kv=emit)
    q, k, v, qm, kmt, vm, om, rowq, colq = outs[:9]
    attn = _attention(q, k, v, n_batch, seq, cache=cache, tq=min(seq, 512))
    ml = _mlstm(qm, kmt, vm, rowq, colq, n_batch, seq, state=state, emit_state=emit)
    mixed = _mix_out(attn, ml[0], om, x2d, mod3, row_of_tile(MIX_TILE), mg, w_out, n2, rw_cat, rb,
                     h2_all, tile0, t_total, tm=MIX_TILE)
    return mixed, outs[9:], ml[1:]


def kernel(x_prompt, x_sample, cache_k, cache_v, state_C, state_n, state_m, c, c_ctx, mod_w, mod_b, norm1_g, norm2_g,
           w_in, b_gates, q_norm_g, k_norm_g, mlstm_norm_g, w_out, router_group_w, router_group_b, router_expert_w,
           router_expert_b, expert_w_gate, expert_w_up, expert_w_down):
    assert mod_w.shape[0] == 1, "single-layer stack"
    n_ctx, s_ctx, _ = x_prompt.shape
    n_lat, s_lat, _ = x_sample.shape
    t_ctx, t_lat = n_ctx * s_ctx, n_lat * s_lat
    t_all = t_ctx + t_lat
    ctx_row = n_lat

    cond = jnp.concatenate([c, c_ctx[None], jnp.zeros((8 - n_lat - 1, D_MODEL), F32)], axis=0)
    mod3 = _modulation(cond, mod_w[0], mod_b[0][None]).reshape(8, 1, N_MOD * D_MODEL)

    rw =jnp.concatenate([router_group_w[0], jnp.moveaxis(router_expert_w[0], 0, 1).reshape(D_MODEL, N_EXPERTS)], axis=1)
    rb = _pad_lanes(jnp.concatenate([router_group_b[0], router_expert_b[0].reshape(-1)])[None])
    w_in_t = w_in[0].T
    lw = (norm1_g, norm2_g, _cast_bf16_t(w_in_t, MAIN_WIDTH), _gate_cols(w_in_t), _pad_lanes(b_gates),
          q_norm_g, k_norm_g, mlstm_norm_g, _cast_bf16(w_out, D_MODEL), _hi_lo_cat(_pad_lanes(rw)), rb)
    rope_tabs = _rope_tables(s_lat)

    ctx_rows = lambda tm: (lambda i: ctx_row)
    lat_rows = lambda tm: (lambda i: i // (s_lat // tm))
    (x1p, h2_all, routep), (ka, va), (s_c, s_n, s_m) = _layer(
        x_prompt.reshape(t_ctx, D_MODEL), n_ctx, s_ctx, mod3, ctx_rows, lw,
        rope_tabs=rope_tabs, cache=None, state=None, emit=True, h2_all=None, tile0=0, t_total=t_all)

    caug0 = jnp.concatenate([state_C[:, 0], state_n[:, 0][..., None],
                             jnp.zeros(state_n[:, 0].shape + (LANES - 1,), F32)], axis=-1)
    caug0 = caug0.reshape(n_lat, N_CHAINS, MLSTM_HEAD_DIM, AUG)
    past = cache_k.shape[2]
    cache = (cache_k[:, 0].reshape(n_lat * past, KV_WIDTH), cache_v[:, 0].reshape(n_lat * past, KV_WIDTH))
    (x1s, h2_all, routes), _, _ = _layer(
        x_sample.reshape(t_lat, D_MODEL), n_lat, s_lat, mod3, lat_rows, lw,
        rope_tabs=rope_tabs, cache=cache, state=(caug0, state_m[:, 0].reshape(-1)), emit=False,
        h2_all=h2_all, tile0=t_ctx // MIX_TILE, t_total=t_all)

    route = jnp.concatenate([routep, routes], axis=0)
    ranks, counts = _ranks(route)
    pos, tile_plan, tails, n_rows = _routing_plan(route, ranks, counts)
    xs = _dispatch(pos, tails, h2_all, n_rows)
    y_sorted = _experts(*tile_plan, xs, expert_w_gate[0], expert_w_up[0], expert_w_down[0])
    yp = _combine(pos, x1p, routep, mod3, ctx_rows(256), y_sorted, 0, t_all)
    ys = _combine(pos, x1s, routes, mod3, lat_rows(256), y_sorted, t_ctx, t_all)

    kv_shape = (n_ctx, 1, s_ctx, ATT_KV_HEADS, ATT_HEAD_DIM)
    return (yp.reshape(x_prompt.shape), ys.reshape(x_sample.shape), ka.reshape(kv_shape), va.reshape(kv_shape),
            s_c.reshape(n_ctx, 1, 2, MLSTM_HEADS, MLSTM_HEAD_DIM, MLSTM_HEAD_DIM),
            s_n.reshape(n_ctx, 1, 2, MLSTM_HEADS, MLSTM_HEAD_DIM), s_m[..., 0, 0].reshape(n_ctx, 1, 2, MLSTM_HEADS))
```

```python
import functools

import numpy as np
import jax
import jax.numpy as jnp
from jax import lax
from jax.experimental import pallas as pl
from jax.experimental.pallas import tpu as pltpu

F32 = jnp.float32
BF16 = jnp.bfloat16

D_MODEL = 2048
GRID_W = 64
ATT_HEADS = 8
ATT_KV_HEADS = 2
ATT_HEAD_DIM = 128
ATT_GROUP = ATT_HEADS // ATT_KV_HEADS
ATT_WIDTH = ATT_HEADS * ATT_HEAD_DIM
KV_WIDTH = ATT_KV_HEADS * ATT_HEAD_DIM
ROPE_THETA = 10000.0
MLSTM_HEADS = 4
MLSTM_HEAD_DIM = 256
MLSTM_WIDTH = MLSTM_HEADS * MLSTM_HEAD_DIM
CHUNK = 256
N_GATES = 4 * MLSTM_HEADS
N_CHAINS = 2 * MLSTM_HEADS
MAIN_WIDTH = ATT_WIDTH + 2 * KV_WIDTH + 4 * MLSTM_WIDTH
N_GROUPS = 4
EXPERTS_PER_GROUP = 8
N_EXPERTS = N_GROUPS * EXPERTS_PER_GROUP
D_EXPERT = 512
N_MOD = 6
TM_MOE = 256
TM_PAIR = 2 * TM_MOE
MIX_TILE = 512
ATT_KEY_CHUNK = 256
Q_SCALE = ATT_HEAD_DIM ** -0.5 * float(np.log2(np.e))
EPS = 1e-6

LANES = 128
AUG = MLSTM_HEAD_DIM + LANES
N_SCAN_ROWS = 5 * N_CHAINS
ROUTER_LANE0 = N_GROUPS
VMEM_LIMIT = 56 * 1024 * 1024


def _cparams(*sem):
    return pltpu.CompilerParams(dimension_semantics=sem, vmem_limit_bytes=VMEM_LIMIT)


def _const_spec(shape):
    nd = len(shape)
    return pl.BlockSpec(shape, lambda *_: (0,) * nd, pipeline_mode=pl.Buffered(1))


def _split_hi_lo(x):
    hi = x.astype(BF16)
    lo = (x - hi.astype(F32)).astype(BF16)
    return hi, lo


def _dot(a, b):
    return jnp.dot(a, b, preferred_element_type=F32)


def _hi_lo_cat(w):
    return jnp.concatenate(_split_hi_lo(w), axis=1)


def _dot_hi_lo(a_hi, a_lo, w_ref):
    r = _dot(a_hi, w_ref[...])
    return r[:, :LANES] + r[:, LANES:] + _dot(a_lo, w_ref[:, :LANES])


def _rms(x, g):
    return x * lax.rsqrt(jnp.mean(x * x, axis=-1, keepdims=True) + EPS) * g


def _mod_kernel(c_ref, w_ref, b_ref, o_ref):
    c = c_ref[...]
    s = c / (1.0 + jnp.exp(-c))
    s_hi = s.astype(BF16).astype(F32)
    lhs = jnp.concatenate([s_hi, s - s_hi], axis=0).astype(BF16)
    w_hi, w_lo = _split_hi_lo(w_ref[...])
    r = _dot(lhs, w_hi)
    r2 = _dot(lhs, w_lo)
    o_ref[...] = r[:8] + r[8:] + r2[:8] + b_ref[...]


def _modulation(cond, mod_w, mod_b):
    n = mod_w.shape[1]
    tn = 1024
    return pl.pallas_call(
        _mod_kernel,
        grid=(n // tn,),
        in_specs=[pl.BlockSpec((8, D_MODEL), lambda j: (0, 0)),
                  pl.BlockSpec((D_MODEL, tn), lambda j: (0, j)),
                  pl.BlockSpec((1, tn), lambda j: (0, j))],
        out_specs=pl.BlockSpec((8, tn), lambda j: (0, j)),
        out_shape=jax.ShapeDtypeStruct((8, n), F32),
        compiler_params=_cparams("parallel"),
        name="modulation",
    )(cond, mod_w, mod_b)


def _cast_kernel(w_ref, o_ref):
    o_ref[...] = w_ref[0].astype(BF16)


def _cast_bf16(w, n_cols, tn=512):
    rows = w.shape[1]
    return pl.pallas_call(
        _cast_kernel, grid=(n_cols // tn,),
        in_specs=[pl.BlockSpec((1, rows, tn), lambda j: (0, 0, j))],
        out_specs=pl.BlockSpec((rows, tn), lambda j: (0, j)),
        out_shape=jax.ShapeDtypeStruct((rows, n_cols), BF16),
        compiler_params=_cparams("parallel"), name="cast_bf16",
    )(w)


def _cast_t_kernel(wt_ref, o_ref):
    o_ref[...] = wt_ref[...].T.astype(BF16)


def _cast_bf16_t(wt, n_cols, tn=512):
    rows = wt.shape[1]
    return pl.pallas_call(
        _cast_t_kernel, grid=(n_cols // tn,),
        in_specs=[pl.BlockSpec((tn, rows), lambda j: (j, 0))],
        out_specs=pl.BlockSpec((rows, tn), lambda j: (0, j)),
        out_shape=jax.ShapeDtypeStruct((rows, n_cols), BF16),
        compiler_params=_cparams("parallel"), name="cast_bf16_t",
    )(wt)


def _gate_cols_kernel(wt_ref, o_ref):
    sub = lax.broadcasted_iota(jnp.int32, wt_ref.shape, 0)
    hi, lo = _split_hi_lo(jnp.where(sub < N_GATES, wt_ref[...], 0.0).T)
    o_ref[:, :LANES] = hi
    o_ref[:, LANES:] = lo


def _gate_cols(wt):
    rows = wt.shape[1]
    return pl.pallas_call(
        _gate_cols_kernel, grid=(1,),
        in_specs=[pl.BlockSpec((LANES, rows), lambda i: (MAIN_WIDTH // LANES, 0))],
        out_specs=pl.BlockSpec((rows, 2 * LANES), lambda i: (0, 0)),
        out_shape=jax.ShapeDtypeStruct((rows, 2 * LANES), BF16),
        compiler_params=_cparams("arbitrary"), name="gate_cols",
    )(wt)


def _pair_swap(x):
    lane = lax.broadcasted_iota(jnp.int32, x.shape, 1)
    return jnp.where((lane & 1) == 0, pltpu.roll(x, LANES - 1, 1), pltpu.roll(x, 1, 1))


def _proj_kernel(x_ref, mod_ref, n1_ref, w_ref, wg_ref, bg_ref, qg_ref, kg_ref, cos_ref, sin_ref,
                 q_ref, k_ref, v_ref, qm_ref, kmt_ref, vm_ref, om_ref, row_ref, col_ref, *kv_refs, rope, tm):
    mod = mod_ref[0]
    shift, scale = mod[:, :D_MODEL], mod[:, D_MODEL:2 * D_MODEL]
    h = _rms(x_ref[...], n1_ref[...]) * (1.0 + scale) + shift
    h_hi, h_lo = _split_hi_lo(h)

    g = _dot_hi_lo(h_hi, h_lo, wg_ref) + bg_ref[...]
    for cc in range(tm // CHUNK):
        rows, cols = _gate_scan_chunk(g[cc * CHUNK:(cc + 1) * CHUNK, :])
        row_ref[cc] = rows
        col_ref[cc * CHUNK:(cc + 1) * CHUNK, :] = cols

    def rot(seg):
        return seg * cos_ref[...] + _pair_swap(seg) * sin_ref[...] if rope else seg

    qa = _dot(h_hi, w_ref[:, :ATT_WIDTH])
    for hh in range(ATT_HEADS):
        sl = slice(hh * ATT_HEAD_DIM, (hh + 1) * ATT_HEAD_DIM)
        seg = rot(_rms(qa[:, sl], qg_ref[...]))
        q_ref[:, sl] = (seg * Q_SCALE).astype(BF16)

    kv = _dot(h_hi, w_ref[:, ATT_WIDTH:ATT_WIDTH + 2 * KV_WIDTH])
    for hh in range(ATT_KV_HEADS):
        sl = slice(hh * ATT_HEAD_DIM, (hh + 1) * ATT_HEAD_DIM)
        seg = _rms(kv[:, sl], kg_ref[...])
        if kv_refs:
            kv_refs[0][:, sl] = seg
        k_ref[:, sl] = rot(seg).astype(BF16)
    va = kv[:, KV_WIDTH:]
    if kv_refs:
        kv_refs[1][...] = va
    v_ref[...] = va.astype(BF16)

    c0 = ATT_WIDTH + 2 * KV_WIDTH
    qm_ref[...] = (_dot(h_hi, w_ref[:, c0:c0 + MLSTM_WIDTH]) * MLSTM_HEAD_DIM ** -0.5).astype(BF16)
    km = _dot(h_hi, w_ref[:, c0 + MLSTM_WIDTH:c0 + 2 * MLSTM_WIDTH])
    for cc in range(tm // CHUNK):
        kmt_ref[cc] = km[cc * CHUNK:(cc + 1) * CHUNK, :].T.astype(BF16)
    vm_ref[...] = _dot(h_hi, w_ref[:, c0 + 2 * MLSTM_WIDTH:c0 + 3 * MLSTM_WIDTH]).astype(BF16)
    om_ref[...] = _dot(h_hi, w_ref[:, c0 + 3 * MLSTM_WIDTH:c0 + 4 * MLSTM_WIDTH]).astype(BF16)


def _project(x2d, mod3, row_of_tile, n1, w_main, wg_cat, bg, qg, kg, cos_t, sin_t, *, rope, emit_kv, tm=256):
    t = x2d.shape[0]
    n_pos = cos_t.shape[0] // tm
    row = lambda i: (i, 0)
    in_specs = [pl.BlockSpec((tm, D_MODEL), row),
                pl.BlockSpec((1, 1, N_MOD * D_MODEL), lambda i: (row_of_tile(i), 0, 0)),
                _const_spec((1, D_MODEL)),
                _const_spec((D_MODEL, MAIN_WIDTH)),
                _const_spec((D_MODEL, 2 * LANES)), _const_spec((1, LANES)),
                _const_spec((1, ATT_HEAD_DIM)), _const_spec((1, ATT_HEAD_DIM)),
                pl.BlockSpec((tm, ATT_HEAD_DIM), lambda i: (i % n_pos, 0)),
                pl.BlockSpec((tm, ATT_HEAD_DIM), lambda i: (i % n_pos, 0))]
    out_shape = [jax.ShapeDtypeStruct((t, ATT_WIDTH), BF16), jax.ShapeDtypeStruct((t, KV_WIDTH), BF16),
                 jax.ShapeDtypeStruct((t, KV_WIDTH), BF16), jax.ShapeDtypeStruct((t, MLSTM_WIDTH), BF16),
                 jax.ShapeDtypeStruct((t // CHUNK, MLSTM_WIDTH, CHUNK), BF16),
                 jax.ShapeDtypeStruct((t, MLSTM_WIDTH), BF16), jax.ShapeDtypeStruct((t, MLSTM_WIDTH), BF16),
                 jax.ShapeDtypeStruct((t // CHUNK, N_SCAN_ROWS, CHUNK), F32), jax.ShapeDtypeStruct((t, LANES), F32)]
    out_specs = [pl.BlockSpec((tm, ATT_WIDTH), row), pl.BlockSpec((tm, KV_WIDTH), row),
                 pl.BlockSpec((tm, KV_WIDTH), row), pl.BlockSpec((tm, MLSTM_WIDTH), row),
                 pl.BlockSpec((tm // CHUNK, MLSTM_WIDTH, CHUNK), lambda i: (i, 0, 0)),
                 pl.BlockSpec((tm, MLSTM_WIDTH), row), pl.BlockSpec((tm, MLSTM_WIDTH), row),
                 pl.BlockSpec((tm // CHUNK, N_SCAN_ROWS, CHUNK), lambda i: (i, 0, 0)), pl.BlockSpec((tm, LANES), row)]
    if emit_kv:
        out_shape += [jax.ShapeDtypeStruct((t, KV_WIDTH), F32)] * 2
        out_specs += [pl.BlockSpec((tm, KV_WIDTH), row)] * 2
    return pl.pallas_call(
        functools.partial(_proj_kernel, rope=rope, tm=tm),
        grid=(t // tm,), in_specs=in_specs, out_specs=out_specs, out_shape=out_shape,
        compiler_params=_cparams("parallel"), name="in_proj",
    )(x2d, mod3, n1, w_main, wg_cat, bg, qg, kg, cos_t, sin_t)


def _lane_scan(x, op, fill, is_fwd, lane):
    s = 1
    while s < CHUNK:
        from_left = jnp.where(lane >= s, pltpu.roll(x, s, 1), fill)
        from_right = jnp.where(lane < CHUNK - s, pltpu.roll(x, CHUNK - s, 1), fill)
        x = op(x, jnp.where(is_fwd, from_left, from_right))
        s *= 2
    return x


def _gate_scan_chunk(g):
    lane = lax.broadcasted_iota(jnp.int32, (N_CHAINS, CHUNK), 1)
    is_fwd = lax.broadcasted_iota(jnp.int32, (N_CHAINS, CHUNK), 0) < MLSTM_HEADS
    gt = g.T
    fwd, bwd = gt[:N_CHAINS], gt[N_CHAINS:2 * N_CHAINS]
    li = jnp.where(is_fwd, fwd, pltpu.roll(bwd, MLSTM_HEADS, 0))
    f = jnp.where(is_fwd, pltpu.roll(fwd, MLSTM_HEADS, 0), bwd)
    lf = jnp.minimum(f, 0.0) - jnp.log(1.0 + jnp.exp(-jnp.abs(f)))
    b = _lane_scan(lf, jnp.add, 0.0, is_fwd, lane)
    a = li - b
    run_max = _lane_scan(a, jnp.maximum, -jnp.inf, is_fwd, lane)
    all_max = jnp.broadcast_to(jnp.max(a, axis=1, keepdims=True), a.shape)
    last = jnp.where(is_fwd, CHUNK - 1, 0)
    total = jnp.broadcast_to(jnp.sum(jnp.where(lane == last, b, 0.0), axis=1, keepdims=True), a.shape)
    rows = jnp.concatenate([a, b, run_max, all_max, total], axis=0)
    padded = jnp.concatenate([rows, jnp.zeros((LANES - N_SCAN_ROWS, CHUNK), F32)], axis=0)
    return rows, padded.T


def _mlstm_kernel(*refs, n_chunks, has_state, emit_state):
    it = iter(refs)
    q_ref, kt_ref, v_ref, row_ref, col_ref = [next(it) for _ in range(5)]
    c0_ref, m0_ref = (next(it), next(it)) if has_state else (None, None)
    o_ref = next(it)
    c_out, n_out, m_out = (next(it), next(it), next(it)) if emit_state else (None, None, None)
    caug = next(it)

    fresh = not has_state and n_chunks == 1
    start = lambda cc: cc * CHUNK if isinstance(cc, int) else pl.multiple_of(cc * CHUNK, CHUNK)
    b = pl.program_id(0)
    if has_state:
        caug[...] = c0_ref[0]
        m_init = tuple(jnp.full((1, 1), m0_ref[b * N_CHAINS + r], F32) for r in range(N_CHAINS))
    else:
        if not fresh:
            caug[...] = jnp.zeros(caug.shape, F32)
        m_init = tuple(jnp.zeros((1, 1), F32) for _ in range(N_CHAINS))

    sub = lax.broadcasted_iota(jnp.int32, (CHUNK, CHUNK), 0)
    lane = lax.broadcasted_iota(jnp.int32, (CHUNK, CHUNK), 1)
    ones_col = (lax.broadcasted_iota(jnp.int32, (CHUNK, LANES), 1) == 0).astype(BF16)

    def chain(r, cc, m_prev):
        d, hd = divmod(r, MLSTM_HEADS)
        t0 = start(cc)
        hs = slice(hd * MLSTM_HEAD_DIM, (hd + 1) * MLSTM_HEAD_DIM)
        rows = row_ref[cc]
        cols = col_ref[pl.ds(t0, CHUNK), :]
        row = lambda k: rows[k * N_CHAINS + r:k * N_CHAINS + r + 1, :]
        col = lambda k: cols[:, k * N_CHAINS + r:k * N_CHAINS + r + 1]
        q = q_ref[pl.ds(t0, CHUNK), hs]
        kt = kt_ref[cc, hs, :]
        vaug = jnp.concatenate([v_ref[pl.ds(t0, CHUNK), hs], ones_col], axis=1)

        m_col = jnp.maximum(m_prev, col(2))
        keep = (lane <= sub) if d == 0 else (lane >= sub)
        w = jnp.where(keep, jnp.exp(row(0) - m_col), 0.0)
        w_inter = jnp.exp(m_prev - m_col)
        p = (_dot(q, kt) * w).astype(BF16)
        intra = _dot(p, vaug)
        num, den = intra[:, :MLSTM_HEAD_DIM], intra[:, MLSTM_HEAD_DIM:MLSTM_HEAD_DIM + 1]
        if not fresh:
            inter = _dot(q, caug[r].astype(BF16))
            num = num + w_inter * inter[:, :MLSTM_HEAD_DIM]
            den = den + w_inter * inter[:, MLSTM_HEAD_DIM:MLSTM_HEAD_DIM + 1]
        h_out = num / jnp.maximum(jnp.abs(den), jnp.exp(-(col(1) + m_col)))

        m_last = jnp.maximum(m_prev, row(3))
        kw = (kt.astype(F32) * jnp.exp(row(0) - m_last)).astype(BF16)
        update = _dot(kw, vaug)
        caug[r] = update if fresh else jnp.exp(m_prev - m_last[:, :1]) * caug[r] + update
        return h_out, (row(4) + m_last)[:, :1]

    def make_body(add_fwd, add_bwd):
        def body(c, ms):
            new = []
            for r in range(N_CHAINS):
                is_fwd = r < MLSTM_HEADS
                cc = c if is_fwd else n_chunks - 1 - c
                h_out, m_new = chain(r, cc, ms[r])
                hd = r % MLSTM_HEADS
                where = (pl.ds(start(cc), CHUNK), slice(hd * MLSTM_HEAD_DIM, (hd + 1) * MLSTM_HEAD_DIM))
                if add_fwd if is_fwd else add_bwd:
                    o_ref[where] += h_out
                else:
                    o_ref[where] = h_out
                new.append(m_new)
            return tuple(new)
        return body

    half = n_chunks // 2
    ms = m_init
    if half:
        ms = lax.fori_loop(0, half, make_body(False, False), ms)
    if n_chunks % 2:
        ms = make_body(False, True)(half, ms)
    if half:
        ms = lax.fori_loop(n_chunks - half, n_chunks, make_body(True, True), ms)
    if emit_state:
        for r in range(N_CHAINS):
            c_out[0, r] = caug[r, :, :MLSTM_HEAD_DIM]
            n_out[0, r] = caug[r, :, MLSTM_HEAD_DIM:].T[:1, :]
            m_out[0, r] = jnp.broadcast_to(ms[r], (1, LANES))


def _mlstm(qm, kmt, vm, rowq, colq, n_batch, seq, state=None, emit_state=False):
    nc = seq // CHUNK
    mode = {}
    in_specs = [pl.BlockSpec((seq, MLSTM_WIDTH), lambda b: (b, 0), **mode),
                pl.BlockSpec((nc, MLSTM_WIDTH, CHUNK), lambda b: (b, 0, 0), **mode),
                pl.BlockSpec((seq, MLSTM_WIDTH), lambda b: (b, 0), **mode),
                pl.BlockSpec((nc, N_SCAN_ROWS, CHUNK), lambda b: (b, 0, 0), **mode),
                pl.BlockSpec((seq, LANES), lambda b: (b, 0), **mode)]
    args = [qm, kmt, vm, rowq, colq]
    if state is not None:
        in_specs += [pl.BlockSpec((1, N_CHAINS, MLSTM_HEAD_DIM, AUG), lambda b: (b, 0, 0, 0), **mode),
                     pl.BlockSpec(memory_space=pltpu.SMEM)]
        args += list(state)
    out_shape = [jax.ShapeDtypeStruct((n_batch * seq, MLSTM_WIDTH), F32)]
    out_specs = [pl.BlockSpec((seq, MLSTM_WIDTH), lambda b: (b, 0))]
    if emit_state:
        out_shape += [jax.ShapeDtypeStruct((n_batch, N_CHAINS, MLSTM_HEAD_DIM, MLSTM_HEAD_DIM), F32),
                      jax.ShapeDtypeStruct((n_batch, N_CHAINS, 1, MLSTM_HEAD_DIM), F32),
                      jax.ShapeDtypeStruct((n_batch, N_CHAINS, 1, LANES), F32)]
        out_specs += [pl.BlockSpec((1, N_CHAINS, MLSTM_HEAD_DIM, MLSTM_HEAD_DIM), lambda b: (b, 0, 0, 0)),
                      pl.BlockSpec((1, N_CHAINS, 1, MLSTM_HEAD_DIM), lambda b: (b, 0, 0, 0)),
                      pl.BlockSpec((1, N_CHAINS, 1, LANES), lambda b: (b, 0, 0, 0))]
    return pl.pallas_call(
        functools.partial(_mlstm_kernel, n_chunks=nc, has_state=state is not None, emit_state=emit_state),
        grid=(n_batch,), in_specs=in_specs, out_specs=out_specs, out_shape=out_shape,
        scratch_shapes=[pltpu.VMEM((N_CHAINS, MLSTM_HEAD_DIM, AUG), F32)],
        compiler_params=_cparams("parallel"), name="mlstm",
    )(*args)


def _attn_kernel(*refs, tq, has_cache):
    if has_cache:
        q_ref, k_ref, v_ref, ck_ref, cv_ref, o_ref = refs
    else:
        q_ref, k_ref, v_ref, o_ref = refs
    q = q_ref[...]
    qs = jnp.concatenate([q[:, g * ATT_HEAD_DIM:(g + 1) * ATT_HEAD_DIM] for g in range(ATT_GROUP)], axis=0)
    nt = (((1,), (1,)), ((), ()))
    seq = k_ref.shape[0]
    kc = min(seq, ATT_KEY_CHUNK)
    chunks = [(k_ref, v_ref, c * kc) for c in range(seq // kc)]
    if has_cache:
        chunks.insert(0, (ck_ref, cv_ref, None))
    m = l = o = None
    for kr, vr, start in chunks:
        kk, vv = (kr[...], vr[...]) if start is None else (kr[start:start + kc, :], vr[start:start + kc, :])
        s = lax.dot_general(qs, kk.astype(BF16), nt, preferred_element_type=F32)
        mc = jnp.max(s, axis=-1, keepdims=True)
        if m is None:
            m = mc
            p = jnp.exp2(s - m)
            l = jnp.sum(p, axis=-1, keepdims=True)
            o = _dot(p.astype(BF16), vv.astype(BF16))
        else:
            m_new = jnp.maximum(m, mc)
            alpha = jnp.exp2(m - m_new)
            p = jnp.exp2(s - m_new)
            l = alpha * l + jnp.sum(p, axis=-1, keepdims=True)
            o = alpha * o + _dot(p.astype(BF16), vv.astype(BF16))
            m = m_new
    o = o / l
    for g in range(ATT_GROUP):
        o_ref[:, g * ATT_HEAD_DIM:(g + 1) * ATT_HEAD_DIM] = o[g * tq:(g + 1) * tq].astype(BF16)


def _attention(q, k, v, n_batch, seq, cache=None, tq=256):
    nqb = seq // tq
    gw = ATT_GROUP * ATT_HEAD_DIM
    in_specs = [pl.BlockSpec((tq, gw), lambda b, h, i: (b * nqb + i, h)),
                pl.BlockSpec((seq, ATT_HEAD_DIM), lambda b, h, i: (b, h)),
                pl.BlockSpec((seq, ATT_HEAD_DIM), lambda b, h, i: (b, h))]
    args = [q, k, v]
    if cache is not None:
        past = cache[0].shape[0] // n_batch
        in_specs += [pl.BlockSpec((past, ATT_HEAD_DIM), lambda b, h, i: (b, h))] * 2
        args += list(cache)
    return pl.pallas_call(
        functools.partial(_attn_kernel, tq=tq, has_cache=cache is not None),
        grid=(n_batch, ATT_KV_HEADS, nqb), in_specs=in_specs,
        out_specs=pl.BlockSpec((tq, gw), lambda b, h, i: (b * nqb + i, h)),
        out_shape=jax.ShapeDtypeStruct((n_batch * seq, ATT_WIDTH), BF16),
        compiler_params=_cparams("parallel", "parallel", "parallel"), name="attention",
    )(*args)


def _mix_kernel(attn_ref, hm_ref, om_ref, x_ref, mod_ref, mg_ref, wo_ref, n2_ref, rw_ref, rb_ref, *rest, sub):
    for r0 in range(0, x_ref.shape[0], sub):
        _mix_rows(slice(r0, r0 + sub), attn_ref, hm_ref, om_ref, x_ref, mod_ref, mg_ref, wo_ref, n2_ref, rw_ref,
                  rb_ref, *rest[-3:])


def _mix_rows(rows, attn_ref, hm_ref, om_ref, x_ref, mod_ref, mg_ref, wo_ref, n2_ref, rw_ref, rb_ref,
              x1_ref, h2_ref, route_ref):
    mod = mod_ref[0]
    gate1 = mod[:, 2 * D_MODEL:3 * D_MODEL]
    shift2, scale2 = mod[:, 3 * D_MODEL:4 * D_MODEL], mod[:, 4 * D_MODEL:5 * D_MODEL]
    hm = hm_ref[rows, :]
    mg = mg_ref[...]
    parts = []
    for hd in range(MLSTM_HEADS):
        sl = slice(hd * MLSTM_HEAD_DIM, (hd + 1) * MLSTM_HEAD_DIM)
        parts.append(_rms(hm[:, sl], mg[:, sl]))
    om = om_ref[rows, :].astype(F32)
    hmg = jnp.concatenate(parts, axis=1) * (1.0 / (1.0 + jnp.exp(-om)))
    y = _dot(attn_ref[rows, :], wo_ref[:ATT_WIDTH, :]) + _dot(hmg.astype(BF16), wo_ref[ATT_WIDTH:, :])
    x1 = x_ref[rows, :] + gate1 * y
    x1_ref[rows, :] = x1
    h2 = _rms(x1, n2_ref[...]) * (1.0 + scale2) + shift2
    h2_hi, h2_lo = _split_hi_lo(h2)
    h2_ref[rows, :] = _pack_halves(h2)

    lg = _dot_hi_lo(h2_hi, h2_lo, rw_ref) + rb_ref[...]
    lane = lax.broadcasted_iota(jnp.int32, lg.shape, 1).astype(F32)
    neg = -jnp.inf
    first = lambda hit: jnp.min(jnp.where(hit, lane, float(LANES)), axis=-1, keepdims=True)
    gl = jnp.where(lane < N_GROUPS, lg, neg)
    gmax = jnp.max(gl, axis=-1, keepdims=True)
    grp = first(gl == gmax)
    p_grp = 1.0 / jnp.sum(jnp.exp(gl - gmax), axis=-1, keepdims=True)
    lo = ROUTER_LANE0 + grp * EXPERTS_PER_GROUP
    el = jnp.where((lane >= lo) & (lane < lo + EXPERTS_PER_GROUP), lg, neg)
    m1 = jnp.max(el, axis=-1, keepdims=True)
    i1 = first(el == m1)
    el2 = jnp.where(lane == i1, neg, el)
    m2 = jnp.max(el2, axis=-1, keepdims=True)
    i2 = first(el2 == m2)
    r = jnp.exp(m2 - m1)
    w1 = p_grp / (1.0 + r)
    w2 = w1 * r
    route_ref[rows, :] = jnp.where(lane == 0.0, i1 - ROUTER_LANE0, jnp.where(lane == 1.0, i2 - ROUTER_LANE0,
                                   jnp.where(lane == 2.0, w1, jnp.where(lane == 3.0, w2, 0.0))))


def _mix_out(attn, hm, om, x2d, mod3, row_of_tile, mg, w_out, n2, rw_cat, rb, h2_all, tile0, t_total, tm=512):
    t = x2d.shape[0]
    row = lambda i: (i, 0)
    in_specs = [pl.BlockSpec((tm, ATT_WIDTH), row), pl.BlockSpec((tm, MLSTM_WIDTH), row),
                pl.BlockSpec((tm, MLSTM_WIDTH), row), pl.BlockSpec((tm, D_MODEL), row),
                pl.BlockSpec((1, 1, N_MOD * D_MODEL), lambda i: (row_of_tile(i), 0, 0)),
                _const_spec((1, MLSTM_WIDTH)), _const_spec((D_MODEL, D_MODEL)), _const_spec((1, D_MODEL)),
                _const_spec((D_MODEL, 2 * LANES)), _const_spec((1, LANES))]
    args = [attn, hm, om, x2d, mod3, mg, w_out, n2, rw_cat, rb]
    aliases = {}
    if h2_all is not None:
        aliases = {len(args): 1}
        in_specs.append(pl.BlockSpec(memory_space=pl.ANY))
        args.append(h2_all)
    return pl.pallas_call(
        functools.partial(_mix_kernel, sub=512), grid=(t // tm,), in_specs=in_specs,
        out_specs=[pl.BlockSpec((tm, D_MODEL), row), pl.BlockSpec((tm, D_MODEL // 2), lambda i: (tile0 + i, 0)),
                   pl.BlockSpec((tm, LANES), row)],
        out_shape=[jax.ShapeDtypeStruct((t, D_MODEL), F32), jax.ShapeDtypeStruct((t_total, D_MODEL // 2), jnp.uint32),
                   jax.ShapeDtypeStruct((t, LANES), F32)],
        input_output_aliases=aliases,
        compiler_params=_cparams("parallel"), name="mix_out",
    )(*args)


def _rank_kernel(route_ref, rank_ref, cnt_ref, run_ref, tri_ref):
    tr = route_ref.shape[0]

    @pl.when(pl.program_id(0) == 0)
    def _():
        run_ref[...] = jnp.zeros(run_ref.shape, F32)
        tri_ref[...] = (lax.broadcasted_iota(jnp.int32, (tr, tr), 1)
                        < lax.broadcasted_iota(jnp.int32, (tr, tr), 0)).astype(BF16)

    route = route_ref[...]
    lane = lax.broadcasted_iota(jnp.int32, route.shape, 1).astype(F32)
    hit1, hit2 = lane == route[:, 0:1], lane == route[:, 1:2]
    onehot = jnp.where(hit1, 1.0, jnp.where(hit2, 1.0, 0.0))
    before = _dot(tri_ref[...], onehot.astype(BF16)) + run_ref[0:1, :]
    r1 = jnp.sum(jnp.where(hit1, before, 0.0), axis=-1, keepdims=True)
    r2 = jnp.sum(jnp.where(hit2, before, 0.0), axis=-1, keepdims=True)
    rank_ref[...] = jnp.where(lane == 0.0, r1, jnp.where(lane == 1.0, r2, 0.0))
    run_ref[...] = run_ref[...] + jnp.sum(onehot, axis=0, keepdims=True)
    cnt_ref[...] = run_ref[...]


def _ranks(route, tr=512):
    t = route.shape[0]
    return pl.pallas_call(
        _rank_kernel, grid=(t // tr,),
        in_specs=[pl.BlockSpec((tr, LANES), lambda i: (i, 0))],
        out_specs=[pl.BlockSpec((tr, LANES), lambda i: (i, 0)), pl.BlockSpec((8, LANES), lambda i: (0, 0))],
        out_shape=[jax.ShapeDtypeStruct((t, LANES), F32), jax.ShapeDtypeStruct((8, LANES), F32)],
        scratch_shapes=[pltpu.VMEM((8, LANES), F32), pltpu.VMEM((tr, tr), BF16)],
        compiler_params=_cparams("arbitrary"), name="expert_ranks",
    )(route)


def _pos_kernel(route_ref, rank_ref, start_ref, o_ref):
    route, rank = route_ref[...], rank_ref[...]
    lane = lax.broadcasted_iota(jnp.int32, route.shape, 1).astype(F32)
    start = start_ref[...]
    first = lambda col: jnp.sum(jnp.where(lane == route[:, col:col + 1], start, 0.0), axis=-1, keepdims=True)
    p1 = first(0) + rank[:, 0:1]
    p2 = first(1) + rank[:, 1:2]
    tile = jnp.where(lane == 0.0, p1, jnp.where(lane == 1.0, p2, 0.0))
    o_ref[...] = tile.T[:8, :].astype(jnp.int32)


def _positions(route, ranks, starts, tr=512):
    t = route.shape[0]
    pos = pl.pallas_call(
        _pos_kernel, grid=(t // tr,),
        in_specs=[pl.BlockSpec((tr, LANES), lambda i: (i, 0)), pl.BlockSpec((tr, LANES), lambda i: (i, 0)),
                  pl.BlockSpec((1, LANES), lambda i: (0, 0))],
        out_specs=pl.BlockSpec((8, tr), lambda i: (0, i)),
        out_shape=jax.ShapeDtypeStruct((8, t), jnp.int32),
        compiler_params=_cparams("parallel"), name="pair_rows",
    )(route, ranks, starts)
    return pos[:2].reshape(-1)


def _routing_plan(route, ranks, counts):
    t = route.shape[0]
    n_tiles = 2 * t // TM_PAIR + N_EXPERTS
    cnt = counts[0, :N_EXPERTS].astype(jnp.int32)
    padded = (cnt + TM_PAIR - 1) // TM_PAIR * TM_PAIR
    ends = jnp.cumsum(padded)
    starts = ends - padded
    pos = _positions(route, ranks, _pad_lanes(starts.astype(F32)[None]))
    n_used = (ends[-1:] // TM_PAIR).astype(jnp.int32)
    tile_start = jnp.arange(n_tiles, dtype=jnp.int32) * TM_PAIR
    tile_expert = jnp.sum((ends[None, :] <= tile_start[:, None]).astype(jnp.int32), axis=1)
    tile_expert = jnp.minimum(tile_expert, tile_expert[n_used[0] - 1]).astype(jnp.int32)
    mine = tile_expert[:, None] == jnp.arange(N_EXPERTS, dtype=jnp.int32)[None, :]
    rows_left = jnp.sum(jnp.where(mine, (starts + cnt)[None, :], 0), axis=1) - tile_start
    second_half = (rows_left > TM_MOE).astype(jnp.int32)
    tails = jnp.where(cnt > 0, starts + (cnt - 1) // TM_MOE * TM_MOE, -1).astype(jnp.int32)
    change = jnp.concatenate([jnp.ones((1,), jnp.int32), (tile_expert[1:] != tile_expert[:-1]).astype(jnp.int32)])
    weight_slot = (jnp.cumsum(change) - 1) % 2
    ids = jnp.arange(N_EXPERTS, dtype=jnp.int32)
    later_used = (padded > 0)[None, :] & (ids[None, :] > ids[:, None])
    next_used = jnp.min(jnp.where(later_used, ids[None, :], N_EXPERTS), axis=1)
    next_used = jnp.where(next_used < N_EXPERTS, next_used, -1)
    next_expert = jnp.sum(jnp.where(tile_expert[:, None] == ids[None, :], next_used[None, :], 0), axis=1)
    return pos, (tile_expert, n_used, weight_slot.astype(jnp.int32), next_expert.astype(jnp.int32), second_half), \
        tails, n_tiles * TM_PAIR


def _row_copies(pos_ref, n_pairs, tok0, n_rows, make_copy):
    def body(r, carry):
        for k in range(2):
            make_copy(k, r, pos_ref[k * n_pairs + tok0 + r]).start()
        return carry
    lax.fori_loop(0, n_rows, body, 0, unroll=8)


def _dispatch_kernel(pos_ref, tail_ref, h_ref, xs_ref, hbuf, zero_ref, tsem, rsem, *, td, n_tokens):
    i = pl.program_id(0)
    last = pl.num_programs(0) - 1
    tile_copy = lambda t: pltpu.make_async_copy(
        h_ref.at[pl.ds(pl.multiple_of(t * td, td), td)], hbuf.at[t % 3], tsem.at[t % 3])

    def wait_rows(t):
        for _ in range(2):
            pltpu.make_async_copy(hbuf.at[0], xs_ref.at[pl.ds(0, td)], rsem.at[t % 2]).wait()

    @pl.when(i == 0)
    def _():
        tile_copy(0).start()
        zero_ref[...] = jnp.zeros(zero_ref.shape, zero_ref.dtype)
        tail_copy = lambda e: pltpu.make_async_copy(
            zero_ref, xs_ref.at[pl.ds(pl.multiple_of(tail_ref[e], TM_MOE), TM_MOE)], rsem.at[1])
        for e in range(N_EXPERTS):
            pl.when(tail_ref[e] >= 0)(lambda e=e: tail_copy(e).start())
        for e in range(N_EXPERTS):
            pl.when(tail_ref[e] >= 0)(lambda e=e: tail_copy(e).wait())

    @pl.when(i < last)
    def _():
        tile_copy(i + 1).start()

    tile_copy(i).wait()
    src = hbuf.at[i % 3]
    _row_copies(pos_ref, n_tokens, i * td, td, lambda k, r, p: pltpu.make_async_copy(
        src.at[pl.ds(r, 1)], xs_ref.at[pl.ds(p, 1)], rsem.at[i % 2]))

    @pl.when(i >= 1)
    def _():
        wait_rows(i - 1)

    @pl.when(i == last)
    def _():
        wait_rows(i)


def _dispatch(pos, tails, h2, n_rows, td=256):
    t, width = h2.shape
    return pl.pallas_call(
        functools.partial(_dispatch_kernel, td=td, n_tokens=t),
        grid_spec=pltpu.PrefetchScalarGridSpec(
            num_scalar_prefetch=2, grid=(t // td,),
            in_specs=[pl.BlockSpec(memory_space=pl.ANY)],
            out_specs=pl.BlockSpec(memory_space=pl.ANY),
            scratch_shapes=[pltpu.VMEM((3, td, width), h2.dtype), pltpu.VMEM((TM_MOE, width), h2.dtype),
                            pltpu.SemaphoreType.DMA((3,)), pltpu.SemaphoreType.DMA((2,))]),
        out_shape=jax.ShapeDtypeStruct((n_rows, width), h2.dtype),
        compiler_params=_cparams("arbitrary"), name="dispatch",
    )(pos, tails, h2)


def _pack_halves(x):
    half = x.shape[1] // 2
    return pltpu.pack_elementwise([x[:, :half], x[:, half:]], packed_dtype=BF16)


def _unpack_halves(p, dtype):
    return tuple(pltpu.unpack_elementwise(p, index=i, packed_dtype=BF16, unpacked_dtype=F32).astype(dtype)
                 for i in range(2))


def _expert_kernel(te_ref, nu_ref, slot_ref, next_ref, full_ref, xs_ref, wg_ref, wu_ref, wd_ref, ys_ref,
                   wgf, wuf, wdf, wgb, wub, wdb, wsem):
    j = pl.program_id(0)

    def mlp(rows):
        xa, xb = _unpack_halves(xs_ref[rows, :], BF16)
        half = D_MODEL // 2
        g = _dot(xa, wgb[:half, :]) + _dot(xb, wgb[half:, :])
        u = _dot(xa, wub[:half, :]) + _dot(xb, wub[half:, :])
        a = (g / (1.0 + jnp.exp(-g))) * u
        ys_ref[rows, :] = _pack_halves(_dot(a.astype(BF16), wdb[...]))

    def weight_copies(e, s):
        return [pltpu.make_async_copy(w_ref.at[e], buf.at[s], wsem.at[s])
                for w_ref, buf in ((wg_ref, wgf), (wu_ref, wuf), (wd_ref, wdf))]

    @pl.when(j < nu_ref[0])
    def _():
        e, s = te_ref[j], slot_ref[j]

        @pl.when(j == 0)
        def _():
            for cp in weight_copies(e, s):
                cp.start()

        @pl.when((j == 0) | (e != te_ref[jnp.maximum(j - 1, 0)]))
        def _():
            for cp in weight_copies(e, s):
                cp.wait()
            nxt = next_ref[j]

            @pl.when(nxt >= 0)
            def _():
                for cp in weight_copies(nxt, 1 - s):
                    cp.start()

            wgb[...] = wgf[s].astype(BF16)
            wub[...] = wuf[s].astype(BF16)
            wdb[...] = wdf[s].astype(BF16)

        @pl.when(full_ref[j] == 1)
        def _():
            mlp(slice(None))

        @pl.when(full_ref[j] == 0)
        def _():
            mlp(slice(0, TM_MOE))
            ys_ref[TM_MOE:, :] = _pack_halves(jnp.zeros((TM_MOE, D_MODEL), F32))


def _experts(tile_expert, n_used, weight_slot, next_expert, second_half, xs, wg, wu, wd):
    n_tiles = xs.shape[0] // TM_PAIR
    tile = lambda j, te, nu, *_: (jnp.minimum(j, nu[0] - 1), 0)
    hbm = pl.BlockSpec(memory_space=pl.ANY)
    return pl.pallas_call(
        _expert_kernel,
        grid_spec=pltpu.PrefetchScalarGridSpec(
            num_scalar_prefetch=5, grid=(n_tiles,),
            in_specs=[pl.BlockSpec((TM_PAIR, D_MODEL // 2), tile), hbm, hbm, hbm],
            out_specs=pl.BlockSpec((TM_PAIR, D_MODEL // 2), tile),
            scratch_shapes=[pltpu.VMEM((2, D_MODEL, D_EXPERT), F32), pltpu.VMEM((2, D_MODEL, D_EXPERT), F32),
                            pltpu.VMEM((2, D_EXPERT, D_MODEL), F32),
                            pltpu.VMEM((D_MODEL, D_EXPERT), BF16), pltpu.VMEM((D_MODEL, D_EXPERT), BF16),
                            pltpu.VMEM((D_EXPERT, D_MODEL), BF16), pltpu.SemaphoreType.DMA((2,))]),
        out_shape=jax.ShapeDtypeStruct(xs.shape, jnp.uint32),
        compiler_params=_cparams("arbitrary"), name="experts",
    )(tile_expert, n_used, weight_slot, next_expert, second_half, xs, wg, wu, wd)


def _combine_kernel(pos_ref, x1_ref, route_ref, mod_ref, ys_ref, o_ref, ybuf, sem, *, tc, n_tokens, tok0):
    i = pl.program_id(0)
    slot = i % 2

    def gather(tile, s):
        _row_copies(pos_ref, n_tokens, tok0 + tile * tc, tc, lambda k, r, p: pltpu.make_async_copy(
            ys_ref.at[pl.ds(p, 1)], ybuf.at[s, k, pl.ds(r, 1)], sem.at[s]))

    @pl.when(i == 0)
    def _():
        gather(0, 0)

    @pl.when(i + 1 < pl.num_programs(0))
    def _():
        gather(i + 1, 1 - slot)

    for k in range(2):
        pltpu.make_async_copy(ys_ref.at[pl.ds(0, tc)], ybuf.at[slot, k], sem.at[slot]).wait()
    route = route_ref[...]
    w1, w2 = route[:, 2:3], route[:, 3:4]
    half = D_MODEL // 2
    for h, (y1, y2) in enumerate(zip(_unpack_halves(ybuf[slot, 0], F32), _unpack_halves(ybuf[slot, 1], F32))):
        cols = slice(h * half, (h + 1) * half)
        gate2 = mod_ref[0, :, 5 * D_MODEL + h * half:5 * D_MODEL + (h + 1) * half]
        o_ref[:, cols] = x1_ref[:, cols] + gate2 * (w1 * y1 + w2 * y2)


def _combine(pos, x1, route, mod3, row_of_tile, ys, tok0, n_tokens, tc=256):
    t = x1.shape[0]
    row = lambda i, *_: (i, 0)
    return pl.pallas_call(
        functools.partial(_combine_kernel, tc=tc, n_tokens=n_tokens, tok0=tok0),
        grid_spec=pltpu.PrefetchScalarGridSpec(
            num_scalar_prefetch=1, grid=(t // tc,),
            in_specs=[pl.BlockSpec((tc, D_MODEL), row), pl.BlockSpec((tc, LANES), row),
                      pl.BlockSpec((1, 1, N_MOD * D_MODEL), lambda i, *_: (row_of_tile(i), 0, 0)),
                      pl.BlockSpec(memory_space=pl.ANY)],
            out_specs=pl.BlockSpec((tc, D_MODEL), row),
            scratch_shapes=[pltpu.VMEM((2, 2, tc, ys.shape[1]), ys.dtype), pltpu.SemaphoreType.DMA((2,))]),
        out_shape=jax.ShapeDtypeStruct((t, D_MODEL), F32),
        compiler_params=_cparams("arbitrary"), name="combine",
    )(pos, x1, route, mod3, ys)


def _rope_tables(seq):
    pos = np.arange(seq)
    n_freq = ATT_HEAD_DIM // 4
    inv = ROPE_THETA ** (-np.arange(n_freq, dtype=np.float32) / n_freq)
    ang = np.concatenate([(pos // GRID_W).astype(np.float32)[:, None] * inv,
                          (pos % GRID_W).astype(np.float32)[:, None] * inv], axis=-1).astype(np.float32)
    ang = jnp.asarray(ang)
    cos, sin = jnp.cos(ang), jnp.sin(ang)
    cos_t = jnp.repeat(cos, 2, axis=1)
    sin_t = jnp.stack([-sin, sin], axis=-1).reshape(seq, ATT_HEAD_DIM)
    return cos_t, sin_t


def _pad_lanes(a):
    return jnp.pad(a, ((0, 0), (0, LANES - a.shape[1])))


def _layer(x2d, n_batch, seq, mod3, row_of_tile, lw, *, rope_tabs, cache, state, emit, h2_all, tile0, t_total):
    (n1, n2, w_main, wg_cat, bg, qg, kg, mg, w_out, rw_cat, rb) = lw
    cos_t, sin_t = rope_tabs
    outs = _project(x2d, mod3, row_of_tile(512), n1, w_main, wg_cat, bg, qg, kg, cos_t, sin_t,
                    rope=cache is not None, emit_kv=emit, tm=512)
    q, k, v, qm, kmt, vm, om, rowq, colq = outs[:9]
    attn = _attention(q, k, v, n_batch, seq, cache=cache, tq=min(seq, 512))
    ml = _mlstm(qm, kmt, vm, rowq, colq, n_batch, seq, state=state, emit_state=emit)
    mixed = _mix_out(attn, ml[0], om, x2d, mod3, row_of_tile(MIX_TILE), mg, w_out, n2, rw_cat, rb,
                     h2_all, tile0, t_total, tm=MIX_TILE)
    return mixed, outs[9:], ml[1:]


def kernel(x_prompt, x_sample, cache_k, cache_v, state_C, state_n, state_m, c, c_ctx, mod_w, mod_b, norm1_g, norm2_g,
           w_in, b_gates, q_norm_g, k_norm_g, mlstm_norm_g, w_out, router_group_w, router_group_b, router_expert_w,
           router_expert_b, expert_w_gate, expert_w_up, expert_w_down):
    assert mod_w.shape[0] == 1, "single-layer stack"
    n_ctx, s_ctx, _ = x_prompt.shape
    n_lat, s_lat, _ = x_sample.shape
    t_ctx, t_lat = n_ctx * s_ctx, n_lat * s_lat
    t_all = t_ctx + t_lat
    ctx_row = n_lat

    cond = jnp.concatenate([c, c_ctx[None], jnp.zeros((8 - n_lat - 1, D_MODEL), F32)], axis=0)
    mod3 = _modulation(cond, mod_w[0], mod_b[0][None]).reshape(8, 1, N_MOD * D_MODEL)

    rw =jnp.concatenate([router_group_w[0], jnp.moveaxis(router_expert_w[0], 0, 1).reshape(D_MODEL, N_EXPERTS)], axis=1)
    rb = _pad_lanes(jnp.concatenate([router_group_b[0], router_expert_b[0].reshape(-1)])[None])
    w_in_t = w_in[0].T
    lw = (norm1_g, norm2_g, _cast_bf16_t(w_in_t, MAIN_WIDTH), _gate_cols(w_in_t), _pad_lanes(b_gates),
          q_norm_g, k_norm_g, mlstm_norm_g, _cast_bf16(w_out, D_MODEL), _hi_lo_cat(_pad_lanes(rw)), rb)
    rope_tabs = _rope_tables(s_lat)

    ctx_rows = lambda tm: (lambda i: ctx_row)
    lat_rows = lambda tm: (lambda i: i // (s_lat // tm))
    (x1p, h2_all, routep), (ka, va), (s_c, s_n, s_m) = _layer(
        x_prompt.reshape(t_ctx, D_MODEL), n_ctx, s_ctx, mod3, ctx_rows, lw,
        rope_tabs=rope_tabs, cache=None, state=None, emit=True, h2_all=None, tile0=0, t_total=t_all)

    caug0 = jnp.concatenate([state_C[:, 0], state_n[:, 0][..., None],
                             jnp.zeros(state_n[:, 0].shape + (LANES - 1,), F32)], axis=-1)
    caug0 = caug0.reshape(n_lat, N_CHAINS, MLSTM_HEAD_DIM, AUG)
    past = cache_k.shape[2]
    cache = (cache_k[:, 0].reshape(n_lat * past, KV_WIDTH), cache_v[:, 0].reshape(n_lat * past, KV_WIDTH))
    (x1s, h2_all, routes), _, _ = _layer(
        x_sample.reshape(t_lat, D_MODEL), n_lat, s_lat, mod3, lat_rows, lw,
        rope_tabs=rope_tabs, cache=cache, state=(caug0, state_m[:, 0].reshape(-1)), emit=False,
        h2_all=h2_all, tile0=t_ctx // MIX_TILE, t_total=t_all)

    route = jnp.concatenate([routep, routes], axis=0)
    ranks, counts = _ranks(route)
    pos, tile_plan, tails, n_rows = _routing_plan(route, ranks, counts)
    xs = _dispatch(pos, tails, h2_all, n_rows)
    y_sorted = _experts(*tile_plan, xs, expert_w_gate[0], expert_w_up[0], expert_w_down[0])
    yp = _combine(pos, x1p, routep, mod3, ctx_rows(256), y_sorted, 0, t_all)
    ys = _combine(pos, x1s, routes, mod3, lat_rows(256), y_sorted, t_ctx, t_all)

    kv_shape = (n_ctx, 1, s_ctx, ATT_KV_HEADS, ATT_HEAD_DIM)
    return (yp.reshape(x_prompt.shape), ys.reshape(x_sample.shape), ka.reshape(kv_shape), va.reshape(kv_shape),
            s_c.reshape(n_ctx, 1, 2, MLSTM_HEADS, MLSTM_HEAD_DIM, MLSTM_HEAD_DIM),
            s_n.reshape(n_ctx, 1, 2, MLSTM_HEADS, MLSTM_HEAD_DIM), s_m[..., 0, 0].reshape(n_ctx, 1, 2, MLSTM_HEADS))
```

```python
import functools

import numpy as np
import jax
import jax.numpy as jnp
from jax import lax
from jax.experimental import pallas as pl
from jax.experimental.pallas import tpu as pltpu

F32 = jnp.float32
BF16 = jnp.bfloat16

D_MODEL = 2048
GRID_W = 64
ATT_HEADS = 8
ATT_KV_HEADS = 2
ATT_HEAD_DIM = 128
ATT_GROUP = ATT_HEADS // ATT_KV_HEADS
ATT_WIDTH = ATT_HEADS * ATT_HEAD_DIM
KV_WIDTH = ATT_KV_HEADS * ATT_HEAD_DIM
ROPE_THETA = 10000.0
MLSTM_HEADS = 4
MLSTM_HEAD_DIM = 256
MLSTM_WIDTH = MLSTM_HEADS * MLSTM_HEAD_DIM
CHUNK = 256
N_GATES = 4 * MLSTM_HEADS
N_CHAINS = 2 * MLSTM_HEADS
MAIN_WIDTH = ATT_WIDTH + 2 * KV_WIDTH + 4 * MLSTM_WIDTH
N_GROUPS = 4
EXPERTS_PER_GROUP = 8
N_EXPERTS = N_GROUPS * EXPERTS_PER_GROUP
D_EXPERT = 512
N_MOD = 6
TM_MOE = 256
TM_PAIR = 2 * TM_MOE
MIX_TILE = 512
ATT_KEY_CHUNK = 256
Q_SCALE = ATT_HEAD_DIM ** -0.5 * float(np.log2(np.e))
EPS = 1e-6

LANES = 128
AUG = MLSTM_HEAD_DIM + LANES
N_SCAN_ROWS = 5 * N_CHAINS
ROUTER_LANE0 = N_GROUPS
VMEM_LIMIT = 56 * 1024 * 1024


def _cparams(*sem):
    return pltpu.CompilerParams(dimension_semantics=sem, vmem_limit_bytes=VMEM_LIMIT)


def _const_spec(shape):
    nd = len(shape)
    return pl.BlockSpec(shape, lambda *_: (0,) * nd, pipeline_mode=pl.Buffered(1))


def _split_hi_lo(x):
    hi = x.astype(BF16)
    lo = (x - hi.astype(F32)).astype(BF16)
    return hi, lo


def _dot(a, b):
    return jnp.dot(a, b, preferred_element_type=F32)


def _hi_lo_cat(w):
    return jnp.concatenate(_split_hi_lo(w), axis=1)


def _dot_hi_lo(a_hi, a_lo, w_ref):
    r = _dot(a_hi, w_ref[...])
    return r[:, :LANES] + r[:, LANES:] + _dot(a_lo, w_ref[:, :LANES])


def _rms(x, g):
    return x * lax.rsqrt(jnp.mean(x * x, axis=-1, keepdims=True) + EPS) * g


def _mod_kernel(c_ref, w_ref, b_ref, o_ref):
    c = c_ref[...]
    s = c / (1.0 + jnp.exp(-c))
    s_hi = s.astype(BF16).astype(F32)
    lhs = jnp.concatenate([s_hi, s - s_hi], axis=0).astype(BF16)
    w_hi, w_lo = _split_hi_lo(w_ref[...])
    r = _dot(lhs, w_hi)
    r2 = _dot(lhs, w_lo)
    o_ref[...] = r[:8] + r[8:] + r2[:8] + b_ref[...]


def _modulation(cond, mod_w, mod_b):
    n = mod_w.shape[1]
    tn = 1024
    return pl.pallas_call(
        _mod_kernel,
        grid=(n // tn,),
        in_specs=[pl.BlockSpec((8, D_MODEL), lambda j: (0, 0)),
                  pl.BlockSpec((D_MODEL, tn), lambda j: (0, j)),
                  pl.BlockSpec((1, tn), lambda j: (0, j))],
        out_specs=pl.BlockSpec((8, tn), lambda j: (0, j)),
        out_shape=jax.ShapeDtypeStruct((8, n), F32),
        compiler_params=_cparams("parallel"),
        name="modulation",
    )(cond, mod_w, mod_b)


def _cast_kernel(w_ref, o_ref):
    o_ref[...] = w_ref[0].astype(BF16)


def _cast_bf16(w, n_cols, tn=512):
    rows = w.shape[1]
    return pl.pallas_call(
        _cast_kernel, grid=(n_cols // tn,),
        in_specs=[pl.BlockSpec((1, rows, tn), lambda j: (0, 0, j))],
        out_specs=pl.BlockSpec((rows, tn), lambda j: (0, j)),
        out_shape=jax.ShapeDtypeStruct((rows, n_cols), BF16),
        compiler_params=_cparams("parallel"), name="cast_bf16",
    )(w)


def _cast_t_kernel(wt_ref, o_ref):
    o_ref[...] = wt_ref[...].T.astype(BF16)


def _cast_bf16_t(wt, n_cols, tn=512):
    rows = wt.shape[1]
    return pl.pallas_call(
        _cast_t_kernel, grid=(n_cols // tn,),
        in_specs=[pl.BlockSpec((tn, rows), lambda j: (j, 0))],
        out_specs=pl.BlockSpec((rows, tn), lambda j: (0, j)),
        out_shape=jax.ShapeDtypeStruct((rows, n_cols), BF16),
        compiler_params=_cparams("parallel"), name="cast_bf16_t",
    )(wt)


def _gate_cols_kernel(wt_ref, o_ref):
    sub = lax.broadcasted_iota(jnp.int32, wt_ref.shape, 0)
    hi, lo = _split_hi_lo(jnp.where(sub < N_GATES, wt_ref[...], 0.0).T)
    o_ref[:, :LANES] = hi
    o_ref[:, LANES:] = lo


def _gate_cols(wt):
    rows = wt.shape[1]
    return pl.pallas_call(
        _gate_cols_kernel, grid=(1,),
        in_specs=[pl.BlockSpec((LANES, rows), lambda i: (MAIN_WIDTH // LANES, 0))],
        out_specs=pl.BlockSpec((rows, 2 * LANES), lambda i: (0, 0)),
        out_shape=jax.ShapeDtypeStruct((rows, 2 * LANES), BF16),
        compiler_params=_cparams("arbitrary"), name="gate_cols",
    )(wt)


def _pair_swap(x):
    lane = lax.broadcasted_iota(jnp.int32, x.shape, 1)
    return jnp.where((lane & 1) == 0, pltpu.roll(x, LANES - 1, 1), pltpu.roll(x, 1, 1))


def _proj_kernel(x_ref, mod_ref, n1_ref, w_ref, wg_ref, bg_ref, qg_ref, kg_ref, cos_ref, sin_ref,
                 q_ref, k_ref, v_ref, qm_ref, kmt_ref, vm_ref, om_ref, row_ref, col_ref, *kv_refs, rope, tm):
    mod = mod_ref[0]
    shift, scale = mod[:, :D_MODEL], mod[:, D_MODEL:2 * D_MODEL]
    h = _rms(x_ref[...], n1_ref[...]) * (1.0 + scale) + shift
    h_hi, h_lo = _split_hi_lo(h)

    g = _dot_hi_lo(h_hi, h_lo, wg_ref) + bg_ref[...]
    for cc in range(tm // CHUNK):
        rows, cols = _gate_scan_chunk(g[cc * CHUNK:(cc + 1) * CHUNK, :])
        row_ref[cc] = rows
        col_ref[cc * CHUNK:(cc + 1) * CHUNK, :] = cols

    def rot(seg):
        return seg * cos_ref[...] + _pair_swap(seg) * sin_ref[...] if rope else seg

    qa = _dot(h_hi, w_ref[:, :ATT_WIDTH])
    for hh in range(ATT_HEADS):
        sl = slice(hh * ATT_HEAD_DIM, (hh + 1) * ATT_HEAD_DIM)
        seg = rot(_rms(qa[:, sl], qg_ref[...]))
        q_ref[:, sl] = (seg * Q_SCALE).astype(BF16)

    kv = _dot(h_hi, w_ref[:, ATT_WIDTH:ATT_WIDTH + 2 * KV_WIDTH])
    for hh in range(ATT_KV_HEADS):
        sl = slice(hh * ATT_HEAD_DIM, (hh + 1) * ATT_HEAD_DIM)
        seg = _rms(kv[:, sl], kg_ref[...])
        if kv_refs:
            kv_refs[0][:, sl] = seg
        k_ref[:, sl] = rot(seg).astype(BF16)
    va = kv[:, KV_WIDTH:]
    if kv_refs:
        kv_refs[1][...] = va
    v_ref[...] = va.astype(BF16)

    c0 = ATT_WIDTH + 2 * KV_WIDTH
    qm_ref[...] = (_dot(h_hi, w_ref[:, c0:c0 + MLSTM_WIDTH]) * MLSTM_HEAD_DIM ** -0.5).astype(BF16)
    km = _dot(h_hi, w_ref[:, c0 + MLSTM_WIDTH:c0 + 2 * MLSTM_WIDTH])
    for cc in range(tm // CHUNK):
        kmt_ref[cc] = km[cc * CHUNK:(cc + 1) * CHUNK, :].T.astype(BF16)
    vm_ref[...] = _dot(h_hi, w_ref[:, c0 + 2 * MLSTM_WIDTH:c0 + 3 * MLSTM_WIDTH]).astype(BF16)
    om_ref[...] = _dot(h_hi, w_ref[:, c0 + 3 * MLSTM_WIDTH:c0 + 4 * MLSTM_WIDTH]).astype(BF16)


def _project(x2d, mod3, row_of_tile, n1, w_main, wg_cat, bg, qg, kg, cos_t, sin_t, *, rope, emit_kv, tm=256):
    t = x2d.shape[0]
    n_pos = cos_t.shape[0] // tm
    row = lambda i: (i, 0)
    in_specs = [pl.BlockSpec((tm, D_MODEL), row),
                pl.BlockSpec((1, 1, N_MOD * D_MODEL), lambda i: (row_of_tile(i), 0, 0)),
                _const_spec((1, D_MODEL)),
                _const_spec((D_MODEL, MAIN_WIDTH)),
                _const_spec((D_MODEL, 2 * LANES)), _const_spec((1, LANES)),
                _const_spec((1, ATT_HEAD_DIM)), _const_spec((1, ATT_HEAD_DIM)),
                pl.BlockSpec((tm, ATT_HEAD_DIM), lambda i: (i % n_pos, 0)),
                pl.BlockSpec((tm, ATT_HEAD_DIM), lambda i: (i % n_pos, 0))]
    out_shape = [jax.ShapeDtypeStruct((t, ATT_WIDTH), BF16), jax.ShapeDtypeStruct((t, KV_WIDTH), BF16),
                 jax.ShapeDtypeStruct((t, KV_WIDTH), BF16), jax.ShapeDtypeStruct((t, MLSTM_WIDTH), BF16),
                 jax.ShapeDtypeStruct((t // CHUNK, MLSTM_WIDTH, CHUNK), BF16),
                 jax.ShapeDtypeStruct((t, MLSTM_WIDTH), BF16), jax.ShapeDtypeStruct((t, MLSTM_WIDTH), BF16),
                 jax.ShapeDtypeStruct((t // CHUNK, N_SCAN_ROWS, CHUNK), F32), jax.ShapeDtypeStruct((t, LANES), F32)]
    out_specs = [pl.BlockSpec((tm, ATT_WIDTH), row), pl.BlockSpec((tm, KV_WIDTH), row),
                 pl.BlockSpec((tm, KV_WIDTH), row), pl.BlockSpec((tm, MLSTM_WIDTH), row),
                 pl.BlockSpec((tm // CHUNK, MLSTM_WIDTH, CHUNK), lambda i: (i, 0, 0)),
                 pl.BlockSpec((tm, MLSTM_WIDTH), row), pl.BlockSpec((tm, MLSTM_WIDTH), row),
                 pl.BlockSpec((tm // CHUNK, N_SCAN_ROWS, CHUNK), lambda i: (i, 0, 0)), pl.BlockSpec((tm, LANES), row)]
    if emit_kv:
        out_shape += [jax.ShapeDtypeStruct((t, KV_WIDTH), F32)] * 2
        out_specs += [pl.BlockSpec((tm, KV_WIDTH), row)] * 2
    return pl.pallas_call(
        functools.partial(_proj_kernel, rope=rope, tm=tm),
        grid=(t // tm,), in_specs=in_specs, out_specs=out_specs, out_shape=out_shape,
        compiler_params=_cparams("parallel"), name="in_proj",
    )(x2d, mod3, n1, w_main, wg_cat, bg, qg, kg, cos_t, sin_t)


def _lane_scan(x, op, fill, is_fwd, lane):
    s = 1
    while s < CHUNK:
        from_left = jnp.where(lane >= s, pltpu.roll(x, s, 1), fill)
        from_right = jnp.where(lane < CHUNK - s, pltpu.roll(x, CHUNK - s, 1), fill)
        x = op(x, jnp.where(is_fwd, from_left, from_right))
        s *= 2
    return x


def _gate_scan_chunk(g):
    lane = lax.broadcasted_iota(jnp.int32, (N_CHAINS, CHUNK), 1)
    is_fwd = lax.broadcasted_iota(jnp.int32, (N_CHAINS, CHUNK), 0) < MLSTM_HEADS
    gt = g.T
    fwd, bwd = gt[:N_CHAINS], gt[N_CHAINS:2 * N_CHAINS]
    li = jnp.where(is_fwd, fwd, pltpu.roll(bwd, MLSTM_HEADS, 0))
    f = jnp.where(is_fwd, pltpu.roll(fwd, MLSTM_HEADS, 0), bwd)
    lf = jnp.minimum(f, 0.0) - jnp.log(1.0 + jnp.exp(-jnp.abs(f)))
    b = _lane_scan(lf, jnp.add, 0.0, is_fwd, lane)
    a = li - b
    run_max = _lane_scan(a, jnp.maximum, -jnp.inf, is_fwd, lane)
    all_max = jnp.broadcast_to(jnp.max(a, axis=1, keepdims=True), a.shape)
    last = jnp.where(is_fwd, CHUNK - 1, 0)
    total = jnp.broadcast_to(jnp.sum(jnp.where(lane == last, b, 0.0), axis=1, keepdims=True), a.shape)
    rows = jnp.concatenate([a, b, run_max, all_max, total], axis=0)
    padded = jnp.concatenate([rows, jnp.zeros((LANES - N_SCAN_ROWS, CHUNK), F32)], axis=0)
    return rows, padded.T


def _mlstm_kernel(*refs, n_chunks, has_state, emit_state):
    it = iter(refs)
    q_ref, kt_ref, v_ref, row_ref, col_ref = [next(it) for _ in range(5)]
    c0_ref, m0_ref = (next(it), next(it)) if has_state else (None, None)
    o_ref = next(it)
    c_out, n_out, m_out = (next(it), next(it), next(it)) if emit_state else (None, None, None)
    caug = next(it)

    fresh = not has_state and n_chunks == 1
    start = lambda cc: cc * CHUNK if isinstance(cc, int) else pl.multiple_of(cc * CHUNK, CHUNK)
    b = pl.program_id(0)
    if has_state:
        caug[...] = c0_ref[0]
        m_init = tuple(jnp.full((1, 1), m0_ref[b * N_CHAINS + r], F32) for r in range(N_CHAINS))
    else:
        if not fresh:
            caug[...] = jnp.zeros(caug.shape, F32)
        m_init = tuple(jnp.zeros((1, 1), F32) for _ in range(N_CHAINS))

    sub = lax.broadcasted_iota(jnp.int32, (CHUNK, CHUNK), 0)
    lane = lax.broadcasted_iota(jnp.int32, (CHUNK, CHUNK), 1)
    ones_col = (lax.broadcasted_iota(jnp.int32, (CHUNK, LANES), 1) == 0).astype(BF16)

    def chain(r, cc, m_prev):
        d, hd = divmod(r, MLSTM_HEADS)
        t0 = start(cc)
        hs = slice(hd * MLSTM_HEAD_DIM, (hd + 1) * MLSTM_HEAD_DIM)
        rows = row_ref[cc]
        cols = col_ref[pl.ds(t0, CHUNK), :]
        row = lambda k: rows[k * N_CHAINS + r:k * N_CHAINS + r + 1, :]
        col = lambda k: cols[:, k * N_CHAINS + r:k * N_CHAINS + r + 1]
        q = q_ref[pl.ds(t0, CHUNK), hs]
        kt = kt_ref[cc, hs, :]
        vaug = jnp.concatenate([v_ref[pl.ds(t0, CHUNK), hs], ones_col], axis=1)

        m_col = jnp.maximum(m_prev, col(2))
        keep = (lane <= sub) if d == 0 else (lane >= sub)
        w = jnp.where(keep, jnp.exp(row(0) - m_col), 0.0)
        w_inter = jnp.exp(m_prev - m_col)
        p = (_dot(q, kt) * w).astype(BF16)
        intra = _dot(p, vaug)
        num, den = intra[:, :MLSTM_HEAD_DIM], intra[:, MLSTM_HEAD_DIM:MLSTM_HEAD_DIM + 1]
        if not fresh:
            inter = _dot(q, caug[r].astype(BF16))
            num = num + w_inter * inter[:, :MLSTM_HEAD_DIM]
            den = den + w_inter * inter[:, MLSTM_HEAD_DIM:MLSTM_HEAD_DIM + 1]
        h_out = num / jnp.maximum(jnp.abs(den), jnp.exp(-(col(1) + m_col)))

        m_last = jnp.maximum(m_prev, row(3))
        kw = (kt.astype(F32) * jnp.exp(row(0) - m_last)).astype(BF16)
        update = _dot(kw, vaug)
        caug[r] = update if fresh else jnp.exp(m_prev - m_last[:, :1]) * caug[r] + update
        return h_out, (row(4) + m_last)[:, :1]

    def make_body(add_fwd, add_bwd):
        def body(c, ms):
            new = []
            for r in range(N_CHAINS):
                is_fwd = r < MLSTM_HEADS
                cc = c if is_fwd else n_chunks - 1 - c
                h_out, m_new = chain(r, cc, ms[r])
                hd = r % MLSTM_HEADS
                where = (pl.ds(start(cc), CHUNK), slice(hd * MLSTM_HEAD_DIM, (hd + 1) * MLSTM_HEAD_DIM))
                if add_fwd if is_fwd else add_bwd:
                    o_ref[where] += h_out
                else:
                    o_ref[where] = h_out
                new.append(m_new)
            return tuple(new)
        return body

    half = n_chunks // 2
    ms = m_init
    if half:
        ms = lax.fori_loop(0, half, make_body(False, False), ms)
    if n_chunks % 2:
        ms = make_body(False, True)(half, ms)
    if half:
        ms = lax.fori_loop(n_chunks - half, n_chunks, make_body(True, True), ms)
    if emit_state:
        for r in range(N_CHAINS):
            c_out[0, r] = caug[r, :, :MLSTM_HEAD_DIM]
            n_out[0, r] = caug[r, :, MLSTM_HEAD_DIM:].T[:1, :]
            m_out[0, r] = jnp.broadcast_to(ms[r], (1, LANES))


def _mlstm(qm, kmt, vm, rowq, colq, n_batch, seq, state=None, emit_state=False):
    nc = seq // CHUNK
    mode = {}
    in_specs = [pl.BlockSpec((seq, MLSTM_WIDTH), lambda b: (b, 0), **mode),
                pl.BlockSpec((nc, MLSTM_WIDTH, CHUNK), lambda b: (b, 0, 0), **mode),
                pl.BlockSpec((seq, MLSTM_WIDTH), lambda b: (b, 0), **mode),
                pl.BlockSpec((nc, N_SCAN_ROWS, CHUNK), lambda b: (b, 0, 0), **mode),
                pl.BlockSpec((seq, LANES), lambda b: (b, 0), **mode)]
    args = [qm, kmt, vm, rowq, colq]
    if state is not None:
        in_specs += [pl.BlockSpec((1, N_CHAINS, MLSTM_HEAD_DIM, AUG), lambda b: (b, 0, 0, 0), **mode),
                     pl.BlockSpec(memory_space=pltpu.SMEM)]
        args += list(state)
    out_shape = [jax.ShapeDtypeStruct((n_batch * seq, MLSTM_WIDTH), F32)]
    out_specs = [pl.BlockSpec((seq, MLSTM_WIDTH), lambda b: (b, 0))]
    if emit_state:
        out_shape += [jax.ShapeDtypeStruct((n_batch, N_CHAINS, MLSTM_HEAD_DIM, MLSTM_HEAD_DIM), F32),
                      jax.ShapeDtypeStruct((n_batch, N_CHAINS, 1, MLSTM_HEAD_DIM), F32),
                      jax.ShapeDtypeStruct((n_batch, N_CHAINS, 1, LANES), F32)]
        out_specs += [pl.BlockSpec((1, N_CHAINS, MLSTM_HEAD_DIM, MLSTM_HEAD_DIM), lambda b: (b, 0, 0, 0)),
                      pl.BlockSpec((1, N_CHAINS, 1, MLSTM_HEAD_DIM), lambda b: (b, 0, 0, 0)),
                      pl.BlockSpec((1, N_CHAINS, 1, LANES), lambda b: (b, 0, 0, 0))]
    return pl.pallas_call(
        functools.partial(_mlstm_kernel, n_chunks=nc, has_state=state is not None, emit_state=emit_state),
        grid=(n_batch,), in_specs=in_specs, out_specs=out_specs, out_shape=out_shape,
        scratch_shapes=[pltpu.VMEM((N_CHAINS, MLSTM_HEAD_DIM, AUG), F32)],
        compiler_params=_cparams("parallel"), name="mlstm",
    )(*args)


def _attn_kernel(*refs, tq, has_cache):
    if has_cache:
        q_ref, k_ref, v_ref, ck_ref, cv_ref, o_ref = refs
    else:
        q_ref, k_ref, v_ref, o_ref = refs
    q = q_ref[...]
    qs = jnp.concatenate([q[:, g * ATT_HEAD_DIM:(g + 1) * ATT_HEAD_DIM] for g in range(ATT_GROUP)], axis=0)
    nt = (((1,), (1,)), ((), ()))
    seq = k_ref.shape[0]
    kc = min(seq, ATT_KEY_CHUNK)
    chunks = [(k_ref, v_ref, c * kc) for c in range(seq // kc)]
    if has_cache:
        chunks.insert(0, (ck_ref, cv_ref, None))
    m = l = o = None
    for kr, vr, start in chunks:
        kk, vv = (kr[...], vr[...]) if start is None else (kr[start:start + kc, :], vr[start:start + kc, :])
        s = lax.dot_general(qs, kk.astype(BF16), nt, preferred_element_type=F32)
        mc = jnp.max(s, axis=-1, keepdims=True)
        if m is None:
            m = mc
            p = jnp.exp2(s - m)
            l = jnp.sum(p, axis=-1, keepdims=True)
            o = _dot(p.astype(BF16), vv.astype(BF16))
        else:
            m_new = jnp.maximum(m, mc)
            alpha = jnp.exp2(m - m_new)
            p = jnp.exp2(s - m_new)
            l = alpha * l + jnp.sum(p, axis=-1, keepdims=True)
            o = alpha * o + _dot(p.astype(BF16), vv.astype(BF16))
            m = m_new
    o = o / l
    for g in range(ATT_GROUP):
        o_ref[:, g * ATT_HEAD_DIM:(g + 1) * ATT_HEAD_DIM] = o[g * tq:(g + 1) * tq].astype(BF16)


def _attention(q, k, v, n_batch, seq, cache=None, tq=256):
    nqb = seq // tq
    gw = ATT_GROUP * ATT_HEAD_DIM
    in_specs = [pl.BlockSpec((tq, gw), lambda b, h, i: (b * nqb + i, h)),
                pl.BlockSpec((seq, ATT_HEAD_DIM), lambda b, h, i: (b, h)),
                pl.BlockSpec((seq, ATT_HEAD_DIM), lambda b, h, i: (b, h))]
    args = [q, k, v]
    if cache is not None:
        past = cache[0].shape[0] // n_batch
        in_specs += [pl.BlockSpec((past, ATT_HEAD_DIM), lambda b, h, i: (b, h))] * 2
        args += list(cache)
    return pl.pallas_call(
        functools.partial(_attn_kernel, tq=tq, has_cache=cache is not None),
        grid=(n_batch, ATT_KV_HEADS, nqb), in_specs=in_specs,
        out_specs=pl.BlockSpec((tq, gw), lambda b, h, i: (b * nqb + i, h)),
        out_shape=jax.ShapeDtypeStruct((n_batch * seq, ATT_WIDTH), BF16),
        compiler_params=_cparams("parallel", "parallel", "parallel"), name="attention",
    )(*args)


def _mix_kernel(attn_ref, hm_ref, om_ref, x_ref, mod_ref, mg_ref, wo_ref, n2_ref, rw_ref, rb_ref, *rest, sub):
    for r0 in range(0, x_ref.shape[0], sub):
        _mix_rows(slice(r0, r0 + sub), attn_ref, hm_ref, om_ref, x_ref, mod_ref, mg_ref, wo_ref, n2_ref, rw_ref,
                  rb_ref, *rest[-3:])


def _mix_rows(rows, attn_ref, hm_ref, om_ref, x_ref, mod_ref, mg_ref, wo_ref, n2_ref, rw_ref, rb_ref,
              x1_ref, h2_ref, route_ref):
    mod = mod_ref[0]
    gate1 = mod[:, 2 * D_MODEL:3 * D_MODEL]
    shift2, scale2 = mod[:, 3 * D_MODEL:4 * D_MODEL], mod[:, 4 * D_MODEL:5 * D_MODEL]
    hm = hm_ref[rows, :]
    mg = mg_ref[...]
    parts = []
    for hd in range(MLSTM_HEADS):
        sl = slice(hd * MLSTM_HEAD_DIM, (hd + 1) * MLSTM_HEAD_DIM)
        parts.append(_rms(hm[:, sl], mg[:, sl]))
    om = om_ref[rows, :].astype(F32)
    hmg = jnp.concatenate(parts, axis=1) * (1.0 / (1.0 + jnp.exp(-om)))
    y = _dot(attn_ref[rows, :], wo_ref[:ATT_WIDTH, :]) + _dot(hmg.astype(BF16), wo_ref[ATT_WIDTH:, :])
    x1 = x_ref[rows, :] + gate1 * y
    x1_ref[rows, :] = x1
    h2 = _rms(x1, n2_ref[...]) * (1.0 + scale2) + shift2
    h2_hi, h2_lo = _split_hi_lo(h2)
    h2_ref[rows, :] = _pack_halves(h2)

    lg = _dot_hi_lo(h2_hi, h2_lo, rw_ref) + rb_ref[...]
    lane = lax.broadcasted_iota(jnp.int32, lg.shape, 1).astype(F32)
    neg = -jnp.inf
    first = lambda hit: jnp.min(jnp.where(hit, lane, float(LANES)), axis=-1, keepdims=True)
    gl = jnp.where(lane < N_GROUPS, lg, neg)
    gmax = jnp.max(gl, axis=-1, keepdims=True)
    grp = first(gl == gmax)
    p_grp = 1.0 / jnp.sum(jnp.exp(gl - gmax), axis=-1, keepdims=True)
    lo = ROUTER_LANE0 + grp * EXPERTS_PER_GROUP
    el = jnp.where((lane >= lo) & (lane < lo + EXPERTS_PER_GROUP), lg, neg)
    m1 = jnp.max(el, axis=-1, keepdims=True)
    i1 = first(el == m1)
    el2 = jnp.where(lane == i1, neg, el)
    m2 = jnp.max(el2, axis=-1, keepdims=True)
    i2 = first(el2 == m2)
    r = jnp.exp(m2 - m1)
    w1 = p_grp / (1.0 + r)
    w2 = w1 * r
    route_ref[rows, :] = jnp.where(lane == 0.0, i1 - ROUTER_LANE0, jnp.where(lane == 1.0, i2 - ROUTER_LANE0,
                                   jnp.where(lane == 2.0, w1, jnp.where(lane == 3.0, w2, 0.0))))


def _mix_out(attn, hm, om, x2d, mod3, row_of_tile, mg, w_out, n2, rw_cat, rb, h2_all, tile0, t_total, tm=512):
    t = x2d.shape[0]
    row = lambda i: (i, 0)
    in_specs = [pl.BlockSpec((tm, ATT_WIDTH), row), pl.BlockSpec((tm, MLSTM_WIDTH), row),
                pl.BlockSpec((tm, MLSTM_WIDTH), row), pl.BlockSpec((tm, D_MODEL), row),
                pl.BlockSpec((1, 1, N_MOD * D_MODEL), lambda i: (row_of_tile(i), 0, 0)),
                _const_spec((1, MLSTM_WIDTH)), _const_spec((D_MODEL, D_MODEL)), _const_spec((1, D_MODEL)),
                _const_spec((D_MODEL, 2 * LANES)), _const_spec((1, LANES))]
    args = [attn, hm, om, x2d, mod3, mg, w_out, n2, rw_cat, rb]
    aliases = {}
    if h2_all is not None:
        aliases = {len(args): 1}
        in_specs.append(pl.BlockSpec(memory_space=pl.ANY))
        args.append(h2_all)
    return pl.pallas_call(
        functools.partial(_mix_kernel, sub=512), grid=(t // tm,), in_specs=in_specs,
        out_specs=[pl.BlockSpec((tm, D_MODEL), row), pl.BlockSpec((tm, D_MODEL // 2), lambda i: (tile0 + i, 0)),
                   pl.BlockSpec((tm, LANES), row)],
        out_shape=[jax.ShapeDtypeStruct((t, D_MODEL), F32), jax.ShapeDtypeStruct((t_total, D_MODEL // 2), jnp.uint32),
                   jax.ShapeDtypeStruct((t, LANES), F32)],
        input_output_aliases=aliases,
        compiler_params=_cparams("parallel"), name="mix_out",
    )(*args)


def _rank_kernel(route_ref, rank_ref, cnt_ref, run_ref, tri_ref):
    tr = route_ref.shape[0]

    @pl.when(pl.program_id(0) == 0)
    def _():
        run_ref[...] = jnp.zeros(run_ref.shape, F32)
        tri_ref[...] = (lax.broadcasted_iota(jnp.int32, (tr, tr), 1)
                        < lax.broadcasted_iota(jnp.int32, (tr, tr), 0)).astype(BF16)

    route = route_ref[...]
    lane = lax.broadcasted_iota(jnp.int32, route.shape, 1).astype(F32)
    hit1, hit2 = lane == route[:, 0:1], lane == route[:, 1:2]
    onehot = jnp.where(hit1, 1.0, jnp.where(hit2, 1.0, 0.0))
    before = _dot(tri_ref[...], onehot.astype(BF16)) + run_ref[0:1, :]
    r1 = jnp.sum(jnp.where(hit1, before, 0.0), axis=-1, keepdims=True)
    r2 = jnp.sum(jnp.where(hit2, before, 0.0), axis=-1, keepdims=True)
    rank_ref[...] = jnp.where(lane == 0.0, r1, jnp.where(lane == 1.0, r2, 0.0))
    run_ref[...] = run_ref[...] + jnp.sum(onehot, axis=0, keepdims=True)
    cnt_ref[...] = run_ref[...]


def _ranks(route, tr=512):
    t = route.shape[0]
    return pl.pallas_call(
        _rank_kernel, grid=(t // tr,),
        in_specs=[pl.BlockSpec((tr, LANES), lambda i: (i, 0))],
        out_specs=[pl.BlockSpec((tr, LANES), lambda i: (i, 0)), pl.BlockSpec((8, LANES), lambda i: (0, 0))],
        out_shape=[jax.ShapeDtypeStruct((t, LANES), F32), jax.ShapeDtypeStruct((8, LANES), F32)],
        scratch_shapes=[pltpu.VMEM((8, LANES), F32), pltpu.VMEM((tr, tr), BF16)],
        compiler_params=_cparams("arbitrary"), name="expert_ranks",
    )(route)


def _pos_kernel(route_ref, rank_ref, start_ref, o_ref):
    route, rank = route_ref[...], rank_ref[...]
    lane = lax.broadcasted_iota(jnp.int32, route.shape, 1).astype(F32)
    start = start_ref[...]
    first = lambda col: jnp.sum(jnp.where(lane == route[:, col:col + 1], start, 0.0), axis=-1, keepdims=True)
    p1 = first(0) + rank[:, 0:1]
    p2 = first(1) + rank[:, 1:2]
    tile = jnp.where(lane == 0.0, p1, jnp.where(lane == 1.0, p2, 0.0))
    o_ref[...] = tile.T[:8, :].astype(jnp.int32)


def _positions(route, ranks, starts, tr=512):
    t = route.shape[0]
    pos = pl.pallas_call(
        _pos_kernel, grid=(t // tr,),
        in_specs=[pl.BlockSpec((tr, LANES), lambda i: (i, 0)), pl.BlockSpec((tr, LANES), lambda i: (i, 0)),
                  pl.BlockSpec((1, LANES), lambda i: (0, 0))],
        out_specs=pl.BlockSpec((8, tr), lambda i: (0, i)),
        out_shape=jax.ShapeDtypeStruct((8, t), jnp.int32),
        compiler_params=_cparams("parallel"), name="pair_rows",
    )(route, ranks, starts)
    return pos[:2].reshape(-1)


def _routing_plan(route, ranks, counts):
    t = route.shape[0]
    n_tiles = 2 * t // TM_PAIR + N_EXPERTS
    cnt = counts[0, :N_EXPERTS].astype(jnp.int32)
    padded = (cnt + TM_PAIR - 1) // TM_PAIR * TM_PAIR
    ends = jnp.cumsum(padded)
    starts = ends - padded
    pos = _positions(route, ranks, _pad_lanes(starts.astype(F32)[None]))
    n_used = (ends[-1:] // TM_PAIR).astype(jnp.int32)
    tile_start = jnp.arange(n_tiles, dtype=jnp.int32) * TM_PAIR
    tile_expert = jnp.sum((ends[None, :] <= tile_start[:, None]).astype(jnp.int32), axis=1)
    tile_expert = jnp.minimum(tile_expert, tile_expert[n_used[0] - 1]).astype(jnp.int32)
    mine = tile_expert[:, None] == jnp.arange(N_EXPERTS, dtype=jnp.int32)[None, :]
    rows_left = jnp.sum(jnp.where(mine, (starts + cnt)[None, :], 0), axis=1) - tile_start
    second_half = (rows_left > TM_MOE).astype(jnp.int32)
    tails = jnp.where(cnt > 0, starts + (cnt - 1) // TM_MOE * TM_MOE, -1).astype(jnp.int32)
    change = jnp.concatenate([jnp.ones((1,), jnp.int32), (tile_expert[1:] != tile_expert[:-1]).astype(jnp.int32)])
    weight_slot = (jnp.cumsum(change) - 1) % 2
    ids = jnp.arange(N_EXPERTS, dtype=jnp.int32)
    later_used = (padded > 0)[None, :] & (ids[None, :] > ids[:, None])
    next_used = jnp.min(jnp.where(later_used, ids[None, :], N_EXPERTS), axis=1)
    next_used = jnp.where(next_used < N_EXPERTS, next_used, -1)
    next_expert = jnp.sum(jnp.where(tile_expert[:, None] == ids[None, :], next_used[None, :], 0), axis=1)
    return pos, (tile_expert, n_used, weight_slot.astype(jnp.int32), next_expert.astype(jnp.int32), second_half), \
        tails, n_tiles * TM_PAIR


def _row_copies(pos_ref, n_pairs, tok0, n_rows, make_copy):
    def body(r, carry):
        for k in range(2):
            make_copy(k, r, pos_ref[k * n_pairs + tok0 + r]).start()
        return carry
    lax.fori_loop(0, n_rows, body, 0, unroll=8)


def _dispatch_kernel(pos_ref, tail_ref, h_ref, xs_ref, hbuf, zero_ref, tsem, rsem, *, td, n_tokens):
    i = pl.program_id(0)
    last = pl.num_programs(0) - 1
    tile_copy = lambda t: pltpu.make_async_copy(
        h_ref.at[pl.ds(pl.multiple_of(t * td, td), td)], hbuf.at[t % 3], tsem.at[t % 3])

    def wait_rows(t):
        for _ in range(2):
            pltpu.make_async_copy(hbuf.at[0], xs_ref.at[pl.ds(0, td)], rsem.at[t % 2]).wait()

    @pl.when(i == 0)
    def _():
        tile_copy(0).start()
        zero_ref[...] = jnp.zeros(zero_ref.shape, zero_ref.dtype)
        tail_copy = lambda e: pltpu.make_async_copy(
            zero_ref, xs_ref.at[pl.ds(pl.multiple_of(tail_ref[e], TM_MOE), TM_MOE)], rsem.at[1])
        for e in range(N_EXPERTS):
            pl.when(tail_ref[e] >= 0)(lambda e=e: tail_copy(e).start())
        for e in range(N_EXPERTS):
            pl.when(tail_ref[e] >= 0)(lambda e=e: tail_copy(e).wait())

    @pl.when(i < last)
    def _():
        tile_copy(i + 1).start()

    tile_copy(i).wait()
    src = hbuf.at[i % 3]
    _row_copies(pos_ref, n_tokens, i * td, td, lambda k, r, p: pltpu.make_async_copy(
        src.at[pl.ds(r, 1)], xs_ref.at[pl.ds(p, 1)], rsem.at[i % 2]))

    @pl.when(i >= 1)
    def _():
        wait_rows(i - 1)

    @pl.when(i == last)
    def _():
        wait_rows(i)


def _dispatch(pos, tails, h2, n_rows, td=512):
    t, width = h2.shape
    return pl.pallas_call(
        functools.partial(_dispatch_kernel, td=td, n_tokens=t),
        grid_spec=pltpu.PrefetchScalarGridSpec(
            num_scalar_prefetch=2, grid=(t // td,),
            in_specs=[pl.BlockSpec(memory_space=pl.ANY)],
            out_specs=pl.BlockSpec(memory_space=pl.ANY),
            scratch_shapes=[pltpu.VMEM((3, td, width), h2.dtype), pltpu.VMEM((TM_MOE, width), h2.dtype),
                            pltpu.SemaphoreType.DMA((3,)), pltpu.SemaphoreType.DMA((2,))]),
        out_shape=jax.ShapeDtypeStruct((n_rows, width), h2.dtype),
        compiler_params=_cparams("arbitrary"), name="dispatch",
    )(pos, tails, h2)


def _pack_halves(x):
    half = x.shape[1] // 2
    return pltpu.pack_elementwise([x[:, :half], x[:, half:]], packed_dtype=BF16)


def _unpack_halves(p, dtype):
    return tuple(pltpu.unpack_elementwise(p, index=i, packed_dtype=BF16, unpacked_dtype=F32).astype(dtype)
                 for i in range(2))


def _expert_kernel(te_ref, nu_ref, slot_ref, next_ref, full_ref, xs_ref, wg_ref, wu_ref, wd_ref, ys_ref,
                   wgf, wuf, wdf, wgb, wub, wdb, wsem):
    j = pl.program_id(0)

    def mlp(rows):
        xa, xb = _unpack_halves(xs_ref[rows, :], BF16)
        half = D_MODEL // 2
        g = _dot(xa, wgb[:half, :]) + _dot(xb, wgb[half:, :])
        u = _dot(xa, wub[:half, :]) + _dot(xb, wub[half:, :])
        a = (g / (1.0 + jnp.exp(-g))) * u
        ys_ref[rows, :] = _pack_halves(_dot(a.astype(BF16), wdb[...]))

    def weight_copies(e, s):
        return [pltpu.make_async_copy(w_ref.at[e], buf.at[s], wsem.at[s])
                for w_ref, buf in ((wg_ref, wgf), (wu_ref, wuf), (wd_ref, wdf))]

    @pl.when(j < nu_ref[0])
    def _():
        e, s = te_ref[j], slot_ref[j]

        @pl.when(j == 0)
        def _():
            for cp in weight_copies(e, s):
                cp.start()

        @pl.when((j == 0) | (e != te_ref[jnp.maximum(j - 1, 0)]))
        def _():
            for cp in weight_copies(e, s):
                cp.wait()
            nxt = next_ref[j]

            @pl.when(nxt >= 0)
            def _():
                for cp in weight_copies(nxt, 1 - s):
                    cp.start()

            wgb[...] = wgf[s].astype(BF16)
            wub[...] = wuf[s].astype(BF16)
            wdb[...] = wdf[s].astype(BF16)

        @pl.when(full_ref[j] == 1)
        def _():
            mlp(slice(None))

        @pl.when(full_ref[j] == 0)
        def _():
            mlp(slice(0, TM_MOE))
            ys_ref[TM_MOE:, :] = _pack_halves(jnp.zeros((TM_MOE, D_MODEL), F32))


def _experts(tile_expert, n_used, weight_slot, next_expert, second_half, xs, wg, wu, wd):
    n_tiles = xs.shape[0] // TM_PAIR
    tile = lambda j, te, nu, *_: (jnp.minimum(j, nu[0] - 1), 0)
    hbm = pl.BlockSpec(memory_space=pl.ANY)
    return pl.pallas_call(
        _expert_kernel,
        grid_spec=pltpu.PrefetchScalarGridSpec(
            num_scalar_prefetch=5, grid=(n_tiles,),
            in_specs=[pl.BlockSpec((TM_PAIR, D_MODEL // 2), tile), hbm, hbm, hbm],
            out_specs=pl.BlockSpec((TM_PAIR, D_MODEL // 2), tile),
            scratch_shapes=[pltpu.VMEM((2, D_MODEL, D_EXPERT), F32), pltpu.VMEM((2, D_MODEL, D_EXPERT), F32),
                            pltpu.VMEM((2, D_EXPERT, D_MODEL), F32),
                            pltpu.VMEM((D_MODEL, D_EXPERT), BF16), pltpu.VMEM((D_MODEL, D_EXPERT), BF16),
                            pltpu.VMEM((D_EXPERT, D_MODEL), BF16), pltpu.SemaphoreType.DMA((2,))]),
        out_shape=jax.ShapeDtypeStruct(xs.shape, jnp.uint32),
        compiler_params=_cparams("arbitrary"), name="experts",
    )(tile_expert, n_used, weight_slot, next_expert, second_half, xs, wg, wu, wd)


def _combine_kernel(pos_ref, x1_ref, route_ref, mod_ref, ys_ref, o_ref, ybuf, sem, *, tc, n_tokens, tok0):
    i = pl.program_id(0)
    slot = i % 2

    def gather(tile, s):
        _row_copies(pos_ref, n_tokens, tok0 + tile * tc, tc, lambda k, r, p: pltpu.make_async_copy(
            ys_ref.at[pl.ds(p, 1)], ybuf.at[s, k, pl.ds(r, 1)], sem.at[s]))

    @pl.when(i == 0)
    def _():
        gather(0, 0)

    @pl.when(i + 1 < pl.num_programs(0))
    def _():
        gather(i + 1, 1 - slot)

    for k in range(2):
        pltpu.make_async_copy(ys_ref.at[pl.ds(0, tc)], ybuf.at[slot, k], sem.at[slot]).wait()
    route = route_ref[...]
    w1, w2 = route[:, 2:3], route[:, 3:4]
    half = D_MODEL // 2
    for h, (y1, y2) in enumerate(zip(_unpack_halves(ybuf[slot, 0], F32), _unpack_halves(ybuf[slot, 1], F32))):
        cols = slice(h * half, (h + 1) * half)
        gate2 = mod_ref[0, :, 5 * D_MODEL + h * half:5 * D_MODEL + (h + 1) * half]
        o_ref[:, cols] = x1_ref[:, cols] + gate2 * (w1 * y1 + w2 * y2)


def _combine(pos, x1, route, mod3, row_of_tile, ys, tok0, n_tokens, tc=512):
    t = x1.shape[0]
    row = lambda i, *_: (i, 0)
    return pl.pallas_call(
        functools.partial(_combine_kernel, tc=tc, n_tokens=n_tokens, tok0=tok0),
        grid_spec=pltpu.PrefetchScalarGridSpec(
            num_scalar_prefetch=1, grid=(t // tc,),
            in_specs=[pl.BlockSpec((tc, D_MODEL), row), pl.BlockSpec((tc, LANES), row),
                      pl.BlockSpec((1, 1, N_MOD * D_MODEL), lambda i, *_: (row_of_tile(i), 0, 0)),
                      pl.BlockSpec(memory_space=pl.ANY)],
            out_specs=pl.BlockSpec((tc, D_MODEL), row),
            scratch_shapes=[pltpu.VMEM((2, 2, tc, ys.shape[1]), ys.dtype), pltpu.SemaphoreType.DMA((2,))]),
        out_shape=jax.ShapeDtypeStruct((t, D_MODEL), F32),
        compiler_params=_cparams("arbitrary"), name="combine",
    )(pos, x1, route, mod3, ys)


def _rope_tables(seq):
    pos = np.arange(seq)
    n_freq = ATT_HEAD_DIM // 4
    inv = ROPE_THETA ** (-np.arange(n_freq, dtype=np.float32) / n_freq)
    ang = np.concatenate([(pos // GRID_W).astype(np.float32)[:, None] * inv,
                          (pos % GRID_W).astype(np.float32)[:, None] * inv], axis=-1).astype(np.float32)
    ang = jnp.asarray(ang)
    cos, sin = jnp.cos(ang), jnp.sin(ang)
    cos_t = jnp.repeat(cos, 2, axis=1)
    sin_t = jnp.stack([-sin, sin], axis=-1).reshape(seq, ATT_HEAD_DIM)
    return cos_t, sin_t


def _pad_lanes(a):
    return jnp.pad(a, ((0, 0), (0, LANES - a.shape[1])))


def _layer(x2d, n_batch, seq, mod3, row_of_tile, lw, *, rope_tabs, cache, state, emit, h2_all, tile0, t_total):
    (n1, n2, w_main, wg_cat, bg, qg, kg, mg, w_out, rw_cat, rb) = lw
    cos_t, sin_t = rope_tabs
    outs = _project(x2d, mod3, row_of_tile(512), n1, w_main, wg_cat, bg, qg, kg, cos_t, sin_t,
                    rope=cache is not None, emit_kv=emit, tm=512)
    q, k, v, qm, kmt, vm, om, rowq, colq = outs[:9]
    attn = _attention(q, k, v, n_batch, seq, cache=cache, tq=min(seq, 1024))
    ml = _mlstm(qm, kmt, vm, rowq, colq, n_batch, seq, state=state, emit_state=emit)
    mixed = _mix_out(attn, ml[0], om, x2d, mod3, row_of_tile(MIX_TILE), mg, w_out, n2, rw_cat, rb,
                     h2_all, tile0, t_total, tm=MIX_TILE)
    return mixed, outs[9:], ml[1:]


def kernel(x_prompt, x_sample, cache_k, cache_v, state_C, state_n, state_m, c, c_ctx, mod_w, mod_b, norm1_g, norm2_g,
           w_in, b_gates, q_norm_g, k_norm_g, mlstm_norm_g, w_out, router_group_w, router_group_b, router_expert_w,
           router_expert_b, expert_w_gate, expert_w_up, expert_w_down):
    assert mod_w.shape[0] == 1, "single-layer stack"
    n_ctx, s_ctx, _ = x_prompt.shape
    n_lat, s_lat, _ = x_sample.shape
    t_ctx, t_lat = n_ctx * s_ctx, n_lat * s_lat
    t_all = t_ctx + t_lat
    ctx_row = n_lat

    cond = jnp.concatenate([c, c_ctx[None], jnp.zeros((8 - n_lat - 1, D_MODEL), F32)], axis=0)
    mod3 = _modulation(cond, mod_w[0], mod_b[0][None]).reshape(8, 1, N_MOD * D_MODEL)

    rw =jnp.concatenate([router_group_w[0], jnp.moveaxis(router_expert_w[0], 0, 1).reshape(D_MODEL, N_EXPERTS)], axis=1)
    rb = _pad_lanes(jnp.concatenate([router_group_b[0], router_expert_b[0].reshape(-1)])[None])
    w_in_t = w_in[0].T
    lw = (norm1_g, norm2_g, _cast_bf16_t(w_in_t, MAIN_WIDTH), _gate_cols(w_in_t), _pad_lanes(b_gates),
          q_norm_g, k_norm_g, mlstm_norm_g, _cast_bf16(w_out, D_MODEL), _hi_lo_cat(_pad_lanes(rw)), rb)
    rope_tabs = _rope_tables(s_lat)

    ctx_rows = lambda tm: (lambda i: ctx_row)
    lat_rows = lambda tm: (lambda i: i // (s_lat // tm))
    (x1p, h2_all, routep), (ka, va), (s_c, s_n, s_m) = _layer(
        x_prompt.reshape(t_ctx, D_MODEL), n_ctx, s_ctx, mod3, ctx_rows, lw,
        rope_tabs=rope_tabs, cache=None, state=None, emit=True, h2_all=None, tile0=0, t_total=t_all)

    caug0 = jnp.concatenate([state_C[:, 0], state_n[:, 0][..., None],
                             jnp.zeros(state_n[:, 0].shape + (LANES - 1,), F32)], axis=-1)
    caug0 = caug0.reshape(n_lat, N_CHAINS, MLSTM_HEAD_DIM, AUG)
    past = cache_k.shape[2]
    cache = (cache_k[:, 0].reshape(n_lat * past, KV_WIDTH), cache_v[:, 0].reshape(n_lat * past, KV_WIDTH))
    (x1s, h2_all, routes), _, _ = _layer(
        x_sample.reshape(t_lat, D_MODEL), n_lat, s_lat, mod3, lat_rows, lw,
        rope_tabs=rope_tabs, cache=cache, state=(caug0, state_m[:, 0].reshape(-1)), emit=False,
        h2_all=h2_all, tile0=t_ctx // MIX_TILE, t_total=t_all)

    route = jnp.concatenate([routep, routes], axis=0)
    ranks, counts = _ranks(route)
    pos, tile_plan, tails, n_rows = _routing_plan(route, ranks, counts)
    xs = _dispatch(pos, tails, h2_all, n_rows)
    y_sorted = _experts(*tile_plan, xs, expert_w_gate[0], expert_w_up[0], expert_w_down[0])
    yp = _combine(pos, x1p, routep, mod3, ctx_rows(512), y_sorted, 0, t_all)
    ys = _combine(pos, x1s, routes, mod3, lat_rows(512), y_sorted, t_ctx, t_all)

    kv_shape = (n_ctx, 1, s_ctx, ATT_KV_HEADS, ATT_HEAD_DIM)
    return (yp.reshape(x_prompt.shape), ys.reshape(x_sample.shape), ka.reshape(kv_shape), va.reshape(kv_shape),
            s_c.reshape(n_ctx, 1, 2, MLSTM_HEADS, MLSTM_HEAD_DIM, MLSTM_HEAD_DIM),
            s_n.reshape(n_ctx, 1, 2, MLSTM_HEADS, MLSTM_HEAD_DIM), s_m[..., 0, 0].reshape(n_ctx, 1, 2, MLSTM_HEADS))
```
